```python
import math
import jax, jax.numpy as jnp
from jax import lax
import numpy as np

D_MODEL = 1024
BATCH = 8
SEQ = 4096
DEPTH = 4

HEAD_DIM = 64
SWA_HEADS = 8
SWA_KV_HEADS = 2
SWA_GROUP = SWA_HEADS // SWA_KV_HEADS
WINDOW = 128
ATTN_BLOCK = 128
HG_HEADS = 4
HG_DK = 64
HG_DV = 64
HG_CHUNK = 16
MEM_LEN = 256
MEM_HEADS = 4
N_BUCKETS = 32
MAX_DISTANCE = 128
N_EXPERTS = 16
N_GROUPS = 4
EXPERTS_PER_GROUP = N_EXPERTS // N_GROUPS
TOP_K = 2
D_EXPERT = 512
MOE_BLOCK = 256
LN_EPS = 1e-5
RMS_EPS = 1e-6
ALPHA = (2 * DEPTH) ** 0.25
BETA = (8 * DEPTH) ** -0.25
SPLITS = (SWA_HEADS * HEAD_DIM, SWA_KV_HEADS * HEAD_DIM, SWA_KV_HEADS * HEAD_DIM,
          HG_HEADS * HG_DK, HG_HEADS * HG_DK, HG_HEADS * HG_DV, HG_HEADS * HG_DV,
          MEM_HEADS * HEAD_DIM)
IN_WIDTH = sum(SPLITS)
MIX_WIDTH = SWA_HEADS * HEAD_DIM + HG_HEADS * HG_DV + MEM_HEADS * HEAD_DIM

kernel_name = "hybrid_swa_hgrn2_memory_moe_deepnorm"


def layer_norm(x, g, b):
    xf = x.astype(jnp.float32)
    mu = jnp.mean(xf, axis=-1, keepdims=True)
    var = jnp.mean(jnp.square(xf - mu), axis=-1, keepdims=True)
    return ((xf - mu) * lax.rsqrt(var + LN_EPS) * g.astype(jnp.float32) + b.astype(jnp.float32)).astype(x.dtype)


def t5_bucket(dist):
    max_exact = N_BUCKETS // 2
    d = jnp.maximum(dist, 0)
    large = max_exact + (jnp.log(jnp.maximum(d, 1).astype(jnp.float32) / max_exact)
                         / math.log(MAX_DISTANCE / max_exact) * (N_BUCKETS - max_exact)).astype(jnp.int32)
    large = jnp.minimum(large, N_BUCKETS - 1)
    return jnp.where(d < max_exact, d, large)


def banded_rel_bias(rel_bias):
    i = jnp.arange(ATTN_BLOCK)[:, None]
    j = jnp.arange(2 * ATTN_BLOCK)[None, :]
    bias = rel_bias[t5_bucket(i + ATTN_BLOCK - j)]
    return jnp.transpose(bias, (2, 0, 1)).astype(jnp.float32)


def swa_with_sinks(q, k, v, sinks, pos_bias):
    B, S, _ = q.shape
    nb = S // ATTN_BLOCK
    qb = q.reshape(B, nb, ATTN_BLOCK, SWA_KV_HEADS, SWA_GROUP, HEAD_DIM)

    def with_prev(t):
        tb = t.reshape(B, nb, ATTN_BLOCK, SWA_KV_HEADS, HEAD_DIM)
        prev = jnp.pad(tb, ((0, 0), (1, 0), (0, 0), (0, 0), (0, 0)))[:, :-1]
        return jnp.concatenate([prev, tb], axis=2)

    kb, vb = with_prev(k), with_prev(v)
    logits = jnp.einsum('bnqhgd,bnkhd->bnhgqk', qb, kb).astype(jnp.float32) * (HEAD_DIM ** -0.5)
    logits = logits + pos_bias.reshape(SWA_KV_HEADS, SWA_GROUP, ATTN_BLOCK, 2 * ATTN_BLOCK)
    i = jnp.arange(ATTN_BLOCK)[:, None]
    j = jnp.arange(2 * ATTN_BLOCK)[None, :]
    dist = i + ATTN_BLOCK - j
    in_window = (dist >= 0) & (dist < WINDOW)
    not_pad = (jnp.arange(nb)[:, None, None] > 0) | (j >= ATTN_BLOCK)[None]
    mask = in_window[None] & not_pad
    logits = jnp.where(mask[None, :, None, None], logits, -jnp.inf)
    sink = jnp.broadcast_to(sinks.astype(jnp.float32).reshape(SWA_KV_HEADS, SWA_GROUP, 1, 1),
                            logits.shape[:-1] + (1,))
    probs = jax.nn.softmax(jnp.concatenate([logits, sink], axis=-1), axis=-1)[..., :-1]
    out = jnp.einsum('bnhgqk,bnkhd->bnqhgd', probs.astype(v.dtype), vb)
    return out.reshape(B, S, SWA_HEADS * HEAD_DIM)


def hgrn2(q, f, i, gate, lb, norm_w):
    B, S, _ = q.shape
    nc = S // HG_CHUNK
    f32 = jnp.float32

    def heads(t, d):
        return t.astype(f32).reshape(B, nc, HG_CHUNK, HG_HEADS, d).transpose(0, 3, 1, 2, 4)

    lbf = lb.astype(f32).reshape(HG_HEADS, 1, 1, HG_DK)
    log_f = jnp.logaddexp(jnp.log(lbf), jnp.log1p(-lbf) + jax.nn.log_sigmoid(heads(f, HG_DK)))
    kk = -jnp.expm1(log_f)
    qq = jax.nn.silu(heads(q, HG_DK)) * (HG_DK ** -0.5)
    vv = heads(i, HG_DV)
    b = jnp.cumsum(log_f, axis=3)
    causal = jnp.tril(jnp.ones((HG_CHUNK, HG_CHUNK), dtype=bool))
    diff = b[..., :, None, :] - b[..., None, :, :]
    decay = jnp.exp(jnp.where(causal[:, :, None], diff, -jnp.inf))
    scores = jnp.einsum('bhntk,bhnsk,bhntsk->bhnts', qq, kk, decay)
    o_intra = jnp.einsum('bhnts,bhnsv->bhntv', scores, vv)
    b_last = b[..., -1:, :]
    u = jnp.einsum('bhnsk,bhnsv->bhnkv', kk * jnp.exp(b_last - b), vv)
    chunk_decay = jnp.exp(b_last[..., 0, :])

    def step(state, inp):
        dec, upd = inp
        return dec[..., None] * state + upd, state

    s0 = jnp.zeros((B, HG_HEADS, HG_DK, HG_DV), f32)
    _, s_prev = lax.scan(step, s0, (jnp.moveaxis(chunk_decay, 2, 0), jnp.moveaxis(u, 2, 0)))
    s_prev = jnp.moveaxis(s_prev, 0, 2)
    o_inter = jnp.einsum('bhntk,bhnkv->bhntv', qq * jnp.exp(b), s_prev)
    o = (o_intra + o_inter).transpose(0, 2, 3, 1, 4).reshape(B, S, HG_HEADS, HG_DV)
    o = o * lax.rsqrt(jnp.mean(o * o, axis=-1, keepdims=True) + RMS_EPS) * norm_w.astype(f32)
    o = o.reshape(B, S, HG_HEADS * HG_DV) * jax.nn.silu(gate.astype(f32))
    return o.astype(q.dtype)


def memory_attention(q, mem_k, mem_v):
    B, S, _ = q.shape
    qh = q.reshape(B, S, MEM_HEADS, HEAD_DIM)
    logits = jnp.einsum('bshd,bmhd->bhsm', qh, mem_k).astype(jnp.float32) * (HEAD_DIM ** -0.5)
    probs = jax.nn.softmax(logits, axis=-1)
    out = jnp.einsum('bhsm,bmhd->bshd', probs.astype(mem_v.dtype), mem_v)
    return out.reshape(B, S, MEM_HEADS * HEAD_DIM)


def route(x2, w_router, router_bias):
    scores = jax.nn.sigmoid((x2 @ w_router).astype(jnp.float32))
    sel = scores + router_bias.astype(jnp.float32)
    grouped = sel.reshape(-1, N_GROUPS, EXPERTS_PER_GROUP)
    group_score = jnp.sum(lax.top_k(grouped, TOP_K)[0], axis=-1)
    best = jnp.argmax(group_score, axis=-1)
    in_group = (jnp.arange(N_EXPERTS) // EXPERTS_PER_GROUP)[None, :] == best[:, None]
    _, idx = lax.top_k(jnp.where(in_group, sel, -jnp.inf), TOP_K)
    w = jnp.take_along_axis(scores, idx, axis=-1)
    w = w / jnp.sum(w, axis=-1, keepdims=True)
    return idx, w


def moe(x, w_router, router_bias, w_gate, w_up, w_down):
    B, S, D = x.shape
    n_tok = B * S
    x2 = x.reshape(n_tok, D)
    idx, w = route(x2, w_router, router_bias)
    n_assign = n_tok * TOP_K
    e_flat = idx.reshape(-1)
    tok = jnp.repeat(jnp.arange(n_tok, dtype=jnp.int32), TOP_K)
    w_flat = w.reshape(-1)
    order = jnp.argsort(e_flat)
    e_sorted = e_flat[order]
    counts = jnp.bincount(e_flat, length=N_EXPERTS)
    padded = (counts + MOE_BLOCK - 1) // MOE_BLOCK * MOE_BLOCK
    starts = jnp.cumsum(counts) - counts
    padded_ends = jnp.cumsum(padded)
    padded_starts = padded_ends - padded
    dest = padded_starts[e_sorted] + jnp.arange(n_assign) - starts[e_sorted]
    cap = n_assign + N_EXPERTS * MOE_BLOCK
    n_blocks = cap // MOE_BLOCK
    slot_tok = jnp.zeros((cap,), jnp.int32).at[dest].set(tok[order])
    slot_w = jnp.zeros((cap,), w_flat.dtype).at[dest].set(w_flat[order])
    block_expert = jnp.minimum(
        jnp.searchsorted(padded_ends, jnp.arange(n_blocks) * MOE_BLOCK, side='right'), N_EXPERTS - 1)
    xs = x2[slot_tok].reshape(n_blocks, MOE_BLOCK, D)

    def expert_block(args):
        xb, e = args
        h = jax.nn.silu(xb @ w_gate[e]) * (xb @ w_up[e])
        return h @ w_down[e]

    ys = lax.map(expert_block, (xs, block_expert)).reshape(cap, D)
    y = jnp.zeros((n_tok, D), x.dtype).at[slot_tok].add(ys * slot_w[:, None].astype(ys.dtype))
    return y.reshape(B, S, D)


def setup_inputs(seed: int = 0) -> dict:
    key = jax.random.key(seed)
    ks = jax.random.split(key, 20)

    def normal(k, shape, scale):
        return jax.random.normal(k, shape, jnp.float32) * scale

    col_scale = jnp.concatenate([jnp.full((wd,), s, jnp.float32) for wd, s in
                                 zip(SPLITS, (1.0, 1.0, BETA, 1.0, 1.0, BETA, 1.0, 1.0))])
    mem_scale = jnp.concatenate([jnp.ones((MEM_HEADS * HEAD_DIM,), jnp.float32),
                                 jnp.full((MEM_HEADS * HEAD_DIM,), BETA, jnp.float32)])
    return {
        "x": normal(ks[0], (BATCH, SEQ, D_MODEL), 1.0),
        "mem": normal(ks[1], (BATCH, MEM_LEN, D_MODEL), 1.0),
        "w_in": normal(ks[2], (DEPTH, D_MODEL, IN_WIDTH), D_MODEL ** -0.5) * col_scale,
        "b_in": normal(ks[3], (DEPTH, IN_WIDTH), 0.02),
        "w_mem_kv": normal(ks[4], (DEPTH, D_MODEL, 2 * MEM_HEADS * HEAD_DIM), D_MODEL ** -0.5) * mem_scale,
        "attn_sinks": normal(ks[5], (DEPTH, SWA_HEADS), 0.5),
        "rel_bias": normal(ks[6], (N_BUCKETS, SWA_HEADS), 0.5),
        "hgrn_lb_logits": normal(ks[7], (DEPTH, HG_HEADS * HG_DK), 1.0),
        "hgrn_norm": 1.0 + normal(ks[8], (DEPTH, HG_DV), 0.02),
        "w_out": normal(ks[9], (DEPTH, MIX_WIDTH, D_MODEL), MIX_WIDTH ** -0.5 * BETA),
        "ln1_g": 1.0 + normal(ks[10], (DEPTH, D_MODEL), 0.02),
        "ln1_b": normal(ks[11], (DEPTH, D_MODEL), 0.02),
        "w_router": normal(ks[12], (D_MODEL, N_EXPERTS), D_MODEL ** -0.5),
        "router_bias": normal(ks[13], (N_EXPERTS,), 0.01),
        "w_gate": normal(ks[14], (DEPTH, N_EXPERTS, D_MODEL, D_EXPERT), D_MODEL ** -0.5),
        "w_up": normal(ks[15], (DEPTH, N_EXPERTS, D_MODEL, D_EXPERT), D_MODEL ** -0.5),
        "w_down": normal(ks[16], (DEPTH, N_EXPERTS, D_EXPERT, D_MODEL), D_EXPERT ** -0.5 * BETA),
        "ln2_g": 1.0 + normal(ks[17], (DEPTH, D_MODEL), 0.02),
        "ln2_b": normal(ks[18], (DEPTH, D_MODEL), 0.02),
    }


def reference(x, mem, w_in, b_in, w_mem_kv, attn_sinks, rel_bias, hgrn_lb_logits, hgrn_norm, w_out,
              ln1_g, ln1_b, w_router, router_bias, w_gate, w_up, w_down, ln2_g, ln2_b):
    B, S, _ = x.shape
    M = mem.shape[1]
    split_points = [int(c) for c in np.cumsum(SPLITS)[:-1]]
    pos_bias = banded_rel_bias(rel_bias)
    lb_all = jnp.cumsum(jax.nn.softmax(hgrn_lb_logits.astype(jnp.float32), axis=0), axis=0)
    lb_all = lb_all - lb_all[0:1]
    for l in range(DEPTH):
        proj = x @ w_in[l] + b_in[l]
        sq, sk, sv, hq, hf, hi, hg, mq = jnp.split(proj, split_points, axis=-1)
        mk, mv = jnp.split(mem @ w_mem_kv[l], 2, axis=-1)
        mk = mk.reshape(B, M, MEM_HEADS, HEAD_DIM)
        mv = mv.reshape(B, M, MEM_HEADS, HEAD_DIM)
        mix = jnp.concatenate([
            swa_with_sinks(sq, sk, sv, attn_sinks[l], pos_bias),
            hgrn2(hq, hf, hi, hg, lb_all[l], hgrn_norm[l]),
            memory_attention(mq, mk, mv),
        ], axis=-1)
        x = layer_norm(ALPHA * x + mix @ w_out[l], ln1_g[l], ln1_b[l])
        x = layer_norm(ALPHA * x + moe(x, w_router, router_bias, w_gate[l], w_up[l], w_down[l]),
                       ln2_g[l], ln2_b[l])
    return x
```

```python
import functools
import math

import numpy as np
import jax
import jax.numpy as jnp
from jax import lax
from jax.experimental import pallas as pl
from jax.experimental.pallas import tpu as pltpu

F32 = jnp.float32
BF16 = jnp.bfloat16
I32 = jnp.int32

D_MODEL = 1024
DEPTH = 4
HEAD_DIM = 64
SWA_HEADS = 8
SWA_KV_HEADS = 2
SWA_GROUP = SWA_HEADS // SWA_KV_HEADS
WINDOW = 128
ATTN_BLOCK = 128
HG_HEADS = 4
HG_DK = 64
HG_DV = 64
HG_WIDTH = HG_HEADS * HG_DK
MEM_HEADS = 4
N_BUCKETS = 32
MAX_DISTANCE = 128
N_EXPERTS = 16
N_GROUPS = 4
EXPERTS_PER_GROUP = 4
D_EXPERT = 512
LN_EPS = 1e-5
RMS_EPS = 1e-6
ALPHA = (2 * DEPTH) ** 0.25
SPLITS = (512, 128, 128, 256, 256, 256, 256, 256)
IN_WIDTH = sum(SPLITS)
MIX_WIDTH = 1024
ATTN_SCALE = HEAD_DIM ** -0.5

SUBLANES = 8
LANES = 128
ROW_TILES = D_MODEL // LANES

HG_CHUNK = 128
HG_SUB = 32
HG_SAFE_DECAY = 80.0
N_PAIRS = 6
N_CLASSES = N_GROUPS * N_PAIRS
CLASS_ROWS = 32
MOE_BLK = 256
PERMUTE_ROWS = 2048

_PAIR_LO = np.array([0, 0, 0, 1, 1, 2], np.int32)
_PAIR_HI = np.array([1, 2, 3, 2, 3, 3], np.int32)
_CLASS_ELO = np.array([g * 4 + _PAIR_LO[p] for g in range(N_GROUPS) for p in range(N_PAIRS)], np.int32)
_CLASS_EHI = np.array([g * 4 + _PAIR_HI[p] for g in range(N_GROUPS) for p in range(N_PAIRS)], np.int32)


def _params(semantics, vmem_mib):
    return pltpu.CompilerParams(dimension_semantics=semantics, vmem_limit_bytes=vmem_mib * 1024 * 1024)


def _dot(a, b):
    return jnp.dot(a, b, preferred_element_type=F32)


def _dot_nt(a, b):
    return lax.dot_general(a, b, (((1,), (1,)), ((), ())), preferred_element_type=F32)


def _dot_tn(a, b):
    return lax.dot_general(a, b, (((0,), (0,)), ((), ())), preferred_element_type=F32)


def _load_rows(z_ref, n_rows):
    return jnp.concatenate(
        [z_ref[pl.ds(cb, n_rows, stride=ROW_TILES), :] for cb in range(ROW_TILES)], axis=1)


def _store_rows(z_ref, val, n_rows):
    for cb in range(ROW_TILES):
        z_ref[pl.ds(cb, n_rows, stride=ROW_TILES), :] = val[:, cb * LANES:(cb + 1) * LANES]


def _in_proj_body(x_ref, w_ref, b_ref, *out_refs):
    acc = _dot(x_ref[...].astype(BF16), w_ref[...]) + b_ref[...]
    off = 0
    for ref, width in zip(out_refs, SPLITS):
        ref[...] = acc[:, off:off + width].astype(ref.dtype)
        off += width


def _in_proj(x2, w_bf, b_row):
    n = x2.shape[0]
    tm = min(512, n)
    dtypes = (BF16, BF16, BF16, F32, F32, F32, F32, BF16)
    return pl.pallas_call(
        _in_proj_body,
        grid=(n // tm,),
        in_specs=[pl.BlockSpec((tm, D_MODEL), lambda i: (i, 0)),
                  pl.BlockSpec((D_MODEL, IN_WIDTH), lambda i: (0, 0)),
                  pl.BlockSpec((1, IN_WIDTH), lambda i: (0, 0))],
        out_specs=[pl.BlockSpec((tm, w), lambda i: (i, 0)) for w in SPLITS],
        out_shape=[jax.ShapeDtypeStruct((n, w), dt) for w, dt in zip(SPLITS, dtypes)],
        compiler_params=_params(("arbitrary",), 48),
        name="in_proj",
    )(x2, w_bf, b_row)


def _mem_kv_body(mem_ref, w_ref, k_ref, v_ref):
    acc = _dot(mem_ref[...].astype(BF16), w_ref[0].astype(BF16))
    half = MEM_HEADS * HEAD_DIM
    k_ref[0] = acc[:, :half].astype(BF16)
    v_ref[0] = acc[:, half:].astype(BF16)


def _mem_kv(mem2, w_mem_kv):
    rows = mem2.shape[0]
    tm = min(512, rows)
    half = MEM_HEADS * HEAD_DIM
    return pl.pallas_call(
        _mem_kv_body,
        grid=(DEPTH, rows // tm),
        in_specs=[pl.BlockSpec((tm, D_MODEL), lambda l, i: (i, 0)),
                  pl.BlockSpec((1, D_MODEL, 2 * half), lambda l, i: (l, 0, 0))],
        out_specs=[pl.BlockSpec((1, tm, half), lambda l, i: (l, i, 0)),
                   pl.BlockSpec((1, tm, half), lambda l, i: (l, i, 0))],
        out_shape=[jax.ShapeDtypeStruct((DEPTH, rows, half), BF16)] * 2,
        compiler_params=_params(("arbitrary", "arbitrary"), 32),
        name="mem_kv",
    )(mem2, w_mem_kv)


def _swa_body(sink_ref, q_ref, kp_ref, ko_ref, vp_ref, vo_ref, bias_ref, o_ref):
    n = pl.program_id(1)
    kk = jnp.concatenate([kp_ref[...], ko_ref[...]], axis=0)
    vv = jnp.concatenate([vp_ref[...], vo_ref[...]], axis=0)
    col = lax.broadcasted_iota(I32, (1, 2 * ATTN_BLOCK), 1)
    prev_pad = jnp.where(n > 0, 0.0, -jnp.inf)
    pad = jnp.where(col >= ATTN_BLOCK, 0.0, prev_pad)
    outs = []
    for h in range(SWA_HEADS):
        g = h // SWA_GROUP
        q_h = q_ref[:, h * HEAD_DIM:(h + 1) * HEAD_DIM]
        k_g = kk[:, g * HEAD_DIM:(g + 1) * HEAD_DIM]
        v_g = vv[:, g * HEAD_DIM:(g + 1) * HEAD_DIM]
        logits = _dot_nt(q_h, k_g) * ATTN_SCALE + bias_ref[h] + pad
        sink = sink_ref[h]
        m = jnp.maximum(jnp.max(logits, axis=-1, keepdims=True), sink)
        e = jnp.exp(logits - m)
        denom = jnp.sum(e, axis=-1, keepdims=True) + jnp.exp(sink - m)
        outs.append(_dot(e.astype(BF16), v_g) / denom)
    o_ref[...] = jnp.concatenate(outs, axis=-1).astype(o_ref.dtype)


def _swa(sq, sk, sv, sinks, bias, batch, seq):
    nb = seq // ATTN_BLOCK
    own = lambda b, n: (b * nb + n, 0)
    prev = lambda b, n: (b * nb + jnp.maximum(n - 1, 0), 0)
    kvw = SWA_KV_HEADS * HEAD_DIM
    qw = SWA_HEADS * HEAD_DIM
    return pl.pallas_call(
        _swa_body,
        grid=(batch, nb),
        in_specs=[pl.BlockSpec(memory_space=pltpu.SMEM),
                  pl.BlockSpec((ATTN_BLOCK, qw), own),
                  pl.BlockSpec((ATTN_BLOCK, kvw), prev),
                  pl.BlockSpec((ATTN_BLOCK, kvw), own),
                  pl.BlockSpec((ATTN_BLOCK, kvw), prev),
                  pl.BlockSpec((ATTN_BLOCK, kvw), own),
                  pl.BlockSpec((SWA_HEADS, ATTN_BLOCK, 2 * ATTN_BLOCK), lambda b, n: (0, 0, 0))],
        out_specs=pl.BlockSpec((ATTN_BLOCK, qw), own),
        out_shape=jax.ShapeDtypeStruct((batch * seq, qw), BF16),
        compiler_params=_params(("arbitrary", "arbitrary"), 32),
        name="swa",
    )(sinks, sq, sk, sk, sv, sv, bias)


def _cumsum_rows(x):
    n = x.shape[0]
    row = lax.broadcasted_iota(I32, (n, 1), 0)
    s = 1
    while s < n:
        x = x + jnp.where(row >= s, pltpu.roll(x, s, 0), 0.0)
        s *= 2
    return x


def _rows_from_blocks(vals, width):
    return jnp.concatenate([jnp.broadcast_to(v, (HG_SUB, width)) for v in vals], axis=0)


def _hgrn_body(hq_ref, hf_ref, hi_ref, hg_ref, loglb_ref, log1mlb_ref, nw_ref, bo_ref, o_ref,
               st_ref, a_scr, q_scr, k_scr, oi_scr):
    c = pl.program_id(1)
    n_sub = HG_CHUNK // HG_SUB
    width = HG_WIDTH

    @pl.when(c == 0)
    def _():
        st_ref[...] = jnp.zeros_like(st_ref)

    z = hf_ref[...]
    log_sig = jnp.minimum(z, 0.0) - jnp.log1p(jnp.exp(-jnp.abs(z)))
    t_a = loglb_ref[...]
    t_b = log1mlb_ref[...] + log_sig
    log_f = jnp.maximum(t_a, t_b) + jnp.log1p(jnp.exp(-jnp.abs(t_a - t_b)))
    kk = jnp.exp(t_b - z)
    hq = hq_ref[...]
    qq = hq * jax.nn.sigmoid(hq) * (HG_DK ** -0.5)
    vv = hi_ref[...]

    a_cum = _cumsum_rows(log_f)
    ends = [a_cum[HG_SUB * j + HG_SUB - 1:HG_SUB * j + HG_SUB, :] for j in range(n_sub)]
    zero_row = jnp.zeros((1, width), F32)
    starts = [zero_row] + ends[:-1]
    a_loc = a_cum - _rows_from_blocks(starts, width)
    e_loc = _rows_from_blocks(ends, width) - a_cum
    worst = starts[0] - ends[0]
    for j in range(1, n_sub):
        worst = jnp.maximum(worst, starts[j] - ends[j])
    unsafe = jnp.max(worst) > HG_SAFE_DECAY

    q_sub = qq * jnp.exp(a_loc)
    k_sub = kk * jnp.exp(-a_loc)
    k_end = kk * jnp.exp(e_loc)
    one_row = jnp.ones((1, width), F32)
    q_dec = q_sub * _rows_from_blocks([jnp.exp(s) for s in starts], width)
    k_dec = k_end * _rows_from_blocks([jnp.exp(ends[-1] - e) for e in ends], width)
    q_top = q_sub * _rows_from_blocks(
        [zero_row, zero_row, one_row, jnp.exp(ends[2] - ends[1])], width)
    k_top = k_end * _rows_from_blocks(
        [jnp.exp(ends[1] - ends[0]), one_row, zero_row, zero_row], width)

    row = lax.broadcasted_iota(I32, (HG_CHUNK, 1), 0)
    sub_of_row = row // HG_SUB
    t_idx = lax.broadcasted_iota(I32, (HG_CHUNK, HG_CHUNK), 0)
    s_idx = lax.broadcasted_iota(I32, (HG_CHUNK, HG_CHUNK), 1)
    m_sub = ((t_idx // HG_SUB) == (s_idx // HG_SUB)) & (t_idx >= s_idx)
    m_sub2 = jnp.concatenate([m_sub, m_sub], axis=0)
    lane = lax.broadcasted_iota(I32, (1, LANES), 1)
    lo = lane < HG_DK
    bd = (lax.broadcasted_iota(I32, (LANES, LANES), 0) // HG_DV) == \
         (lax.broadcasted_iota(I32, (LANES, LANES), 1) // HG_DK)
    dec_row = jnp.exp(ends[-1])

    def heads_on_rows(x):
        return jnp.concatenate([jnp.where(lo, x, 0.0), jnp.where(lo, 0.0, x)], axis=0)

    o_inter = []
    for p in range(width // LANES):
        sl = slice(p * LANES, (p + 1) * LANES)
        qs, ks, ke = q_sub[:, sl], k_sub[:, sl], k_end[:, sl]
        p_sub = _dot_nt(heads_on_rows(qs).astype(BF16), ks.astype(BF16))
        q_x = jnp.concatenate([jnp.where(sub_of_row == 1, qs, 0.0),
                               jnp.where(sub_of_row == 3, qs, 0.0), q_top[:, sl]], axis=1)
        k_x = jnp.concatenate([jnp.where(sub_of_row == 0, ke, 0.0),
                               jnp.where(sub_of_row == 2, ke, 0.0), k_top[:, sl]], axis=1)
        q_x2 = jnp.concatenate([jnp.where(jnp.tile(lo, (1, 3)), q_x, 0.0),
                                jnp.where(jnp.tile(lo, (1, 3)), 0.0, q_x)], axis=0)
        p_x = _dot_nt(q_x2.astype(BF16), k_x.astype(BF16))
        p_all = jnp.where(m_sub2, p_sub, 0.0) + p_x
        p_cat = jnp.concatenate([p_all[:HG_CHUNK], p_all[HG_CHUNK:]], axis=1)
        v_p = vv[:, sl]
        oi_scr[:, sl] = _dot(p_cat.astype(BF16), heads_on_rows(v_p).astype(BF16))
        st = st_ref[p]
        o_inter.append(_dot_nt(q_dec[:, sl].astype(BF16), st.astype(BF16)))
        upd = _dot_tn(v_p.astype(BF16), k_dec[:, sl].astype(BF16))
        st_ref[p] = st * dec_row[:, sl] + jnp.where(bd, upd, 0.0)
    o_inter = jnp.concatenate(o_inter, axis=1)

    @pl.when(unsafe)
    def _():
        a_scr[...] = a_cum
        q_scr[...] = qq
        k_scr[...] = kk

        def body(t, carry):
            d = a_scr[pl.ds(t, 1), :] - a_scr[...]
            w = jnp.exp(jnp.where(row <= t, d, -jnp.inf))
            prod = (q_scr[pl.ds(t, 1), :] * k_scr[...]) * w
            e = _dot(prod.astype(BF16), bo_ref[...])
            oi_scr[pl.ds(t, 1), :] = jnp.sum(e * hi_ref[...], axis=0, keepdims=True)
            return carry

        lax.fori_loop(0, HG_CHUNK, body, 0)

    o = oi_scr[...] + o_inter
    sq = o * o
    sq_hi = sq.astype(BF16)
    sq_lo = (sq - sq_hi.astype(F32)).astype(BF16)
    ms = (_dot(sq_hi, bo_ref[...]) + _dot(sq_lo, bo_ref[...])) * (1.0 / HG_DV)
    gate = hg_ref[...]
    o = o * lax.rsqrt(ms + RMS_EPS) * nw_ref[...] * (gate * jax.nn.sigmoid(gate))
    o_ref[...] = o.astype(o_ref.dtype)


def _hgrn(hq, hf, hi, hg, loglb, log1mlb, nw_row, head_ones, batch, seq):
    nc = seq // HG_CHUNK
    blk = lambda b, c: (b * nc + c, 0)
    row = lambda b, c: (0, 0)
    w = HG_WIDTH
    return pl.pallas_call(
        _hgrn_body,
        grid=(batch, nc),
        in_specs=[pl.BlockSpec((HG_CHUNK, w), blk)] * 4 + [pl.BlockSpec((1, w), row)] * 3
                 + [pl.BlockSpec((w, w), row)],
        out_specs=pl.BlockSpec((HG_CHUNK, w), blk),
        out_shape=jax.ShapeDtypeStruct((batch * seq, w), BF16),
        scratch_shapes=[pltpu.VMEM((w // LANES, LANES, LANES), F32)]
                       + [pltpu.VMEM((HG_CHUNK, w), F32)] * 4,
        compiler_params=_params(("arbitrary", "arbitrary"), 32),
        name="hgrn",
    )(hq, hf, hi, hg, loglb, log1mlb, nw_row, head_ones)


def _mem_attn_body(q_ref, k_ref, v_ref, o_ref):
    outs = []
    for h in range(MEM_HEADS):
        sl = slice(h * HEAD_DIM, (h + 1) * HEAD_DIM)
        logits = _dot_nt(q_ref[:, sl], k_ref[0, :, sl]) * ATTN_SCALE
        m = jnp.max(logits, axis=-1, keepdims=True)
        e = jnp.exp(logits - m)
        denom = jnp.sum(e, axis=-1, keepdims=True)
        outs.append(_dot(e.astype(BF16), v_ref[0, :, sl]) / denom)
    o_ref[...] = jnp.concatenate(outs, axis=-1).astype(o_ref.dtype)


def _mem_attn(mq, mk, mv, layer, batch, seq, mem_len):
    tq = min(512, seq)
    nq = seq // tq
    w = MEM_HEADS * HEAD_DIM
    return pl.pallas_call(
        _mem_attn_body,
        grid=(batch, nq),
        in_specs=[pl.BlockSpec((tq, w), lambda b, i: (b * nq + i, 0)),
                  pl.BlockSpec((1, mem_len, w), lambda b, i: (layer, b, 0)),
                  pl.BlockSpec((1, mem_len, w), lambda b, i: (layer, b, 0))],
        out_specs=pl.BlockSpec((tq, w), lambda b, i: (b * nq + i, 0)),
        out_shape=jax.ShapeDtypeStruct((batch * seq, w), BF16),
        compiler_params=_params(("arbitrary", "arbitrary"), 32),
        name="mem_attn",
    )(mq, mk, mv)


def _layer_norm(h, g, b):
    mu = jnp.mean(h, axis=-1, keepdims=True)
    d = h - mu
    var = jnp.mean(d * d, axis=-1, keepdims=True)
    return d * lax.rsqrt(var + LN_EPS) * g + b


def _route_rows(sel, scores):
    def row(a, r):
        return a[r:r + 1, :]

    best = None
    for g in range(N_GROUPS):
        a = [row(sel, 4 * g + i) for i in range(4)]
        gs = a[0] + a[1]
        for i, j in ((0, 2), (0, 3), (1, 2), (1, 3), (2, 3)):
            gs = jnp.maximum(gs, a[i] + a[j])
        if best is None:
            best, best_score = jnp.zeros_like(gs, dtype=I32), gs
        else:
            better = gs > best_score
            best = jnp.where(better, g, best)
            best_score = jnp.where(better, gs, best_score)

    def pick(arr, i):
        out = row(arr, i)
        for g in range(1, N_GROUPS):
            out = jnp.where(best == g, row(arr, 4 * g + i), out)
        return out

    a = [pick(sel, i) for i in range(4)]
    s = [pick(scores, i) for i in range(4)]
    i1, m1, s1 = jnp.zeros_like(best), a[0], s[0]
    for i in range(1, 4):
        gt = a[i] > m1
        i1 = jnp.where(gt, i, i1)
        m1 = jnp.where(gt, a[i], m1)
        s1 = jnp.where(gt, s[i], s1)
    i2 = jnp.full_like(best, -1)
    m2 = jnp.full_like(m1, -jnp.inf)
    s2 = jnp.zeros_like(s1)
    for i in range(4):
        gt = (i1 != i) & (a[i] > m2)
        i2 = jnp.where(gt, i, i2)
        m2 = jnp.where(gt, a[i], m2)
        s2 = jnp.where(gt, s[i], s2)
    denom = s1 + s2
    w1, w2 = s1 / denom, s2 / denom
    first_lower = i1 < i2
    lo = jnp.where(first_lower, i1, i2)
    hi = jnp.where(first_lower, i2, i1)
    pair = jnp.where(lo == 0, hi - 1, jnp.where(lo == 1, hi + 1, 5))
    cls = best * N_PAIRS + pair
    return cls, jnp.where(first_lower, w1, w2), jnp.where(first_lower, w2, w1)


def _out_proj_body(x_ref, swa_ref, hg_ref, mem_ref, w_ref, g_ref, b_ref, wr_ref, rb_ref, tri_ref,
                   x1_ref, cls_ref, rank_ref, wlo_ref, whi_ref, cnt_ref, carry_ref):
    i = pl.program_id(0)
    t = x_ref.shape[0]

    @pl.when(i == 0)
    def _():
        carry_ref[...] = jnp.zeros_like(carry_ref)

    acc = _dot(swa_ref[...], w_ref[0:512, :]) + _dot(hg_ref[...], w_ref[512:768, :]) \
        + _dot(mem_ref[...], w_ref[768:1024, :])
    x1 = _layer_norm(ALPHA * x_ref[...] + acc, g_ref[...], b_ref[...])
    _store_rows(x1_ref, x1, t)

    logits = _dot_nt(wr_ref[...], x1.astype(BF16))
    scores = jax.nn.sigmoid(logits)
    cls, w_lo, w_hi = _route_rows(scores + rb_ref[...], scores)
    onehot = (lax.broadcasted_iota(I32, (CLASS_ROWS, t), 0) == cls).astype(F32)
    prefix = _dot(onehot.astype(BF16), tri_ref[...])
    carry = carry_ref[...]
    rank = jnp.sum(onehot * (prefix - 1.0 + carry[:, 0:1]), axis=0, keepdims=True)
    new_carry = carry + prefix[:, t - 1:t]
    carry_ref[...] = new_carry
    cnt_ref[...] = new_carry
    cls_ref[0] = cls
    rank_ref[0] = rank.astype(I32)
    wlo_ref[0] = w_lo
    whi_ref[0] = w_hi


def _out_proj(x2, swa_o, hg_o, mem_o, w_bf, g_row, b_row, wr_t, rb_col, tri):
    n = x2.shape[0]
    t = tri.shape[0]
    nt = n // t
    tok = lambda i: (i, 0)
    fixed = lambda i: (0, 0)
    per_tile = pl.BlockSpec((1, 1, t), lambda i: (i, 0, 0))
    return pl.pallas_call(
        _out_proj_body,
        grid=(nt,),
        in_specs=[pl.BlockSpec((t, D_MODEL), tok), pl.BlockSpec((t, 512), tok),
                  pl.BlockSpec((t, 256), tok), pl.BlockSpec((t, 256), tok),
                  pl.BlockSpec((MIX_WIDTH, D_MODEL), fixed),
                  pl.BlockSpec((1, D_MODEL), fixed), pl.BlockSpec((1, D_MODEL), fixed),
                  pl.BlockSpec((N_EXPERTS, D_MODEL), fixed), pl.BlockSpec((N_EXPERTS, 1), fixed),
                  pl.BlockSpec((t, t), fixed)],
        out_specs=[pl.BlockSpec((t * ROW_TILES, LANES), tok), per_tile, per_tile, per_tile, per_tile,
                   pl.BlockSpec((CLASS_ROWS, LANES), fixed)],
        out_shape=[jax.ShapeDtypeStruct((n * ROW_TILES, LANES), F32),
                   jax.ShapeDtypeStruct((nt, 1, t), I32), jax.ShapeDtypeStruct((nt, 1, t), I32),
                   jax.ShapeDtypeStruct((nt, 1, t), F32), jax.ShapeDtypeStruct((nt, 1, t), F32),
                   jax.ShapeDtypeStruct((CLASS_ROWS, LANES), F32)],
        scratch_shapes=[pltpu.VMEM((CLASS_ROWS, LANES), F32)],
        compiler_params=_params(("arbitrary",), 48),
        name="out_proj_ln_route",
    )(x2, swa_o, hg_o, mem_o, w_bf, g_row, b_row, wr_t, rb_col, tri)


def _row_copy(src_ref, dst_ref, sem, src_row, dst_row):
    return pltpu.make_async_copy(
        src_ref.at[pl.ds(pl.multiple_of(src_row * ROW_TILES, ROW_TILES), ROW_TILES), :],
        dst_ref.at[pl.ds(pl.multiple_of(dst_row * ROW_TILES, ROW_TILES), ROW_TILES), :], sem)


def _permute_body(idx_ref, src_ref, dst_ref, sem, *, rows_per_step):
    base = pl.program_id(0) * rows_per_step

    def issue(j, carry):
        _row_copy(src_ref, dst_ref, sem, idx_ref[base + j], base + j).start()
        return carry

    lax.fori_loop(0, rows_per_step, issue, 0, unroll=8)

    def drain(j, carry):
        _row_copy(src_ref, dst_ref, sem, 0, base + j).wait()
        return carry

    lax.fori_loop(0, rows_per_step, drain, 0, unroll=8)


def _permute_rows(src_z, idx):
    n_out = idx.shape[0]
    step = math.gcd(PERMUTE_ROWS, n_out)
    return pl.pallas_call(
        functools.partial(_permute_body, rows_per_step=step),
        grid_spec=pltpu.PrefetchScalarGridSpec(
            num_scalar_prefetch=1,
            grid=(n_out // step,),
            in_specs=[pl.BlockSpec(memory_space=pl.ANY)],
            out_specs=pl.BlockSpec(memory_space=pl.ANY),
            scratch_shapes=[pltpu.SemaphoreType.DMA(())]),
        out_shape=jax.ShapeDtypeStruct((n_out * ROW_TILES, LANES), src_z.dtype),
        compiler_params=_params(("arbitrary",), 16),
        name="permute_rows",
    )(idx, src_z)


def _expert_body(elo_ref, ehi_ref, nblk_ref, xs_ref, wlo_ref, whi_ref,
                 g_lo, u_lo, d_lo, g_hi, u_hi, d_hi, ys_ref):
    b = pl.program_id(0)

    @pl.when(b < nblk_ref[0])
    def _():
        x = _load_rows(xs_ref, MOE_BLK).astype(BF16)

        def ffn(gw, uw, dw, w_col):
            g = _dot(x, gw[0, 0])
            u = _dot(x, uw[0, 0])
            h = (g * jax.nn.sigmoid(g)) * u * w_col
            return _dot(h.astype(BF16), dw[0, 0])

        y = ffn(g_lo, u_lo, d_lo, wlo_ref[...]) + ffn(g_hi, u_hi, d_hi, whi_ref[...])
        _store_rows(ys_ref, y, MOE_BLK)

    @pl.when(b >= nblk_ref[0])
    def _():
        ys_ref[...] = jnp.zeros_like(ys_ref)


def _experts(xs_z, slot_wlo, slot_whi, blk_elo, blk_ehi, n_blk, wg, wu, wd, layer):
    n_blocks = blk_elo.shape[0]

    def data_map(b, elo, ehi, nb):
        return (jnp.minimum(b, nb[0] - 1), 0)

    def w_lo_map(b, elo, ehi, nb):
        return (layer, elo[b], 0, 0)

    def w_hi_map(b, elo, ehi, nb):
        return (layer, ehi[b], 0, 0)

    up_spec = lambda m: pl.BlockSpec((1, 1, D_MODEL, D_EXPERT), m)
    down_spec = lambda m: pl.BlockSpec((1, 1, D_EXPERT, D_MODEL), m)
    return pl.pallas_call(
        _expert_body,
        grid_spec=pltpu.PrefetchScalarGridSpec(
            num_scalar_prefetch=3,
            grid=(n_blocks,),
            in_specs=[pl.BlockSpec((MOE_BLK * ROW_TILES, LANES), data_map),
                      pl.BlockSpec((MOE_BLK, 1), data_map), pl.BlockSpec((MOE_BLK, 1), data_map),
                      up_spec(w_lo_map), up_spec(w_lo_map), down_spec(w_lo_map),
                      up_spec(w_hi_map), up_spec(w_hi_map), down_spec(w_hi_map)],
            out_specs=pl.BlockSpec((MOE_BLK * ROW_TILES, LANES), lambda b, elo, ehi, nb: (b, 0))),
        out_shape=jax.ShapeDtypeStruct((n_blocks * MOE_BLK * ROW_TILES, LANES), F32),
        compiler_params=_params(("arbitrary",), 48),
        name="experts",
    )(blk_elo, blk_ehi, n_blk, xs_z, slot_wlo, slot_whi, wg, wu, wd, wg, wu, wd)


def _ln2_body(x1_ref, y_ref, g_ref, b_ref, o_ref):
    t = o_ref.shape[0]
    h = ALPHA * _load_rows(x1_ref, t) + _load_rows(y_ref, t)
    o_ref[...] = _layer_norm(h, g_ref[...], b_ref[...])


def _ln2(x1_z, y_z, g_row, b_row):
    n = x1_z.shape[0] // ROW_TILES
    t = min(512, n)
    tok = lambda i: (i, 0)
    fixed = lambda i: (0, 0)
    return pl.pallas_call(
        _ln2_body,
        grid=(n // t,),
        in_specs=[pl.BlockSpec((t * ROW_TILES, LANES), tok), pl.BlockSpec((t * ROW_TILES, LANES), tok),
                  pl.BlockSpec((1, D_MODEL), fixed), pl.BlockSpec((1, D_MODEL), fixed)],
        out_specs=pl.BlockSpec((t, D_MODEL), tok),
        out_shape=jax.ShapeDtypeStruct((n, D_MODEL), F32),
        compiler_params=_params(("arbitrary",), 32),
        name="ln2",
    )(x1_z, y_z, g_row, b_row)


def _t5_bucket(dist):
    max_exact = N_BUCKETS // 2
    d = jnp.maximum(dist, 0)
    large = max_exact + (jnp.log(jnp.maximum(d, 1).astype(F32) / max_exact)
                         / math.log(MAX_DISTANCE / max_exact) * (N_BUCKETS - max_exact)).astype(I32)
    large = jnp.minimum(large, N_BUCKETS - 1)
    return jnp.where(d < max_exact, d, large)


def _banded_bias(rel_bias):
    i = jnp.arange(ATTN_BLOCK)[:, None]
    j = jnp.arange(2 * ATTN_BLOCK)[None, :]
    dist = i + ATTN_BLOCK - j
    bias = jnp.transpose(rel_bias.astype(F32)[_t5_bucket(dist)], (2, 0, 1))
    in_window = (dist >= 0) & (dist < WINDOW)
    return jnp.where(in_window[None], bias, -jnp.inf)


def _moe_schedule(cls, rank, counts, w_lo, w_hi, n_tok):
    n_blocks = n_tok // MOE_BLK + N_CLASSES
    cap = n_blocks * MOE_BLK
    padded = (counts + MOE_BLK - 1) // MOE_BLK * MOE_BLK
    pend = jnp.cumsum(padded)
    dest = (pend - padded)[cls] + rank
    slot_tok = jnp.zeros((cap,), I32).at[dest].set(jnp.arange(n_tok, dtype=I32))
    slot_wlo = jnp.zeros((cap,), F32).at[dest].set(w_lo)
    slot_whi = jnp.zeros((cap,), F32).at[dest].set(w_hi)
    blk_cls = jnp.minimum(
        jnp.searchsorted(pend, jnp.arange(n_blocks, dtype=I32) * MOE_BLK, side='right'), N_CLASSES - 1)
    n_blk = (pend[-1] // MOE_BLK).astype(I32).reshape(1)
    return (dest.astype(I32), slot_tok, slot_wlo.reshape(cap, 1), slot_whi.reshape(cap, 1),
            jnp.asarray(_CLASS_ELO)[blk_cls], jnp.asarray(_CLASS_EHI)[blk_cls], n_blk)


def kernel(x, mem, w_in, b_in, w_mem_kv, attn_sinks, rel_bias, hgrn_lb_logits, hgrn_norm, w_out,
           ln1_g, ln1_b, w_router, router_bias, w_gate, w_up, w_down, ln2_g, ln2_b):
    batch, seq, _ = x.shape
    mem_len = mem.shape[1]
    n_tok = batch * seq
    route_tile = min(512, n_tok)

    bias = _banded_bias(rel_bias)
    lb = jnp.cumsum(jax.nn.softmax(hgrn_lb_logits.astype(F32), axis=0), axis=0)
    lb = lb - lb[0:1]
    log_lb = jnp.log(lb)
    log_1m_lb = jnp.log1p(-lb)
    head_ones = (jnp.arange(HG_WIDTH)[:, None] // HG_DV == jnp.arange(HG_WIDTH)[None, :] // HG_DV).astype(BF16)
    tri = (jnp.arange(route_tile)[:, None] <= jnp.arange(route_tile)[None, :]).astype(BF16)
    wr_t = jnp.transpose(w_router).astype(BF16)
    rb_col = router_bias.astype(F32).reshape(N_EXPERTS, 1)
    w_in_bf = w_in.astype(BF16)
    w_out_bf = w_out.astype(BF16)
    wg_bf, wu_bf, wd_bf = w_gate.astype(BF16), w_up.astype(BF16), w_down.astype(BF16)

    mk, mv = _mem_kv(mem.reshape(batch * mem_len, D_MODEL), w_mem_kv)

    x2 = x.reshape(n_tok, D_MODEL)
    for l in range(DEPTH):
        sq, sk, sv, hq, hf, hi, hg, mq = _in_proj(x2, w_in_bf[l], b_in[l].reshape(1, IN_WIDTH))
        swa_o = _swa(sq, sk, sv, attn_sinks[l].astype(F32), bias, batch, seq)
        hg_o = _hgrn(hq, hf, hi, hg, log_lb[l].reshape(1, HG_WIDTH), log_1m_lb[l].reshape(1, HG_WIDTH),
                     jnp.tile(hgrn_norm[l].astype(F32), HG_HEADS).reshape(1, HG_WIDTH), head_ones, batch, seq)
        mem_o = _mem_attn(mq, mk, mv, l, batch, seq, mem_len)
        x1_z, cls, rank, w_lo, w_hi, counts = _out_proj(
            x2, swa_o, hg_o, mem_o, w_out_bf[l], ln1_g[l].reshape(1, D_MODEL), ln1_b[l].reshape(1, D_MODEL),
            wr_t, rb_col, tri)
        dest, slot_tok, slot_wlo, slot_whi, blk_elo, blk_ehi, n_blk = _moe_schedule(
            cls.reshape(n_tok), rank.reshape(n_tok), counts[:N_CLASSES, 0].astype(I32),
            w_lo.reshape(n_tok), w_hi.reshape(n_tok), n_tok)
        xs_z = _permute_rows(x1_z, slot_tok)
        ys_z = _experts(xs_z, slot_wlo, slot_whi, blk_elo, blk_ehi, n_blk, wg_bf, wu_bf, wd_bf, l)
        y_z = _permute_rows(ys_z, dest)
        x2 = _ln2(x1_z, y_z, ln2_g[l].reshape(1, D_MODEL), ln2_b[l].reshape(1, D_MODEL))
    return x2.reshape(batch, seq, D_MODEL)
```

```python
import functools
import math

import numpy as np
import jax
import jax.numpy as jnp
from jax import lax
from jax.experimental import pallas as pl
from jax.experimental.pallas import tpu as pltpu

F32 = jnp.float32
BF16 = jnp.bfloat16
I32 = jnp.int32

D_MODEL = 1024
DEPTH = 4
HEAD_DIM = 64
SWA_HEADS = 8
SWA_KV_HEADS = 2
SWA_GROUP = SWA_HEADS // SWA_KV_HEADS
WINDOW = 128
ATTN_BLOCK = 128
HG_HEADS = 4
HG_DK = 64
HG_DV = 64
HG_WIDTH = HG_HEADS * HG_DK
MEM_HEADS = 4
N_BUCKETS = 32
MAX_DISTANCE = 128
N_EXPERTS = 16
N_GROUPS = 4
EXPERTS_PER_GROUP = 4
D_EXPERT = 512
LN_EPS = 1e-5
RMS_EPS = 1e-6
ALPHA = (2 * DEPTH) ** 0.25
SPLITS = (512, 128, 128, 256, 256, 256, 256, 256)
IN_WIDTH = sum(SPLITS)
MIX_WIDTH = 1024
ATTN_SCALE = HEAD_DIM ** -0.5

SUBLANES = 8
LANES = 128
ROW_TILES = D_MODEL // LANES

HG_CHUNK = 128
HG_SUB = 32
HG_SAFE_DECAY = 80.0
N_PAIRS = 6
N_CLASSES = N_GROUPS * N_PAIRS
CLASS_ROWS = 32
MOE_BLK = 256


def _params(semantics, vmem_mib):
    return pltpu.CompilerParams(dimension_semantics=semantics, vmem_limit_bytes=vmem_mib * 1024 * 1024)


def _dot(a, b):
    return jnp.dot(a, b, preferred_element_type=F32)


def _dot_nt(a, b):
    return lax.dot_general(a, b, (((1,), (1,)), ((), ())), preferred_element_type=F32)


def _dot_tn(a, b):
    return lax.dot_general(a, b, (((0,), (0,)), ((), ())), preferred_element_type=F32)


def _load_rows(z_ref, n_rows):
    return jnp.concatenate(
        [z_ref[pl.ds(cb, n_rows, stride=ROW_TILES), :] for cb in range(ROW_TILES)], axis=1)


def _store_rows(z_ref, val, n_rows):
    for cb in range(ROW_TILES):
        z_ref[pl.ds(cb, n_rows, stride=ROW_TILES), :] = val[:, cb * LANES:(cb + 1) * LANES]


def _in_proj_body(x_ref, w_ref, b_ref, *out_refs):
    acc = _dot(x_ref[...].astype(BF16), w_ref[...]) + b_ref[...]
    off = 0
    for ref, width in zip(out_refs, SPLITS):
        ref[...] = acc[:, off:off + width].astype(ref.dtype)
        off += width


def _in_proj(x2, w_bf, b_row):
    n = x2.shape[0]
    tm = min(512, n)
    dtypes = (BF16, BF16, BF16, F32, F32, F32, F32, BF16)
    return pl.pallas_call(
        _in_proj_body,
        grid=(n // tm,),
        in_specs=[pl.BlockSpec((tm, D_MODEL), lambda i: (i, 0)),
                  pl.BlockSpec((D_MODEL, IN_WIDTH), lambda i: (0, 0)),
                  pl.BlockSpec((1, IN_WIDTH), lambda i: (0, 0))],
        out_specs=[pl.BlockSpec((tm, w), lambda i: (i, 0)) for w in SPLITS],
        out_shape=[jax.ShapeDtypeStruct((n, w), dt) for w, dt in zip(SPLITS, dtypes)],
        compiler_params=_params(("arbitrary",), 48),
        name="in_proj",
    )(x2, w_bf, b_row)


def _mem_kv_body(mem_ref, w_ref, k_ref, v_ref):
    acc = _dot(mem_ref[...].astype(BF16), w_ref[0].astype(BF16))
    half = MEM_HEADS * HEAD_DIM
    k_ref[0] = acc[:, :half].astype(BF16)
    v_ref[0] = acc[:, half:].astype(BF16)


def _mem_kv(mem2, w_mem_kv):
    rows = mem2.shape[0]
    tm = min(512, rows)
    half = MEM_HEADS * HEAD_DIM
    return pl.pallas_call(
        _mem_kv_body,
        grid=(DEPTH, rows // tm),
        in_specs=[pl.BlockSpec((tm, D_MODEL), lambda l, i: (i, 0)),
                  pl.BlockSpec((1, D_MODEL, 2 * half), lambda l, i: (l, 0, 0))],
        out_specs=[pl.BlockSpec((1, tm, half), lambda l, i: (l, i, 0)),
                   pl.BlockSpec((1, tm, half), lambda l, i: (l, i, 0))],
        out_shape=[jax.ShapeDtypeStruct((DEPTH, rows, half), BF16)] * 2,
        compiler_params=_params(("arbitrary", "arbitrary"), 32),
        name="mem_kv",
    )(mem2, w_mem_kv)


def _swa_body(sink_ref, q_ref, kp_ref, ko_ref, vp_ref, vo_ref, bias_ref, o_ref):
    n = pl.program_id(1)
    kk = jnp.concatenate([kp_ref[...], ko_ref[...]], axis=0)
    vv = jnp.concatenate([vp_ref[...], vo_ref[...]], axis=0)
    col = lax.broadcasted_iota(I32, (1, 2 * ATTN_BLOCK), 1)
    prev_pad = jnp.where(n > 0, 0.0, -jnp.inf)
    pad = jnp.where(col >= ATTN_BLOCK, 0.0, prev_pad)
    outs = []
    for h in range(SWA_HEADS):
        g = h // SWA_GROUP
        q_h = q_ref[:, h * HEAD_DIM:(h + 1) * HEAD_DIM]
        k_g = kk[:, g * HEAD_DIM:(g + 1) * HEAD_DIM]
        v_g = vv[:, g * HEAD_DIM:(g + 1) * HEAD_DIM]
        logits = _dot_nt(q_h, k_g) * ATTN_SCALE + bias_ref[h] + pad
        sink = sink_ref[h]
        m = jnp.maximum(jnp.max(logits, axis=-1, keepdims=True), sink)
        e = jnp.exp(logits - m)
        denom = jnp.sum(e, axis=-1, keepdims=True) + jnp.exp(sink - m)
        outs.append(_dot(e.astype(BF16), v_g) / denom)
    o_ref[...] = jnp.concatenate(outs, axis=-1).astype(o_ref.dtype)


def _swa(sq, sk, sv, sinks, bias, batch, seq):
    nb = seq // ATTN_BLOCK
    own = lambda b, n: (b * nb + n, 0)
    prev = lambda b, n: (b * nb + jnp.maximum(n - 1, 0), 0)
    kvw = SWA_KV_HEADS * HEAD_DIM
    qw = SWA_HEADS * HEAD_DIM
    return pl.pallas_call(
        _swa_body,
        grid=(batch, nb),
        in_specs=[pl.BlockSpec(memory_space=pltpu.SMEM),
                  pl.BlockSpec((ATTN_BLOCK, qw), own),
                  pl.BlockSpec((ATTN_BLOCK, kvw), prev),
                  pl.BlockSpec((ATTN_BLOCK, kvw), own),
                  pl.BlockSpec((ATTN_BLOCK, kvw), prev),
                  pl.BlockSpec((ATTN_BLOCK, kvw), own),
                  pl.BlockSpec((SWA_HEADS, ATTN_BLOCK, 2 * ATTN_BLOCK), lambda b, n: (0, 0, 0))],
        out_specs=pl.BlockSpec((ATTN_BLOCK, qw), own),
        out_shape=jax.ShapeDtypeStruct((batch * seq, qw), BF16),
        compiler_params=_params(("arbitrary", "arbitrary"), 32),
        name="swa",
    )(sinks, sq, sk, sk, sv, sv, bias)


def _cumsum_rows(x):
    n = x.shape[0]
    row = lax.broadcasted_iota(I32, (n, 1), 0)
    s = 1
    while s < n:
        x = x + jnp.where(row >= s, pltpu.roll(x, s, 0), 0.0)
        s *= 2
    return x


def _rows_from_blocks(vals, width):
    return jnp.concatenate([jnp.broadcast_to(v, (HG_SUB, width)) for v in vals], axis=0)


def _hgrn_body(hq_ref, hf_ref, hi_ref, hg_ref, loglb_ref, log1mlb_ref, nw_ref, bo_ref, o_ref,
               st_ref, a_scr, q_scr, k_scr, oi_scr):
    c = pl.program_id(1)
    n_sub = HG_CHUNK // HG_SUB
    width = HG_WIDTH

    @pl.when(c == 0)
    def _():
        st_ref[...] = jnp.zeros_like(st_ref)

    z = hf_ref[...]
    log_sig = jnp.minimum(z, 0.0) - jnp.log1p(jnp.exp(-jnp.abs(z)))
    t_a = loglb_ref[...]
    t_b = log1mlb_ref[...] + log_sig
    log_f = jnp.maximum(t_a, t_b) + jnp.log1p(jnp.exp(-jnp.abs(t_a - t_b)))
    kk = jnp.exp(t_b - z)
    hq = hq_ref[...]
    qq = hq * jax.nn.sigmoid(hq) * (HG_DK ** -0.5)
    vv = hi_ref[...]

    a_cum = _cumsum_rows(log_f)
    ends = [a_cum[HG_SUB * j + HG_SUB - 1:HG_SUB * j + HG_SUB, :] for j in range(n_sub)]
    zero_row = jnp.zeros((1, width), F32)
    starts = [zero_row] + ends[:-1]
    a_loc = a_cum - _rows_from_blocks(starts, width)
    e_loc = _rows_from_blocks(ends, width) - a_cum
    worst = starts[0] - ends[0]
    for j in range(1, n_sub):
        worst = jnp.maximum(worst, starts[j] - ends[j])
    unsafe = jnp.max(worst) > HG_SAFE_DECAY

    q_sub = qq * jnp.exp(a_loc)
    k_sub = kk * jnp.exp(-a_loc)
    k_end = kk * jnp.exp(e_loc)
    one_row = jnp.ones((1, width), F32)
    q_dec = q_sub * _rows_from_blocks([jnp.exp(s) for s in starts], width)
    k_dec = k_end * _rows_from_blocks([jnp.exp(ends[-1] - e) for e in ends], width)
    q_top = q_sub * _rows_from_blocks(
        [zero_row, zero_row, one_row, jnp.exp(ends[2] - ends[1])], width)
    k_top = k_end * _rows_from_blocks(
        [jnp.exp(ends[1] - ends[0]), one_row, zero_row, zero_row], width)

    row = lax.broadcasted_iota(I32, (HG_CHUNK, 1), 0)
    sub_of_row = row // HG_SUB
    t_idx = lax.broadcasted_iota(I32, (HG_CHUNK, HG_CHUNK), 0)
    s_idx = lax.broadcasted_iota(I32, (HG_CHUNK, HG_CHUNK), 1)
    m_sub = ((t_idx // HG_SUB) == (s_idx // HG_SUB)) & (t_idx >= s_idx)
    m_sub2 = jnp.concatenate([m_sub, m_sub], axis=0)
    lane = lax.broadcasted_iota(I32, (1, LANES), 1)
    lo = lane < HG_DK
    bd = (lax.broadcasted_iota(I32, (LANES, LANES), 0) // HG_DV) == \
         (lax.broadcasted_iota(I32, (LANES, LANES), 1) // HG_DK)
    dec_row = jnp.exp(ends[-1])

    def heads_on_rows(x):
        return jnp.concatenate([jnp.where(lo, x, 0.0), jnp.where(lo, 0.0, x)], axis=0)

    o_inter = []
    for p in range(width // LANES):
        sl = slice(p * LANES, (p + 1) * LANES)
        qs, ks, ke = q_sub[:, sl], k_sub[:, sl], k_end[:, sl]
        p_sub = _dot_nt(heads_on_rows(qs).astype(BF16), ks.astype(BF16))
        q_x = jnp.concatenate([jnp.where(sub_of_row == 1, qs, 0.0),
                               jnp.where(sub_of_row == 3, qs, 0.0), q_top[:, sl]], axis=1)
        k_x = jnp.concatenate([jnp.where(sub_of_row == 0, ke, 0.0),
                               jnp.where(sub_of_row == 2, ke, 0.0), k_top[:, sl]], axis=1)
        q_x2 = jnp.concatenate([jnp.where(jnp.tile(lo, (1, 3)), q_x, 0.0),
                                jnp.where(jnp.tile(lo, (1, 3)), 0.0, q_x)], axis=0)
        p_x = _dot_nt(q_x2.astype(BF16), k_x.astype(BF16))
        p_all = jnp.where(m_sub2, p_sub, 0.0) + p_x
        p_cat = jnp.concatenate([p_all[:HG_CHUNK], p_all[HG_CHUNK:]], axis=1)
        v_p = vv[:, sl]
        oi_scr[:, sl] = _dot(p_cat.astype(BF16), heads_on_rows(v_p).astype(BF16))
        st = st_ref[p]
        o_inter.append(_dot_nt(q_dec[:, sl].astype(BF16), st.astype(BF16)))
        upd = _dot_tn(v_p.astype(BF16), k_dec[:, sl].astype(BF16))
        st_ref[p] = st * dec_row[:, sl] + jnp.where(bd, upd, 0.0)
    o_inter = jnp.concatenate(o_inter, axis=1)

    @pl.when(unsafe)
    def _():
        a_scr[...] = a_cum
        q_scr[...] = qq
        k_scr[...] = kk

        def body(t, carry):
            d = a_scr[pl.ds(t, 1), :] - a_scr[...]
            w = jnp.exp(jnp.where(row <= t, d, -jnp.inf))
            prod = (q_scr[pl.ds(t, 1), :] * k_scr[...]) * w
            e = _dot(prod.astype(BF16), bo_ref[...])
            oi_scr[pl.ds(t, 1), :] = jnp.sum(e * hi_ref[...], axis=0, keepdims=True)
            return carry

        lax.fori_loop(0, HG_CHUNK, body, 0)

    o = oi_scr[...] + o_inter
    sq = o * o
    sq_hi = sq.astype(BF16)
    sq_lo = (sq - sq_hi.astype(F32)).astype(BF16)
    ms = (_dot(sq_hi, bo_ref[...]) + _dot(sq_lo, bo_ref[...])) * (1.0 / HG_DV)
    gate = hg_ref[...]
    o = o * lax.rsqrt(ms + RMS_EPS) * nw_ref[...] * (gate * jax.nn.sigmoid(gate))
    o_ref[...] = o.astype(o_ref.dtype)


def _hgrn(hq, hf, hi, hg, loglb, log1mlb, nw_row, head_ones, batch, seq):
    nc = seq // HG_CHUNK
    blk = lambda b, c: (b * nc + c, 0)
    row = lambda b, c: (0, 0)
    w = HG_WIDTH
    return pl.pallas_call(
        _hgrn_body,
        grid=(batch, nc),
        in_specs=[pl.BlockSpec((HG_CHUNK, w), blk)] * 4 + [pl.BlockSpec((1, w), row)] * 3
                 + [pl.BlockSpec((w, w), row)],
        out_specs=pl.BlockSpec((HG_CHUNK, w), blk),
        out_shape=jax.ShapeDtypeStruct((batch * seq, w), BF16),
        scratch_shapes=[pltpu.VMEM((w // LANES, LANES, LANES), F32)]
                       + [pltpu.VMEM((HG_CHUNK, w), F32)] * 4,
        compiler_params=_params(("arbitrary", "arbitrary"), 32),
        name="hgrn",
    )(hq, hf, hi, hg, loglb, log1mlb, nw_row, head_ones)


def _mem_attn_body(q_ref, k_ref, v_ref, o_ref):
    outs = []
    for h in range(MEM_HEADS):
        sl = slice(h * HEAD_DIM, (h + 1) * HEAD_DIM)
        logits = _dot_nt(q_ref[:, sl], k_ref[0, :, sl]) * ATTN_SCALE
        m = jnp.max(logits, axis=-1, keepdims=True)
        e = jnp.exp(logits - m)
        denom = jnp.sum(e, axis=-1, keepdims=True)
        outs.append(_dot(e.astype(BF16), v_ref[0, :, sl]) / denom)
    o_ref[...] = jnp.concatenate(outs, axis=-1).astype(o_ref.dtype)


def _mem_attn(mq, mk, mv, layer, batch, seq, mem_len):
    tq = min(512, seq)
    nq = seq // tq
    w = MEM_HEADS * HEAD_DIM
    return pl.pallas_call(
        _mem_attn_body,
        grid=(batch, nq),
        in_specs=[pl.BlockSpec((tq, w), lambda b, i: (b * nq + i, 0)),
                  pl.BlockSpec((1, mem_len, w), lambda b, i: (layer, b, 0)),
                  pl.BlockSpec((1, mem_len, w), lambda b, i: (layer, b, 0))],
        out_specs=pl.BlockSpec((tq, w), lambda b, i: (b * nq + i, 0)),
        out_shape=jax.ShapeDtypeStruct((batch * seq, w), BF16),
        compiler_params=_params(("arbitrary", "arbitrary"), 32),
        name="mem_attn",
    )(mq, mk, mv)


def _layer_norm(h, g, b):
    mu = jnp.mean(h, axis=-1, keepdims=True)
    d = h - mu
    var = jnp.mean(d * d, axis=-1, keepdims=True)
    return d * lax.rsqrt(var + LN_EPS) * g + b


def _route_rows(sel, scores):
    def row(a, r):
        return a[r:r + 1, :]

    best = None
    for g in range(N_GROUPS):
        a = [row(sel, 4 * g + i) for i in range(4)]
        gs = a[0] + a[1]
        for i, j in ((0, 2), (0, 3), (1, 2), (1, 3), (2, 3)):
            gs = jnp.maximum(gs, a[i] + a[j])
        if best is None:
            best, best_score = jnp.zeros_like(gs, dtype=I32), gs
        else:
            better = gs > best_score
            best = jnp.where(better, g, best)
            best_score = jnp.where(better, gs, best_score)

    def pick(arr, i):
        out = row(arr, i)
        for g in range(1, N_GROUPS):
            out = jnp.where(best == g, row(arr, 4 * g + i), out)
        return out

    a = [pick(sel, i) for i in range(4)]
    s = [pick(scores, i) for i in range(4)]
    i1, m1, s1 = jnp.zeros_like(best), a[0], s[0]
    for i in range(1, 4):
        gt = a[i] > m1
        i1 = jnp.where(gt, i, i1)
        m1 = jnp.where(gt, a[i], m1)
        s1 = jnp.where(gt, s[i], s1)
    i2 = jnp.full_like(best, -1)
    m2 = jnp.full_like(m1, -jnp.inf)
    s2 = jnp.zeros_like(s1)
    for i in range(4):
        gt = (i1 != i) & (a[i] > m2)
        i2 = jnp.where(gt, i, i2)
        m2 = jnp.where(gt, a[i], m2)
        s2 = jnp.where(gt, s[i], s2)
    denom = s1 + s2
    w1, w2 = s1 / denom, s2 / denom
    first_lower = i1 < i2
    lo = jnp.where(first_lower, i1, i2)
    hi = jnp.where(first_lower, i2, i1)
    pair = jnp.where(lo == 0, hi - 1, jnp.where(lo == 1, hi + 1, 5))
    cls = best * N_PAIRS + pair
    return cls, jnp.where(first_lower, w1, w2), jnp.where(first_lower, w2, w1)


def _out_proj_body(x_ref, swa_ref, hg_ref, mem_ref, w_ref, g_ref, b_ref, wr_ref, rb_ref, tri_ref,
                   x1_ref, cls_ref, rank_ref, wlo_ref, whi_ref, cnt_ref, carry_ref):
    i = pl.program_id(0)
    t = x_ref.shape[0]

    @pl.when(i == 0)
    def _():
        carry_ref[...] = jnp.zeros_like(carry_ref)

    acc = _dot(swa_ref[...], w_ref[0:512, :]) + _dot(hg_ref[...], w_ref[512:768, :]) \
        + _dot(mem_ref[...], w_ref[768:1024, :])
    x1 = _layer_norm(ALPHA * x_ref[...] + acc, g_ref[...], b_ref[...])
    _store_rows(x1_ref, x1, t)

    logits = _dot_nt(wr_ref[...], x1.astype(BF16))
    scores = jax.nn.sigmoid(logits)
    cls, w_lo, w_hi = _route_rows(scores + rb_ref[...], scores)
    onehot = (lax.broadcasted_iota(I32, (CLASS_ROWS, t), 0) == cls).astype(F32)
    prefix = _dot(onehot.astype(BF16), tri_ref[...])
    carry = carry_ref[...]
    rank = jnp.sum(onehot * (prefix - 1.0 + carry[:, 0:1]), axis=0, keepdims=True)
    new_carry = carry + prefix[:, t - 1:t]
    carry_ref[...] = new_carry
    cnt_ref[...] = new_carry
    cls_ref[0] = cls
    rank_ref[0] = rank.astype(I32)
    wlo_ref[0] = w_lo
    whi_ref[0] = w_hi


def _out_proj(x2, swa_o, hg_o, mem_o, w_bf, g_row, b_row, wr_t, rb_col, tri):
    n = x2.shape[0]
    t = tri.shape[0]
    nt = n // t
    tok = lambda i: (i, 0)
    fixed = lambda i: (0, 0)
    per_tile = pl.BlockSpec((1, 1, t), lambda i: (i, 0, 0))
    return pl.pallas_call(
        _out_proj_body,
        grid=(nt,),
        in_specs=[pl.BlockSpec((t, D_MODEL), tok), pl.BlockSpec((t, 512), tok),
                  pl.BlockSpec((t, 256), tok), pl.BlockSpec((t, 256), tok),
                  pl.BlockSpec((MIX_WIDTH, D_MODEL), fixed),
                  pl.BlockSpec((1, D_MODEL), fixed), pl.BlockSpec((1, D_MODEL), fixed),
                  pl.BlockSpec((N_EXPERTS, D_MODEL), fixed), pl.BlockSpec((N_EXPERTS, 1), fixed),
                  pl.BlockSpec((t, t), fixed)],
        out_specs=[pl.BlockSpec((t * ROW_TILES, LANES), tok), per_tile, per_tile, per_tile, per_tile,
                   pl.BlockSpec((CLASS_ROWS, LANES), fixed)],
        out_shape=[jax.ShapeDtypeStruct((n * ROW_TILES, LANES), F32),
                   jax.ShapeDtypeStruct((nt, 1, t), I32), jax.ShapeDtypeStruct((nt, 1, t), I32),
                   jax.ShapeDtypeStruct((nt, 1, t), F32), jax.ShapeDtypeStruct((nt, 1, t), F32),
                   jax.ShapeDtypeStruct((CLASS_ROWS, LANES), F32)],
        scratch_shapes=[pltpu.VMEM((CLASS_ROWS, LANES), F32)],
        compiler_params=_params(("arbitrary",), 48),
        name="out_proj_ln_route",
    )(x2, swa_o, hg_o, mem_o, w_bf, g_row, b_row, wr_t, rb_col, tri)


def _tile_copy(src_ref, dst_ref, sem, src_row, dst_row, tiles):
    n = tiles * SUBLANES
    return pltpu.make_async_copy(
        src_ref.at[pl.ds(pl.multiple_of(src_row * n, n), n), :],
        dst_ref.at[pl.ds(pl.multiple_of(dst_row * n, n), n), :], sem)


def _scatter_body(cstart_ref, cls_ref, rank_ref, x_hbm, dest_ref, xs_hbm,
                  buf, dest_vmem, dest_smem, load_sem, scat_sem, misc_sem, *, n_steps, tile):
    i = pl.program_id(0)
    rows = tile * SUBLANES
    slot = i % 3

    def load(step, into):
        return pltpu.make_async_copy(
            x_hbm.at[pl.ds(pl.multiple_of(step * rows, rows), rows), :], buf.at[into], load_sem.at[into])

    def scatter_done(of):
        return pltpu.make_async_copy(buf.at[of], xs_hbm.at[pl.ds(0, rows), :], scat_sem.at[of])

    @pl.when(i == 0)
    def _():
        load(0, 0).start()

    @pl.when(i + 1 < n_steps)
    def _():
        load(i + 1, (i + 1) % 3).start()

    cls = cls_ref[0]
    dest = rank_ref[0]
    for c in range(N_CLASSES):
        dest = dest + jnp.where(cls == c, cstart_ref[c], 0)
    dest_ref[0] = dest
    dest_vmem[...] = dest
    to_smem = pltpu.make_async_copy(dest_vmem, dest_smem, misc_sem)
    to_smem.start()
    to_smem.wait()

    load(i, slot).wait()

    def issue(j, carry):
        _tile_copy(buf.at[slot], xs_hbm, scat_sem.at[slot], j, dest_smem[0, j], 1).start()
        return carry

    lax.fori_loop(0, tile, issue, 0, unroll=8)

    @pl.when(i >= 1)
    def _():
        scatter_done((i + 2) % 3).wait()

    @pl.when(i == n_steps - 1)
    def _():
        scatter_done(slot).wait()


def _scatter_rows(x1_z, cls, rank, cstart):
    nt, _, tile = cls.shape
    n = nt * tile
    per_tile = pl.BlockSpec((1, 1, tile), lambda i, cs: (i, 0, 0))
    return pl.pallas_call(
        functools.partial(_scatter_body, n_steps=nt, tile=tile),
        grid_spec=pltpu.PrefetchScalarGridSpec(
            num_scalar_prefetch=1,
            grid=(nt,),
            in_specs=[per_tile, per_tile, pl.BlockSpec(memory_space=pl.ANY)],
            out_specs=[per_tile, pl.BlockSpec(memory_space=pl.ANY)],
            scratch_shapes=[pltpu.VMEM((3, tile * SUBLANES, LANES), F32),
                            pltpu.VMEM((1, tile), I32), pltpu.SMEM((1, tile), I32),
                            pltpu.SemaphoreType.DMA((3,)), pltpu.SemaphoreType.DMA((3,)),
                            pltpu.SemaphoreType.DMA(())]),
        out_shape=[jax.ShapeDtypeStruct((nt, 1, tile), I32),
                   jax.ShapeDtypeStruct((n * SUBLANES, LANES), F32)],
        compiler_params=_params(("arbitrary",), 32),
        name="scatter_rows",
    )(cstart, cls, rank, x1_z)


def _expert_body(blk_ref, elo_ref, ehi_ref, start_ref, end_ref, xs_ref,
                 g_lo, u_lo, d_lo, g_hi, u_hi, d_hi, ys_ref):
    i = pl.program_id(0)
    start = start_ref[i]
    end = end_ref[i]
    pair = 2 * ROW_TILES

    @pl.when(end > start)
    def _():
        x = _load_rows(xs_ref, MOE_BLK).astype(BF16)
        row = blk_ref[i] * MOE_BLK + lax.broadcasted_iota(I32, (MOE_BLK, 1), 0)
        mine = (row >= start) & (row < end)
        first = start % MOE_BLK == 0

        def ffn(gw, uw, dw):
            g = _dot(x, gw[0, 0])
            u = _dot(x, uw[0, 0])
            return _dot(((g * jax.nn.sigmoid(g)) * u).astype(BF16), dw[0, 0])

        for half, (gw, uw, dw) in enumerate(((g_lo, u_lo, d_lo), (g_hi, u_hi, d_hi))):
            y = ffn(gw, uw, dw)

            @pl.when(first)
            def _():
                for cb in range(ROW_TILES):
                    ys_ref[pl.ds(half * ROW_TILES + cb, MOE_BLK, stride=pair), :] = \
                        y[:, cb * LANES:(cb + 1) * LANES]

            @pl.when(jnp.logical_not(first))
            def _():
                for cb in range(ROW_TILES):
                    sl = pl.ds(half * ROW_TILES + cb, MOE_BLK, stride=pair)
                    ys_ref[sl, :] = jnp.where(mine, y[:, cb * LANES:(cb + 1) * LANES], ys_ref[sl, :])


def _experts(xs_z, items, wg, wu, wd, layer):
    blk, elo, ehi, start, end = items
    n_items = blk.shape[0]
    n_slots = xs_z.shape[0] // SUBLANES

    def data_map(i, blk, elo, ehi, start, end):
        return (blk[i], 0)

    def w_lo_map(i, blk, elo, ehi, start, end):
        return (layer, elo[i], 0, 0)

    def w_hi_map(i, blk, elo, ehi, start, end):
        return (layer, ehi[i], 0, 0)

    up_spec = lambda m: pl.BlockSpec((1, 1, D_MODEL, D_EXPERT), m)
    down_spec = lambda m: pl.BlockSpec((1, 1, D_EXPERT, D_MODEL), m)
    return pl.pallas_call(
        _expert_body,
        grid_spec=pltpu.PrefetchScalarGridSpec(
            num_scalar_prefetch=5,
            grid=(n_items,),
            in_specs=[pl.BlockSpec((MOE_BLK * SUBLANES, LANES), data_map),
                      up_spec(w_lo_map), up_spec(w_lo_map), down_spec(w_lo_map),
                      up_spec(w_hi_map), up_spec(w_hi_map), down_spec(w_hi_map)],
            out_specs=pl.BlockSpec((MOE_BLK * 2 * SUBLANES, LANES), data_map)),
        out_shape=jax.ShapeDtypeStruct((n_slots * 2 * SUBLANES, LANES), F32),
        compiler_params=_params(("arbitrary",), 48),
        name="experts",
    )(blk, elo, ehi, start, end, xs_z, wg, wu, wd, wg, wu, wd)


def _ln2_body(dest_ref, x1_ref, ys_hbm, wlo_ref, whi_ref, g_ref, b_ref, o_ref, buf, sem, *, n_steps, tile):
    i = pl.program_id(0)
    pair = 2 * ROW_TILES
    slot = i % 2

    def gather(step, into):
        def issue(j, carry):
            _tile_copy(ys_hbm, buf.at[into], sem.at[into], dest_ref[step * tile + j], j, 2).start()
            return carry

        lax.fori_loop(0, tile, issue, 0, unroll=8)

    @pl.when(i == 0)
    def _():
        gather(0, 0)

    @pl.when(i + 1 < n_steps)
    def _():
        gather(i + 1, (i + 1) % 2)

    pltpu.make_async_copy(ys_hbm.at[pl.ds(0, tile * pair), :], buf.at[slot], sem.at[slot]).wait()
    rows = buf.at[slot]
    y_lo = jnp.concatenate([rows[pl.ds(cb, tile, stride=pair), :] for cb in range(ROW_TILES)], axis=1)
    y_hi = jnp.concatenate([rows[pl.ds(ROW_TILES + cb, tile, stride=pair), :] for cb in range(ROW_TILES)], axis=1)
    h = ALPHA * _load_rows(x1_ref, tile) + (wlo_ref[...] * y_lo + whi_ref[...] * y_hi)
    o_ref[...] = _layer_norm(h, g_ref[...], b_ref[...])


def _ln2(x1_z, ys_z, dest, w_lo_col, w_hi_col, g_row, b_row, tile):
    n = x1_z.shape[0] // SUBLANES
    nt = n // tile
    tok = lambda i, d: (i, 0)
    fixed = lambda i, d: (0, 0)
    return pl.pallas_call(
        functools.partial(_ln2_body, n_steps=nt, tile=tile),
        grid_spec=pltpu.PrefetchScalarGridSpec(
            num_scalar_prefetch=1,
            grid=(nt,),
            in_specs=[pl.BlockSpec((tile * SUBLANES, LANES), tok), pl.BlockSpec(memory_space=pl.ANY),
                      pl.BlockSpec((tile, 1), tok), pl.BlockSpec((tile, 1), tok),
                      pl.BlockSpec((1, D_MODEL), fixed), pl.BlockSpec((1, D_MODEL), fixed)],
            out_specs=pl.BlockSpec((tile, D_MODEL), tok),
            scratch_shapes=[pltpu.VMEM((2, tile * 2 * SUBLANES, LANES), F32),
                            pltpu.SemaphoreType.DMA((2,))]),
        out_shape=jax.ShapeDtypeStruct((n, D_MODEL), F32),
        compiler_params=_params(("arbitrary",), 48),
        name="gather_ln2",
    )(dest, x1_z, ys_z, w_lo_col, w_hi_col, g_row, b_row)


def _t5_bucket(dist):
    max_exact = N_BUCKETS // 2
    d = jnp.maximum(dist, 0)
    large = max_exact + (jnp.log(jnp.maximum(d, 1).astype(F32) / max_exact)
                         / math.log(MAX_DISTANCE / max_exact) * (N_BUCKETS - max_exact)).astype(I32)
    large = jnp.minimum(large, N_BUCKETS - 1)
    return jnp.where(d < max_exact, d, large)


def _banded_bias(rel_bias):
    i = jnp.arange(ATTN_BLOCK)[:, None]
    j = jnp.arange(2 * ATTN_BLOCK)[None, :]
    dist = i + ATTN_BLOCK - j
    bias = jnp.transpose(rel_bias.astype(F32)[_t5_bucket(dist)], (2, 0, 1))
    in_window = (dist >= 0) & (dist < WINDOW)
    return jnp.where(in_window[None], bias, -jnp.inf)


def _work_items(counts, n_tok):
    n_blocks = n_tok // MOE_BLK
    cend = jnp.cumsum(counts)
    cstart = cend - counts
    blk_starts = jnp.arange(n_blocks, dtype=I32) * MOE_BLK
    cls_starts = jnp.where(counts > 0, cstart, n_tok)
    start = jnp.sort(jnp.concatenate([blk_starts, cls_starts]))
    end = jnp.concatenate([start[1:], jnp.full((1,), n_tok, I32)])
    blk = jnp.minimum(start, n_tok - 1) // MOE_BLK
    cls = jnp.minimum(jnp.sum((cend[None, :] <= start[:, None]).astype(I32), axis=1), N_CLASSES - 1)
    group, pair = cls // N_PAIRS, cls % N_PAIRS
    lo = (pair >= 3).astype(I32) + (pair >= 5).astype(I32)
    hi = jnp.where(pair < 3, pair + 1, jnp.where(pair < 5, pair - 1, 3))
    return cstart, (blk, group * EXPERTS_PER_GROUP + lo, group * EXPERTS_PER_GROUP + hi, start, end)


def kernel(x, mem, w_in, b_in, w_mem_kv, attn_sinks, rel_bias, hgrn_lb_logits, hgrn_norm, w_out,
           ln1_g, ln1_b, w_router, router_bias, w_gate, w_up, w_down, ln2_g, ln2_b):
    batch, seq, _ = x.shape
    mem_len = mem.shape[1]
    n_tok = batch * seq
    route_tile = min(512, n_tok)

    bias = _banded_bias(rel_bias)
    lb = jnp.cumsum(jax.nn.softmax(hgrn_lb_logits.astype(F32), axis=0), axis=0)
    lb = lb - lb[0:1]
    log_lb = jnp.log(lb)
    log_1m_lb = jnp.log1p(-lb)
    head_ones = (jnp.arange(HG_WIDTH)[:, None] // HG_DV == jnp.arange(HG_WIDTH)[None, :] // HG_DV).astype(BF16)
    tri = (jnp.arange(route_tile)[:, None] <= jnp.arange(route_tile)[None, :]).astype(BF16)
    wr_t = jnp.transpose(w_router).astype(BF16)
    rb_col = router_bias.astype(F32).reshape(N_EXPERTS, 1)
    w_in_bf = w_in.astype(BF16)
    w_out_bf = w_out.astype(BF16)
    wg_bf, wu_bf, wd_bf = w_gate.astype(BF16), w_up.astype(BF16), w_down.astype(BF16)

    mk, mv = _mem_kv(mem.reshape(batch * mem_len, D_MODEL), w_mem_kv)

    x2 = x.reshape(n_tok, D_MODEL)
    for l in range(DEPTH):
        sq, sk, sv, hq, hf, hi, hg, mq = _in_proj(x2, w_in_bf[l], b_in[l].reshape(1, IN_WIDTH))
        swa_o = _swa(sq, sk, sv, attn_sinks[l].astype(F32), bias, batch, seq)
        hg_o = _hgrn(hq, hf, hi, hg, log_lb[l].reshape(1, HG_WIDTH), log_1m_lb[l].reshape(1, HG_WIDTH),
                     jnp.tile(hgrn_norm[l].astype(F32), HG_HEADS).reshape(1, HG_WIDTH), head_ones, batch, seq)
        mem_o = _mem_attn(mq, mk, mv, l, batch, seq, mem_len)
        x1_z, cls, rank, w_lo, w_hi, counts = _out_proj(
            x2, swa_o, hg_o, mem_o, w_out_bf[l], ln1_g[l].reshape(1, D_MODEL), ln1_b[l].reshape(1, D_MODEL),
            wr_t, rb_col, tri)
        cstart, items = _work_items(counts[:N_CLASSES, 0].astype(I32), n_tok)
        dest, xs_z = _scatter_rows(x1_z, cls, rank, cstart)
        ys_z = _experts(xs_z, items, wg_bf, wu_bf, wd_bf, l)
        x2 = _ln2(x1_z, ys_z, dest.reshape(n_tok), w_lo.reshape(n_tok, 1), w_hi.reshape(n_tok, 1),
                  ln2_g[l].reshape(1, D_MODEL), ln2_b[l].reshape(1, D_MODEL), route_tile)
    return x2.reshape(batch, seq, D_MODEL)
```

```python
import functools
import math

import numpy as np
import jax
import jax.numpy as jnp
from jax import lax
from jax.experimental import pallas as pl
from jax.experimental.pallas import tpu as pltpu

F32 = jnp.float32
BF16 = jnp.bfloat16
I32 = jnp.int32

D_MODEL = 1024
DEPTH = 4
HEAD_DIM = 64
SWA_HEADS = 8
SWA_KV_HEADS = 2
SWA_GROUP = SWA_HEADS // SWA_KV_HEADS
WINDOW = 128
ATTN_BLOCK = 128
HG_HEADS = 4
HG_DK = 64
HG_DV = 64
HG_WIDTH = HG_HEADS * HG_DK
MEM_HEADS = 4
N_BUCKETS = 32
MAX_DISTANCE = 128
N_EXPERTS = 16
N_GROUPS = 4
EXPERTS_PER_GROUP = 4
D_EXPERT = 512
LN_EPS = 1e-5
RMS_EPS = 1e-6
ALPHA = (2 * DEPTH) ** 0.25
SPLITS = (512, 128, 128, 256, 256, 256, 256, 256)
IN_WIDTH = sum(SPLITS)
MIX_WIDTH = 1024
ATTN_SCALE = HEAD_DIM ** -0.5

SUBLANES = 8
LANES = 128
ROW_TILES = D_MODEL // LANES

HG_CHUNK = 128
HG_SUB = 32
HG_SAFE_DECAY = 80.0
N_PAIRS = 6
N_CLASSES = N_GROUPS * N_PAIRS
CLASS_ROWS = 32
MOE_BLK = 256


def _params(semantics, vmem_mib):
    return pltpu.CompilerParams(dimension_semantics=semantics, vmem_limit_bytes=vmem_mib * 1024 * 1024)


def _dot(a, b):
    return jnp.dot(a, b, preferred_element_type=F32)


def _dot_nt(a, b):
    return lax.dot_general(a, b, (((1,), (1,)), ((), ())), preferred_element_type=F32)


def _dot_tn(a, b):
    return lax.dot_general(a, b, (((0,), (0,)), ((), ())), preferred_element_type=F32)


def _from_tiles(z, n_rows):
    k = z.shape[0] // n_rows
    return z.reshape(n_rows, k, LANES).reshape(n_rows, k * LANES)


def _to_tiles(val):
    n_rows, width = val.shape
    return val.reshape(n_rows, width // LANES, LANES).reshape(n_rows * (width // LANES), LANES)


def _in_proj_body(x_ref, w_ref, b_ref, *out_refs):
    acc = _dot(x_ref[...].astype(BF16), w_ref[...]) + b_ref[...]
    off = 0
    for ref, width in zip(out_refs, SPLITS):
        ref[...] = acc[:, off:off + width].astype(ref.dtype)
        off += width


def _in_proj(x2, w_bf, b_row):
    n = x2.shape[0]
    tm = min(512, n)
    dtypes = (BF16, BF16, BF16, F32, F32, F32, F32, BF16)
    return pl.pallas_call(
        _in_proj_body,
        grid=(n // tm,),
        in_specs=[pl.BlockSpec((tm, D_MODEL), lambda i: (i, 0)),
                  pl.BlockSpec((D_MODEL, IN_WIDTH), lambda i: (0, 0)),
                  pl.BlockSpec((1, IN_WIDTH), lambda i: (0, 0))],
        out_specs=[pl.BlockSpec((tm, w), lambda i: (i, 0)) for w in SPLITS],
        out_shape=[jax.ShapeDtypeStruct((n, w), dt) for w, dt in zip(SPLITS, dtypes)],
        compiler_params=_params(("arbitrary",), 48),
        name="in_proj",
    )(x2, w_bf, b_row)


def _mem_kv_body(mem_ref, w_ref, k_ref, v_ref):
    acc = _dot(mem_ref[...].astype(BF16), w_ref[0].astype(BF16))
    half = MEM_HEADS * HEAD_DIM
    k_ref[0] = acc[:, :half].astype(BF16)
    v_ref[0] = acc[:, half:].astype(BF16)


def _mem_kv(mem2, w_mem_kv):
    rows = mem2.shape[0]
    tm = min(512, rows)
    half = MEM_HEADS * HEAD_DIM
    return pl.pallas_call(
        _mem_kv_body,
        grid=(DEPTH, rows // tm),
        in_specs=[pl.BlockSpec((tm, D_MODEL), lambda l, i: (i, 0)),
                  pl.BlockSpec((1, D_MODEL, 2 * half), lambda l, i: (l, 0, 0))],
        out_specs=[pl.BlockSpec((1, tm, half), lambda l, i: (l, i, 0)),
                   pl.BlockSpec((1, tm, half), lambda l, i: (l, i, 0))],
        out_shape=[jax.ShapeDtypeStruct((DEPTH, rows, half), BF16)] * 2,
        compiler_params=_params(("arbitrary", "arbitrary"), 32),
        name="mem_kv",
    )(mem2, w_mem_kv)


def _swa_body(sink_ref, q_ref, kp_ref, ko_ref, vp_ref, vo_ref, bias_ref, o_ref):
    n = pl.program_id(1)
    kk = jnp.concatenate([kp_ref[...], ko_ref[...]], axis=0)
    vv = jnp.concatenate([vp_ref[...], vo_ref[...]], axis=0)
    col = lax.broadcasted_iota(I32, (1, 2 * ATTN_BLOCK), 1)
    prev_pad = jnp.where(n > 0, 0.0, -jnp.inf)
    pad = jnp.where(col >= ATTN_BLOCK, 0.0, prev_pad)
    outs = []
    for h in range(SWA_HEADS):
        g = h // SWA_GROUP
        q_h = q_ref[:, h * HEAD_DIM:(h + 1) * HEAD_DIM]
        k_g = kk[:, g * HEAD_DIM:(g + 1) * HEAD_DIM]
        v_g = vv[:, g * HEAD_DIM:(g + 1) * HEAD_DIM]
        logits = _dot_nt(q_h, k_g) * ATTN_SCALE + bias_ref[h] + pad
        sink = sink_ref[h]
        m = jnp.maximum(jnp.max(logits, axis=-1, keepdims=True), sink)
        e = jnp.exp(logits - m)
        denom = jnp.sum(e, axis=-1, keepdims=True) + jnp.exp(sink - m)
        outs.append(_dot(e.astype(BF16), v_g) / denom)
    o_ref[...] = jnp.concatenate(outs, axis=-1).astype(o_ref.dtype)


def _swa(sq, sk, sv, sinks, bias, batch, seq):
    nb = seq // ATTN_BLOCK
    own = lambda b, n: (b * nb + n, 0)
    prev = lambda b, n: (b * nb + jnp.maximum(n - 1, 0), 0)
    kvw = SWA_KV_HEADS * HEAD_DIM
    qw = SWA_HEADS * HEAD_DIM
    return pl.pallas_call(
        _swa_body,
        grid=(batch, nb),
        in_specs=[pl.BlockSpec(memory_space=pltpu.SMEM),
                  pl.BlockSpec((ATTN_BLOCK, qw), own),
                  pl.BlockSpec((ATTN_BLOCK, kvw), prev),
                  pl.BlockSpec((ATTN_BLOCK, kvw), own),
                  pl.BlockSpec((ATTN_BLOCK, kvw), prev),
                  pl.BlockSpec((ATTN_BLOCK, kvw), own),
                  pl.BlockSpec((SWA_HEADS, ATTN_BLOCK, 2 * ATTN_BLOCK), lambda b, n: (0, 0, 0))],
        out_specs=pl.BlockSpec((ATTN_BLOCK, qw), own),
        out_shape=jax.ShapeDtypeStruct((batch * seq, qw), BF16),
        compiler_params=_params(("arbitrary", "arbitrary"), 32),
        name="swa",
    )(sinks, sq, sk, sk, sv, sv, bias)


def _cumsum_rows(x):
    n = x.shape[0]
    row = lax.broadcasted_iota(I32, (n, 1), 0)
    s = 1
    while s < n:
        x = x + jnp.where(row >= s, pltpu.roll(x, s, 0), 0.0)
        s *= 2
    return x


def _rows_from_blocks(vals, width):
    return jnp.concatenate([jnp.broadcast_to(v, (HG_SUB, width)) for v in vals], axis=0)


def _hgrn_body(hq_ref, hf_ref, hi_ref, hg_ref, loglb_ref, log1mlb_ref, nw_ref, bo_ref, o_ref,
               st_ref, a_scr, q_scr, k_scr, oi_scr):
    c = pl.program_id(1)
    n_sub = HG_CHUNK // HG_SUB
    width = HG_WIDTH

    @pl.when(c == 0)
    def _():
        st_ref[...] = jnp.zeros_like(st_ref)

    z = hf_ref[...]
    log_sig = jnp.minimum(z, 0.0) - jnp.log1p(jnp.exp(-jnp.abs(z)))
    t_a = loglb_ref[...]
    t_b = log1mlb_ref[...] + log_sig
    log_f = jnp.maximum(t_a, t_b) + jnp.log1p(jnp.exp(-jnp.abs(t_a - t_b)))
    kk = jnp.exp(t_b - z)
    hq = hq_ref[...]
    qq = hq * jax.nn.sigmoid(hq) * (HG_DK ** -0.5)
    vv = hi_ref[...]

    a_cum = _cumsum_rows(log_f)
    ends = [a_cum[HG_SUB * j + HG_SUB - 1:HG_SUB * j + HG_SUB, :] for j in range(n_sub)]
    zero_row = jnp.zeros((1, width), F32)
    starts = [zero_row] + ends[:-1]
    a_loc = a_cum - _rows_from_blocks(starts, width)
    e_loc = _rows_from_blocks(ends, width) - a_cum
    worst = starts[0] - ends[0]
    for j in range(1, n_sub):
        worst = jnp.maximum(worst, starts[j] - ends[j])
    unsafe = jnp.max(worst) > HG_SAFE_DECAY

    q_sub = qq * jnp.exp(a_loc)
    k_sub = kk * jnp.exp(-a_loc)
    k_end = kk * jnp.exp(e_loc)
    one_row = jnp.ones((1, width), F32)
    q_dec = q_sub * _rows_from_blocks([jnp.exp(s) for s in starts], width)
    k_dec = k_end * _rows_from_blocks([jnp.exp(ends[-1] - e) for e in ends], width)
    q_top = q_sub * _rows_from_blocks(
        [zero_row, zero_row, one_row, jnp.exp(ends[2] - ends[1])], width)
    k_top = k_end * _rows_from_blocks(
        [jnp.exp(ends[1] - ends[0]), one_row, zero_row, zero_row], width)

    row = lax.broadcasted_iota(I32, (HG_CHUNK, 1), 0)
    sub_of_row = row // HG_SUB
    t_idx = lax.broadcasted_iota(I32, (HG_CHUNK, HG_CHUNK), 0)
    s_idx = lax.broadcasted_iota(I32, (HG_CHUNK, HG_CHUNK), 1)
    m_sub = ((t_idx // HG_SUB) == (s_idx // HG_SUB)) & (t_idx >= s_idx)
    m_sub2 = jnp.concatenate([m_sub, m_sub], axis=0)
    lane = lax.broadcasted_iota(I32, (1, LANES), 1)
    lo = lane < HG_DK
    bd = (lax.broadcasted_iota(I32, (LANES, LANES), 0) // HG_DV) == \
         (lax.broadcasted_iota(I32, (LANES, LANES), 1) // HG_DK)
    dec_row = jnp.exp(ends[-1])

    def heads_on_rows(x):
        return jnp.concatenate([jnp.where(lo, x, 0.0), jnp.where(lo, 0.0, x)], axis=0)

    o_inter = []
    for p in range(width // LANES):
        sl = slice(p * LANES, (p + 1) * LANES)
        qs, ks, ke = q_sub[:, sl], k_sub[:, sl], k_end[:, sl]
        p_sub = _dot_nt(heads_on_rows(qs).astype(BF16), ks.astype(BF16))
        q_x = jnp.concatenate([jnp.where(sub_of_row == 1, qs, 0.0),
                               jnp.where(sub_of_row == 3, qs, 0.0), q_top[:, sl]], axis=1)
        k_x = jnp.concatenate([jnp.where(sub_of_row == 0, ke, 0.0),
                               jnp.where(sub_of_row == 2, ke, 0.0), k_top[:, sl]], axis=1)
        q_x2 = jnp.concatenate([jnp.where(jnp.tile(lo, (1, 3)), q_x, 0.0),
                                jnp.where(jnp.tile(lo, (1, 3)), 0.0, q_x)], axis=0)
        p_x = _dot_nt(q_x2.astype(BF16), k_x.astype(BF16))
        p_all = jnp.where(m_sub2, p_sub, 0.0) + p_x
        p_cat = jnp.concatenate([p_all[:HG_CHUNK], p_all[HG_CHUNK:]], axis=1)
        v_p = vv[:, sl]
        oi_scr[:, sl] = _dot(p_cat.astype(BF16), heads_on_rows(v_p).astype(BF16))
        st = st_ref[p]
        o_inter.append(_dot_nt(q_dec[:, sl].astype(BF16), st.astype(BF16)))
        upd = _dot_tn(v_p.astype(BF16), k_dec[:, sl].astype(BF16))
        st_ref[p] = st * dec_row[:, sl] + jnp.where(bd, upd, 0.0)
    o_inter = jnp.concatenate(o_inter, axis=1)

    @pl.when(unsafe)
    def _():
        a_scr[...] = a_cum
        q_scr[...] = qq
        k_scr[...] = kk

        def body(t, carry):
            d = a_scr[pl.ds(t, 1), :] - a_scr[...]
            w = jnp.exp(jnp.where(row <= t, d, -jnp.inf))
            prod = (q_scr[pl.ds(t, 1), :] * k_scr[...]) * w
            e = _dot(prod.astype(BF16), bo_ref[...])
            oi_scr[pl.ds(t, 1), :] = jnp.sum(e * hi_ref[...], axis=0, keepdims=True)
            return carry

        lax.fori_loop(0, HG_CHUNK, body, 0)

    o = oi_scr[...] + o_inter
    sq = o * o
    sq_hi = sq.astype(BF16)
    sq_lo = (sq - sq_hi.astype(F32)).astype(BF16)
    ms = (_dot(sq_hi, bo_ref[...]) + _dot(sq_lo, bo_ref[...])) * (1.0 / HG_DV)
    gate = hg_ref[...]
    o = o * lax.rsqrt(ms + RMS_EPS) * nw_ref[...] * (gate * jax.nn.sigmoid(gate))
    o_ref[...] = o.astype(o_ref.dtype)


def _hgrn(hq, hf, hi, hg, loglb, log1mlb, nw_row, head_ones, batch, seq):
    nc = seq // HG_CHUNK
    blk = lambda b, c: (b * nc + c, 0)
    row = lambda b, c: (0, 0)
    w = HG_WIDTH
    return pl.pallas_call(
        _hgrn_body,
        grid=(batch, nc),
        in_specs=[pl.BlockSpec((HG_CHUNK, w), blk)] * 4 + [pl.BlockSpec((1, w), row)] * 3
                 + [pl.BlockSpec((w, w), row)],
        out_specs=pl.BlockSpec((HG_CHUNK, w), blk),
        out_shape=jax.ShapeDtypeStruct((batch * seq, w), BF16),
        scratch_shapes=[pltpu.VMEM((w // LANES, LANES, LANES), F32)]
                       + [pltpu.VMEM((HG_CHUNK, w), F32)] * 4,
        compiler_params=_params(("arbitrary", "arbitrary"), 32),
        name="hgrn",
    )(hq, hf, hi, hg, loglb, log1mlb, nw_row, head_ones)


def _mem_attn_body(q_ref, k_ref, v_ref, o_ref):
    outs = []
    for h in range(MEM_HEADS):
        sl = slice(h * HEAD_DIM, (h + 1) * HEAD_DIM)
        logits = _dot_nt(q_ref[:, sl], k_ref[0, :, sl]) * ATTN_SCALE
        m = jnp.max(logits, axis=-1, keepdims=True)
        e = jnp.exp(logits - m)
        denom = jnp.sum(e, axis=-1, keepdims=True)
        outs.append(_dot(e.astype(BF16), v_ref[0, :, sl]) / denom)
    o_ref[...] = jnp.concatenate(outs, axis=-1).astype(o_ref.dtype)


def _mem_attn(mq, mk, mv, layer, batch, seq, mem_len):
    tq = min(512, seq)
    nq = seq // tq
    w = MEM_HEADS * HEAD_DIM
    return pl.pallas_call(
        _mem_attn_body,
        grid=(batch, nq),
        in_specs=[pl.BlockSpec((tq, w), lambda b, i: (b * nq + i, 0)),
                  pl.BlockSpec((1, mem_len, w), lambda b, i: (layer, b, 0)),
                  pl.BlockSpec((1, mem_len, w), lambda b, i: (layer, b, 0))],
        out_specs=pl.BlockSpec((tq, w), lambda b, i: (b * nq + i, 0)),
        out_shape=jax.ShapeDtypeStruct((batch * seq, w), BF16),
        compiler_params=_params(("arbitrary", "arbitrary"), 32),
        name="mem_attn",
    )(mq, mk, mv)


def _layer_norm(h, g, b):
    mu = jnp.mean(h, axis=-1, keepdims=True)
    d = h - mu
    var = jnp.mean(d * d, axis=-1, keepdims=True)
    return d * lax.rsqrt(var + LN_EPS) * g + b


def _route_rows(sel, scores):
    def row(a, r):
        return a[r:r + 1, :]

    best = None
    for g in range(N_GROUPS):
        a = [row(sel, 4 * g + i) for i in range(4)]
        gs = a[0] + a[1]
        for i, j in ((0, 2), (0, 3), (1, 2), (1, 3), (2, 3)):
            gs = jnp.maximum(gs, a[i] + a[j])
        if best is None:
            best, best_score = jnp.zeros_like(gs, dtype=I32), gs
        else:
            better = gs > best_score
            best = jnp.where(better, g, best)
            best_score = jnp.where(better, gs, best_score)

    def pick(arr, i):
        out = row(arr, i)
        for g in range(1, N_GROUPS):
            out = jnp.where(best == g, row(arr, 4 * g + i), out)
        return out

    a = [pick(sel, i) for i in range(4)]
    s = [pick(scores, i) for i in range(4)]
    i1, m1, s1 = jnp.zeros_like(best), a[0], s[0]
    for i in range(1, 4):
        gt = a[i] > m1
        i1 = jnp.where(gt, i, i1)
        m1 = jnp.where(gt, a[i], m1)
        s1 = jnp.where(gt, s[i], s1)
    i2 = jnp.full_like(best, -1)
    m2 = jnp.full_like(m1, -jnp.inf)
    s2 = jnp.zeros_like(s1)
    for i in range(4):
        gt = (i1 != i) & (a[i] > m2)
        i2 = jnp.where(gt, i, i2)
        m2 = jnp.where(gt, a[i], m2)
        s2 = jnp.where(gt, s[i], s2)
    denom = s1 + s2
    w1, w2 = s1 / denom, s2 / denom
    first_lower = i1 < i2
    lo = jnp.where(first_lower, i1, i2)
    hi = jnp.where(first_lower, i2, i1)
    pair = jnp.where(lo == 0, hi - 1, jnp.where(lo == 1, hi + 1, 5))
    cls = best * N_PAIRS + pair
    return cls, jnp.where(first_lower, w1, w2), jnp.where(first_lower, w2, w1)


def _out_proj_body(x_ref, swa_ref, hg_ref, mem_ref, w_ref, g_ref, b_ref, wr_ref, rb_ref, tri_ref,
                   x1_ref, cls_ref, rank_ref, wlo_ref, whi_ref, cnt_ref, carry_ref):
    i = pl.program_id(0)
    t = x_ref.shape[0]

    @pl.when(i == 0)
    def _():
        carry_ref[...] = jnp.zeros_like(carry_ref)

    acc = _dot(swa_ref[...], w_ref[0:512, :]) + _dot(hg_ref[...], w_ref[512:768, :]) \
        + _dot(mem_ref[...], w_ref[768:1024, :])
    x1 = _layer_norm(ALPHA * x_ref[...] + acc, g_ref[...], b_ref[...])
    x1_ref[...] = _to_tiles(x1)

    logits = _dot_nt(wr_ref[...], x1.astype(BF16))
    scores = jax.nn.sigmoid(logits)
    cls, w_lo, w_hi = _route_rows(scores + rb_ref[...], scores)
    onehot = (lax.broadcasted_iota(I32, (CLASS_ROWS, t), 0) == cls).astype(F32)
    prefix = _dot(onehot.astype(BF16), tri_ref[...])
    carry = carry_ref[...]
    rank = jnp.sum(onehot * (prefix - 1.0 + carry[:, 0:1]), axis=0, keepdims=True)
    new_carry = carry + prefix[:, t - 1:t]
    carry_ref[...] = new_carry
    cnt_ref[...] = new_carry
    cls_ref[0] = cls
    rank_ref[0] = rank.astype(I32)
    wlo_ref[0] = w_lo
    whi_ref[0] = w_hi


def _out_proj(x2, swa_o, hg_o, mem_o, w_bf, g_row, b_row, wr_t, rb_col, tri):
    n = x2.shape[0]
    t = tri.shape[0]
    nt = n // t
    tok = lambda i: (i, 0)
    fixed = lambda i: (0, 0)
    per_tile = pl.BlockSpec((1, 1, t), lambda i: (i, 0, 0))
    return pl.pallas_call(
        _out_proj_body,
        grid=(nt,),
        in_specs=[pl.BlockSpec((t, D_MODEL), tok), pl.BlockSpec((t, 512), tok),
                  pl.BlockSpec((t, 256), tok), pl.BlockSpec((t, 256), tok),
                  pl.BlockSpec((MIX_WIDTH, D_MODEL), fixed),
                  pl.BlockSpec((1, D_MODEL), fixed), pl.BlockSpec((1, D_MODEL), fixed),
                  pl.BlockSpec((N_EXPERTS, D_MODEL), fixed), pl.BlockSpec((N_EXPERTS, 1), fixed),
                  pl.BlockSpec((t, t), fixed)],
        out_specs=[pl.BlockSpec((t * ROW_TILES, LANES), tok), per_tile, per_tile, per_tile, per_tile,
                   pl.BlockSpec((CLASS_ROWS, LANES), fixed)],
        out_shape=[jax.ShapeDtypeStruct((n * ROW_TILES, LANES), F32),
                   jax.ShapeDtypeStruct((nt, 1, t), I32), jax.ShapeDtypeStruct((nt, 1, t), I32),
                   jax.ShapeDtypeStruct((nt, 1, t), F32), jax.ShapeDtypeStruct((nt, 1, t), F32),
                   jax.ShapeDtypeStruct((CLASS_ROWS, LANES), F32)],
        scratch_shapes=[pltpu.VMEM((CLASS_ROWS, LANES), F32)],
        compiler_params=_params(("arbitrary",), 48),
        name="out_proj_ln_route",
    )(x2, swa_o, hg_o, mem_o, w_bf, g_row, b_row, wr_t, rb_col, tri)


def _tile_copy(src_ref, dst_ref, sem, src_row, dst_row, tiles):
    n = tiles * SUBLANES
    return pltpu.make_async_copy(
        src_ref.at[pl.ds(pl.multiple_of(src_row * n, n), n), :],
        dst_ref.at[pl.ds(pl.multiple_of(dst_row * n, n), n), :], sem)


def _scatter_body(cstart_ref, cls_ref, rank_ref, x_hbm, dest_ref, xs_hbm,
                  buf, dest_vmem, dest_smem, load_sem, scat_sem, misc_sem, *, n_steps, tile):
    i = pl.program_id(0)
    rows = tile * SUBLANES
    slot = i % 3

    def load(step, into):
        return pltpu.make_async_copy(
            x_hbm.at[pl.ds(pl.multiple_of(step * rows, rows), rows), :], buf.at[into], load_sem.at[into])

    def scatter_done(of):
        return pltpu.make_async_copy(buf.at[of], xs_hbm.at[pl.ds(0, rows), :], scat_sem.at[of])

    @pl.when(i == 0)
    def _():
        load(0, 0).start()

    @pl.when(i + 1 < n_steps)
    def _():
        load(i + 1, (i + 1) % 3).start()

    cls = cls_ref[0]
    dest = rank_ref[0]
    for c in range(N_CLASSES):
        dest = dest + jnp.where(cls == c, cstart_ref[c], 0)
    dest_ref[0] = dest
    dest_vmem[...] = dest
    to_smem = pltpu.make_async_copy(dest_vmem, dest_smem, misc_sem)
    to_smem.start()
    to_smem.wait()

    load(i, slot).wait()

    def issue(j, carry):
        _tile_copy(buf.at[slot], xs_hbm, scat_sem.at[slot], j, dest_smem[0, j], 1).start()
        return carry

    lax.fori_loop(0, tile, issue, 0, unroll=8)

    @pl.when(i >= 1)
    def _():
        scatter_done((i + 2) % 3).wait()

    @pl.when(i == n_steps - 1)
    def _():
        scatter_done(slot).wait()


def _scatter_rows(x1_z, cls, rank, cstart):
    nt, _, tile = cls.shape
    n = nt * tile
    per_tile = pl.BlockSpec((1, 1, tile), lambda i, cs: (i, 0, 0))
    return pl.pallas_call(
        functools.partial(_scatter_body, n_steps=nt, tile=tile),
        grid_spec=pltpu.PrefetchScalarGridSpec(
            num_scalar_prefetch=1,
            grid=(nt,),
            in_specs=[per_tile, per_tile, pl.BlockSpec(memory_space=pl.ANY)],
            out_specs=[per_tile, pl.BlockSpec(memory_space=pl.ANY)],
            scratch_shapes=[pltpu.VMEM((3, tile * SUBLANES, LANES), F32),
                            pltpu.VMEM((1, tile), I32), pltpu.SMEM((1, tile), I32),
                            pltpu.SemaphoreType.DMA((3,)), pltpu.SemaphoreType.DMA((3,)),
                            pltpu.SemaphoreType.DMA(())]),
        out_shape=[jax.ShapeDtypeStruct((nt, 1, tile), I32),
                   jax.ShapeDtypeStruct((n * SUBLANES, LANES), F32)],
        compiler_params=_params(("arbitrary",), 32),
        name="scatter_rows",
    )(cstart, cls, rank, x1_z)


def _expert_body(blk_ref, elo_ref, ehi_ref, start_ref, end_ref, xs_ref,
                 g_lo, u_lo, d_lo, g_hi, u_hi, d_hi, ys_ref):
    i = pl.program_id(0)
    start = start_ref[i]
    end = end_ref[i]
    pair = 2 * ROW_TILES

    @pl.when(end > start)
    def _():
        x = _from_tiles(xs_ref[...], MOE_BLK).astype(BF16)

        def ffn(gw, uw, dw):
            g = _dot(x, gw[0, 0])
            u = _dot(x, uw[0, 0])
            return _dot(((g * jax.nn.sigmoid(g)) * u).astype(BF16), dw[0, 0])

        y = _to_tiles(jnp.concatenate([ffn(g_lo, u_lo, d_lo), ffn(g_hi, u_hi, d_hi)], axis=1))
        first = start % MOE_BLK == 0

        @pl.when(first)
        def _():
            ys_ref[...] = y

        @pl.when(jnp.logical_not(first))
        def _():
            slot = blk_ref[i] * MOE_BLK + lax.broadcasted_iota(I32, (MOE_BLK * pair, 1), 0) // pair
            ys_ref[...] = jnp.where((slot >= start) & (slot < end), y, ys_ref[...])


def _experts(xs_z, items, wg, wu, wd, layer):
    blk, elo, ehi, start, end = items
    n_items = blk.shape[0]
    n_slots = xs_z.shape[0] // SUBLANES

    def data_map(i, blk, elo, ehi, start, end):
        return (blk[i], 0)

    def w_lo_map(i, blk, elo, ehi, start, end):
        return (layer, elo[i], 0, 0)

    def w_hi_map(i, blk, elo, ehi, start, end):
        return (layer, ehi[i], 0, 0)

    up_spec = lambda m: pl.BlockSpec((1, 1, D_MODEL, D_EXPERT), m)
    down_spec = lambda m: pl.BlockSpec((1, 1, D_EXPERT, D_MODEL), m)
    return pl.pallas_call(
        _expert_body,
        grid_spec=pltpu.PrefetchScalarGridSpec(
            num_scalar_prefetch=5,
            grid=(n_items,),
            in_specs=[pl.BlockSpec((MOE_BLK * SUBLANES, LANES), data_map),
                      up_spec(w_lo_map), up_spec(w_lo_map), down_spec(w_lo_map),
                      up_spec(w_hi_map), up_spec(w_hi_map), down_spec(w_hi_map)],
            out_specs=pl.BlockSpec((MOE_BLK * 2 * SUBLANES, LANES), data_map)),
        out_shape=jax.ShapeDtypeStruct((n_slots * 2 * SUBLANES, LANES), F32),
        compiler_params=_params(("arbitrary",), 48),
        name="experts",
    )(blk, elo, ehi, start, end, xs_z, wg, wu, wd, wg, wu, wd)


def _ln2_body(dest_ref, x1_ref, ys_hbm, wlo_ref, whi_ref, g_ref, b_ref, o_ref, buf, sem, *, n_steps, tile):
    i = pl.program_id(0)
    pair = 2 * ROW_TILES
    slot = i % 2

    def gather(step, into):
        def issue(j, carry):
            _tile_copy(ys_hbm, buf.at[into], sem.at[into], dest_ref[step * tile + j], j, 2).start()
            return carry

        lax.fori_loop(0, tile, issue, 0, unroll=8)

    @pl.when(i == 0)
    def _():
        gather(0, 0)

    @pl.when(i + 1 < n_steps)
    def _():
        gather(i + 1, (i + 1) % 2)

    pltpu.make_async_copy(ys_hbm.at[pl.ds(0, tile * pair), :], buf.at[slot], sem.at[slot]).wait()
    y = _from_tiles(buf[slot], tile)
    y = wlo_ref[...] * y[:, :D_MODEL] + whi_ref[...] * y[:, D_MODEL:]
    h = ALPHA * _from_tiles(x1_ref[...], tile) + y
    o_ref[...] = _layer_norm(h, g_ref[...], b_ref[...])


def _ln2(x1_z, ys_z, dest, w_lo_col, w_hi_col, g_row, b_row, tile):
    n = x1_z.shape[0] // SUBLANES
    nt = n // tile
    tok = lambda i, d: (i, 0)
    fixed = lambda i, d: (0, 0)
    return pl.pallas_call(
        functools.partial(_ln2_body, n_steps=nt, tile=tile),
        grid_spec=pltpu.PrefetchScalarGridSpec(
            num_scalar_prefetch=1,
            grid=(nt,),
            in_specs=[pl.BlockSpec((tile * SUBLANES, LANES), tok), pl.BlockSpec(memory_space=pl.ANY),
                      pl.BlockSpec((tile, 1), tok), pl.BlockSpec((tile, 1), tok),
                      pl.BlockSpec((1, D_MODEL), fixed), pl.BlockSpec((1, D_MODEL), fixed)],
            out_specs=pl.BlockSpec((tile, D_MODEL), tok),
            scratch_shapes=[pltpu.VMEM((2, tile * 2 * SUBLANES, LANES), F32),
                            pltpu.SemaphoreType.DMA((2,))]),
        out_shape=jax.ShapeDtypeStruct((n, D_MODEL), F32),
        compiler_params=_params(("arbitrary",), 48),
        name="gather_ln2",
    )(dest, x1_z, ys_z, w_lo_col, w_hi_col, g_row, b_row)


def _t5_bucket(dist):
    max_exact = N_BUCKETS // 2
    d = jnp.maximum(dist, 0)
    large = max_exact + (jnp.log(jnp.maximum(d, 1).astype(F32) / max_exact)
                         / math.log(MAX_DISTANCE / max_exact) * (N_BUCKETS - max_exact)).astype(I32)
    large = jnp.minimum(large, N_BUCKETS - 1)
    return jnp.where(d < max_exact, d, large)


def _banded_bias(rel_bias):
    i = jnp.arange(ATTN_BLOCK)[:, None]
    j = jnp.arange(2 * ATTN_BLOCK)[None, :]
    dist = i + ATTN_BLOCK - j
    bucket = _t5_bucket(dist)[None]
    table = rel_bias.astype(F32)
    bias = jnp.zeros((SWA_HEADS, ATTN_BLOCK, 2 * ATTN_BLOCK), F32)
    for b in range(N_BUCKETS):
        bias = jnp.where(bucket == b, table[b][:, None, None], bias)
    in_window = (dist >= 0) & (dist < WINDOW)
    return jnp.where(in_window[None], bias, -jnp.inf)


def _work_items(counts, n_tok):
    n_blocks = n_tok // MOE_BLK
    cend = jnp.cumsum(counts)
    cstart = cend - counts
    blk_starts = jnp.arange(n_blocks, dtype=I32) * MOE_BLK
    cls_starts = jnp.where(counts > 0, cstart, n_tok)
    start = jnp.sort(jnp.concatenate([blk_starts, cls_starts]))
    end = jnp.concatenate([start[1:], jnp.full((1,), n_tok, I32)])
    blk = jnp.minimum(start, n_tok - 1) // MOE_BLK
    cls = jnp.minimum(jnp.sum((cend[None, :] <= start[:, None]).astype(I32), axis=1), N_CLASSES - 1)
    group, pair = cls // N_PAIRS, cls % N_PAIRS
    lo = (pair >= 3).astype(I32) + (pair >= 5).astype(I32)
    hi = jnp.where(pair < 3, pair + 1, jnp.where(pair < 5, pair - 1, 3))
    return cstart, (blk, group * EXPERTS_PER_GROUP + lo, group * EXPERTS_PER_GROUP + hi, start, end)


def kernel(x, mem, w_in, b_in, w_mem_kv, attn_sinks, rel_bias, hgrn_lb_logits, hgrn_norm, w_out,
           ln1_g, ln1_b, w_router, router_bias, w_gate, w_up, w_down, ln2_g, ln2_b):
    batch, seq, _ = x.shape
    mem_len = mem.shape[1]
    n_tok = batch * seq
    route_tile = min(512, n_tok)

    bias = _banded_bias(rel_bias)
    lb = jnp.cumsum(jax.nn.softmax(hgrn_lb_logits.astype(F32), axis=0), axis=0)
    lb = lb - lb[0:1]
    log_lb = jnp.log(lb)
    log_1m_lb = jnp.log1p(-lb)
    head_ones = (jnp.arange(HG_WIDTH)[:, None] // HG_DV == jnp.arange(HG_WIDTH)[None, :] // HG_DV).astype(BF16)
    tri = (jnp.arange(route_tile)[:, None] <= jnp.arange(route_tile)[None, :]).astype(BF16)
    wr_t = jnp.transpose(w_router).astype(BF16)
    rb_col = router_bias.astype(F32).reshape(N_EXPERTS, 1)
    w_in_bf = w_in.astype(BF16)
    w_out_bf = w_out.astype(BF16)
    wg_bf, wu_bf, wd_bf = w_gate.astype(BF16), w_up.astype(BF16), w_down.astype(BF16)

    mk, mv = _mem_kv(mem.reshape(batch * mem_len, D_MODEL), w_mem_kv)

    x2 = x.reshape(n_tok, D_MODEL)
    for l in range(DEPTH):
        sq, sk, sv, hq, hf, hi, hg, mq = _in_proj(x2, w_in_bf[l], b_in[l].reshape(1, IN_WIDTH))
        swa_o = _swa(sq, sk, sv, attn_sinks[l].astype(F32), bias, batch, seq)
        hg_o = _hgrn(hq, hf, hi, hg, log_lb[l].reshape(1, HG_WIDTH), log_1m_lb[l].reshape(1, HG_WIDTH),
                     jnp.tile(hgrn_norm[l].astype(F32), HG_HEADS).reshape(1, HG_WIDTH), head_ones, batch, seq)
        mem_o = _mem_attn(mq, mk, mv, l, batch, seq, mem_len)
        x1_z, cls, rank, w_lo, w_hi, counts = _out_proj(
            x2, swa_o, hg_o, mem_o, w_out_bf[l], ln1_g[l].reshape(1, D_MODEL), ln1_b[l].reshape(1, D_MODEL),
            wr_t, rb_col, tri)
        cstart, items = _work_items(counts[:N_CLASSES, 0].astype(I32), n_tok)
        dest, xs_z = _scatter_rows(x1_z, cls, rank, cstart)
        ys_z = _experts(xs_z, items, wg_bf, wu_bf, wd_bf, l)
        x2 = _ln2(x1_z, ys_z, dest.reshape(n_tok), w_lo.reshape(n_tok, 1), w_hi.reshape(n_tok, 1),
                  ln2_g[l].reshape(1, D_MODEL), ln2_b[l].reshape(1, D_MODEL), route_tile)
    return x2.reshape(batch, seq, D_MODEL)
```

```python
import functools
import math

import numpy as np
import jax
import jax.numpy as jnp
from jax import lax
from jax.experimental import pallas as pl
from jax.experimental.pallas import tpu as pltpu

F32 = jnp.float32
BF16 = jnp.bfloat16
I32 = jnp.int32

D_MODEL = 1024
DEPTH = 4
HEAD_DIM = 64
SWA_HEADS = 8
SWA_KV_HEADS = 2
SWA_GROUP = SWA_HEADS // SWA_KV_HEADS
WINDOW = 128
ATTN_BLOCK = 128
HG_HEADS = 4
HG_DK = 64
HG_DV = 64
HG_WIDTH = HG_HEADS * HG_DK
MEM_HEADS = 4
N_BUCKETS = 32
MAX_DISTANCE = 128
N_EXPERTS = 16
N_GROUPS = 4
EXPERTS_PER_GROUP = 4
D_EXPERT = 512
LN_EPS = 1e-5
RMS_EPS = 1e-6
ALPHA = (2 * DEPTH) ** 0.25
SPLITS = (512, 128, 128, 256, 256, 256, 256, 256)
IN_WIDTH = sum(SPLITS)
MIX_WIDTH = 1024
ATTN_SCALE = HEAD_DIM ** -0.5

SUBLANES = 8
LANES = 128
ROW_TILES = D_MODEL // LANES

SWA_STEP_BLOCKS = 4
HG_CHUNK = 128
HG_SUB = 32
HG_SAFE_DECAY = 80.0
N_PAIRS = 6
N_CLASSES = N_GROUPS * N_PAIRS
CLASS_ROWS = 32
MOE_BLK = 256


def _params(semantics, vmem_mib):
    return pltpu.CompilerParams(dimension_semantics=semantics, vmem_limit_bytes=vmem_mib * 1024 * 1024)


def _dot(a, b):
    return jnp.dot(a, b, preferred_element_type=F32)


def _dot_nt(a, b):
    return lax.dot_general(a, b, (((1,), (1,)), ((), ())), preferred_element_type=F32)


def _dot_tn(a, b):
    return lax.dot_general(a, b, (((0,), (0,)), ((), ())), preferred_element_type=F32)


def _from_tiles(z, n_rows):
    k = z.shape[0] // n_rows
    return z.reshape(n_rows, k, LANES).reshape(n_rows, k * LANES)


def _to_tiles(val):
    n_rows, width = val.shape
    return val.reshape(n_rows, width // LANES, LANES).reshape(n_rows * (width // LANES), LANES)


def _in_proj_body(x_ref, w_ref, b_ref, *out_refs):
    acc = _dot(x_ref[...].astype(BF16), w_ref[...]) + b_ref[...]
    off = 0
    for ref, width in zip(out_refs, SPLITS):
        ref[...] = acc[:, off:off + width].astype(ref.dtype)
        off += width


def _in_proj(x2, w_bf, b_row):
    n = x2.shape[0]
    tm = min(512, n)
    dtypes = (BF16, BF16, BF16, F32, F32, F32, F32, BF16)
    return pl.pallas_call(
        _in_proj_body,
        grid=(n // tm,),
        in_specs=[pl.BlockSpec((tm, D_MODEL), lambda i: (i, 0)),
                  pl.BlockSpec((D_MODEL, IN_WIDTH), lambda i: (0, 0)),
                  pl.BlockSpec((1, IN_WIDTH), lambda i: (0, 0))],
        out_specs=[pl.BlockSpec((tm, w), lambda i: (i, 0)) for w in SPLITS],
        out_shape=[jax.ShapeDtypeStruct((n, w), dt) for w, dt in zip(SPLITS, dtypes)],
        compiler_params=_params(("arbitrary",), 48),
        name="in_proj",
    )(x2, w_bf, b_row)


def _mem_kv_body(mem_ref, w_ref, k_ref, v_ref):
    acc = _dot(mem_ref[...].astype(BF16), w_ref[0].astype(BF16))
    half = MEM_HEADS * HEAD_DIM
    k_ref[0] = acc[:, :half].astype(BF16)
    v_ref[0] = acc[:, half:].astype(BF16)


def _mem_kv(mem2, w_mem_kv):
    rows = mem2.shape[0]
    tm = min(512, rows)
    half = MEM_HEADS * HEAD_DIM
    return pl.pallas_call(
        _mem_kv_body,
        grid=(DEPTH, rows // tm),
        in_specs=[pl.BlockSpec((tm, D_MODEL), lambda l, i: (i, 0)),
                  pl.BlockSpec((1, D_MODEL, 2 * half), lambda l, i: (l, 0, 0))],
        out_specs=[pl.BlockSpec((1, tm, half), lambda l, i: (l, i, 0)),
                   pl.BlockSpec((1, tm, half), lambda l, i: (l, i, 0))],
        out_shape=[jax.ShapeDtypeStruct((DEPTH, rows, half), BF16)] * 2,
        compiler_params=_params(("arbitrary", "arbitrary"), 32),
        name="mem_kv",
    )(mem2, w_mem_kv)


def _swa_body(sink_ref, q_ref, kp_ref, ko_ref, vp_ref, vo_ref, bias_ref, o_ref, *, blocks):
    n = pl.program_id(1)
    q_all = q_ref[...] * ATTN_SCALE
    k_all = jnp.concatenate([kp_ref[...], ko_ref[...]], axis=0)
    v_all = jnp.concatenate([vp_ref[...], vo_ref[...]], axis=0)
    for j in range(blocks):
        lo = j * ATTN_BLOCK
        kk = k_all[lo:lo + 2 * ATTN_BLOCK]
        vv = v_all[lo:lo + 2 * ATTN_BLOCK]
        table = jnp.where(n > 0, 1, 0) if j == 0 else 1
        outs = []
        for h in range(SWA_HEADS):
            g = h // SWA_GROUP
            q_h = q_all[lo:lo + ATTN_BLOCK, h * HEAD_DIM:(h + 1) * HEAD_DIM]
            k_g = kk[:, g * HEAD_DIM:(g + 1) * HEAD_DIM]
            v_g = vv[:, g * HEAD_DIM:(g + 1) * HEAD_DIM]
            logits = _dot_nt(q_h, k_g) + bias_ref[table, h]
            sink = sink_ref[h]
            m = jnp.maximum(jnp.max(logits, axis=-1, keepdims=True), sink)
            e = jnp.exp(logits - m)
            denom = jnp.sum(e, axis=-1, keepdims=True) + jnp.exp(sink - m)
            outs.append(_dot(e.astype(BF16), v_g) / denom)
        o_ref[lo:lo + ATTN_BLOCK, :] = jnp.concatenate(outs, axis=-1).astype(o_ref.dtype)


def _swa(sq, sk, sv, sinks, bias2, batch, seq):
    nb = seq // ATTN_BLOCK
    blocks = math.gcd(SWA_STEP_BLOCKS, nb)
    ns = nb // blocks
    own = lambda b, n: (b * ns + n, 0)
    prev = lambda b, n: (b * nb + jnp.maximum(n * blocks - 1, 0), 0)
    kvw = SWA_KV_HEADS * HEAD_DIM
    qw = SWA_HEADS * HEAD_DIM
    rows = blocks * ATTN_BLOCK
    return pl.pallas_call(
        functools.partial(_swa_body, blocks=blocks),
        grid=(batch, ns),
        in_specs=[pl.BlockSpec(memory_space=pltpu.SMEM),
                  pl.BlockSpec((rows, qw), own),
                  pl.BlockSpec((ATTN_BLOCK, kvw), prev),
                  pl.BlockSpec((rows, kvw), own),
                  pl.BlockSpec((ATTN_BLOCK, kvw), prev),
                  pl.BlockSpec((rows, kvw), own),
                  pl.BlockSpec((2, SWA_HEADS, ATTN_BLOCK, 2 * ATTN_BLOCK), lambda b, n: (0, 0, 0, 0))],
        out_specs=pl.BlockSpec((rows, qw), own),
        out_shape=jax.ShapeDtypeStruct((batch * seq, qw), BF16),
        compiler_params=_params(("arbitrary", "arbitrary"), 32),
        name="swa",
    )(sinks, sq, sk, sk, sv, sv, bias2)


def _cumsum_rows(x):
    n = x.shape[0]
    row = lax.broadcasted_iota(I32, (n, 1), 0)
    s = 1
    while s < n:
        x = x + jnp.where(row >= s, pltpu.roll(x, s, 0), 0.0)
        s *= 2
    return x


def _rows_from_blocks(vals, width):
    return jnp.concatenate([jnp.broadcast_to(v, (HG_SUB, width)) for v in vals], axis=0)


def _hgrn_body(hq_ref, hf_ref, hi_ref, hg_ref, loglb_ref, log1mlb_ref, nw_ref, bo_ref, o_ref,
               st_ref, a_scr, q_scr, k_scr, oi_scr):
    c = pl.program_id(1)
    n_sub = HG_CHUNK // HG_SUB
    width = HG_WIDTH

    @pl.when(c == 0)
    def _():
        st_ref[...] = jnp.zeros_like(st_ref)

    z = hf_ref[...]
    log_sig = jnp.minimum(z, 0.0) - jnp.log1p(jnp.exp(-jnp.abs(z)))
    t_a = loglb_ref[...]
    t_b = log1mlb_ref[...] + log_sig
    log_f = jnp.maximum(t_a, t_b) + jnp.log1p(jnp.exp(-jnp.abs(t_a - t_b)))
    kk = jnp.exp(t_b - z)
    hq = hq_ref[...]
    qq = hq * jax.nn.sigmoid(hq) * (HG_DK ** -0.5)
    vv = hi_ref[...]

    a_cum = _cumsum_rows(log_f)
    ends = [a_cum[HG_SUB * j + HG_SUB - 1:HG_SUB * j + HG_SUB, :] for j in range(n_sub)]
    zero_row = jnp.zeros((1, width), F32)
    starts = [zero_row] + ends[:-1]
    a_loc = a_cum - _rows_from_blocks(starts, width)
    e_loc = _rows_from_blocks(ends, width) - a_cum
    worst = starts[0] - ends[0]
    for j in range(1, n_sub):
        worst = jnp.maximum(worst, starts[j] - ends[j])
    unsafe = jnp.max(worst) > HG_SAFE_DECAY

    q_sub = qq * jnp.exp(a_loc)
    k_sub = kk * jnp.exp(-a_loc)
    k_end = kk * jnp.exp(e_loc)
    one_row = jnp.ones((1, width), F32)
    q_dec = q_sub * _rows_from_blocks([jnp.exp(s) for s in starts], width)
    k_dec = k_end * _rows_from_blocks([jnp.exp(ends[-1] - e) for e in ends], width)
    q_top = q_sub * _rows_from_blocks(
        [zero_row, zero_row, one_row, jnp.exp(ends[2] - ends[1])], width)
    k_top = k_end * _rows_from_blocks(
        [jnp.exp(ends[1] - ends[0]), one_row, zero_row, zero_row], width)

    row = lax.broadcasted_iota(I32, (HG_CHUNK, 1), 0)
    sub_of_row = row // HG_SUB
    t_idx = lax.broadcasted_iota(I32, (HG_CHUNK, HG_CHUNK), 0)
    s_idx = lax.broadcasted_iota(I32, (HG_CHUNK, HG_CHUNK), 1)
    m_sub = ((t_idx // HG_SUB) == (s_idx // HG_SUB)) & (t_idx >= s_idx)
    m_sub2 = jnp.concatenate([m_sub, m_sub], axis=0)
    lane = lax.broadcasted_iota(I32, (1, LANES), 1)
    lo = lane < HG_DK
    bd = (lax.broadcasted_iota(I32, (LANES, LANES), 0) // HG_DV) == \
         (lax.broadcasted_iota(I32, (LANES, LANES), 1) // HG_DK)
    dec_row = jnp.exp(ends[-1])

    def heads_on_rows(x):
        return jnp.concatenate([jnp.where(lo, x, 0.0), jnp.where(lo, 0.0, x)], axis=0)

    o_inter = []
    for p in range(width // LANES):
        sl = slice(p * LANES, (p + 1) * LANES)
        qs, ks, ke = q_sub[:, sl], k_sub[:, sl], k_end[:, sl]
        p_sub = _dot_nt(heads_on_rows(qs).astype(BF16), ks.astype(BF16))
        q_x = jnp.concatenate([jnp.where(sub_of_row == 1, qs, 0.0),
                               jnp.where(sub_of_row == 3, qs, 0.0), q_top[:, sl]], axis=1)
        k_x = jnp.concatenate([jnp.where(sub_of_row == 0, ke, 0.0),
                               jnp.where(sub_of_row == 2, ke, 0.0), k_top[:, sl]], axis=1)
        q_x2 = jnp.concatenate([jnp.where(jnp.tile(lo, (1, 3)), q_x, 0.0),
                                jnp.where(jnp.tile(lo, (1, 3)), 0.0, q_x)], axis=0)
        p_x = _dot_nt(q_x2.astype(BF16), k_x.astype(BF16))
        p_all = jnp.where(m_sub2, p_sub, 0.0) + p_x
        p_cat = jnp.concatenate([p_all[:HG_CHUNK], p_all[HG_CHUNK:]], axis=1)
        v_p = vv[:, sl]
        oi_scr[:, sl] = _dot(p_cat.astype(BF16), heads_on_rows(v_p).astype(BF16))
        st = st_ref[p]
        o_inter.append(_dot_nt(q_dec[:, sl].astype(BF16), st.astype(BF16)))
        upd = _dot_tn(v_p.astype(BF16), k_dec[:, sl].astype(BF16))
        st_ref[p] = st * dec_row[:, sl] + jnp.where(bd, upd, 0.0)
    o_inter = jnp.concatenate(o_inter, axis=1)

    @pl.when(unsafe)
    def _():
        a_scr[...] = a_cum
        q_scr[...] = qq
        k_scr[...] = kk

        def body(t, carry):
            d = a_scr[pl.ds(t, 1), :] - a_scr[...]
            w = jnp.exp(jnp.where(row <= t, d, -jnp.inf))
            prod = (q_scr[pl.ds(t, 1), :] * k_scr[...]) * w
            e = _dot(prod.astype(BF16), bo_ref[...])
            oi_scr[pl.ds(t, 1), :] = jnp.sum(e * hi_ref[...], axis=0, keepdims=True)
            return carry

        lax.fori_loop(0, HG_CHUNK, body, 0)

    o = oi_scr[...] + o_inter
    sq = o * o
    sq_hi = sq.astype(BF16)
    sq_lo = (sq - sq_hi.astype(F32)).astype(BF16)
    ms = (_dot(sq_hi, bo_ref[...]) + _dot(sq_lo, bo_ref[...])) * (1.0 / HG_DV)
    gate = hg_ref[...]
    o = o * lax.rsqrt(ms + RMS_EPS) * nw_ref[...] * (gate * jax.nn.sigmoid(gate))
    o_ref[...] = o.astype(o_ref.dtype)


def _hgrn(hq, hf, hi, hg, loglb, log1mlb, nw_row, head_ones, batch, seq):
    nc = seq // HG_CHUNK
    blk = lambda b, c: (b * nc + c, 0)
    row = lambda b, c: (0, 0)
    w = HG_WIDTH
    return pl.pallas_call(
        _hgrn_body,
        grid=(batch, nc),
        in_specs=[pl.BlockSpec((HG_CHUNK, w), blk)] * 4 + [pl.BlockSpec((1, w), row)] * 3
                 + [pl.BlockSpec((w, w), row)],
        out_specs=pl.BlockSpec((HG_CHUNK, w), blk),
        out_shape=jax.ShapeDtypeStruct((batch * seq, w), BF16),
        scratch_shapes=[pltpu.VMEM((w // LANES, LANES, LANES), F32)]
                       + [pltpu.VMEM((HG_CHUNK, w), F32)] * 4,
        compiler_params=_params(("arbitrary", "arbitrary"), 32),
        name="hgrn",
    )(hq, hf, hi, hg, loglb, log1mlb, nw_row, head_ones)


def _mem_attn_body(q_ref, k_ref, v_ref, o_ref):
    outs = []
    for h in range(MEM_HEADS):
        sl = slice(h * HEAD_DIM, (h + 1) * HEAD_DIM)
        logits = _dot_nt(q_ref[:, sl], k_ref[0, :, sl]) * ATTN_SCALE
        m = jnp.max(logits, axis=-1, keepdims=True)
        e = jnp.exp(logits - m)
        denom = jnp.sum(e, axis=-1, keepdims=True)
        outs.append(_dot(e.astype(BF16), v_ref[0, :, sl]) / denom)
    o_ref[...] = jnp.concatenate(outs, axis=-1).astype(o_ref.dtype)


def _mem_attn(mq, mk, mv, layer, batch, seq, mem_len):
    tq = min(512, seq)
    nq = seq // tq
    w = MEM_HEADS * HEAD_DIM
    return pl.pallas_call(
        _mem_attn_body,
        grid=(batch, nq),
        in_specs=[pl.BlockSpec((tq, w), lambda b, i: (b * nq + i, 0)),
                  pl.BlockSpec((1, mem_len, w), lambda b, i: (layer, b, 0)),
                  pl.BlockSpec((1, mem_len, w), lambda b, i: (layer, b, 0))],
        out_specs=pl.BlockSpec((tq, w), lambda b, i: (b * nq + i, 0)),
        out_shape=jax.ShapeDtypeStruct((batch * seq, w), BF16),
        compiler_params=_params(("arbitrary", "arbitrary"), 32),
        name="mem_attn",
    )(mq, mk, mv)


def _layer_norm(h, g, b):
    mu = jnp.mean(h, axis=-1, keepdims=True)
    d = h - mu
    var = jnp.mean(d * d, axis=-1, keepdims=True)
    return d * lax.rsqrt(var + LN_EPS) * g + b


def _route_rows(sel):
    def row(a, r):
        return a[r:r + 1, :]

    best = None
    for g in range(N_GROUPS):
        a = [row(sel, 4 * g + i) for i in range(4)]
        gs = a[0] + a[1]
        for i, j in ((0, 2), (0, 3), (1, 2), (1, 3), (2, 3)):
            gs = jnp.maximum(gs, a[i] + a[j])
        if best is None:
            best, best_score = jnp.zeros_like(gs, dtype=I32), gs
        else:
            better = gs > best_score
            best = jnp.where(better, g, best)
            best_score = jnp.where(better, gs, best_score)

    def pick(arr, i):
        out = row(arr, i)
        for g in range(1, N_GROUPS):
            out = jnp.where(best == g, row(arr, 4 * g + i), out)
        return out

    a = [pick(sel, i) for i in range(4)]
    i1, m1 = jnp.zeros_like(best), a[0]
    for i in range(1, 4):
        gt = a[i] > m1
        i1 = jnp.where(gt, i, i1)
        m1 = jnp.where(gt, a[i], m1)
    i2 = jnp.full_like(best, -1)
    m2 = jnp.full_like(m1, -jnp.inf)
    for i in range(4):
        gt = (i1 != i) & (a[i] > m2)
        i2 = jnp.where(gt, i, i2)
        m2 = jnp.where(gt, a[i], m2)
    lo = jnp.minimum(i1, i2)
    hi = jnp.maximum(i1, i2)
    pair = jnp.where(lo == 0, hi - 1, jnp.where(lo == 1, hi + 1, 5))
    return best * N_PAIRS + pair


def _out_proj_body(x_ref, swa_ref, hg_ref, mem_ref, w_ref, g_ref, b_ref, wr_ref, rb_ref, tri_ref,
                   x1_ref, cls_ref, rank_ref, cnt_ref, carry_ref):
    i = pl.program_id(0)
    t = x_ref.shape[0]

    @pl.when(i == 0)
    def _():
        carry_ref[...] = jnp.zeros_like(carry_ref)

    acc = _dot(swa_ref[...], w_ref[0:512, :]) + _dot(hg_ref[...], w_ref[512:768, :]) \
        + _dot(mem_ref[...], w_ref[768:1024, :])
    x1 = _layer_norm(ALPHA * x_ref[...] + acc, g_ref[...], b_ref[...])
    x1_ref[...] = _to_tiles(x1)

    logits = _dot_nt(wr_ref[...], x1.astype(BF16))
    cls = _route_rows(jax.nn.sigmoid(logits) + rb_ref[...])
    onehot = (lax.broadcasted_iota(I32, (CLASS_ROWS, t), 0) == cls).astype(F32)
    prefix = _dot(onehot.astype(BF16), tri_ref[...])
    carry = carry_ref[...]
    rank = jnp.sum(onehot * (prefix - 1.0 + carry[:, 0:1]), axis=0, keepdims=True)
    new_carry = carry + prefix[:, t - 1:t]
    carry_ref[...] = new_carry
    cnt_ref[...] = new_carry
    cls_ref[0] = cls
    rank_ref[0] = rank.astype(I32)


def _out_proj(x2, swa_o, hg_o, mem_o, w_bf, g_row, b_row, wr_t, rb_col, tri):
    n = x2.shape[0]
    t = tri.shape[0]
    nt = n // t
    tok = lambda i: (i, 0)
    fixed = lambda i: (0, 0)
    per_tile = pl.BlockSpec((1, 1, t), lambda i: (i, 0, 0))
    return pl.pallas_call(
        _out_proj_body,
        grid=(nt,),
        in_specs=[pl.BlockSpec((t, D_MODEL), tok), pl.BlockSpec((t, 512), tok),
                  pl.BlockSpec((t, 256), tok), pl.BlockSpec((t, 256), tok),
                  pl.BlockSpec((MIX_WIDTH, D_MODEL), fixed),
                  pl.BlockSpec((1, D_MODEL), fixed), pl.BlockSpec((1, D_MODEL), fixed),
                  pl.BlockSpec((N_EXPERTS, D_MODEL), fixed), pl.BlockSpec((N_EXPERTS, 1), fixed),
                  pl.BlockSpec((t, t), fixed)],
        out_specs=[pl.BlockSpec((t * ROW_TILES, LANES), tok), per_tile, per_tile,
                   pl.BlockSpec((CLASS_ROWS, LANES), fixed)],
        out_shape=[jax.ShapeDtypeStruct((n * ROW_TILES, LANES), F32),
                   jax.ShapeDtypeStruct((nt, 1, t), I32), jax.ShapeDtypeStruct((nt, 1, t), I32),
                   jax.ShapeDtypeStruct((CLASS_ROWS, LANES), F32)],
        scratch_shapes=[pltpu.VMEM((CLASS_ROWS, LANES), F32)],
        compiler_params=_params(("arbitrary",), 48),
        name="out_proj_ln_route",
    )(x2, swa_o, hg_o, mem_o, w_bf, g_row, b_row, wr_t, rb_col, tri)


def _tile_copy(src_ref, dst_ref, sem, src_row, dst_row, tiles):
    n = tiles * SUBLANES
    return pltpu.make_async_copy(
        src_ref.at[pl.ds(pl.multiple_of(src_row * n, n), n), :],
        dst_ref.at[pl.ds(pl.multiple_of(dst_row * n, n), n), :], sem)


def _scatter_body(cstart_ref, cls_ref, rank_ref, x_hbm, dest_ref, xs_hbm,
                  buf, dest_vmem, dest_smem, load_sem, scat_sem, misc_sem, *, n_steps, tile):
    i = pl.program_id(0)
    rows = tile * SUBLANES
    slot = i % 3

    def load(step, into):
        return pltpu.make_async_copy(
            x_hbm.at[pl.ds(pl.multiple_of(step * rows, rows), rows), :], buf.at[into], load_sem.at[into])

    def scatter_done(of):
        return pltpu.make_async_copy(buf.at[of], xs_hbm.at[pl.ds(0, rows), :], scat_sem.at[of])

    @pl.when(i == 0)
    def _():
        load(0, 0).start()

    @pl.when(i + 1 < n_steps)
    def _():
        load(i + 1, (i + 1) % 3).start()

    cls = cls_ref[0]
    dest = rank_ref[0]
    for c in range(N_CLASSES):
        dest = dest + jnp.where(cls == c, cstart_ref[c], 0)
    dest_ref[0] = dest
    dest_vmem[...] = dest
    to_smem = pltpu.make_async_copy(dest_vmem, dest_smem, misc_sem)
    to_smem.start()
    to_smem.wait()

    load(i, slot).wait()

    def issue(j, carry):
        _tile_copy(buf.at[slot], xs_hbm, scat_sem.at[slot], j, dest_smem[0, j], 1).start()
        return carry

    lax.fori_loop(0, tile, issue, 0, unroll=8)

    @pl.when(i >= 1)
    def _():
        scatter_done((i + 2) % 3).wait()

    @pl.when(i == n_steps - 1)
    def _():
        scatter_done(slot).wait()


def _scatter_rows(x1_z, cls, rank, cstart):
    nt, _, tile = cls.shape
    n = nt * tile
    per_tile = pl.BlockSpec((1, 1, tile), lambda i, cs: (i, 0, 0))
    return pl.pallas_call(
        functools.partial(_scatter_body, n_steps=nt, tile=tile),
        grid_spec=pltpu.PrefetchScalarGridSpec(
            num_scalar_prefetch=1,
            grid=(nt,),
            in_specs=[per_tile, per_tile, pl.BlockSpec(memory_space=pl.ANY)],
            out_specs=[per_tile, pl.BlockSpec(memory_space=pl.ANY)],
            scratch_shapes=[pltpu.VMEM((3, tile * SUBLANES, LANES), F32),
                            pltpu.VMEM((1, tile), I32), pltpu.SMEM((1, tile), I32),
                            pltpu.SemaphoreType.DMA((3,)), pltpu.SemaphoreType.DMA((3,)),
                            pltpu.SemaphoreType.DMA(())]),
        out_shape=[jax.ShapeDtypeStruct((nt, 1, tile), I32),
                   jax.ShapeDtypeStruct((n * SUBLANES, LANES), F32)],
        compiler_params=_params(("arbitrary",), 32),
        name="scatter_rows",
    )(cstart, cls, rank, x1_z)


def _expert_body(blk_ref, elo_ref, ehi_ref, start_ref, end_ref, xs_ref, wr_ref,
                 g_lo, u_lo, d_lo, g_hi, u_hi, d_hi, ys_ref):
    i = pl.program_id(0)
    start = start_ref[i]
    end = end_ref[i]

    @pl.when(end > start)
    def _():
        x = _from_tiles(xs_ref[...], MOE_BLK).astype(BF16)
        scores = jax.nn.sigmoid(_dot(x, wr_ref[...]))
        lane = lax.broadcasted_iota(I32, (1, LANES), 1)
        s_lo = jnp.sum(jnp.where(lane == elo_ref[i], scores, 0.0), axis=-1, keepdims=True)
        s_hi = jnp.sum(jnp.where(lane == ehi_ref[i], scores, 0.0), axis=-1, keepdims=True)
        denom = s_lo + s_hi

        def ffn(gw, uw, dw, w_col):
            g = _dot(x, gw[0, 0])
            u = _dot(x, uw[0, 0])
            return _dot((((g * jax.nn.sigmoid(g)) * u) * w_col).astype(BF16), dw[0, 0])

        y = _to_tiles(ffn(g_lo, u_lo, d_lo, s_lo / denom) + ffn(g_hi, u_hi, d_hi, s_hi / denom))
        first = start % MOE_BLK == 0

        @pl.when(first)
        def _():
            ys_ref[...] = y

        @pl.when(jnp.logical_not(first))
        def _():
            slot = blk_ref[i] * MOE_BLK \
                + lax.broadcasted_iota(I32, (MOE_BLK * ROW_TILES, 1), 0) // ROW_TILES
            ys_ref[...] = jnp.where((slot >= start) & (slot < end), y, ys_ref[...])


def _experts(xs_z, items, wr_pad, wg, wu, wd, layer):
    blk, elo, ehi, start, end = items
    n_items = blk.shape[0]
    n_slots = xs_z.shape[0] // SUBLANES

    def data_map(i, blk, elo, ehi, start, end):
        return (blk[i], 0)

    def w_lo_map(i, blk, elo, ehi, start, end):
        return (layer, elo[i], 0, 0)

    def w_hi_map(i, blk, elo, ehi, start, end):
        return (layer, ehi[i], 0, 0)

    up_spec = lambda m: pl.BlockSpec((1, 1, D_MODEL, D_EXPERT), m)
    down_spec = lambda m: pl.BlockSpec((1, 1, D_EXPERT, D_MODEL), m)
    return pl.pallas_call(
        _expert_body,
        grid_spec=pltpu.PrefetchScalarGridSpec(
            num_scalar_prefetch=5,
            grid=(n_items,),
            in_specs=[pl.BlockSpec((MOE_BLK * SUBLANES, LANES), data_map),
                      pl.BlockSpec((D_MODEL, LANES), lambda i, *_: (0, 0)),
                      up_spec(w_lo_map), up_spec(w_lo_map), down_spec(w_lo_map),
                      up_spec(w_hi_map), up_spec(w_hi_map), down_spec(w_hi_map)],
            out_specs=pl.BlockSpec((MOE_BLK * SUBLANES, LANES), data_map)),
        out_shape=jax.ShapeDtypeStruct((n_slots * SUBLANES, LANES), F32),
        compiler_params=_params(("arbitrary",), 48),
        name="experts",
    )(blk, elo, ehi, start, end, xs_z, wr_pad, wg, wu, wd, wg, wu, wd)


def _ln2_body(dest_ref, x1_ref, ys_hbm, g_ref, b_ref, o_ref, buf, sem, *, n_steps, tile):
    i = pl.program_id(0)
    slot = i % 2

    def gather(step, into):
        def issue(j, carry):
            _tile_copy(ys_hbm, buf.at[into], sem.at[into], dest_ref[step * tile + j], j, 1).start()
            return carry

        lax.fori_loop(0, tile, issue, 0, unroll=8)

    @pl.when(i == 0)
    def _():
        gather(0, 0)

    @pl.when(i + 1 < n_steps)
    def _():
        gather(i + 1, (i + 1) % 2)

    pltpu.make_async_copy(ys_hbm.at[pl.ds(0, tile * SUBLANES), :], buf.at[slot], sem.at[slot]).wait()
    h = ALPHA * _from_tiles(x1_ref[...], tile) + _from_tiles(buf[slot], tile)
    o_ref[...] = _layer_norm(h, g_ref[...], b_ref[...])


def _ln2(x1_z, ys_z, dest, g_row, b_row, tile):
    n = x1_z.shape[0] // SUBLANES
    nt = n // tile
    tok = lambda i, d: (i, 0)
    fixed = lambda i, d: (0, 0)
    return pl.pallas_call(
        functools.partial(_ln2_body, n_steps=nt, tile=tile),
        grid_spec=pltpu.PrefetchScalarGridSpec(
            num_scalar_prefetch=1,
            grid=(nt,),
            in_specs=[pl.BlockSpec((tile * SUBLANES, LANES), tok), pl.BlockSpec(memory_space=pl.ANY),
                      pl.BlockSpec((1, D_MODEL), fixed), pl.BlockSpec((1, D_MODEL), fixed)],
            out_specs=pl.BlockSpec((tile, D_MODEL), tok),
            scratch_shapes=[pltpu.VMEM((2, tile * SUBLANES, LANES), F32),
                            pltpu.SemaphoreType.DMA((2,))]),
        out_shape=jax.ShapeDtypeStruct((n, D_MODEL), F32),
        compiler_params=_params(("arbitrary",), 48),
        name="gather_ln2",
    )(dest, x1_z, ys_z, g_row, b_row)


def _t5_bucket(dist):
    max_exact = N_BUCKETS // 2
    d = jnp.maximum(dist, 0)
    large = max_exact + (jnp.log(jnp.maximum(d, 1).astype(F32) / max_exact)
                         / math.log(MAX_DISTANCE / max_exact) * (N_BUCKETS - max_exact)).astype(I32)
    large = jnp.minimum(large, N_BUCKETS - 1)
    return jnp.where(d < max_exact, d, large)


def _banded_bias(rel_bias):
    i = jnp.arange(ATTN_BLOCK)[:, None]
    j = jnp.arange(2 * ATTN_BLOCK)[None, :]
    dist = i + ATTN_BLOCK - j
    bucket = _t5_bucket(dist)[None]
    table = rel_bias.astype(F32)
    bias = jnp.zeros((SWA_HEADS, ATTN_BLOCK, 2 * ATTN_BLOCK), F32)
    for b in range(N_BUCKETS):
        bias = jnp.where(bucket == b, table[b][:, None, None], bias)
    in_window = (dist >= 0) & (dist < WINDOW)
    normal = jnp.where(in_window[None], bias, -jnp.inf)
    first = jnp.where((j >= ATTN_BLOCK)[None], normal, -jnp.inf)
    return jnp.stack([first, normal])


def _work_items(counts, n_tok):
    n_blocks = n_tok // MOE_BLK
    cend = jnp.cumsum(counts)
    cstart = cend - counts
    blk_starts = jnp.arange(n_blocks, dtype=I32) * MOE_BLK
    cls_starts = jnp.where(counts > 0, cstart, n_tok)
    start = jnp.sort(jnp.concatenate([blk_starts, cls_starts]))
    end = jnp.concatenate([start[1:], jnp.full((1,), n_tok, I32)])
    blk = jnp.minimum(start, n_tok - 1) // MOE_BLK
    cls = jnp.minimum(jnp.sum((cend[None, :] <= start[:, None]).astype(I32), axis=1), N_CLASSES - 1)
    group, pair = cls // N_PAIRS, cls % N_PAIRS
    lo = (pair >= 3).astype(I32) + (pair >= 5).astype(I32)
    hi = jnp.where(pair < 3, pair + 1, jnp.where(pair < 5, pair - 1, 3))
    return cstart, (blk, group * EXPERTS_PER_GROUP + lo, group * EXPERTS_PER_GROUP + hi, start, end)


def kernel(x, mem, w_in, b_in, w_mem_kv, attn_sinks, rel_bias, hgrn_lb_logits, hgrn_norm, w_out,
           ln1_g, ln1_b, w_router, router_bias, w_gate, w_up, w_down, ln2_g, ln2_b):
    batch, seq, _ = x.shape
    mem_len = mem.shape[1]
    n_tok = batch * seq
    route_tile = min(512, n_tok)

    bias = _banded_bias(rel_bias)
    lb = jnp.cumsum(jax.nn.softmax(hgrn_lb_logits.astype(F32), axis=0), axis=0)
    lb = lb - lb[0:1]
    log_lb = jnp.log(lb)
    log_1m_lb = jnp.log1p(-lb)
    head_ones = (jnp.arange(HG_WIDTH)[:, None] // HG_DV == jnp.arange(HG_WIDTH)[None, :] // HG_DV).astype(BF16)
    tri = (jnp.arange(route_tile)[:, None] <= jnp.arange(route_tile)[None, :]).astype(BF16)
    wr_t = jnp.transpose(w_router).astype(BF16)
    wr_pad = jnp.pad(w_router.astype(BF16), ((0, 0), (0, LANES - N_EXPERTS)))
    rb_col = router_bias.astype(F32).reshape(N_EXPERTS, 1)
    w_in_bf = w_in.astype(BF16)
    w_out_bf = w_out.astype(BF16)
    wg_bf, wu_bf, wd_bf = w_gate.astype(BF16), w_up.astype(BF16), w_down.astype(BF16)

    mk, mv = _mem_kv(mem.reshape(batch * mem_len, D_MODEL), w_mem_kv)

    x2 = x.reshape(n_tok, D_MODEL)
    for l in range(DEPTH):
        sq, sk, sv, hq, hf, hi, hg, mq = _in_proj(x2, w_in_bf[l], b_in[l].reshape(1, IN_WIDTH))
        swa_o = _swa(sq, sk, sv, attn_sinks[l].astype(F32), bias, batch, seq)
        hg_o = _hgrn(hq, hf, hi, hg, log_lb[l].reshape(1, HG_WIDTH), log_1m_lb[l].reshape(1, HG_WIDTH),
                     jnp.tile(hgrn_norm[l].astype(F32), HG_HEADS).reshape(1, HG_WIDTH), head_ones, batch, seq)
        mem_o = _mem_attn(mq, mk, mv, l, batch, seq, mem_len)
        x1_z, cls, rank, counts = _out_proj(
            x2, swa_o, hg_o, mem_o, w_out_bf[l], ln1_g[l].reshape(1, D_MODEL), ln1_b[l].reshape(1, D_MODEL),
            wr_t, rb_col, tri)
        cstart, items = _work_items(counts[:N_CLASSES, 0].astype(I32), n_tok)
        dest, xs_z = _scatter_rows(x1_z, cls, rank, cstart)
        ys_z = _experts(xs_z, items, wr_pad, wg_bf, wu_bf, wd_bf, l)
        x2 = _ln2(x1_z, ys_z, dest.reshape(n_tok), ln2_g[l].reshape(1, D_MODEL), ln2_b[l].reshape(1, D_MODEL),
                  route_tile)
    return x2.reshape(batch, seq, D_MODEL)
```

```python
import functools
import math

import numpy as np
import jax
import jax.numpy as jnp
from jax import lax
from jax.experimental import pallas as pl
from jax.experimental.pallas import tpu as pltpu

F32 = jnp.float32
BF16 = jnp.bfloat16
I32 = jnp.int32

D_MODEL = 1024
DEPTH = 4
HEAD_DIM = 64
SWA_HEADS = 8
SWA_KV_HEADS = 2
SWA_GROUP = SWA_HEADS // SWA_KV_HEADS
WINDOW = 128
ATTN_BLOCK = 128
HG_HEADS = 4
HG_DK = 64
HG_DV = 64
HG_WIDTH = HG_HEADS * HG_DK
MEM_HEADS = 4
N_BUCKETS = 32
MAX_DISTANCE = 128
N_EXPERTS = 16
N_GROUPS = 4
EXPERTS_PER_GROUP = 4
D_EXPERT = 512
LN_EPS = 1e-5
RMS_EPS = 1e-6
ALPHA = (2 * DEPTH) ** 0.25
SPLITS = (512, 128, 128, 256, 256, 256, 256, 256)
IN_WIDTH = sum(SPLITS)
MIX_WIDTH = 1024
ATTN_SCALE = HEAD_DIM ** -0.5

SUBLANES = 8
LANES = 128
ROW_TILES = D_MODEL // LANES

SWA_STEP_BLOCKS = 4
HG_CHUNK = 128
HG_SUB = 32
HG_SAFE_DECAY = 80.0
N_PAIRS = 6
N_CLASSES = N_GROUPS * N_PAIRS
CLASS_ROWS = 32
MOE_BLK = 256
DMA_GROUP = 8


def _params(semantics, vmem_mib):
    return pltpu.CompilerParams(dimension_semantics=semantics, vmem_limit_bytes=vmem_mib * 1024 * 1024)


def _dot(a, b):
    return jnp.dot(a, b, preferred_element_type=F32)


def _dot_nt(a, b):
    return lax.dot_general(a, b, (((1,), (1,)), ((), ())), preferred_element_type=F32)


def _dot_tn(a, b):
    return lax.dot_general(a, b, (((0,), (0,)), ((), ())), preferred_element_type=F32)


def _from_tiles(z, n_rows):
    k = z.shape[0] // n_rows
    return z.reshape(n_rows, k, LANES).reshape(n_rows, k * LANES)


def _to_tiles(val):
    n_rows, width = val.shape
    return val.reshape(n_rows, width // LANES, LANES).reshape(n_rows * (width // LANES), LANES)


def _in_proj_body(x_ref, w_ref, b_ref, *out_refs):
    acc = _dot(x_ref[...].astype(BF16), w_ref[...]) + b_ref[...]
    off = 0
    for ref, width in zip(out_refs, SPLITS):
        ref[...] = acc[:, off:off + width].astype(ref.dtype)
        off += width


def _in_proj(x2, w_bf, b_row):
    n = x2.shape[0]
    tm = min(512, n)
    dtypes = (BF16, BF16, BF16, F32, F32, F32, F32, BF16)
    return pl.pallas_call(
        _in_proj_body,
        grid=(n // tm,),
        in_specs=[pl.BlockSpec((tm, D_MODEL), lambda i: (i, 0)),
                  pl.BlockSpec((D_MODEL, IN_WIDTH), lambda i: (0, 0)),
                  pl.BlockSpec((1, IN_WIDTH), lambda i: (0, 0))],
        out_specs=[pl.BlockSpec((tm, w), lambda i: (i, 0)) for w in SPLITS],
        out_shape=[jax.ShapeDtypeStruct((n, w), dt) for w, dt in zip(SPLITS, dtypes)],
        compiler_params=_params(("arbitrary",), 48),
        name="in_proj",
    )(x2, w_bf, b_row)


def _mem_kv_body(mem_ref, w_ref, k_ref, v_ref):
    acc = _dot(mem_ref[...].astype(BF16), w_ref[0].astype(BF16))
    half = MEM_HEADS * HEAD_DIM
    k_ref[0] = acc[:, :half].astype(BF16)
    v_ref[0] = acc[:, half:].astype(BF16)


def _mem_kv(mem2, w_mem_kv):
    rows = mem2.shape[0]
    tm = min(512, rows)
    half = MEM_HEADS * HEAD_DIM
    return pl.pallas_call(
        _mem_kv_body,
        grid=(DEPTH, rows // tm),
        in_specs=[pl.BlockSpec((tm, D_MODEL), lambda l, i: (i, 0)),
                  pl.BlockSpec((1, D_MODEL, 2 * half), lambda l, i: (l, 0, 0))],
        out_specs=[pl.BlockSpec((1, tm, half), lambda l, i: (l, i, 0)),
                   pl.BlockSpec((1, tm, half), lambda l, i: (l, i, 0))],
        out_shape=[jax.ShapeDtypeStruct((DEPTH, rows, half), BF16)] * 2,
        compiler_params=_params(("arbitrary", "arbitrary"), 32),
        name="mem_kv",
    )(mem2, w_mem_kv)


def _swa_body(sink_ref, q_ref, kp_ref, ko_ref, vp_ref, vo_ref, bias_ref, o_ref, *, blocks):
    n = pl.program_id(1)
    q_all = q_ref[...] * ATTN_SCALE
    k_all = jnp.concatenate([kp_ref[...], ko_ref[...]], axis=0)
    v_all = jnp.concatenate([vp_ref[...], vo_ref[...]], axis=0)
    from_prev = lax.broadcasted_iota(I32, (ATTN_BLOCK, ATTN_BLOCK), 0) > \
        lax.broadcasted_iota(I32, (ATTN_BLOCK, ATTN_BLOCK), 1)
    for j in range(blocks):
        lo = j * ATTN_BLOCK
        kk = k_all[lo:lo + 2 * ATTN_BLOCK]
        vv = v_all[lo:lo + 2 * ATTN_BLOCK]
        table = jnp.where(n > 0, 1, 0) if j == 0 else 1
        outs = []
        for h in range(SWA_HEADS):
            g = h // SWA_GROUP
            q_h = q_all[lo:lo + ATTN_BLOCK, h * HEAD_DIM:(h + 1) * HEAD_DIM]
            scores = _dot_nt(kk[:, g * HEAD_DIM:(g + 1) * HEAD_DIM], q_h)
            band = jnp.where(from_prev, scores[:ATTN_BLOCK], scores[ATTN_BLOCK:]) + bias_ref[table, h]
            sink = sink_ref[h]
            m = jnp.maximum(jnp.max(band, axis=0, keepdims=True), sink)
            e = jnp.exp(band - m)
            p = e * (1.0 / (jnp.sum(e, axis=0, keepdims=True) + jnp.exp(sink - m)))
            p = jnp.concatenate([jnp.where(from_prev, p, 0.0), jnp.where(from_prev, 0.0, p)], axis=0)
            outs.append(_dot_tn(p.astype(BF16), vv[:, g * HEAD_DIM:(g + 1) * HEAD_DIM]))
        o_ref[lo:lo + ATTN_BLOCK, :] = jnp.concatenate(outs, axis=-1).astype(o_ref.dtype)


def _swa(sq, sk, sv, sinks, bias2, batch, seq):
    nb = seq // ATTN_BLOCK
    blocks = math.gcd(SWA_STEP_BLOCKS, nb)
    ns = nb // blocks
    own = lambda b, n: (b * ns + n, 0)
    prev = lambda b, n: (b * nb + jnp.maximum(n * blocks - 1, 0), 0)
    kvw = SWA_KV_HEADS * HEAD_DIM
    qw = SWA_HEADS * HEAD_DIM
    rows = blocks * ATTN_BLOCK
    return pl.pallas_call(
        functools.partial(_swa_body, blocks=blocks),
        grid=(batch, ns),
        in_specs=[pl.BlockSpec(memory_space=pltpu.SMEM),
                  pl.BlockSpec((rows, qw), own),
                  pl.BlockSpec((ATTN_BLOCK, kvw), prev),
                  pl.BlockSpec((rows, kvw), own),
                  pl.BlockSpec((ATTN_BLOCK, kvw), prev),
                  pl.BlockSpec((rows, kvw), own),
                  pl.BlockSpec((2, SWA_HEADS, ATTN_BLOCK, ATTN_BLOCK), lambda b, n: (0, 0, 0, 0))],
        out_specs=pl.BlockSpec((rows, qw), own),
        out_shape=jax.ShapeDtypeStruct((batch * seq, qw), BF16),
        compiler_params=_params(("arbitrary", "arbitrary"), 32),
        name="swa",
    )(sinks, sq, sk, sk, sv, sv, bias2)


def _cumsum_rows(x):
    n = x.shape[0]
    row = lax.broadcasted_iota(I32, (n, 1), 0)
    s = 1
    while s < n:
        x = x + jnp.where(row >= s, pltpu.roll(x, s, 0), 0.0)
        s *= 2
    return x


def _rows_from_blocks(vals, width):
    return jnp.concatenate([jnp.broadcast_to(v, (HG_SUB, width)) for v in vals], axis=0)


def _hgrn_body(hq_ref, hf_ref, hi_ref, hg_ref, loglb_ref, log1mlb_ref, nw_ref, bo_ref, o_ref,
               st_ref, a_scr, q_scr, k_scr, oi_scr):
    c = pl.program_id(1)
    n_sub = HG_CHUNK // HG_SUB
    width = HG_WIDTH

    @pl.when(c == 0)
    def _():
        st_ref[...] = jnp.zeros_like(st_ref)

    z = hf_ref[...]
    log_sig = jnp.minimum(z, 0.0) - jnp.log1p(jnp.exp(-jnp.abs(z)))
    t_a = loglb_ref[...]
    t_b = log1mlb_ref[...] + log_sig
    log_f = jnp.maximum(t_a, t_b) + jnp.log1p(jnp.exp(-jnp.abs(t_a - t_b)))
    kk = jnp.exp(t_b - z)
    hq = hq_ref[...]
    qq = hq * jax.nn.sigmoid(hq) * (HG_DK ** -0.5)
    vv = hi_ref[...]

    a_cum = _cumsum_rows(log_f)
    ends = [a_cum[HG_SUB * j + HG_SUB - 1:HG_SUB * j + HG_SUB, :] for j in range(n_sub)]
    zero_row = jnp.zeros((1, width), F32)
    starts = [zero_row] + ends[:-1]
    a_loc = a_cum - _rows_from_blocks(starts, width)
    e_loc = _rows_from_blocks(ends, width) - a_cum
    worst = starts[0] - ends[0]
    for j in range(1, n_sub):
        worst = jnp.maximum(worst, starts[j] - ends[j])
    unsafe = jnp.max(worst) > HG_SAFE_DECAY

    q_sub = qq * jnp.exp(a_loc)
    k_sub = kk * jnp.exp(-a_loc)
    k_end = kk * jnp.exp(e_loc)
    one_row = jnp.ones((1, width), F32)
    q_dec = q_sub * _rows_from_blocks([jnp.exp(s) for s in starts], width)
    k_dec = k_end * _rows_from_blocks([jnp.exp(ends[-1] - e) for e in ends], width)
    q_top = q_sub * _rows_from_blocks(
        [zero_row, zero_row, one_row, jnp.exp(ends[2] - ends[1])], width)
    k_top = k_end * _rows_from_blocks(
        [jnp.exp(ends[1] - ends[0]), one_row, zero_row, zero_row], width)

    row = lax.broadcasted_iota(I32, (HG_CHUNK, 1), 0)
    sub_of_row = row // HG_SUB
    t_idx = lax.broadcasted_iota(I32, (HG_CHUNK, HG_CHUNK), 0)
    s_idx = lax.broadcasted_iota(I32, (HG_CHUNK, HG_CHUNK), 1)
    m_sub = ((t_idx // HG_SUB) == (s_idx // HG_SUB)) & (t_idx >= s_idx)
    m_sub2 = jnp.concatenate([m_sub, m_sub], axis=0)
    lane = lax.broadcasted_iota(I32, (1, LANES), 1)
    lo = lane < HG_DK
    bd = (lax.broadcasted_iota(I32, (LANES, LANES), 0) // HG_DV) == \
         (lax.broadcasted_iota(I32, (LANES, LANES), 1) // HG_DK)
    dec_row = jnp.exp(ends[-1])

    def heads_on_rows(x):
        return jnp.concatenate([jnp.where(lo, x, 0.0), jnp.where(lo, 0.0, x)], axis=0)

    o_inter = []
    for p in range(width // LANES):
        sl = slice(p * LANES, (p + 1) * LANES)
        qs, ks, ke = q_sub[:, sl], k_sub[:, sl], k_end[:, sl]
        p_sub = _dot_nt(heads_on_rows(qs).astype(BF16), ks.astype(BF16))
        q_x = jnp.concatenate([jnp.where(sub_of_row == 1, qs, 0.0),
                               jnp.where(sub_of_row == 3, qs, 0.0), q_top[:, sl]], axis=1)
        k_x = jnp.concatenate([jnp.where(sub_of_row == 0, ke, 0.0),
                               jnp.where(sub_of_row == 2, ke, 0.0), k_top[:, sl]], axis=1)
        q_x2 = jnp.concatenate([jnp.where(jnp.tile(lo, (1, 3)), q_x, 0.0),
                                jnp.where(jnp.tile(lo, (1, 3)), 0.0, q_x)], axis=0)
        p_x = _dot_nt(q_x2.astype(BF16), k_x.astype(BF16))
        p_all = jnp.where(m_sub2, p_sub, 0.0) + p_x
        p_cat = jnp.concatenate([p_all[:HG_CHUNK], p_all[HG_CHUNK:]], axis=1)
        v_p = vv[:, sl]
        oi_scr[:, sl] = _dot(p_cat.astype(BF16), heads_on_rows(v_p).astype(BF16))
        st = st_ref[p]
        o_inter.append(_dot_nt(q_dec[:, sl].astype(BF16), st.astype(BF16)))
        upd = _dot_tn(v_p.astype(BF16), k_dec[:, sl].astype(BF16))
        st_ref[p] = st * dec_row[:, sl] + jnp.where(bd, upd, 0.0)
    o_inter = jnp.concatenate(o_inter, axis=1)

    @pl.when(unsafe)
    def _():
        a_scr[...] = a_cum
        q_scr[...] = qq
        k_scr[...] = kk

        def body(t, carry):
            d = a_scr[pl.ds(t, 1), :] - a_scr[...]
            w = jnp.exp(jnp.where(row <= t, d, -jnp.inf))
            prod = (q_scr[pl.ds(t, 1), :] * k_scr[...]) * w
            e = _dot(prod.astype(BF16), bo_ref[...])
            oi_scr[pl.ds(t, 1), :] = jnp.sum(e * hi_ref[...], axis=0, keepdims=True)
            return carry

        lax.fori_loop(0, HG_CHUNK, body, 0)

    o = oi_scr[...] + o_inter
    sq = o * o
    sq_hi = sq.astype(BF16)
    sq_lo = (sq - sq_hi.astype(F32)).astype(BF16)
    ms = (_dot(sq_hi, bo_ref[...]) + _dot(sq_lo, bo_ref[...])) * (1.0 / HG_DV)
    gate = hg_ref[...]
    o = o * lax.rsqrt(ms + RMS_EPS) * nw_ref[...] * (gate * jax.nn.sigmoid(gate))
    o_ref[...] = o.astype(o_ref.dtype)


def _hgrn(hq, hf, hi, hg, loglb, log1mlb, nw_row, head_ones, batch, seq):
    nc = seq // HG_CHUNK
    blk = lambda b, c: (b * nc + c, 0)
    row = lambda b, c: (0, 0)
    w = HG_WIDTH
    return pl.pallas_call(
        _hgrn_body,
        grid=(batch, nc),
        in_specs=[pl.BlockSpec((HG_CHUNK, w), blk)] * 4 + [pl.BlockSpec((1, w), row)] * 3
                 + [pl.BlockSpec((w, w), row)],
        out_specs=pl.BlockSpec((HG_CHUNK, w), blk),
        out_shape=jax.ShapeDtypeStruct((batch * seq, w), BF16),
        scratch_shapes=[pltpu.VMEM((w // LANES, LANES, LANES), F32)]
                       + [pltpu.VMEM((HG_CHUNK, w), F32)] * 4,
        compiler_params=_params(("arbitrary", "arbitrary"), 32),
        name="hgrn",
    )(hq, hf, hi, hg, loglb, log1mlb, nw_row, head_ones)


def _mem_attn_body(q_ref, k_ref, v_ref, o_ref):
    q_all = q_ref[...] * ATTN_SCALE
    outs = []
    for h in range(MEM_HEADS):
        sl = slice(h * HEAD_DIM, (h + 1) * HEAD_DIM)
        scores = _dot_nt(k_ref[0, :, sl], q_all[:, sl])
        e = jnp.exp(scores - jnp.max(scores, axis=0, keepdims=True))
        p = e * (1.0 / jnp.sum(e, axis=0, keepdims=True))
        outs.append(_dot_tn(p.astype(BF16), v_ref[0, :, sl]))
    o_ref[...] = jnp.concatenate(outs, axis=-1).astype(o_ref.dtype)


def _mem_attn(mq, mk, mv, layer, batch, seq, mem_len):
    tq = min(512, seq)
    nq = seq // tq
    w = MEM_HEADS * HEAD_DIM
    return pl.pallas_call(
        _mem_attn_body,
        grid=(batch, nq),
        in_specs=[pl.BlockSpec((tq, w), lambda b, i: (b * nq + i, 0)),
                  pl.BlockSpec((1, mem_len, w), lambda b, i: (layer, b, 0)),
                  pl.BlockSpec((1, mem_len, w), lambda b, i: (layer, b, 0))],
        out_specs=pl.BlockSpec((tq, w), lambda b, i: (b * nq + i, 0)),
        out_shape=jax.ShapeDtypeStruct((batch * seq, w), BF16),
        compiler_params=_params(("arbitrary", "arbitrary"), 32),
        name="mem_attn",
    )(mq, mk, mv)


def _layer_norm(h, g, b):
    mu = jnp.mean(h, axis=-1, keepdims=True)
    d = h - mu
    var = jnp.mean(d * d, axis=-1, keepdims=True)
    return d * lax.rsqrt(var + LN_EPS) * g + b


def _route_rows(sel):
    def row(a, r):
        return a[r:r + 1, :]

    best = None
    for g in range(N_GROUPS):
        a = [row(sel, 4 * g + i) for i in range(4)]
        gs = a[0] + a[1]
        for i, j in ((0, 2), (0, 3), (1, 2), (1, 3), (2, 3)):
            gs = jnp.maximum(gs, a[i] + a[j])
        if best is None:
            best, best_score = jnp.zeros_like(gs, dtype=I32), gs
        else:
            better = gs > best_score
            best = jnp.where(better, g, best)
            best_score = jnp.where(better, gs, best_score)

    def pick(arr, i):
        out = row(arr, i)
        for g in range(1, N_GROUPS):
            out = jnp.where(best == g, row(arr, 4 * g + i), out)
        return out

    a = [pick(sel, i) for i in range(4)]
    i1, m1 = jnp.zeros_like(best), a[0]
    for i in range(1, 4):
        gt = a[i] > m1
        i1 = jnp.where(gt, i, i1)
        m1 = jnp.where(gt, a[i], m1)
    i2 = jnp.full_like(best, -1)
    m2 = jnp.full_like(m1, -jnp.inf)
    for i in range(4):
        gt = (i1 != i) & (a[i] > m2)
        i2 = jnp.where(gt, i, i2)
        m2 = jnp.where(gt, a[i], m2)
    lo = jnp.minimum(i1, i2)
    hi = jnp.maximum(i1, i2)
    pair = jnp.where(lo == 0, hi - 1, jnp.where(lo == 1, hi + 1, 5))
    return best * N_PAIRS + pair


def _out_proj_body(x_ref, swa_ref, hg_ref, mem_ref, w_ref, g_ref, b_ref, wr_ref, rb_ref, tri_ref,
                   x1_ref, cls_ref, rank_ref, cnt_ref, carry_ref):
    i = pl.program_id(0)
    t = x_ref.shape[0]

    @pl.when(i == 0)
    def _():
        carry_ref[...] = jnp.zeros_like(carry_ref)

    acc = _dot(swa_ref[...], w_ref[0:512, :]) + _dot(hg_ref[...], w_ref[512:768, :]) \
        + _dot(mem_ref[...], w_ref[768:1024, :])
    x1 = _layer_norm(ALPHA * x_ref[...] + acc, g_ref[...], b_ref[...])
    x1_ref[...] = _to_tiles(x1)

    logits = _dot_nt(wr_ref[...], x1.astype(BF16))
    cls = _route_rows(jax.nn.sigmoid(logits) + rb_ref[...])
    onehot = (lax.broadcasted_iota(I32, (CLASS_ROWS, t), 0) == cls).astype(F32)
    prefix = _dot(onehot.astype(BF16), tri_ref[...])
    carry = carry_ref[...]
    rank = jnp.sum(onehot * (prefix - 1.0 + carry[:, 0:1]), axis=0, keepdims=True)
    new_carry = carry + prefix[:, t - 1:t]
    carry_ref[...] = new_carry
    cnt_ref[...] = new_carry
    cls_ref[0] = cls
    rank_ref[0] = rank.astype(I32)


def _out_proj(x2, swa_o, hg_o, mem_o, w_bf, g_row, b_row, wr_t, rb_col, tri):
    n = x2.shape[0]
    t = tri.shape[0]
    nt = n // t
    tok = lambda i: (i, 0)
    fixed = lambda i: (0, 0)
    per_tile = pl.BlockSpec((1, 1, t), lambda i: (i, 0, 0))
    return pl.pallas_call(
        _out_proj_body,
        grid=(nt,),
        in_specs=[pl.BlockSpec((t, D_MODEL), tok), pl.BlockSpec((t, 512), tok),
                  pl.BlockSpec((t, 256), tok), pl.BlockSpec((t, 256), tok),
                  pl.BlockSpec((MIX_WIDTH, D_MODEL), fixed),
                  pl.BlockSpec((1, D_MODEL), fixed), pl.BlockSpec((1, D_MODEL), fixed),
                  pl.BlockSpec((N_EXPERTS, D_MODEL), fixed), pl.BlockSpec((N_EXPERTS, 1), fixed),
                  pl.BlockSpec((t, t), fixed)],
        out_specs=[pl.BlockSpec((t * ROW_TILES, LANES), tok), per_tile, per_tile,
                   pl.BlockSpec((CLASS_ROWS, LANES), fixed)],
        out_shape=[jax.ShapeDtypeStruct((n * ROW_TILES, LANES), F32),
                   jax.ShapeDtypeStruct((nt, 1, t), I32), jax.ShapeDtypeStruct((nt, 1, t), I32),
                   jax.ShapeDtypeStruct((CLASS_ROWS, LANES), F32)],
        scratch_shapes=[pltpu.VMEM((CLASS_ROWS, LANES), F32)],
        compiler_params=_params(("arbitrary",), 48),
        name="out_proj_ln_route",
    )(x2, swa_o, hg_o, mem_o, w_bf, g_row, b_row, wr_t, rb_col, tri)


def _tile_copy(src_ref, dst_ref, sem, src_row, dst_row, tiles):
    n = tiles * SUBLANES
    return pltpu.make_async_copy(
        src_ref.at[pl.ds(pl.multiple_of(src_row * n, n), n), :],
        dst_ref.at[pl.ds(pl.multiple_of(dst_row * n, n), n), :], sem)


def _scatter_body(cstart_ref, cls_ref, rank_ref, x_hbm, dest_ref, xs_hbm,
                  buf, dest_vmem, dest_smem, load_sem, scat_sem, misc_sem, *, n_steps, tile):
    i = pl.program_id(0)
    rows = tile * SUBLANES
    slot = i % 3

    def load(step, into):
        return pltpu.make_async_copy(
            x_hbm.at[pl.ds(pl.multiple_of(step * rows, rows), rows), :], buf.at[into], load_sem.at[into])

    def scatter_done(of):
        return pltpu.make_async_copy(buf.at[of], xs_hbm.at[pl.ds(0, rows), :], scat_sem.at[of])

    @pl.when(i == 0)
    def _():
        load(0, 0).start()

    @pl.when(i + 1 < n_steps)
    def _():
        load(i + 1, (i + 1) % 3).start()

    cls = cls_ref[0]
    dest = rank_ref[0]
    for c in range(N_CLASSES):
        dest = dest + jnp.where(cls == c, cstart_ref[c], 0)
    dest_ref[0] = dest
    dest_vmem[...] = dest
    to_smem = pltpu.make_async_copy(dest_vmem, dest_smem, misc_sem)
    to_smem.start()
    to_smem.wait()

    load(i, slot).wait()

    def issue(group, carry):
        for k in range(DMA_GROUP):
            j = group * DMA_GROUP + k
            _tile_copy(buf.at[slot], xs_hbm, scat_sem.at[slot], j, dest_smem[0, j], 1).start(priority=k % 2)
        return carry

    lax.fori_loop(0, tile // DMA_GROUP, issue, 0)

    @pl.when(i >= 1)
    def _():
        scatter_done((i + 2) % 3).wait()

    @pl.when(i == n_steps - 1)
    def _():
        scatter_done(slot).wait()


def _scatter_rows(x1_z, cls, rank, cstart):
    nt, _, tile = cls.shape
    n = nt * tile
    per_tile = pl.BlockSpec((1, 1, tile), lambda i, cs: (i, 0, 0))
    return pl.pallas_call(
        functools.partial(_scatter_body, n_steps=nt, tile=tile),
        grid_spec=pltpu.PrefetchScalarGridSpec(
            num_scalar_prefetch=1,
            grid=(nt,),
            in_specs=[per_tile, per_tile, pl.BlockSpec(memory_space=pl.ANY)],
            out_specs=[per_tile, pl.BlockSpec(memory_space=pl.ANY)],
            scratch_shapes=[pltpu.VMEM((3, tile * SUBLANES, LANES), F32),
                            pltpu.VMEM((1, tile), I32), pltpu.SMEM((1, tile), I32),
                            pltpu.SemaphoreType.DMA((3,)), pltpu.SemaphoreType.DMA((3,)),
                            pltpu.SemaphoreType.DMA(())]),
        out_shape=[jax.ShapeDtypeStruct((nt, 1, tile), I32),
                   jax.ShapeDtypeStruct((n * SUBLANES, LANES), F32)],
        compiler_params=_params(("arbitrary",), 32),
        name="scatter_rows",
    )(cstart, cls, rank, x1_z)


def _expert_body(blk_ref, elo_ref, ehi_ref, start_ref, end_ref, xs_ref, wr_ref,
                 g_lo, u_lo, d_lo, g_hi, u_hi, d_hi, ys_ref):
    i = pl.program_id(0)
    start = start_ref[i]
    end = end_ref[i]

    @pl.when(end > start)
    def _():
        x = _from_tiles(xs_ref[...], MOE_BLK).astype(BF16)
        scores = jax.nn.sigmoid(_dot(x, wr_ref[...]))
        lane = lax.broadcasted_iota(I32, (1, LANES), 1)
        s_lo = jnp.sum(jnp.where(lane == elo_ref[i], scores, 0.0), axis=-1, keepdims=True)
        s_hi = jnp.sum(jnp.where(lane == ehi_ref[i], scores, 0.0), axis=-1, keepdims=True)
        denom = s_lo + s_hi

        def ffn(gw, uw, dw, w_col):
            g = _dot(x, gw[0, 0])
            u = _dot(x, uw[0, 0])
            return _dot((((g * jax.nn.sigmoid(g)) * u) * w_col).astype(BF16), dw[0, 0])

        y = _to_tiles(ffn(g_lo, u_lo, d_lo, s_lo / denom) + ffn(g_hi, u_hi, d_hi, s_hi / denom))
        first = start % MOE_BLK == 0

        @pl.when(first)
        def _():
            ys_ref[...] = y

        @pl.when(jnp.logical_not(first))
        def _():
            slot = blk_ref[i] * MOE_BLK \
                + lax.broadcasted_iota(I32, (MOE_BLK * ROW_TILES, 1), 0) // ROW_TILES
            ys_ref[...] = jnp.where((slot >= start) & (slot < end), y, ys_ref[...])


def _experts(xs_z, items, wr_pad, wg, wu, wd, layer):
    blk, elo, ehi, start, end = items
    n_items = blk.shape[0]
    n_slots = xs_z.shape[0] // SUBLANES

    def data_map(i, blk, elo, ehi, start, end):
        return (blk[i], 0)

    def w_lo_map(i, blk, elo, ehi, start, end):
        return (layer, elo[i], 0, 0)

    def w_hi_map(i, blk, elo, ehi, start, end):
        return (layer, ehi[i], 0, 0)

    up_spec = lambda m: pl.BlockSpec((1, 1, D_MODEL, D_EXPERT), m)
    down_spec = lambda m: pl.BlockSpec((1, 1, D_EXPERT, D_MODEL), m)
    return pl.pallas_call(
        _expert_body,
        grid_spec=pltpu.PrefetchScalarGridSpec(
            num_scalar_prefetch=5,
            grid=(n_items,),
            in_specs=[pl.BlockSpec((MOE_BLK * SUBLANES, LANES), data_map),
                      pl.BlockSpec((D_MODEL, LANES), lambda i, *_: (0, 0)),
                      up_spec(w_lo_map), up_spec(w_lo_map), down_spec(w_lo_map),
                      up_spec(w_hi_map), up_spec(w_hi_map), down_spec(w_hi_map)],
            out_specs=pl.BlockSpec((MOE_BLK * SUBLANES, LANES), data_map)),
        out_shape=jax.ShapeDtypeStruct((n_slots * SUBLANES, LANES), F32),
        compiler_params=_params(("arbitrary",), 48),
        name="experts",
    )(blk, elo, ehi, start, end, xs_z, wr_pad, wg, wu, wd, wg, wu, wd)


def _ln2_body(dest_ref, x1_ref, ys_hbm, g_ref, b_ref, o_ref, buf, sem, *, n_steps, tile):
    i = pl.program_id(0)
    slot = i % 2

    def gather(step, into):
        def issue(group, carry):
            for k in range(DMA_GROUP):
                j = group * DMA_GROUP + k
                _tile_copy(ys_hbm, buf.at[into], sem.at[into], dest_ref[step * tile + j], j, 1).start(
                    priority=k % 2)
            return carry

        lax.fori_loop(0, tile // DMA_GROUP, issue, 0)

    @pl.when(i == 0)
    def _():
        gather(0, 0)

    @pl.when(i + 1 < n_steps)
    def _():
        gather(i + 1, (i + 1) % 2)

    pltpu.make_async_copy(ys_hbm.at[pl.ds(0, tile * SUBLANES), :], buf.at[slot], sem.at[slot]).wait()
    h = ALPHA * _from_tiles(x1_ref[...], tile) + _from_tiles(buf[slot], tile)
    o_ref[...] = _layer_norm(h, g_ref[...], b_ref[...])


def _ln2(x1_z, ys_z, dest, g_row, b_row, tile):
    n = x1_z.shape[0] // SUBLANES
    nt = n // tile
    tok = lambda i, d: (i, 0)
    fixed = lambda i, d: (0, 0)
    return pl.pallas_call(
        functools.partial(_ln2_body, n_steps=nt, tile=tile),
        grid_spec=pltpu.PrefetchScalarGridSpec(
            num_scalar_prefetch=1,
            grid=(nt,),
            in_specs=[pl.BlockSpec((tile * SUBLANES, LANES), tok), pl.BlockSpec(memory_space=pl.ANY),
                      pl.BlockSpec((1, D_MODEL), fixed), pl.BlockSpec((1, D_MODEL), fixed)],
            out_specs=pl.BlockSpec((tile, D_MODEL), tok),
            scratch_shapes=[pltpu.VMEM((2, tile * SUBLANES, LANES), F32),
                            pltpu.SemaphoreType.DMA((2,))]),
        out_shape=jax.ShapeDtypeStruct((n, D_MODEL), F32),
        compiler_params=_params(("arbitrary",), 48),
        name="gather_ln2",
    )(dest, x1_z, ys_z, g_row, b_row)


def _t5_bucket(dist):
    max_exact = N_BUCKETS // 2
    d = jnp.maximum(dist, 0)
    large = max_exact + (jnp.log(jnp.maximum(d, 1).astype(F32) / max_exact)
                         / math.log(MAX_DISTANCE / max_exact) * (N_BUCKETS - max_exact)).astype(I32)
    large = jnp.minimum(large, N_BUCKETS - 1)
    return jnp.where(d < max_exact, d, large)


def _banded_bias(rel_bias):
    i = jnp.arange(ATTN_BLOCK)[:, None]
    j = jnp.arange(2 * ATTN_BLOCK)[None, :]
    dist = i + ATTN_BLOCK - j
    bucket = _t5_bucket(dist)[None]
    table = rel_bias.astype(F32)
    bias = jnp.zeros((SWA_HEADS, ATTN_BLOCK, 2 * ATTN_BLOCK), F32)
    for b in range(N_BUCKETS):
        bias = jnp.where(bucket == b, table[b][:, None, None], bias)
    from_prev = (jnp.arange(ATTN_BLOCK)[None, :] > i)[None]
    prev, own = bias[:, :, :ATTN_BLOCK], bias[:, :, ATTN_BLOCK:]
    normal = jnp.where(from_prev, prev, own)
    first = jnp.where(from_prev, -jnp.inf, own)
    return jnp.swapaxes(jnp.stack([first, normal]), 2, 3)


def _work_items(counts, n_tok):
    n_blocks = n_tok // MOE_BLK
    cend = jnp.cumsum(counts)
    cstart = cend - counts
    blk_starts = jnp.arange(n_blocks, dtype=I32) * MOE_BLK
    cls_starts = jnp.where(counts > 0, cstart, n_tok)
    start = jnp.sort(jnp.concatenate([blk_starts, cls_starts]))
    end = jnp.concatenate([start[1:], jnp.full((1,), n_tok, I32)])
    blk = jnp.minimum(start, n_tok - 1) // MOE_BLK
    cls = jnp.minimum(jnp.sum((cend[None, :] <= start[:, None]).astype(I32), axis=1), N_CLASSES - 1)
    group, pair = cls // N_PAIRS, cls % N_PAIRS
    lo = (pair >= 3).astype(I32) + (pair >= 5).astype(I32)
    hi = jnp.where(pair < 3, pair + 1, jnp.where(pair < 5, pair - 1, 3))
    return cstart, (blk, group * EXPERTS_PER_GROUP + lo, group * EXPERTS_PER_GROUP + hi, start, end)


def kernel(x, mem, w_in, b_in, w_mem_kv, attn_sinks, rel_bias, hgrn_lb_logits, hgrn_norm, w_out,
           ln1_g, ln1_b, w_router, router_bias, w_gate, w_up, w_down, ln2_g, ln2_b):
    batch, seq, _ = x.shape
    mem_len = mem.shape[1]
    n_tok = batch * seq
    route_tile = min(512, n_tok)

    bias = _banded_bias(rel_bias)
    lb = jnp.cumsum(jax.nn.softmax(hgrn_lb_logits.astype(F32), axis=0), axis=0)
    lb = lb - lb[0:1]
    log_lb = jnp.log(lb)
    log_1m_lb = jnp.log1p(-lb)
    head_ones = (jnp.arange(HG_WIDTH)[:, None] // HG_DV == jnp.arange(HG_WIDTH)[None, :] // HG_DV).astype(BF16)
    tri = (jnp.arange(route_tile)[:, None] <= jnp.arange(route_tile)[None, :]).astype(BF16)
    wr_t = jnp.transpose(w_router).astype(BF16)
    wr_pad = jnp.pad(w_router.astype(BF16), ((0, 0), (0, LANES - N_EXPERTS)))
    rb_col = router_bias.astype(F32).reshape(N_EXPERTS, 1)
    w_in_bf = w_in.astype(BF16)
    w_out_bf = w_out.astype(BF16)
    wg_bf, wu_bf, wd_bf = w_gate.astype(BF16), w_up.astype(BF16), w_down.astype(BF16)

    mk, mv = _mem_kv(mem.reshape(batch * mem_len, D_MODEL), w_mem_kv)

    x2 = x.reshape(n_tok, D_MODEL)
    for l in range(DEPTH):
        sq, sk, sv, hq, hf, hi, hg, mq = _in_proj(x2, w_in_bf[l], b_in[l].reshape(1, IN_WIDTH))
        swa_o = _swa(sq, sk, sv, attn_sinks[l].astype(F32), bias, batch, seq)
        hg_o = _hgrn(hq, hf, hi, hg, log_lb[l].reshape(1, HG_WIDTH), log_1m_lb[l].reshape(1, HG_WIDTH),
                     jnp.tile(hgrn_norm[l].astype(F32), HG_HEADS).reshape(1, HG_WIDTH), head_ones, batch, seq)
        mem_o = _mem_attn(mq, mk, mv, l, batch, seq, mem_len)
        x1_z, cls, rank, counts = _out_proj(
            x2, swa_o, hg_o, mem_o, w_out_bf[l], ln1_g[l].reshape(1, D_MODEL), ln1_b[l].reshape(1, D_MODEL),
            wr_t, rb_col, tri)
        cstart, items = _work_items(counts[:N_CLASSES, 0].astype(I32), n_tok)
        dest, xs_z = _scatter_rows(x1_z, cls, rank, cstart)
        ys_z = _experts(xs_z, items, wr_pad, wg_bf, wu_bf, wd_bf, l)
        x2 = _ln2(x1_z, ys_z, dest.reshape(n_tok), ln2_g[l].reshape(1, D_MODEL), ln2_b[l].reshape(1, D_MODEL),
                  route_tile)
    return x2.reshape(batch, seq, D_MODEL)
```

```python
import functools
import math

import numpy as np
import jax
import jax.numpy as jnp
from jax import lax
from jax.experimental import pallas as pl
from jax.experimental.pallas import tpu as pltpu

F32 = jnp.float32
BF16 = jnp.bfloat16
I32 = jnp.int32

D_MODEL = 1024
DEPTH = 4
HEAD_DIM = 64
SWA_HEADS = 8
SWA_KV_HEADS = 2
SWA_GROUP = SWA_HEADS // SWA_KV_HEADS
WINDOW = 128
ATTN_BLOCK = 128
HG_HEADS = 4
HG_DK = 64
HG_DV = 64
HG_WIDTH = HG_HEADS * HG_DK
MEM_HEADS = 4
N_BUCKETS = 32
MAX_DISTANCE = 128
N_EXPERTS = 16
N_GROUPS = 4
EXPERTS_PER_GROUP = 4
D_EXPERT = 512
LN_EPS = 1e-5
RMS_EPS = 1e-6
ALPHA = (2 * DEPTH) ** 0.25
SPLITS = (512, 128, 128, 256, 256, 256, 256, 256)
IN_WIDTH = sum(SPLITS)
MIX_WIDTH = 1024
ATTN_SCALE = HEAD_DIM ** -0.5

SUBLANES = 8
LANES = 128
ROW_TILES = D_MODEL // LANES

SWA_STEP_BLOCKS = 4
HG_CHUNK = 128
HG_SUB = 32
HG_SAFE_DECAY = 80.0
N_PAIRS = 6
N_CLASSES = N_GROUPS * N_PAIRS
CLASS_ROWS = 32
MOE_BLK = 256
DMA_GROUP = 8


def _params(semantics, vmem_mib):
    return pltpu.CompilerParams(dimension_semantics=semantics, vmem_limit_bytes=vmem_mib * 1024 * 1024)


def _dot(a, b):
    return jnp.dot(a, b, preferred_element_type=F32)


def _dot_nt(a, b):
    return lax.dot_general(a, b, (((1,), (1,)), ((), ())), preferred_element_type=F32)


def _dot_tn(a, b):
    return lax.dot_general(a, b, (((0,), (0,)), ((), ())), preferred_element_type=F32)


def _in_proj_body(x_ref, w_ref, b_ref, *out_refs):
    acc = _dot(x_ref[...].astype(BF16), w_ref[...]) + b_ref[...]
    off = 0
    for ref, width in zip(out_refs, SPLITS):
        ref[...] = acc[:, off:off + width].astype(ref.dtype)
        off += width


def _in_proj(x2, w_bf, b_row):
    n = x2.shape[0]
    tm = min(512, n)
    dtypes = (BF16, BF16, BF16, F32, F32, F32, F32, BF16)
    return pl.pallas_call(
        _in_proj_body,
        grid=(n // tm,),
        in_specs=[pl.BlockSpec((tm, D_MODEL), lambda i: (i, 0)),
                  pl.BlockSpec((D_MODEL, IN_WIDTH), lambda i: (0, 0)),
                  pl.BlockSpec((1, IN_WIDTH), lambda i: (0, 0))],
        out_specs=[pl.BlockSpec((tm, w), lambda i: (i, 0)) for w in SPLITS],
        out_shape=[jax.ShapeDtypeStruct((n, w), dt) for w, dt in zip(SPLITS, dtypes)],
        compiler_params=_params(("arbitrary",), 48),
        name="in_proj",
    )(x2, w_bf, b_row)


def _mem_kv_body(mem_ref, w_ref, k_ref, v_ref):
    acc = _dot(mem_ref[...].astype(BF16), w_ref[0].astype(BF16))
    half = MEM_HEADS * HEAD_DIM
    k_ref[0] = acc[:, :half].astype(BF16)
    v_ref[0] = acc[:, half:].astype(BF16)


def _mem_kv(mem2, w_mem_kv):
    rows = mem2.shape[0]
    tm = min(512, rows)
    half = MEM_HEADS * HEAD_DIM
    return pl.pallas_call(
        _mem_kv_body,
        grid=(DEPTH, rows // tm),
        in_specs=[pl.BlockSpec((tm, D_MODEL), lambda l, i: (i, 0)),
                  pl.BlockSpec((1, D_MODEL, 2 * half), lambda l, i: (l, 0, 0))],
        out_specs=[pl.BlockSpec((1, tm, half), lambda l, i: (l, i, 0)),
                   pl.BlockSpec((1, tm, half), lambda l, i: (l, i, 0))],
        out_shape=[jax.ShapeDtypeStruct((DEPTH, rows, half), BF16)] * 2,
        compiler_params=_params(("arbitrary", "arbitrary"), 32),
        name="mem_kv",
    )(mem2, w_mem_kv)


def _swa_body(sink_ref, q_ref, kp_ref, ko_ref, vp_ref, vo_ref, bias_ref, o_ref, *, blocks):
    n = pl.program_id(1)
    q_all = q_ref[...] * ATTN_SCALE
    k_all = jnp.concatenate([kp_ref[...], ko_ref[...]], axis=0)
    v_all = jnp.concatenate([vp_ref[...], vo_ref[...]], axis=0)
    from_prev = lax.broadcasted_iota(I32, (ATTN_BLOCK, ATTN_BLOCK), 0) > \
        lax.broadcasted_iota(I32, (ATTN_BLOCK, ATTN_BLOCK), 1)
    for j in range(blocks):
        lo = j * ATTN_BLOCK
        kk = k_all[lo:lo + 2 * ATTN_BLOCK]
        vv = v_all[lo:lo + 2 * ATTN_BLOCK]
        table = jnp.where(n > 0, 1, 0) if j == 0 else 1
        outs = []
        for h in range(SWA_HEADS):
            g = h // SWA_GROUP
            q_h = q_all[lo:lo + ATTN_BLOCK, h * HEAD_DIM:(h + 1) * HEAD_DIM]
            scores = _dot_nt(kk[:, g * HEAD_DIM:(g + 1) * HEAD_DIM], q_h)
            band = jnp.where(from_prev, scores[:ATTN_BLOCK], scores[ATTN_BLOCK:]) + bias_ref[table, h]
            sink = sink_ref[h]
            m = jnp.maximum(jnp.max(band, axis=0, keepdims=True), sink)
            e = jnp.exp(band - m)
            p = e * (1.0 / (jnp.sum(e, axis=0, keepdims=True) + jnp.exp(sink - m)))
            p = jnp.concatenate([jnp.where(from_prev, p, 0.0), jnp.where(from_prev, 0.0, p)], axis=0)
            outs.append(_dot_tn(p.astype(BF16), vv[:, g * HEAD_DIM:(g + 1) * HEAD_DIM]))
        o_ref[lo:lo + ATTN_BLOCK, :] = jnp.concatenate(outs, axis=-1).astype(o_ref.dtype)


def _swa(sq, sk, sv, sinks, bias2, batch, seq):
    nb = seq // ATTN_BLOCK
    blocks = math.gcd(SWA_STEP_BLOCKS, nb)
    ns = nb // blocks
    own = lambda b, n: (b * ns + n, 0)
    prev = lambda b, n: (b * nb + jnp.maximum(n * blocks - 1, 0), 0)
    kvw = SWA_KV_HEADS * HEAD_DIM
    qw = SWA_HEADS * HEAD_DIM
    rows = blocks * ATTN_BLOCK
    return pl.pallas_call(
        functools.partial(_swa_body, blocks=blocks),
        grid=(batch, ns),
        in_specs=[pl.BlockSpec(memory_space=pltpu.SMEM),
                  pl.BlockSpec((rows, qw), own),
                  pl.BlockSpec((ATTN_BLOCK, kvw), prev),
                  pl.BlockSpec((rows, kvw), own),
                  pl.BlockSpec((ATTN_BLOCK, kvw), prev),
                  pl.BlockSpec((rows, kvw), own),
                  pl.BlockSpec((2, SWA_HEADS, ATTN_BLOCK, ATTN_BLOCK), lambda b, n: (0, 0, 0, 0))],
        out_specs=pl.BlockSpec((rows, qw), own),
        out_shape=jax.ShapeDtypeStruct((batch * seq, qw), BF16),
        compiler_params=_params(("arbitrary", "arbitrary"), 32),
        name="swa",
    )(sinks, sq, sk, sk, sv, sv, bias2)


def _cumsum_rows(x):
    n = x.shape[0]
    row = lax.broadcasted_iota(I32, (n, 1), 0)
    s = 1
    while s < n:
        x = x + jnp.where(row >= s, pltpu.roll(x, s, 0), 0.0)
        s *= 2
    return x


def _rows_from_blocks(vals, width):
    return jnp.concatenate([jnp.broadcast_to(v, (HG_SUB, width)) for v in vals], axis=0)


def _hgrn_body(hq_ref, hf_ref, hi_ref, hg_ref, loglb_ref, log1mlb_ref, nw_ref, bo_ref, o_ref,
               st_ref, a_scr, q_scr, k_scr, oi_scr):
    c = pl.program_id(1)
    n_sub = HG_CHUNK // HG_SUB
    width = HG_WIDTH

    @pl.when(c == 0)
    def _():
        st_ref[...] = jnp.zeros_like(st_ref)

    z = hf_ref[...]
    log_sig = jnp.minimum(z, 0.0) - jnp.log1p(jnp.exp(-jnp.abs(z)))
    t_a = loglb_ref[...]
    t_b = log1mlb_ref[...] + log_sig
    log_f = jnp.maximum(t_a, t_b) + jnp.log1p(jnp.exp(-jnp.abs(t_a - t_b)))
    kk = jnp.exp(t_b - z)
    hq = hq_ref[...]
    qq = hq * jax.nn.sigmoid(hq) * (HG_DK ** -0.5)
    vv = hi_ref[...]

    a_cum = _cumsum_rows(log_f)
    ends = [a_cum[HG_SUB * j + HG_SUB - 1:HG_SUB * j + HG_SUB, :] for j in range(n_sub)]
    zero_row = jnp.zeros((1, width), F32)
    starts = [zero_row] + ends[:-1]
    a_loc = a_cum - _rows_from_blocks(starts, width)
    e_loc = _rows_from_blocks(ends, width) - a_cum
    worst = starts[0] - ends[0]
    for j in range(1, n_sub):
        worst = jnp.maximum(worst, starts[j] - ends[j])
    unsafe = jnp.max(worst) > HG_SAFE_DECAY

    q_sub = qq * jnp.exp(a_loc)
    k_sub = kk * jnp.exp(-a_loc)
    k_end = kk * jnp.exp(e_loc)
    one_row = jnp.ones((1, width), F32)
    q_dec = q_sub * _rows_from_blocks([jnp.exp(s) for s in starts], width)
    k_dec = k_end * _rows_from_blocks([jnp.exp(ends[-1] - e) for e in ends], width)
    q_top = q_sub * _rows_from_blocks(
        [zero_row, zero_row, one_row, jnp.exp(ends[2] - ends[1])], width)
    k_top = k_end * _rows_from_blocks(
        [jnp.exp(ends[1] - ends[0]), one_row, zero_row, zero_row], width)

    row = lax.broadcasted_iota(I32, (HG_CHUNK, 1), 0)
    sub_of_row = row // HG_SUB
    t_idx = lax.broadcasted_iota(I32, (HG_CHUNK, HG_CHUNK), 0)
    s_idx = lax.broadcasted_iota(I32, (HG_CHUNK, HG_CHUNK), 1)
    m_sub = ((t_idx // HG_SUB) == (s_idx // HG_SUB)) & (t_idx >= s_idx)
    m_sub2 = jnp.concatenate([m_sub, m_sub], axis=0)
    lane = lax.broadcasted_iota(I32, (1, LANES), 1)
    lo = lane < HG_DK
    bd = (lax.broadcasted_iota(I32, (LANES, LANES), 0) // HG_DV) == \
         (lax.broadcasted_iota(I32, (LANES, LANES), 1) // HG_DK)
    dec_row = jnp.exp(ends[-1])

    def heads_on_rows(x):
        return jnp.concatenate([jnp.where(lo, x, 0.0), jnp.where(lo, 0.0, x)], axis=0)

    o_inter = []
    for p in range(width // LANES):
        sl = slice(p * LANES, (p + 1) * LANES)
        qs, ks, ke = q_sub[:, sl], k_sub[:, sl], k_end[:, sl]
        p_sub = _dot_nt(heads_on_rows(qs).astype(BF16), ks.astype(BF16))
        q_x = jnp.concatenate([jnp.where(sub_of_row == 1, qs, 0.0),
                               jnp.where(sub_of_row == 3, qs, 0.0), q_top[:, sl]], axis=1)
        k_x = jnp.concatenate([jnp.where(sub_of_row == 0, ke, 0.0),
                               jnp.where(sub_of_row == 2, ke, 0.0), k_top[:, sl]], axis=1)
        q_x2 = jnp.concatenate([jnp.where(jnp.tile(lo, (1, 3)), q_x, 0.0),
                                jnp.where(jnp.tile(lo, (1, 3)), 0.0, q_x)], axis=0)
        p_x = _dot_nt(q_x2.astype(BF16), k_x.astype(BF16))
        p_all = jnp.where(m_sub2, p_sub, 0.0) + p_x
        p_cat = jnp.concatenate([p_all[:HG_CHUNK], p_all[HG_CHUNK:]], axis=1)
        v_p = vv[:, sl]
        oi_scr[:, sl] = _dot(p_cat.astype(BF16), heads_on_rows(v_p).astype(BF16))
        st = st_ref[p]
        o_inter.append(_dot_nt(q_dec[:, sl].astype(BF16), st.astype(BF16)))
        upd = _dot_tn(v_p.astype(BF16), k_dec[:, sl].astype(BF16))
        st_ref[p] = st * dec_row[:, sl] + jnp.where(bd, upd, 0.0)
    o_inter = jnp.concatenate(o_inter, axis=1)

    @pl.when(unsafe)
    def _():
        a_scr[...] = a_cum
        q_scr[...] = qq
        k_scr[...] = kk

        def body(t, carry):
            d = a_scr[pl.ds(t, 1), :] - a_scr[...]
            w = jnp.exp(jnp.where(row <= t, d, -jnp.inf))
            prod = (q_scr[pl.ds(t, 1), :] * k_scr[...]) * w
            e = _dot(prod.astype(BF16), bo_ref[...])
            oi_scr[pl.ds(t, 1), :] = jnp.sum(e * hi_ref[...], axis=0, keepdims=True)
            return carry

        lax.fori_loop(0, HG_CHUNK, body, 0)

    o = oi_scr[...] + o_inter
    sq = o * o
    sq_hi = sq.astype(BF16)
    sq_lo = (sq - sq_hi.astype(F32)).astype(BF16)
    ms = (_dot(sq_hi, bo_ref[...]) + _dot(sq_lo, bo_ref[...])) * (1.0 / HG_DV)
    gate = hg_ref[...]
    o = o * lax.rsqrt(ms + RMS_EPS) * nw_ref[...] * (gate * jax.nn.sigmoid(gate))
    o_ref[...] = o.astype(o_ref.dtype)


def _hgrn(hq, hf, hi, hg, loglb, log1mlb, nw_row, head_ones, batch, seq):
    nc = seq // HG_CHUNK
    blk = lambda b, c: (b * nc + c, 0)
    row = lambda b, c: (0, 0)
    w = HG_WIDTH
    return pl.pallas_call(
        _hgrn_body,
        grid=(batch, nc),
        in_specs=[pl.BlockSpec((HG_CHUNK, w), blk)] * 4 + [pl.BlockSpec((1, w), row)] * 3
                 + [pl.BlockSpec((w, w), row)],
        out_specs=pl.BlockSpec((HG_CHUNK, w), blk),
        out_shape=jax.ShapeDtypeStruct((batch * seq, w), BF16),
        scratch_shapes=[pltpu.VMEM((w // LANES, LANES, LANES), F32)]
                       + [pltpu.VMEM((HG_CHUNK, w), F32)] * 4,
        compiler_params=_params(("arbitrary", "arbitrary"), 32),
        name="hgrn",
    )(hq, hf, hi, hg, loglb, log1mlb, nw_row, head_ones)


def _mem_attn_body(q_ref, k_ref, v_ref, o_ref):
    q_all = q_ref[...] * ATTN_SCALE
    outs = []
    for h in range(MEM_HEADS):
        sl = slice(h * HEAD_DIM, (h + 1) * HEAD_DIM)
        scores = _dot_nt(k_ref[0, :, sl], q_all[:, sl])
        e = jnp.exp(scores - jnp.max(scores, axis=0, keepdims=True))
        p = e * (1.0 / jnp.sum(e, axis=0, keepdims=True))
        outs.append(_dot_tn(p.astype(BF16), v_ref[0, :, sl]))
    o_ref[...] = jnp.concatenate(outs, axis=-1).astype(o_ref.dtype)


def _mem_attn(mq, mk, mv, layer, batch, seq, mem_len):
    tq = min(512, seq)
    nq = seq // tq
    w = MEM_HEADS * HEAD_DIM
    return pl.pallas_call(
        _mem_attn_body,
        grid=(batch, nq),
        in_specs=[pl.BlockSpec((tq, w), lambda b, i: (b * nq + i, 0)),
                  pl.BlockSpec((1, mem_len, w), lambda b, i: (layer, b, 0)),
                  pl.BlockSpec((1, mem_len, w), lambda b, i: (layer, b, 0))],
        out_specs=pl.BlockSpec((tq, w), lambda b, i: (b * nq + i, 0)),
        out_shape=jax.ShapeDtypeStruct((batch * seq, w), BF16),
        compiler_params=_params(("arbitrary", "arbitrary"), 32),
        name="mem_attn",
    )(mq, mk, mv)


def _layer_norm(h, g, b):
    mu = jnp.mean(h, axis=-1, keepdims=True)
    d = h - mu
    var = jnp.mean(d * d, axis=-1, keepdims=True)
    return d * lax.rsqrt(var + LN_EPS) * g + b


def _route_rows(sel):
    def row(a, r):
        return a[r:r + 1, :]

    best = None
    for g in range(N_GROUPS):
        a = [row(sel, 4 * g + i) for i in range(4)]
        gs = a[0] + a[1]
        for i, j in ((0, 2), (0, 3), (1, 2), (1, 3), (2, 3)):
            gs = jnp.maximum(gs, a[i] + a[j])
        if best is None:
            best, best_score = jnp.zeros_like(gs, dtype=I32), gs
        else:
            better = gs > best_score
            best = jnp.where(better, g, best)
            best_score = jnp.where(better, gs, best_score)

    def pick(arr, i):
        out = row(arr, i)
        for g in range(1, N_GROUPS):
            out = jnp.where(best == g, row(arr, 4 * g + i), out)
        return out

    a = [pick(sel, i) for i in range(4)]
    i1, m1 = jnp.zeros_like(best), a[0]
    for i in range(1, 4):
        gt = a[i] > m1
        i1 = jnp.where(gt, i, i1)
        m1 = jnp.where(gt, a[i], m1)
    i2 = jnp.full_like(best, -1)
    m2 = jnp.full_like(m1, -jnp.inf)
    for i in range(4):
        gt = (i1 != i) & (a[i] > m2)
        i2 = jnp.where(gt, i, i2)
        m2 = jnp.where(gt, a[i], m2)
    lo = jnp.minimum(i1, i2)
    hi = jnp.maximum(i1, i2)
    pair = jnp.where(lo == 0, hi - 1, jnp.where(lo == 1, hi + 1, 5))
    return best * N_PAIRS + pair


def _out_proj_body(x_ref, swa_ref, hg_ref, mem_ref, w_ref, g_ref, b_ref, wr_ref, rb_ref, tri_ref,
                   x1_ref, cls_ref, rank_ref, cnt_ref, carry_ref):
    i = pl.program_id(0)
    t = x_ref.shape[0]

    @pl.when(i == 0)
    def _():
        carry_ref[...] = jnp.zeros_like(carry_ref)

    acc = _dot(swa_ref[...], w_ref[0:512, :]) + _dot(hg_ref[...], w_ref[512:768, :]) \
        + _dot(mem_ref[...], w_ref[768:1024, :])
    x1 = _layer_norm(ALPHA * x_ref[...] + acc, g_ref[...], b_ref[...])
    x1_ref[...] = x1

    logits = _dot_nt(wr_ref[...], x1.astype(BF16))
    cls = _route_rows(jax.nn.sigmoid(logits) + rb_ref[...])
    onehot = (lax.broadcasted_iota(I32, (CLASS_ROWS, t), 0) == cls).astype(F32)
    prefix = _dot(onehot.astype(BF16), tri_ref[...])
    carry = carry_ref[...]
    rank = jnp.sum(onehot * (prefix - 1.0 + carry[:, 0:1]), axis=0, keepdims=True)
    new_carry = carry + prefix[:, t - 1:t]
    carry_ref[...] = new_carry
    cnt_ref[...] = new_carry
    cls_ref[0] = cls
    rank_ref[0] = rank.astype(I32)


def _out_proj(x2, swa_o, hg_o, mem_o, w_bf, g_row, b_row, wr_t, rb_col, tri):
    n = x2.shape[0]
    t = tri.shape[0]
    nt = n // t
    tok = lambda i: (i, 0)
    fixed = lambda i: (0, 0)
    per_tile = pl.BlockSpec((1, 1, t), lambda i: (i, 0, 0))
    return pl.pallas_call(
        _out_proj_body,
        grid=(nt,),
        in_specs=[pl.BlockSpec((t, D_MODEL), tok), pl.BlockSpec((t, 512), tok),
                  pl.BlockSpec((t, 256), tok), pl.BlockSpec((t, 256), tok),
                  pl.BlockSpec((MIX_WIDTH, D_MODEL), fixed),
                  pl.BlockSpec((1, D_MODEL), fixed), pl.BlockSpec((1, D_MODEL), fixed),
                  pl.BlockSpec((N_EXPERTS, D_MODEL), fixed), pl.BlockSpec((N_EXPERTS, 1), fixed),
                  pl.BlockSpec((t, t), fixed)],
        out_specs=[pl.BlockSpec((t, D_MODEL), tok), per_tile, per_tile,
                   pl.BlockSpec((CLASS_ROWS, LANES), fixed)],
        out_shape=[jax.ShapeDtypeStruct((n, D_MODEL), F32),
                   jax.ShapeDtypeStruct((nt, 1, t), I32), jax.ShapeDtypeStruct((nt, 1, t), I32),
                   jax.ShapeDtypeStruct((CLASS_ROWS, LANES), F32)],
        scratch_shapes=[pltpu.VMEM((CLASS_ROWS, LANES), F32)],
        compiler_params=_params(("arbitrary",), 48),
        name="out_proj_ln_route",
    )(x2, swa_o, hg_o, mem_o, w_bf, g_row, b_row, wr_t, rb_col, tri)


def _row_copy(src_ref, dst_ref, sem, src_row, dst_row):
    return pltpu.make_async_copy(src_ref.at[pl.ds(src_row, 1), :], dst_ref.at[pl.ds(dst_row, 1), :], sem)


def _scatter_body(cstart_ref, cls_ref, rank_ref, x_hbm, dest_ref, xs_hbm,
                  buf, dest_vmem, dest_smem, load_sem, scat_sem, misc_sem, *, n_steps, tile):
    i = pl.program_id(0)
    slot = i % 3

    def load(step, into):
        return pltpu.make_async_copy(
            x_hbm.at[pl.ds(pl.multiple_of(step * tile, tile), tile), :], buf.at[into], load_sem.at[into])

    def scatter_done(of):
        return pltpu.make_async_copy(buf.at[of], xs_hbm.at[pl.ds(0, tile), :], scat_sem.at[of])

    @pl.when(i == 0)
    def _():
        load(0, 0).start()

    @pl.when(i + 1 < n_steps)
    def _():
        load(i + 1, (i + 1) % 3).start()

    cls = cls_ref[0]
    dest = rank_ref[0]
    for c in range(N_CLASSES):
        dest = dest + jnp.where(cls == c, cstart_ref[c], 0)
    dest_ref[0] = dest
    dest_vmem[...] = dest
    to_smem = pltpu.make_async_copy(dest_vmem, dest_smem, misc_sem)
    to_smem.start()
    to_smem.wait()

    load(i, slot).wait()

    def issue(group, carry):
        for k in range(DMA_GROUP):
            j = group * DMA_GROUP + k
            _row_copy(buf.at[slot], xs_hbm, scat_sem.at[slot], j, dest_smem[0, j]).start(priority=k % 2)
        return carry

    lax.fori_loop(0, tile // DMA_GROUP, issue, 0)

    @pl.when(i >= 1)
    def _():
        scatter_done((i + 2) % 3).wait()

    @pl.when(i == n_steps - 1)
    def _():
        scatter_done(slot).wait()


def _scatter_rows(x1_z, cls, rank, cstart):
    nt, _, tile = cls.shape
    n = nt * tile
    per_tile = pl.BlockSpec((1, 1, tile), lambda i, cs: (i, 0, 0))
    return pl.pallas_call(
        functools.partial(_scatter_body, n_steps=nt, tile=tile),
        grid_spec=pltpu.PrefetchScalarGridSpec(
            num_scalar_prefetch=1,
            grid=(nt,),
            in_specs=[per_tile, per_tile, pl.BlockSpec(memory_space=pl.ANY)],
            out_specs=[per_tile, pl.BlockSpec(memory_space=pl.ANY)],
            scratch_shapes=[pltpu.VMEM((3, tile, D_MODEL), F32),
                            pltpu.VMEM((1, tile), I32), pltpu.SMEM((1, tile), I32),
                            pltpu.SemaphoreType.DMA((3,)), pltpu.SemaphoreType.DMA((3,)),
                            pltpu.SemaphoreType.DMA(())]),
        out_shape=[jax.ShapeDtypeStruct((nt, 1, tile), I32),
                   jax.ShapeDtypeStruct((n, D_MODEL), F32)],
        compiler_params=_params(("arbitrary",), 32),
        name="scatter_rows",
    )(cstart, cls, rank, x1_z)


def _expert_body(blk_ref, elo_ref, ehi_ref, start_ref, end_ref, xs_ref, wr_ref,
                 g_lo, u_lo, d_lo, g_hi, u_hi, d_hi, ys_ref):
    i = pl.program_id(0)
    start = start_ref[i]
    end = end_ref[i]

    @pl.when(end > start)
    def _():
        x = xs_ref[...].astype(BF16)
        scores = jax.nn.sigmoid(_dot(x, wr_ref[...]))
        lane = lax.broadcasted_iota(I32, (1, LANES), 1)
        s_lo = jnp.sum(jnp.where(lane == elo_ref[i], scores, 0.0), axis=-1, keepdims=True)
        s_hi = jnp.sum(jnp.where(lane == ehi_ref[i], scores, 0.0), axis=-1, keepdims=True)
        denom = s_lo + s_hi

        def ffn(gw, uw, dw, w_col):
            g = _dot(x, gw[0, 0])
            u = _dot(x, uw[0, 0])
            return _dot((((g * jax.nn.sigmoid(g)) * u) * w_col).astype(BF16), dw[0, 0])

        y = ffn(g_lo, u_lo, d_lo, s_lo / denom) + ffn(g_hi, u_hi, d_hi, s_hi / denom)
        first = start % MOE_BLK == 0

        @pl.when(first)
        def _():
            ys_ref[...] = y

        @pl.when(jnp.logical_not(first))
        def _():
            slot = blk_ref[i] * MOE_BLK + lax.broadcasted_iota(I32, (MOE_BLK, 1), 0)
            ys_ref[...] = jnp.where((slot >= start) & (slot < end), y, ys_ref[...])


def _experts(xs_z, items, wr_pad, wg, wu, wd, layer):
    blk, elo, ehi, start, end = items
    n_items = blk.shape[0]
    n_slots = xs_z.shape[0]

    def data_map(i, blk, elo, ehi, start, end):
        return (blk[i], 0)

    def w_lo_map(i, blk, elo, ehi, start, end):
        return (layer, elo[i], 0, 0)

    def w_hi_map(i, blk, elo, ehi, start, end):
        return (layer, ehi[i], 0, 0)

    up_spec = lambda m: pl.BlockSpec((1, 1, D_MODEL, D_EXPERT), m)
    down_spec = lambda m: pl.BlockSpec((1, 1, D_EXPERT, D_MODEL), m)
    return pl.pallas_call(
        _expert_body,
        grid_spec=pltpu.PrefetchScalarGridSpec(
            num_scalar_prefetch=5,
            grid=(n_items,),
            in_specs=[pl.BlockSpec((MOE_BLK, D_MODEL), data_map),
                      pl.BlockSpec((D_MODEL, LANES), lambda i, *_: (0, 0)),
                      up_spec(w_lo_map), up_spec(w_lo_map), down_spec(w_lo_map),
                      up_spec(w_hi_map), up_spec(w_hi_map), down_spec(w_hi_map)],
            out_specs=pl.BlockSpec((MOE_BLK, D_MODEL), data_map)),
        out_shape=jax.ShapeDtypeStruct((n_slots, D_MODEL), F32),
        compiler_params=_params(("arbitrary",), 48),
        name="experts",
    )(blk, elo, ehi, start, end, xs_z, wr_pad, wg, wu, wd, wg, wu, wd)


def _ln2_body(dest_ref, x1_ref, ys_hbm, g_ref, b_ref, o_ref, buf, sem, *, n_steps, tile):
    i = pl.program_id(0)
    slot = i % 2

    def gather(step, into):
        def issue(group, carry):
            for k in range(DMA_GROUP):
                j = group * DMA_GROUP + k
                _row_copy(ys_hbm, buf.at[into], sem.at[into], dest_ref[step * tile + j], j).start(
                    priority=k % 2)
            return carry

        lax.fori_loop(0, tile // DMA_GROUP, issue, 0)

    @pl.when(i == 0)
    def _():
        gather(0, 0)

    @pl.when(i + 1 < n_steps)
    def _():
        gather(i + 1, (i + 1) % 2)

    pltpu.make_async_copy(ys_hbm.at[pl.ds(0, tile), :], buf.at[slot], sem.at[slot]).wait()
    h = ALPHA * x1_ref[...] + buf[slot]
    o_ref[...] = _layer_norm(h, g_ref[...], b_ref[...])


def _ln2(x1_z, ys_z, dest, g_row, b_row, tile):
    n = x1_z.shape[0]
    nt = n // tile
    tok = lambda i, d: (i, 0)
    fixed = lambda i, d: (0, 0)
    return pl.pallas_call(
        functools.partial(_ln2_body, n_steps=nt, tile=tile),
        grid_spec=pltpu.PrefetchScalarGridSpec(
            num_scalar_prefetch=1,
            grid=(nt,),
            in_specs=[pl.BlockSpec((tile, D_MODEL), tok), pl.BlockSpec(memory_space=pl.ANY),
                      pl.BlockSpec((1, D_MODEL), fixed), pl.BlockSpec((1, D_MODEL), fixed)],
            out_specs=pl.BlockSpec((tile, D_MODEL), tok),
            scratch_shapes=[pltpu.VMEM((2, tile, D_MODEL), F32),
                            pltpu.SemaphoreType.DMA((2,))]),
        out_shape=jax.ShapeDtypeStruct((n, D_MODEL), F32),
        compiler_params=_params(("arbitrary",), 48),
        name="gather_ln2",
    )(dest, x1_z, ys_z, g_row, b_row)


def _t5_bucket(dist):
    max_exact = N_BUCKETS // 2
    d = jnp.maximum(dist, 0)
    large = max_exact + (jnp.log(jnp.maximum(d, 1).astype(F32) / max_exact)
                         / math.log(MAX_DISTANCE / max_exact) * (N_BUCKETS - max_exact)).astype(I32)
    large = jnp.minimum(large, N_BUCKETS - 1)
    return jnp.where(d < max_exact, d, large)


def _banded_bias(rel_bias):
    i = jnp.arange(ATTN_BLOCK)[:, None]
    j = jnp.arange(2 * ATTN_BLOCK)[None, :]
    dist = i + ATTN_BLOCK - j
    bucket = _t5_bucket(dist)[None]
    table = rel_bias.astype(F32)
    bias = jnp.zeros((SWA_HEADS, ATTN_BLOCK, 2 * ATTN_BLOCK), F32)
    for b in range(N_BUCKETS):
        bias = jnp.where(bucket == b, table[b][:, None, None], bias)
    from_prev = (jnp.arange(ATTN_BLOCK)[None, :] > i)[None]
    prev, own = bias[:, :, :ATTN_BLOCK], bias[:, :, ATTN_BLOCK:]
    normal = jnp.where(from_prev, prev, own)
    first = jnp.where(from_prev, -jnp.inf, own)
    return jnp.swapaxes(jnp.stack([first, normal]), 2, 3)


def _work_items(counts, n_tok):
    n_blocks = n_tok // MOE_BLK
    cend = jnp.cumsum(counts)
    cstart = cend - counts
    blk_starts = jnp.arange(n_blocks, dtype=I32) * MOE_BLK
    cls_starts = jnp.where(counts > 0, cstart, n_tok)
    start = jnp.sort(jnp.concatenate([blk_starts, cls_starts]))
    end = jnp.concatenate([start[1:], jnp.full((1,), n_tok, I32)])
    blk = jnp.minimum(start, n_tok - 1) // MOE_BLK
    cls = jnp.minimum(jnp.sum((cend[None, :] <= start[:, None]).astype(I32), axis=1), N_CLASSES - 1)
    group, pair = cls // N_PAIRS, cls % N_PAIRS
    lo = (pair >= 3).astype(I32) + (pair >= 5).astype(I32)
    hi = jnp.where(pair < 3, pair + 1, jnp.where(pair < 5, pair - 1, 3))
    return cstart, (blk, group * EXPERTS_PER_GROUP + lo, group * EXPERTS_PER_GROUP + hi, start, end)


def kernel(x, mem, w_in, b_in, w_mem_kv, attn_sinks, rel_bias, hgrn_lb_logits, hgrn_norm, w_out,
           ln1_g, ln1_b, w_router, router_bias, w_gate, w_up, w_down, ln2_g, ln2_b):
    batch, seq, _ = x.shape
    mem_len = mem.shape[1]
    n_tok = batch * seq
    route_tile = min(512, n_tok)

    bias = _banded_bias(rel_bias)
    lb = jnp.cumsum(jax.nn.softmax(hgrn_lb_logits.astype(F32), axis=0), axis=0)
    lb = lb - lb[0:1]
    log_lb = jnp.log(lb)
    log_1m_lb = jnp.log1p(-lb)
    head_ones = (jnp.arange(HG_WIDTH)[:, None] // HG_DV == jnp.arange(HG_WIDTH)[None, :] // HG_DV).astype(BF16)
    tri = (jnp.arange(route_tile)[:, None] <= jnp.arange(route_tile)[None, :]).astype(BF16)
    wr_t = jnp.transpose(w_router).astype(BF16)
    wr_pad = jnp.pad(w_router.astype(BF16), ((0, 0), (0, LANES - N_EXPERTS)))
    rb_col = router_bias.astype(F32).reshape(N_EXPERTS, 1)
    w_in_bf = w_in.astype(BF16)
    w_out_bf = w_out.astype(BF16)
    wg_bf, wu_bf, wd_bf = w_gate.astype(BF16), w_up.astype(BF16), w_down.astype(BF16)

    mk, mv = _mem_kv(mem.reshape(batch * mem_len, D_MODEL), w_mem_kv)

    x2 = x.reshape(n_tok, D_MODEL)
    for l in range(DEPTH):
        sq, sk, sv, hq, hf, hi, hg, mq = _in_proj(x2, w_in_bf[l], b_in[l].reshape(1, IN_WIDTH))
        swa_o = _swa(sq, sk, sv, attn_sinks[l].astype(F32), bias, batch, seq)
        hg_o = _hgrn(hq, hf, hi, hg, log_lb[l].reshape(1, HG_WIDTH), log_1m_lb[l].reshape(1, HG_WIDTH),
                     jnp.tile(hgrn_norm[l].astype(F32), HG_HEADS).reshape(1, HG_WIDTH), head_ones, batch, seq)
        mem_o = _mem_attn(mq, mk, mv, l, batch, seq, mem_len)
        x1_z, cls, rank, counts = _out_proj(
            x2, swa_o, hg_o, mem_o, w_out_bf[l], ln1_g[l].reshape(1, D_MODEL), ln1_b[l].reshape(1, D_MODEL),
            wr_t, rb_col, tri)
        cstart, items = _work_items(counts[:N_CLASSES, 0].astype(I32), n_tok)
        dest, xs_z = _scatter_rows(x1_z, cls, rank, cstart)
        ys_z = _experts(xs_z, items, wr_pad, wg_bf, wu_bf, wd_bf, l)
        x2 = _ln2(x1_z, ys_z, dest.reshape(n_tok), ln2_g[l].reshape(1, D_MODEL), ln2_b[l].reshape(1, D_MODEL),
                  route_tile)
    return x2.reshape(batch, seq, D_MODEL)
```

```python
import functools
import math

import numpy as np
import jax
import jax.numpy as jnp
from jax import lax
from jax.experimental import pallas as pl
from jax.experimental.pallas import tpu as pltpu

F32 = jnp.float32
BF16 = jnp.bfloat16
I32 = jnp.int32

D_MODEL = 1024
DEPTH = 4
HEAD_DIM = 64
SWA_HEADS = 8
SWA_KV_HEADS = 2
SWA_GROUP = SWA_HEADS // SWA_KV_HEADS
WINDOW = 128
ATTN_BLOCK = 128
HG_HEADS = 4
HG_DK = 64
HG_DV = 64
HG_WIDTH = HG_HEADS * HG_DK
MEM_HEADS = 4
N_BUCKETS = 32
MAX_DISTANCE = 128
N_EXPERTS = 16
N_GROUPS = 4
EXPERTS_PER_GROUP = 4
D_EXPERT = 512
LN_EPS = 1e-5
RMS_EPS = 1e-6
ALPHA = (2 * DEPTH) ** 0.25
SPLITS = (512, 128, 128, 256, 256, 256, 256, 256)
IN_WIDTH = sum(SPLITS)
PROJ_DTYPES = (BF16, BF16, BF16, F32, F32, F32, F32, BF16)
MIX_WIDTH = 1024
ATTN_SCALE = HEAD_DIM ** -0.5

SUBLANES = 8
LANES = 128
ROW_TILES = D_MODEL // LANES

SWA_STEP_BLOCKS = 4
HG_CHUNK = 128
HG_SUB = 32
HG_SAFE_DECAY = 80.0
N_PAIRS = 6
N_CLASSES = N_GROUPS * N_PAIRS
CLASS_ROWS = 32
MOE_BLK = 256
DMA_GROUP = 8


def _params(semantics, vmem_mib):
    return pltpu.CompilerParams(dimension_semantics=semantics, vmem_limit_bytes=vmem_mib * 1024 * 1024)


def _dot(a, b):
    return jnp.dot(a, b, preferred_element_type=F32)


def _dot_nt(a, b):
    return lax.dot_general(a, b, (((1,), (1,)), ((), ())), preferred_element_type=F32)


def _dot_tn(a, b):
    return lax.dot_general(a, b, (((0,), (0,)), ((), ())), preferred_element_type=F32)


def _from_tiles(z, n_rows):
    k = z.shape[0] // n_rows
    return z.reshape(n_rows, k, LANES).reshape(n_rows, k * LANES)


def _to_tiles(val):
    n_rows, width = val.shape
    return val.reshape(n_rows, width // LANES, LANES).reshape(n_rows * (width // LANES), LANES)


def _in_proj_body(x_ref, w_ref, b_ref, *out_refs):
    acc = _dot(x_ref[...].astype(BF16), w_ref[...]) + b_ref[...]
    off = 0
    for ref, width in zip(out_refs, SPLITS):
        ref[...] = acc[:, off:off + width].astype(ref.dtype)
        off += width


def _in_proj(x2, w_bf, b_row):
    n = x2.shape[0]
    tm = min(512, n)
    dtypes = PROJ_DTYPES
    return pl.pallas_call(
        _in_proj_body,
        grid=(n // tm,),
        in_specs=[pl.BlockSpec((tm, D_MODEL), lambda i: (i, 0)),
                  pl.BlockSpec((D_MODEL, IN_WIDTH), lambda i: (0, 0)),
                  pl.BlockSpec((1, IN_WIDTH), lambda i: (0, 0))],
        out_specs=[pl.BlockSpec((tm, w), lambda i: (i, 0)) for w in SPLITS],
        out_shape=[jax.ShapeDtypeStruct((n, w), dt) for w, dt in zip(SPLITS, dtypes)],
        compiler_params=_params(("arbitrary",), 48),
        name="in_proj",
    )(x2, w_bf, b_row)


def _mem_kv_body(mem_ref, w_ref, k_ref, v_ref):
    acc = _dot(mem_ref[...].astype(BF16), w_ref[0].astype(BF16))
    half = MEM_HEADS * HEAD_DIM
    k_ref[0] = acc[:, :half].astype(BF16)
    v_ref[0] = acc[:, half:].astype(BF16)


def _mem_kv(mem2, w_mem_kv):
    rows = mem2.shape[0]
    tm = min(512, rows)
    half = MEM_HEADS * HEAD_DIM
    return pl.pallas_call(
        _mem_kv_body,
        grid=(DEPTH, rows // tm),
        in_specs=[pl.BlockSpec((tm, D_MODEL), lambda l, i: (i, 0)),
                  pl.BlockSpec((1, D_MODEL, 2 * half), lambda l, i: (l, 0, 0))],
        out_specs=[pl.BlockSpec((1, tm, half), lambda l, i: (l, i, 0)),
                   pl.BlockSpec((1, tm, half), lambda l, i: (l, i, 0))],
        out_shape=[jax.ShapeDtypeStruct((DEPTH, rows, half), BF16)] * 2,
        compiler_params=_params(("arbitrary", "arbitrary"), 32),
        name="mem_kv",
    )(mem2, w_mem_kv)


def _swa_body(sink_ref, q_ref, kp_ref, ko_ref, vp_ref, vo_ref, bias_ref, o_ref, *, blocks):
    n = pl.program_id(1)
    q_all = q_ref[...] * ATTN_SCALE
    k_all = jnp.concatenate([kp_ref[...], ko_ref[...]], axis=0)
    v_all = jnp.concatenate([vp_ref[...], vo_ref[...]], axis=0)
    from_prev = lax.broadcasted_iota(I32, (ATTN_BLOCK, ATTN_BLOCK), 0) > \
        lax.broadcasted_iota(I32, (ATTN_BLOCK, ATTN_BLOCK), 1)
    for j in range(blocks):
        lo = j * ATTN_BLOCK
        kk = k_all[lo:lo + 2 * ATTN_BLOCK]
        vv = v_all[lo:lo + 2 * ATTN_BLOCK]
        table = jnp.where(n > 0, 1, 0) if j == 0 else 1
        outs = []
        for h in range(SWA_HEADS):
            g = h // SWA_GROUP
            q_h = q_all[lo:lo + ATTN_BLOCK, h * HEAD_DIM:(h + 1) * HEAD_DIM]
            scores = _dot_nt(kk[:, g * HEAD_DIM:(g + 1) * HEAD_DIM], q_h)
            band = jnp.where(from_prev, scores[:ATTN_BLOCK], scores[ATTN_BLOCK:]) + bias_ref[table, h]
            sink = sink_ref[h]
            m = jnp.maximum(jnp.max(band, axis=0, keepdims=True), sink)
            e = jnp.exp(band - m)
            p = e * (1.0 / (jnp.sum(e, axis=0, keepdims=True) + jnp.exp(sink - m)))
            p = jnp.concatenate([jnp.where(from_prev, p, 0.0), jnp.where(from_prev, 0.0, p)], axis=0)
            outs.append(_dot_tn(p.astype(BF16), vv[:, g * HEAD_DIM:(g + 1) * HEAD_DIM]))
        o_ref[lo:lo + ATTN_BLOCK, :] = jnp.concatenate(outs, axis=-1).astype(o_ref.dtype)


def _swa(sq, sk, sv, sinks, bias2, batch, seq):
    nb = seq // ATTN_BLOCK
    blocks = math.gcd(SWA_STEP_BLOCKS, nb)
    ns = nb // blocks
    own = lambda b, n: (b * ns + n, 0)
    prev = lambda b, n: (b * nb + jnp.maximum(n * blocks - 1, 0), 0)
    kvw = SWA_KV_HEADS * HEAD_DIM
    qw = SWA_HEADS * HEAD_DIM
    rows = blocks * ATTN_BLOCK
    return pl.pallas_call(
        functools.partial(_swa_body, blocks=blocks),
        grid=(batch, ns),
        in_specs=[pl.BlockSpec(memory_space=pltpu.SMEM),
                  pl.BlockSpec((rows, qw), own),
                  pl.BlockSpec((ATTN_BLOCK, kvw), prev),
                  pl.BlockSpec((rows, kvw), own),
                  pl.BlockSpec((ATTN_BLOCK, kvw), prev),
                  pl.BlockSpec((rows, kvw), own),
                  pl.BlockSpec((2, SWA_HEADS, ATTN_BLOCK, ATTN_BLOCK), lambda b, n: (0, 0, 0, 0))],
        out_specs=pl.BlockSpec((rows, qw), own),
        out_shape=jax.ShapeDtypeStruct((batch * seq, qw), BF16),
        compiler_params=_params(("arbitrary", "arbitrary"), 32),
        name="swa",
    )(sinks, sq, sk, sk, sv, sv, bias2)


def _cumsum_rows(x):
    n = x.shape[0]
    row = lax.broadcasted_iota(I32, (n, 1), 0)
    s = 1
    while s < n:
        x = x + jnp.where(row >= s, pltpu.roll(x, s, 0), 0.0)
        s *= 2
    return x


def _rows_from_blocks(vals, width):
    return jnp.concatenate([jnp.broadcast_to(v, (HG_SUB, width)) for v in vals], axis=0)


def _hgrn_body(hq_ref, hf_ref, hi_ref, hg_ref, loglb_ref, log1mlb_ref, nw_ref, bo_ref, o_ref,
               st_ref, a_scr, q_scr, k_scr, oi_scr):
    c = pl.program_id(1)
    n_sub = HG_CHUNK // HG_SUB
    width = HG_WIDTH

    @pl.when(c == 0)
    def _():
        st_ref[...] = jnp.zeros_like(st_ref)

    z = hf_ref[...]
    log_sig = jnp.minimum(z, 0.0) - jnp.log1p(jnp.exp(-jnp.abs(z)))
    t_a = loglb_ref[...]
    t_b = log1mlb_ref[...] + log_sig
    log_f = jnp.maximum(t_a, t_b) + jnp.log1p(jnp.exp(-jnp.abs(t_a - t_b)))
    kk = jnp.exp(t_b - z)
    hq = hq_ref[...]
    qq = hq * jax.nn.sigmoid(hq) * (HG_DK ** -0.5)
    vv = hi_ref[...]

    a_cum = _cumsum_rows(log_f)
    ends = [a_cum[HG_SUB * j + HG_SUB - 1:HG_SUB * j + HG_SUB, :] for j in range(n_sub)]
    zero_row = jnp.zeros((1, width), F32)
    starts = [zero_row] + ends[:-1]
    a_loc = a_cum - _rows_from_blocks(starts, width)
    e_loc = _rows_from_blocks(ends, width) - a_cum
    worst = starts[0] - ends[0]
    for j in range(1, n_sub):
        worst = jnp.maximum(worst, starts[j] - ends[j])
    unsafe = jnp.max(worst) > HG_SAFE_DECAY

    q_sub = qq * jnp.exp(a_loc)
    k_sub = kk * jnp.exp(-a_loc)
    k_end = kk * jnp.exp(e_loc)
    one_row = jnp.ones((1, width), F32)
    q_dec = q_sub * _rows_from_blocks([jnp.exp(s) for s in starts], width)
    k_dec = k_end * _rows_from_blocks([jnp.exp(ends[-1] - e) for e in ends], width)
    q_top = q_sub * _rows_from_blocks(
        [zero_row, zero_row, one_row, jnp.exp(ends[2] - ends[1])], width)
    k_top = k_end * _rows_from_blocks(
        [jnp.exp(ends[1] - ends[0]), one_row, zero_row, zero_row], width)

    row = lax.broadcasted_iota(I32, (HG_CHUNK, 1), 0)
    sub_of_row = row // HG_SUB
    t_idx = lax.broadcasted_iota(I32, (HG_CHUNK, HG_CHUNK), 0)
    s_idx = lax.broadcasted_iota(I32, (HG_CHUNK, HG_CHUNK), 1)
    m_sub = ((t_idx // HG_SUB) == (s_idx // HG_SUB)) & (t_idx >= s_idx)
    m_sub2 = jnp.concatenate([m_sub, m_sub], axis=0)
    lane = lax.broadcasted_iota(I32, (1, LANES), 1)
    lo = lane < HG_DK
    bd = (lax.broadcasted_iota(I32, (LANES, LANES), 0) // HG_DV) == \
         (lax.broadcasted_iota(I32, (LANES, LANES), 1) // HG_DK)
    dec_row = jnp.exp(ends[-1])

    def heads_on_rows(x):
        return jnp.concatenate([jnp.where(lo, x, 0.0), jnp.where(lo, 0.0, x)], axis=0)

    o_inter = []
    for p in range(width // LANES):
        sl = slice(p * LANES, (p + 1) * LANES)
        qs, ks, ke = q_sub[:, sl], k_sub[:, sl], k_end[:, sl]
        p_sub = _dot_nt(heads_on_rows(qs).astype(BF16), ks.astype(BF16))
        q_x = jnp.concatenate([jnp.where(sub_of_row == 1, qs, 0.0),
                               jnp.where(sub_of_row == 3, qs, 0.0), q_top[:, sl]], axis=1)
        k_x = jnp.concatenate([jnp.where(sub_of_row == 0, ke, 0.0),
                               jnp.where(sub_of_row == 2, ke, 0.0), k_top[:, sl]], axis=1)
        q_x2 = jnp.concatenate([jnp.where(jnp.tile(lo, (1, 3)), q_x, 0.0),
                                jnp.where(jnp.tile(lo, (1, 3)), 0.0, q_x)], axis=0)
        p_x = _dot_nt(q_x2.astype(BF16), k_x.astype(BF16))
        p_all = jnp.where(m_sub2, p_sub, 0.0) + p_x
        p_cat = jnp.concatenate([p_all[:HG_CHUNK], p_all[HG_CHUNK:]], axis=1)
        v_p = vv[:, sl]
        oi_scr[:, sl] = _dot(p_cat.astype(BF16), heads_on_rows(v_p).astype(BF16))
        st = st_ref[p]
        o_inter.append(_dot_nt(q_dec[:, sl].astype(BF16), st.astype(BF16)))
        upd = _dot_tn(v_p.astype(BF16), k_dec[:, sl].astype(BF16))
        st_ref[p] = st * dec_row[:, sl] + jnp.where(bd, upd, 0.0)
    o_inter = jnp.concatenate(o_inter, axis=1)

    @pl.when(unsafe)
    def _():
        a_scr[...] = a_cum
        q_scr[...] = qq
        k_scr[...] = kk

        def body(t, carry):
            d = a_scr[pl.ds(t, 1), :] - a_scr[...]
            w = jnp.exp(jnp.where(row <= t, d, -jnp.inf))
            prod = (q_scr[pl.ds(t, 1), :] * k_scr[...]) * w
            e = _dot(prod.astype(BF16), bo_ref[...])
            oi_scr[pl.ds(t, 1), :] = jnp.sum(e * hi_ref[...], axis=0, keepdims=True)
            return carry

        lax.fori_loop(0, HG_CHUNK, body, 0)

    o = oi_scr[...] + o_inter
    sq = o * o
    sq_hi = sq.astype(BF16)
    sq_lo = (sq - sq_hi.astype(F32)).astype(BF16)
    ms = (_dot(sq_hi, bo_ref[...]) + _dot(sq_lo, bo_ref[...])) * (1.0 / HG_DV)
    gate = hg_ref[...]
    o = o * lax.rsqrt(ms + RMS_EPS) * nw_ref[...] * (gate * jax.nn.sigmoid(gate))
    o_ref[...] = o.astype(o_ref.dtype)


def _hgrn(hq, hf, hi, hg, loglb, log1mlb, nw_row, head_ones, batch, seq):
    nc = seq // HG_CHUNK
    blk = lambda b, c: (b * nc + c, 0)
    row = lambda b, c: (0, 0)
    w = HG_WIDTH
    return pl.pallas_call(
        _hgrn_body,
        grid=(batch, nc),
        in_specs=[pl.BlockSpec((HG_CHUNK, w), blk)] * 4 + [pl.BlockSpec((1, w), row)] * 3
                 + [pl.BlockSpec((w, w), row)],
        out_specs=pl.BlockSpec((HG_CHUNK, w), blk),
        out_shape=jax.ShapeDtypeStruct((batch * seq, w), BF16),
        scratch_shapes=[pltpu.VMEM((w // LANES, LANES, LANES), F32)]
                       + [pltpu.VMEM((HG_CHUNK, w), F32)] * 4,
        compiler_params=_params(("arbitrary", "arbitrary"), 32),
        name="hgrn",
    )(hq, hf, hi, hg, loglb, log1mlb, nw_row, head_ones)


def _mem_attn_body(q_ref, k_ref, v_ref, o_ref):
    q_all = q_ref[...] * ATTN_SCALE
    outs = []
    for h in range(MEM_HEADS):
        sl = slice(h * HEAD_DIM, (h + 1) * HEAD_DIM)
        scores = _dot_nt(k_ref[0, :, sl], q_all[:, sl])
        e = jnp.exp(scores - jnp.max(scores, axis=0, keepdims=True))
        p = e * (1.0 / jnp.sum(e, axis=0, keepdims=True))
        outs.append(_dot_tn(p.astype(BF16), v_ref[0, :, sl]))
    o_ref[...] = jnp.concatenate(outs, axis=-1).astype(o_ref.dtype)


def _mem_attn(mq, mk, mv, layer, batch, seq, mem_len):
    tq = min(512, seq)
    nq = seq // tq
    w = MEM_HEADS * HEAD_DIM
    return pl.pallas_call(
        _mem_attn_body,
        grid=(batch, nq),
        in_specs=[pl.BlockSpec((tq, w), lambda b, i: (b * nq + i, 0)),
                  pl.BlockSpec((1, mem_len, w), lambda b, i: (layer, b, 0)),
                  pl.BlockSpec((1, mem_len, w), lambda b, i: (layer, b, 0))],
        out_specs=pl.BlockSpec((tq, w), lambda b, i: (b * nq + i, 0)),
        out_shape=jax.ShapeDtypeStruct((batch * seq, w), BF16),
        compiler_params=_params(("arbitrary", "arbitrary"), 32),
        name="mem_attn",
    )(mq, mk, mv)


def _layer_norm(h, g, b):
    mu = jnp.mean(h, axis=-1, keepdims=True)
    d = h - mu
    var = jnp.mean(d * d, axis=-1, keepdims=True)
    return d * lax.rsqrt(var + LN_EPS) * g + b


def _route_rows(sel):
    def row(a, r):
        return a[r:r + 1, :]

    best = None
    for g in range(N_GROUPS):
        a = [row(sel, 4 * g + i) for i in range(4)]
        gs = a[0] + a[1]
        for i, j in ((0, 2), (0, 3), (1, 2), (1, 3), (2, 3)):
            gs = jnp.maximum(gs, a[i] + a[j])
        if best is None:
            best, best_score = jnp.zeros_like(gs, dtype=I32), gs
        else:
            better = gs > best_score
            best = jnp.where(better, g, best)
            best_score = jnp.where(better, gs, best_score)

    def pick(arr, i):
        out = row(arr, i)
        for g in range(1, N_GROUPS):
            out = jnp.where(best == g, row(arr, 4 * g + i), out)
        return out

    a = [pick(sel, i) for i in range(4)]
    i1, m1 = jnp.zeros_like(best), a[0]
    for i in range(1, 4):
        gt = a[i] > m1
        i1 = jnp.where(gt, i, i1)
        m1 = jnp.where(gt, a[i], m1)
    i2 = jnp.full_like(best, -1)
    m2 = jnp.full_like(m1, -jnp.inf)
    for i in range(4):
        gt = (i1 != i) & (a[i] > m2)
        i2 = jnp.where(gt, i, i2)
        m2 = jnp.where(gt, a[i], m2)
    lo = jnp.minimum(i1, i2)
    hi = jnp.maximum(i1, i2)
    pair = jnp.where(lo == 0, hi - 1, jnp.where(lo == 1, hi + 1, 5))
    return best * N_PAIRS + pair


def _out_proj_body(x_ref, swa_ref, hg_ref, mem_ref, w_ref, g_ref, b_ref, wr_ref, rb_ref, tri_ref,
                   x1_ref, cls_ref, rank_ref, cnt_ref, carry_ref):
    i = pl.program_id(0)
    t = x_ref.shape[0]

    @pl.when(i == 0)
    def _():
        carry_ref[...] = jnp.zeros_like(carry_ref)

    acc = _dot(swa_ref[...], w_ref[0:512, :]) + _dot(hg_ref[...], w_ref[512:768, :]) \
        + _dot(mem_ref[...], w_ref[768:1024, :])
    x1 = _layer_norm(ALPHA * x_ref[...] + acc, g_ref[...], b_ref[...])
    x1_ref[...] = _to_tiles(x1)

    logits = _dot_nt(wr_ref[...], x1.astype(BF16))
    cls = _route_rows(jax.nn.sigmoid(logits) + rb_ref[...])
    onehot = (lax.broadcasted_iota(I32, (CLASS_ROWS, t), 0) == cls).astype(F32)
    prefix = _dot(onehot.astype(BF16), tri_ref[...])
    carry = carry_ref[...]
    rank = jnp.sum(onehot * (prefix - 1.0 + carry[:, 0:1]), axis=0, keepdims=True)
    new_carry = carry + prefix[:, t - 1:t]
    carry_ref[...] = new_carry
    cnt_ref[...] = new_carry
    cls_ref[0] = cls
    rank_ref[0] = rank.astype(I32)


def _out_proj(x2, swa_o, hg_o, mem_o, w_bf, g_row, b_row, wr_t, rb_col, tri):
    n = x2.shape[0]
    t = tri.shape[0]
    nt = n // t
    tok = lambda i: (i, 0)
    fixed = lambda i: (0, 0)
    per_tile = pl.BlockSpec((1, 1, t), lambda i: (i, 0, 0))
    return pl.pallas_call(
        _out_proj_body,
        grid=(nt,),
        in_specs=[pl.BlockSpec((t, D_MODEL), tok), pl.BlockSpec((t, 512), tok),
                  pl.BlockSpec((t, 256), tok), pl.BlockSpec((t, 256), tok),
                  pl.BlockSpec((MIX_WIDTH, D_MODEL), fixed),
                  pl.BlockSpec((1, D_MODEL), fixed), pl.BlockSpec((1, D_MODEL), fixed),
                  pl.BlockSpec((N_EXPERTS, D_MODEL), fixed), pl.BlockSpec((N_EXPERTS, 1), fixed),
                  pl.BlockSpec((t, t), fixed)],
        out_specs=[pl.BlockSpec((t * ROW_TILES, LANES), tok), per_tile, per_tile,
                   pl.BlockSpec((CLASS_ROWS, LANES), fixed)],
        out_shape=[jax.ShapeDtypeStruct((n * ROW_TILES, LANES), F32),
                   jax.ShapeDtypeStruct((nt, 1, t), I32), jax.ShapeDtypeStruct((nt, 1, t), I32),
                   jax.ShapeDtypeStruct((CLASS_ROWS, LANES), F32)],
        scratch_shapes=[pltpu.VMEM((CLASS_ROWS, LANES), F32)],
        compiler_params=_params(("arbitrary",), 48),
        name="out_proj_ln_route",
    )(x2, swa_o, hg_o, mem_o, w_bf, g_row, b_row, wr_t, rb_col, tri)


def _tile_copy(src_ref, dst_ref, sem, src_row, dst_row, tiles):
    n = tiles * SUBLANES
    return pltpu.make_async_copy(
        src_ref.at[pl.ds(pl.multiple_of(src_row * n, n), n), :],
        dst_ref.at[pl.ds(pl.multiple_of(dst_row * n, n), n), :], sem)


def _scatter_body(cstart_ref, cls_ref, rank_ref, x_hbm, dest_ref, xs_hbm,
                  buf, dest_vmem, dest_smem, load_sem, scat_sem, misc_sem, *, n_steps, tile):
    i = pl.program_id(0)
    rows = tile * SUBLANES
    slot = i % 3

    def load(step, into):
        return pltpu.make_async_copy(
            x_hbm.at[pl.ds(pl.multiple_of(step * rows, rows), rows), :], buf.at[into], load_sem.at[into])

    def scatter_done(of):
        return pltpu.make_async_copy(buf.at[of], xs_hbm.at[pl.ds(0, rows), :], scat_sem.at[of])

    @pl.when(i == 0)
    def _():
        load(0, 0).start()

    @pl.when(i + 1 < n_steps)
    def _():
        load(i + 1, (i + 1) % 3).start()

    cls = cls_ref[0]
    dest = rank_ref[0]
    for c in range(N_CLASSES):
        dest = dest + jnp.where(cls == c, cstart_ref[c], 0)
    dest_ref[0] = dest
    dest_vmem[...] = dest
    to_smem = pltpu.make_async_copy(dest_vmem, dest_smem, misc_sem)
    to_smem.start()
    to_smem.wait()

    load(i, slot).wait()

    def issue(group, carry):
        for k in range(DMA_GROUP):
            j = group * DMA_GROUP + k
            _tile_copy(buf.at[slot], xs_hbm, scat_sem.at[slot], j, dest_smem[0, j], 1).start(priority=k % 2)
        return carry

    lax.fori_loop(0, tile // DMA_GROUP, issue, 0)

    @pl.when(i >= 1)
    def _():
        scatter_done((i + 2) % 3).wait()

    @pl.when(i == n_steps - 1)
    def _():
        scatter_done(slot).wait()


def _scatter_rows(x1_z, cls, rank, cstart):
    nt, _, tile = cls.shape
    n = nt * tile
    per_tile = pl.BlockSpec((1, 1, tile), lambda i, cs: (i, 0, 0))
    return pl.pallas_call(
        functools.partial(_scatter_body, n_steps=nt, tile=tile),
        grid_spec=pltpu.PrefetchScalarGridSpec(
            num_scalar_prefetch=1,
            grid=(nt,),
            in_specs=[per_tile, per_tile, pl.BlockSpec(memory_space=pl.ANY)],
            out_specs=[per_tile, pl.BlockSpec(memory_space=pl.ANY)],
            scratch_shapes=[pltpu.VMEM((3, tile * SUBLANES, LANES), F32),
                            pltpu.VMEM((1, tile), I32), pltpu.SMEM((1, tile), I32),
                            pltpu.SemaphoreType.DMA((3,)), pltpu.SemaphoreType.DMA((3,)),
                            pltpu.SemaphoreType.DMA(())]),
        out_shape=[jax.ShapeDtypeStruct((nt, 1, tile), I32),
                   jax.ShapeDtypeStruct((n * SUBLANES, LANES), F32)],
        compiler_params=_params(("arbitrary",), 32),
        name="scatter_rows",
    )(cstart, cls, rank, x1_z)


def _expert_body(blk_ref, elo_ref, ehi_ref, start_ref, end_ref, xs_ref, wr_ref,
                 g_lo, u_lo, d_lo, g_hi, u_hi, d_hi, ys_ref):
    i = pl.program_id(0)
    start = start_ref[i]
    end = end_ref[i]

    @pl.when(end > start)
    def _():
        x = _from_tiles(xs_ref[...], MOE_BLK).astype(BF16)
        scores = jax.nn.sigmoid(_dot(x, wr_ref[...]))
        lane = lax.broadcasted_iota(I32, (1, LANES), 1)
        s_lo = jnp.sum(jnp.where(lane == elo_ref[i], scores, 0.0), axis=-1, keepdims=True)
        s_hi = jnp.sum(jnp.where(lane == ehi_ref[i], scores, 0.0), axis=-1, keepdims=True)
        denom = s_lo + s_hi

        def ffn(gw, uw, dw, w_col):
            g = _dot(x, gw[0, 0])
            u = _dot(x, uw[0, 0])
            return _dot((((g * jax.nn.sigmoid(g)) * u) * w_col).astype(BF16), dw[0, 0])

        y = _to_tiles(ffn(g_lo, u_lo, d_lo, s_lo / denom) + ffn(g_hi, u_hi, d_hi, s_hi / denom))
        first = start % MOE_BLK == 0

        @pl.when(first)
        def _():
            ys_ref[...] = y

        @pl.when(jnp.logical_not(first))
        def _():
            slot = blk_ref[i] * MOE_BLK \
                + lax.broadcasted_iota(I32, (MOE_BLK * ROW_TILES, 1), 0) // ROW_TILES
            ys_ref[...] = jnp.where((slot >= start) & (slot < end), y, ys_ref[...])


def _experts(xs_z, items, wr_pad, wg, wu, wd, layer):
    blk, elo, ehi, start, end = items
    n_items = blk.shape[0]
    n_slots = xs_z.shape[0] // SUBLANES

    def data_map(i, blk, elo, ehi, start, end):
        return (blk[i], 0)

    def w_lo_map(i, blk, elo, ehi, start, end):
        return (layer, elo[i], 0, 0)

    def w_hi_map(i, blk, elo, ehi, start, end):
        return (layer, ehi[i], 0, 0)

    up_spec = lambda m: pl.BlockSpec((1, 1, D_MODEL, D_EXPERT), m)
    down_spec = lambda m: pl.BlockSpec((1, 1, D_EXPERT, D_MODEL), m)
    return pl.pallas_call(
        _expert_body,
        grid_spec=pltpu.PrefetchScalarGridSpec(
            num_scalar_prefetch=5,
            grid=(n_items,),
            in_specs=[pl.BlockSpec((MOE_BLK * SUBLANES, LANES), data_map),
                      pl.BlockSpec((D_MODEL, LANES), lambda i, *_: (0, 0)),
                      up_spec(w_lo_map), up_spec(w_lo_map), down_spec(w_lo_map),
                      up_spec(w_hi_map), up_spec(w_hi_map), down_spec(w_hi_map)],
            out_specs=pl.BlockSpec((MOE_BLK * SUBLANES, LANES), data_map)),
        out_shape=jax.ShapeDtypeStruct((n_slots * SUBLANES, LANES), F32),
        compiler_params=_params(("arbitrary",), 48),
        name="experts",
    )(blk, elo, ehi, start, end, xs_z, wr_pad, wg, wu, wd, wg, wu, wd)


def _gathered_ln2(dest_ref, x1_ref, ys_hbm, g_ref, b_ref, buf, sem, n_steps, tile):
    i = pl.program_id(0)
    slot = i % 2

    def gather(step, into):
        def issue(group, carry):
            for k in range(DMA_GROUP):
                j = group * DMA_GROUP + k
                _tile_copy(ys_hbm, buf.at[into], sem.at[into], dest_ref[step * tile + j], j, 1).start(
                    priority=k % 2)
            return carry

        lax.fori_loop(0, tile // DMA_GROUP, issue, 0)

    @pl.when(i == 0)
    def _():
        gather(0, 0)

    @pl.when(i + 1 < n_steps)
    def _():
        gather(i + 1, (i + 1) % 2)

    pltpu.make_async_copy(ys_hbm.at[pl.ds(0, tile * SUBLANES), :], buf.at[slot], sem.at[slot]).wait()
    h = ALPHA * _from_tiles(x1_ref[...], tile) + _from_tiles(buf[slot], tile)
    return _layer_norm(h, g_ref[...], b_ref[...])


def _ln2_body(dest_ref, x1_ref, ys_hbm, g_ref, b_ref, o_ref, buf, sem, *, n_steps, tile):
    o_ref[...] = _gathered_ln2(dest_ref, x1_ref, ys_hbm, g_ref, b_ref, buf, sem, n_steps, tile)


def _ln2_in_proj_body(dest_ref, x1_ref, ys_hbm, g_ref, b_ref, w_ref, bias_ref, o_ref, *rest, n_steps, tile):
    *proj_refs, buf, sem = rest
    x = _gathered_ln2(dest_ref, x1_ref, ys_hbm, g_ref, b_ref, buf, sem, n_steps, tile)
    o_ref[...] = x
    acc = _dot(x.astype(BF16), w_ref[...]) + bias_ref[...]
    off = 0
    for ref, width in zip(proj_refs, SPLITS):
        ref[...] = acc[:, off:off + width].astype(ref.dtype)
        off += width


def _ln2(x1_z, ys_z, dest, g_row, b_row, tile, w_bf=None, bias_row=None):
    n = x1_z.shape[0] // SUBLANES
    nt = n // tile
    tok = lambda i, d: (i, 0)
    fixed = lambda i, d: (0, 0)
    in_specs = [pl.BlockSpec((tile * SUBLANES, LANES), tok), pl.BlockSpec(memory_space=pl.ANY),
                pl.BlockSpec((1, D_MODEL), fixed), pl.BlockSpec((1, D_MODEL), fixed)]
    out_specs = [pl.BlockSpec((tile, D_MODEL), tok)]
    out_shape = [jax.ShapeDtypeStruct((n, D_MODEL), F32)]
    operands = [dest, x1_z, ys_z, g_row, b_row]
    body = _ln2_body
    if w_bf is not None:
        in_specs += [pl.BlockSpec((D_MODEL, IN_WIDTH), fixed), pl.BlockSpec((1, IN_WIDTH), fixed)]
        out_specs += [pl.BlockSpec((tile, w), tok) for w in SPLITS]
        out_shape += [jax.ShapeDtypeStruct((n, w), dt) for w, dt in zip(SPLITS, PROJ_DTYPES)]
        operands += [w_bf, bias_row]
        body = _ln2_in_proj_body
    return pl.pallas_call(
        functools.partial(body, n_steps=nt, tile=tile),
        grid_spec=pltpu.PrefetchScalarGridSpec(
            num_scalar_prefetch=1,
            grid=(nt,),
            in_specs=in_specs,
            out_specs=out_specs,
            scratch_shapes=[pltpu.VMEM((2, tile * SUBLANES, LANES), F32),
                            pltpu.SemaphoreType.DMA((2,))]),
        out_shape=out_shape,
        compiler_params=_params(("arbitrary",), 56),
        name="gather_ln2" if w_bf is None else "gather_ln2_in_proj",
    )(*operands)


def _t5_bucket(dist):
    max_exact = N_BUCKETS // 2
    d = jnp.maximum(dist, 0)
    large = max_exact + (jnp.log(jnp.maximum(d, 1).astype(F32) / max_exact)
                         / math.log(MAX_DISTANCE / max_exact) * (N_BUCKETS - max_exact)).astype(I32)
    large = jnp.minimum(large, N_BUCKETS - 1)
    return jnp.where(d < max_exact, d, large)


def _banded_bias(rel_bias):
    i = jnp.arange(ATTN_BLOCK)[:, None]
    j = jnp.arange(2 * ATTN_BLOCK)[None, :]
    dist = i + ATTN_BLOCK - j
    bucket = _t5_bucket(dist)[None]
    table = rel_bias.astype(F32)
    bias = jnp.zeros((SWA_HEADS, ATTN_BLOCK, 2 * ATTN_BLOCK), F32)
    for b in range(N_BUCKETS):
        bias = jnp.where(bucket == b, table[b][:, None, None], bias)
    from_prev = (jnp.arange(ATTN_BLOCK)[None, :] > i)[None]
    prev, own = bias[:, :, :ATTN_BLOCK], bias[:, :, ATTN_BLOCK:]
    normal = jnp.where(from_prev, prev, own)
    first = jnp.where(from_prev, -jnp.inf, own)
    return jnp.swapaxes(jnp.stack([first, normal]), 2, 3)


def _work_items(counts, n_tok):
    n_blocks = n_tok // MOE_BLK
    cend = jnp.cumsum(counts)
    cstart = cend - counts
    blk_starts = jnp.arange(n_blocks, dtype=I32) * MOE_BLK
    cls_starts = jnp.where(counts > 0, cstart, n_tok)
    start = jnp.sort(jnp.concatenate([blk_starts, cls_starts]))
    end = jnp.concatenate([start[1:], jnp.full((1,), n_tok, I32)])
    blk = jnp.minimum(start, n_tok - 1) // MOE_BLK
    cls = jnp.minimum(jnp.sum((cend[None, :] <= start[:, None]).astype(I32), axis=1), N_CLASSES - 1)
    group, pair = cls // N_PAIRS, cls % N_PAIRS
    lo = (pair >= 3).astype(I32) + (pair >= 5).astype(I32)
    hi = jnp.where(pair < 3, pair + 1, jnp.where(pair < 5, pair - 1, 3))
    return cstart, (blk, group * EXPERTS_PER_GROUP + lo, group * EXPERTS_PER_GROUP + hi, start, end)


def kernel(x, mem, w_in, b_in, w_mem_kv, attn_sinks, rel_bias, hgrn_lb_logits, hgrn_norm, w_out,
           ln1_g, ln1_b, w_router, router_bias, w_gate, w_up, w_down, ln2_g, ln2_b):
    batch, seq, _ = x.shape
    mem_len = mem.shape[1]
    n_tok = batch * seq
    route_tile = min(512, n_tok)

    bias = _banded_bias(rel_bias)
    lb = jnp.cumsum(jax.nn.softmax(hgrn_lb_logits.astype(F32), axis=0), axis=0)
    lb = lb - lb[0:1]
    log_lb = jnp.log(lb)
    log_1m_lb = jnp.log1p(-lb)
    head_ones = (jnp.arange(HG_WIDTH)[:, None] // HG_DV == jnp.arange(HG_WIDTH)[None, :] // HG_DV).astype(BF16)
    tri = (jnp.arange(route_tile)[:, None] <= jnp.arange(route_tile)[None, :]).astype(BF16)
    wr_t = jnp.transpose(w_router).astype(BF16)
    wr_pad = jnp.pad(w_router.astype(BF16), ((0, 0), (0, LANES - N_EXPERTS)))
    rb_col = router_bias.astype(F32).reshape(N_EXPERTS, 1)
    w_in_bf = w_in.astype(BF16)
    w_out_bf = w_out.astype(BF16)
    wg_bf, wu_bf, wd_bf = w_gate.astype(BF16), w_up.astype(BF16), w_down.astype(BF16)

    mk, mv = _mem_kv(mem.reshape(batch * mem_len, D_MODEL), w_mem_kv)

    x2 = x.reshape(n_tok, D_MODEL)
    proj = _in_proj(x2, w_in_bf[0], b_in[0].reshape(1, IN_WIDTH))
    for l in range(DEPTH):
        sq, sk, sv, hq, hf, hi, hg, mq = proj
        swa_o = _swa(sq, sk, sv, attn_sinks[l].astype(F32), bias, batch, seq)
        hg_o = _hgrn(hq, hf, hi, hg, log_lb[l].reshape(1, HG_WIDTH), log_1m_lb[l].reshape(1, HG_WIDTH),
                     jnp.tile(hgrn_norm[l].astype(F32), HG_HEADS).reshape(1, HG_WIDTH), head_ones, batch, seq)
        mem_o = _mem_attn(mq, mk, mv, l, batch, seq, mem_len)
        x1_z, cls, rank, counts = _out_proj(
            x2, swa_o, hg_o, mem_o, w_out_bf[l], ln1_g[l].reshape(1, D_MODEL), ln1_b[l].reshape(1, D_MODEL),
            wr_t, rb_col, tri)
        cstart, items = _work_items(counts[:N_CLASSES, 0].astype(I32), n_tok)
        dest, xs_z = _scatter_rows(x1_z, cls, rank, cstart)
        ys_z = _experts(xs_z, items, wr_pad, wg_bf, wu_bf, wd_bf, l)
        ln2_args = (x1_z, ys_z, dest.reshape(n_tok), ln2_g[l].reshape(1, D_MODEL), ln2_b[l].reshape(1, D_MODEL),
                    route_tile)
        if l + 1 < DEPTH:
            x2, *proj = _ln2(*ln2_args, w_in_bf[l + 1], b_in[l + 1].reshape(1, IN_WIDTH))
        else:
            (x2,) = _ln2(*ln2_args)
    return x2.reshape(batch, seq, D_MODEL)
```

```python
import functools
import math

import numpy as np
import jax
import jax.numpy as jnp
from jax import lax
from jax.experimental import pallas as pl
from jax.experimental.pallas import tpu as pltpu

F32 = jnp.float32
BF16 = jnp.bfloat16
I32 = jnp.int32

D_MODEL = 1024
DEPTH = 4
HEAD_DIM = 64
SWA_HEADS = 8
SWA_KV_HEADS = 2
SWA_GROUP = SWA_HEADS // SWA_KV_HEADS
WINDOW = 128
ATTN_BLOCK = 128
HG_HEADS = 4
HG_DK = 64
HG_DV = 64
HG_WIDTH = HG_HEADS * HG_DK
MEM_HEADS = 4
N_BUCKETS = 32
MAX_DISTANCE = 128
N_EXPERTS = 16
N_GROUPS = 4
EXPERTS_PER_GROUP = 4
D_EXPERT = 512
LN_EPS = 1e-5
RMS_EPS = 1e-6
ALPHA = (2 * DEPTH) ** 0.25
SPLITS = (512, 128, 128, 256, 256, 256, 256, 256)
IN_WIDTH = sum(SPLITS)
PROJ_DTYPES = (BF16, BF16, BF16, F32, F32, F32, F32, BF16)
MIX_WIDTH = 1024
ATTN_SCALE = HEAD_DIM ** -0.5

SUBLANES = 8
LANES = 128
ROW_TILES = D_MODEL // LANES

SWA_STEP_BLOCKS = 4
HG_CHUNK = 128
HG_SUB = 32
HG_SAFE_DECAY = 80.0
N_PAIRS = 6
N_CLASSES = N_GROUPS * N_PAIRS
CLASS_ROWS = 32
MOE_BLK = 256
DMA_GROUP = 8


def _params(semantics, vmem_mib):
    return pltpu.CompilerParams(dimension_semantics=semantics, vmem_limit_bytes=vmem_mib * 1024 * 1024)


def _dot(a, b):
    return jnp.dot(a, b, preferred_element_type=F32)


def _dot_nt(a, b):
    return lax.dot_general(a, b, (((1,), (1,)), ((), ())), preferred_element_type=F32)


def _dot_tn(a, b):
    return lax.dot_general(a, b, (((0,), (0,)), ((), ())), preferred_element_type=F32)


def _from_tiles(z, n_rows):
    k = z.shape[0] // n_rows
    return z.reshape(n_rows, k, LANES).reshape(n_rows, k * LANES)


def _to_tiles(val):
    n_rows, width = val.shape
    return val.reshape(n_rows, width // LANES, LANES).reshape(n_rows * (width // LANES), LANES)


def _in_proj_body(x_ref, w_ref, b_ref, *out_refs):
    acc = _dot(x_ref[...].astype(BF16), w_ref[...]) + b_ref[...]
    off = 0
    for ref, width in zip(out_refs, SPLITS):
        ref[...] = acc[:, off:off + width].astype(ref.dtype)
        off += width


def _in_proj(x2, w_bf, b_row):
    n = x2.shape[0]
    tm = min(512, n)
    dtypes = PROJ_DTYPES
    return pl.pallas_call(
        _in_proj_body,
        grid=(n // tm,),
        in_specs=[pl.BlockSpec((tm, D_MODEL), lambda i: (i, 0)),
                  pl.BlockSpec((D_MODEL, IN_WIDTH), lambda i: (0, 0)),
                  pl.BlockSpec((1, IN_WIDTH), lambda i: (0, 0))],
        out_specs=[pl.BlockSpec((tm, w), lambda i: (i, 0)) for w in SPLITS],
        out_shape=[jax.ShapeDtypeStruct((n, w), dt) for w, dt in zip(SPLITS, dtypes)],
        compiler_params=_params(("arbitrary",), 48),
        name="in_proj",
    )(x2, w_bf, b_row)


def _mem_kv_body(mem_ref, w_ref, k_ref, v_ref):
    acc = _dot(mem_ref[...].astype(BF16), w_ref[0].astype(BF16))
    half = MEM_HEADS * HEAD_DIM
    k_ref[0] = acc[:, :half].astype(BF16)
    v_ref[0] = acc[:, half:].astype(BF16)


def _mem_kv(mem2, w_mem_kv):
    rows = mem2.shape[0]
    tm = min(512, rows)
    half = MEM_HEADS * HEAD_DIM
    return pl.pallas_call(
        _mem_kv_body,
        grid=(DEPTH, rows // tm),
        in_specs=[pl.BlockSpec((tm, D_MODEL), lambda l, i: (i, 0)),
                  pl.BlockSpec((1, D_MODEL, 2 * half), lambda l, i: (l, 0, 0))],
        out_specs=[pl.BlockSpec((1, tm, half), lambda l, i: (l, i, 0)),
                   pl.BlockSpec((1, tm, half), lambda l, i: (l, i, 0))],
        out_shape=[jax.ShapeDtypeStruct((DEPTH, rows, half), BF16)] * 2,
        compiler_params=_params(("arbitrary", "arbitrary"), 32),
        name="mem_kv",
    )(mem2, w_mem_kv)


def _swa_body(sink_ref, q_ref, kp_ref, ko_ref, vp_ref, vo_ref, bias_ref, o_ref, *, blocks):
    n = pl.program_id(1)
    q_all = q_ref[...] * ATTN_SCALE
    k_all = jnp.concatenate([kp_ref[...], ko_ref[...]], axis=0)
    v_all = jnp.concatenate([vp_ref[...], vo_ref[...]], axis=0)
    from_prev = lax.broadcasted_iota(I32, (ATTN_BLOCK, ATTN_BLOCK), 0) > \
        lax.broadcasted_iota(I32, (ATTN_BLOCK, ATTN_BLOCK), 1)
    for j in range(blocks):
        lo = j * ATTN_BLOCK
        kk = k_all[lo:lo + 2 * ATTN_BLOCK]
        vv = v_all[lo:lo + 2 * ATTN_BLOCK]
        table = jnp.where(n > 0, 1, 0) if j == 0 else 1
        outs = []
        for h in range(SWA_HEADS):
            g = h // SWA_GROUP
            q_h = q_all[lo:lo + ATTN_BLOCK, h * HEAD_DIM:(h + 1) * HEAD_DIM]
            scores = _dot_nt(kk[:, g * HEAD_DIM:(g + 1) * HEAD_DIM], q_h)
            band = jnp.where(from_prev, scores[:ATTN_BLOCK], scores[ATTN_BLOCK:]) + bias_ref[table, h]
            sink = sink_ref[h]
            m = jnp.maximum(jnp.max(band, axis=0, keepdims=True), sink)
            e = jnp.exp(band - m)
            p = e * (1.0 / (jnp.sum(e, axis=0, keepdims=True) + jnp.exp(sink - m)))
            p = jnp.concatenate([jnp.where(from_prev, p, 0.0), jnp.where(from_prev, 0.0, p)], axis=0)
            outs.append(_dot_tn(p.astype(BF16), vv[:, g * HEAD_DIM:(g + 1) * HEAD_DIM]))
        o_ref[lo:lo + ATTN_BLOCK, :] = jnp.concatenate(outs, axis=-1).astype(o_ref.dtype)


def _swa(sq, sk, sv, sinks, bias2, batch, seq):
    nb = seq // ATTN_BLOCK
    blocks = math.gcd(SWA_STEP_BLOCKS, nb)
    ns = nb // blocks
    own = lambda b, n: (b * ns + n, 0)
    prev = lambda b, n: (b * nb + jnp.maximum(n * blocks - 1, 0), 0)
    kvw = SWA_KV_HEADS * HEAD_DIM
    qw = SWA_HEADS * HEAD_DIM
    rows = blocks * ATTN_BLOCK
    return pl.pallas_call(
        functools.partial(_swa_body, blocks=blocks),
        grid=(batch, ns),
        in_specs=[pl.BlockSpec(memory_space=pltpu.SMEM),
                  pl.BlockSpec((rows, qw), own),
                  pl.BlockSpec((ATTN_BLOCK, kvw), prev),
                  pl.BlockSpec((rows, kvw), own),
                  pl.BlockSpec((ATTN_BLOCK, kvw), prev),
                  pl.BlockSpec((rows, kvw), own),
                  pl.BlockSpec((2, SWA_HEADS, ATTN_BLOCK, ATTN_BLOCK), lambda b, n: (0, 0, 0, 0))],
        out_specs=pl.BlockSpec((rows, qw), own),
        out_shape=jax.ShapeDtypeStruct((batch * seq, qw), BF16),
        compiler_params=_params(("arbitrary", "arbitrary"), 32),
        name="swa",
    )(sinks, sq, sk, sk, sv, sv, bias2)


def _cumsum_rows(x):
    n = x.shape[0]
    row = lax.broadcasted_iota(I32, (n, 1), 0)
    s = 1
    while s < n:
        x = x + jnp.where(row >= s, pltpu.roll(x, s, 0), 0.0)
        s *= 2
    return x


def _rows_from_blocks(vals, width):
    return jnp.concatenate([jnp.broadcast_to(v, (HG_SUB, width)) for v in vals], axis=0)


def _hgrn_body(hq_ref, hf_ref, hi_ref, hg_ref, loglb_ref, log1mlb_ref, nw_ref, bo_ref, o_ref,
               st_ref, a_scr, q_scr, k_scr, oi_scr):
    c = pl.program_id(1)
    n_sub = HG_CHUNK // HG_SUB
    width = HG_WIDTH

    @pl.when(c == 0)
    def _():
        st_ref[...] = jnp.zeros_like(st_ref)

    z = hf_ref[...]
    log_sig = jnp.minimum(z, 0.0) - jnp.log1p(jnp.exp(-jnp.abs(z)))
    t_a = loglb_ref[...]
    t_b = log1mlb_ref[...] + log_sig
    log_f = jnp.maximum(t_a, t_b) + jnp.log1p(jnp.exp(-jnp.abs(t_a - t_b)))
    kk = jnp.exp(t_b - z)
    hq = hq_ref[...]
    qq = hq * jax.nn.sigmoid(hq) * (HG_DK ** -0.5)
    vv = hi_ref[...]

    a_cum = _cumsum_rows(log_f)
    ends = [a_cum[HG_SUB * j + HG_SUB - 1:HG_SUB * j + HG_SUB, :] for j in range(n_sub)]
    zero_row = jnp.zeros((1, width), F32)
    starts = [zero_row] + ends[:-1]
    a_loc = a_cum - _rows_from_blocks(starts, width)
    e_loc = _rows_from_blocks(ends, width) - a_cum
    worst = starts[0] - ends[0]
    for j in range(1, n_sub):
        worst = jnp.maximum(worst, starts[j] - ends[j])
    unsafe = jnp.max(worst) > HG_SAFE_DECAY

    q_sub = qq * jnp.exp(a_loc)
    k_sub = kk * jnp.exp(-a_loc)
    k_end = kk * jnp.exp(e_loc)
    one_row = jnp.ones((1, width), F32)
    q_dec = q_sub * _rows_from_blocks([jnp.exp(s) for s in starts], width)
    k_dec = k_end * _rows_from_blocks([jnp.exp(ends[-1] - e) for e in ends], width)
    q_top = q_sub * _rows_from_blocks(
        [zero_row, zero_row, one_row, jnp.exp(ends[2] - ends[1])], width)
    k_top = k_end * _rows_from_blocks(
        [jnp.exp(ends[1] - ends[0]), one_row, zero_row, zero_row], width)

    row = lax.broadcasted_iota(I32, (HG_CHUNK, 1), 0)
    sub_of_row = row // HG_SUB
    t_idx = lax.broadcasted_iota(I32, (HG_CHUNK, HG_CHUNK), 0)
    s_idx = lax.broadcasted_iota(I32, (HG_CHUNK, HG_CHUNK), 1)
    m_sub = ((t_idx // HG_SUB) == (s_idx // HG_SUB)) & (t_idx >= s_idx)
    m_sub2 = jnp.concatenate([m_sub, m_sub], axis=0)
    lane = lax.broadcasted_iota(I32, (1, LANES), 1)
    lo = lane < HG_DK
    bd = (lax.broadcasted_iota(I32, (LANES, LANES), 0) // HG_DV) == \
         (lax.broadcasted_iota(I32, (LANES, LANES), 1) // HG_DK)
    dec_row = jnp.exp(ends[-1])

    def heads_on_rows(x):
        return jnp.concatenate([jnp.where(lo, x, 0.0), jnp.where(lo, 0.0, x)], axis=0)

    o_inter = []
    for p in range(width // LANES):
        sl = slice(p * LANES, (p + 1) * LANES)
        qs, ks, ke = q_sub[:, sl], k_sub[:, sl], k_end[:, sl]
        p_sub = _dot_nt(heads_on_rows(qs).astype(BF16), ks.astype(BF16))
        q_x = jnp.concatenate([jnp.where(sub_of_row == 1, qs, 0.0),
                               jnp.where(sub_of_row == 3, qs, 0.0), q_top[:, sl]], axis=1)
        k_x = jnp.concatenate([jnp.where(sub_of_row == 0, ke, 0.0),
                               jnp.where(sub_of_row == 2, ke, 0.0), k_top[:, sl]], axis=1)
        q_x2 = jnp.concatenate([jnp.where(jnp.tile(lo, (1, 3)), q_x, 0.0),
                                jnp.where(jnp.tile(lo, (1, 3)), 0.0, q_x)], axis=0)
        p_x = _dot_nt(q_x2.astype(BF16), k_x.astype(BF16))
        p_all = jnp.where(m_sub2, p_sub, 0.0) + p_x
        p_cat = jnp.concatenate([p_all[:HG_CHUNK], p_all[HG_CHUNK:]], axis=1)
        v_p = vv[:, sl]
        oi_scr[:, sl] = _dot(p_cat.astype(BF16), heads_on_rows(v_p).astype(BF16))
        st = st_ref[p]
        o_inter.append(_dot_nt(q_dec[:, sl].astype(BF16), st.astype(BF16)))
        upd = _dot_tn(v_p.astype(BF16), k_dec[:, sl].astype(BF16))
        st_ref[p] = st * dec_row[:, sl] + jnp.where(bd, upd, 0.0)
    o_inter = jnp.concatenate(o_inter, axis=1)

    @pl.when(unsafe)
    def _():
        a_scr[...] = a_cum
        q_scr[...] = qq
        k_scr[...] = kk

        def body(t, carry):
            d = a_scr[pl.ds(t, 1), :] - a_scr[...]
            w = jnp.exp(jnp.where(row <= t, d, -jnp.inf))
            prod = (q_scr[pl.ds(t, 1), :] * k_scr[...]) * w
            e = _dot(prod.astype(BF16), bo_ref[...])
            oi_scr[pl.ds(t, 1), :] = jnp.sum(e * hi_ref[...], axis=0, keepdims=True)
            return carry

        lax.fori_loop(0, HG_CHUNK, body, 0)

    o = oi_scr[...] + o_inter
    sq = o * o
    sq_hi = sq.astype(BF16)
    sq_lo = (sq - sq_hi.astype(F32)).astype(BF16)
    ms = (_dot(sq_hi, bo_ref[...]) + _dot(sq_lo, bo_ref[...])) * (1.0 / HG_DV)
    gate = hg_ref[...]
    o = o * lax.rsqrt(ms + RMS_EPS) * nw_ref[...] * (gate * jax.nn.sigmoid(gate))
    o_ref[...] = o.astype(o_ref.dtype)


def _hgrn(hq, hf, hi, hg, loglb, log1mlb, nw_row, head_ones, batch, seq):
    nc = seq // HG_CHUNK
    blk = lambda b, c: (b * nc + c, 0)
    row = lambda b, c: (0, 0)
    w = HG_WIDTH
    return pl.pallas_call(
        _hgrn_body,
        grid=(batch, nc),
        in_specs=[pl.BlockSpec((HG_CHUNK, w), blk)] * 4 + [pl.BlockSpec((1, w), row)] * 3
                 + [pl.BlockSpec((w, w), row)],
        out_specs=pl.BlockSpec((HG_CHUNK, w), blk),
        out_shape=jax.ShapeDtypeStruct((batch * seq, w), BF16),
        scratch_shapes=[pltpu.VMEM((w // LANES, LANES, LANES), F32)]
                       + [pltpu.VMEM((HG_CHUNK, w), F32)] * 4,
        compiler_params=_params(("arbitrary", "arbitrary"), 32),
        name="hgrn",
    )(hq, hf, hi, hg, loglb, log1mlb, nw_row, head_ones)


def _mem_attn_body(q_ref, k_ref, v_ref, o_ref):
    q_all = q_ref[...] * ATTN_SCALE
    outs = []
    for h in range(MEM_HEADS):
        sl = slice(h * HEAD_DIM, (h + 1) * HEAD_DIM)
        scores = _dot_nt(k_ref[0, :, sl], q_all[:, sl])
        e = jnp.exp(scores - jnp.max(scores, axis=0, keepdims=True))
        p = e * (1.0 / jnp.sum(e, axis=0, keepdims=True))
        outs.append(_dot_tn(p.astype(BF16), v_ref[0, :, sl]))
    o_ref[...] = jnp.concatenate(outs, axis=-1).astype(o_ref.dtype)


def _mem_attn(mq, mk, mv, layer, batch, seq, mem_len):
    tq = min(512, seq)
    nq = seq // tq
    w = MEM_HEADS * HEAD_DIM
    return pl.pallas_call(
        _mem_attn_body,
        grid=(batch, nq),
        in_specs=[pl.BlockSpec((tq, w), lambda b, i: (b * nq + i, 0)),
                  pl.BlockSpec((1, mem_len, w), lambda b, i: (layer, b, 0)),
                  pl.BlockSpec((1, mem_len, w), lambda b, i: (layer, b, 0))],
        out_specs=pl.BlockSpec((tq, w), lambda b, i: (b * nq + i, 0)),
        out_shape=jax.ShapeDtypeStruct((batch * seq, w), BF16),
        compiler_params=_params(("arbitrary", "arbitrary"), 32),
        name="mem_attn",
    )(mq, mk, mv)


def _layer_norm(h, g, b):
    mu = jnp.mean(h, axis=-1, keepdims=True)
    d = h - mu
    var = jnp.mean(d * d, axis=-1, keepdims=True)
    return d * lax.rsqrt(var + LN_EPS) * g + b


def _route_rows(sel):
    def row(a, r):
        return a[r:r + 1, :]

    best = None
    for g in range(N_GROUPS):
        a = [row(sel, 4 * g + i) for i in range(4)]
        gs = a[0] + a[1]
        for i, j in ((0, 2), (0, 3), (1, 2), (1, 3), (2, 3)):
            gs = jnp.maximum(gs, a[i] + a[j])
        if best is None:
            best, best_score = jnp.zeros_like(gs, dtype=I32), gs
        else:
            better = gs > best_score
            best = jnp.where(better, g, best)
            best_score = jnp.where(better, gs, best_score)

    def pick(arr, i):
        out = row(arr, i)
        for g in range(1, N_GROUPS):
            out = jnp.where(best == g, row(arr, 4 * g + i), out)
        return out

    a = [pick(sel, i) for i in range(4)]
    i1, m1 = jnp.zeros_like(best), a[0]
    for i in range(1, 4):
        gt = a[i] > m1
        i1 = jnp.where(gt, i, i1)
        m1 = jnp.where(gt, a[i], m1)
    i2 = jnp.full_like(best, -1)
    m2 = jnp.full_like(m1, -jnp.inf)
    for i in range(4):
        gt = (i1 != i) & (a[i] > m2)
        i2 = jnp.where(gt, i, i2)
        m2 = jnp.where(gt, a[i], m2)
    lo = jnp.minimum(i1, i2)
    hi = jnp.maximum(i1, i2)
    pair = jnp.where(lo == 0, hi - 1, jnp.where(lo == 1, hi + 1, 5))
    return best * N_PAIRS + pair


def _out_proj_body(x_ref, swa_ref, hg_ref, mem_ref, w_ref, g_ref, b_ref, wr_ref, rb_ref, tri_ref,
                   x1_ref, cls_ref, rank_ref, cnt_ref, carry_ref):
    i = pl.program_id(0)
    t = x_ref.shape[0]

    @pl.when(i == 0)
    def _():
        carry_ref[...] = jnp.zeros_like(carry_ref)

    acc = _dot(swa_ref[...], w_ref[0:512, :]) + _dot(hg_ref[...], w_ref[512:768, :]) \
        + _dot(mem_ref[...], w_ref[768:1024, :])
    x1 = _layer_norm(ALPHA * x_ref[...] + acc, g_ref[...], b_ref[...])
    x1_ref[...] = _to_tiles(x1)

    logits = _dot_nt(wr_ref[...], x1.astype(BF16))
    cls = _route_rows(jax.nn.sigmoid(logits) + rb_ref[...])
    onehot = (lax.broadcasted_iota(I32, (CLASS_ROWS, t), 0) == cls).astype(F32)
    prefix = _dot(onehot.astype(BF16), tri_ref[...])
    carry = carry_ref[...]
    rank = jnp.sum(onehot * (prefix - 1.0 + carry[:, 0:1]), axis=0, keepdims=True)
    new_carry = carry + prefix[:, t - 1:t]
    carry_ref[...] = new_carry
    cnt_ref[...] = new_carry
    cls_ref[0] = cls
    rank_ref[0] = rank.astype(I32)


def _out_proj(x2, swa_o, hg_o, mem_o, w_bf, g_row, b_row, wr_t, rb_col, tri):
    n = x2.shape[0]
    t = tri.shape[0]
    nt = n // t
    tok = lambda i: (i, 0)
    fixed = lambda i: (0, 0)
    per_tile = pl.BlockSpec((1, 1, t), lambda i: (i, 0, 0))
    return pl.pallas_call(
        _out_proj_body,
        grid=(nt,),
        in_specs=[pl.BlockSpec((t, D_MODEL), tok), pl.BlockSpec((t, 512), tok),
                  pl.BlockSpec((t, 256), tok), pl.BlockSpec((t, 256), tok),
                  pl.BlockSpec((MIX_WIDTH, D_MODEL), fixed),
                  pl.BlockSpec((1, D_MODEL), fixed), pl.BlockSpec((1, D_MODEL), fixed),
                  pl.BlockSpec((N_EXPERTS, D_MODEL), fixed), pl.BlockSpec((N_EXPERTS, 1), fixed),
                  pl.BlockSpec((t, t), fixed)],
        out_specs=[pl.BlockSpec((t * ROW_TILES, LANES), tok), per_tile, per_tile,
                   pl.BlockSpec((CLASS_ROWS, LANES), fixed)],
        out_shape=[jax.ShapeDtypeStruct((n * ROW_TILES, LANES), F32),
                   jax.ShapeDtypeStruct((nt, 1, t), I32), jax.ShapeDtypeStruct((nt, 1, t), I32),
                   jax.ShapeDtypeStruct((CLASS_ROWS, LANES), F32)],
        scratch_shapes=[pltpu.VMEM((CLASS_ROWS, LANES), F32)],
        compiler_params=_params(("arbitrary",), 48),
        name="out_proj_ln_route",
    )(x2, swa_o, hg_o, mem_o, w_bf, g_row, b_row, wr_t, rb_col, tri)


def _tile_copy(src_ref, dst_ref, sem, src_row, dst_row, tiles):
    n = tiles * SUBLANES
    return pltpu.make_async_copy(
        src_ref.at[pl.ds(pl.multiple_of(src_row * n, n), n), :],
        dst_ref.at[pl.ds(pl.multiple_of(dst_row * n, n), n), :], sem)


def _scatter_body(cstart_ref, cls_ref, rank_ref, x_hbm, dest_ref, xs_hbm,
                  buf, dest_vmem, dest_smem, load_sem, scat_sem, misc_sem, *, n_steps, tile):
    i = pl.program_id(0)
    rows = tile * SUBLANES
    slot = i % 3

    def load(step, into):
        return pltpu.make_async_copy(
            x_hbm.at[pl.ds(pl.multiple_of(step * rows, rows), rows), :], buf.at[into], load_sem.at[into])

    def scatter_done(of):
        return pltpu.make_async_copy(buf.at[of], xs_hbm.at[pl.ds(0, rows), :], scat_sem.at[of])

    @pl.when(i == 0)
    def _():
        load(0, 0).start()

    @pl.when(i + 1 < n_steps)
    def _():
        load(i + 1, (i + 1) % 3).start()

    cls = cls_ref[0]
    dest = rank_ref[0]
    for c in range(N_CLASSES):
        dest = dest + jnp.where(cls == c, cstart_ref[c], 0)
    dest_ref[0] = dest
    dest_vmem[...] = dest
    to_smem = pltpu.make_async_copy(dest_vmem, dest_smem, misc_sem)
    to_smem.start()
    to_smem.wait()

    load(i, slot).wait()

    def issue(group, carry):
        for k in range(DMA_GROUP):
            j = group * DMA_GROUP + k
            _tile_copy(buf.at[slot], xs_hbm, scat_sem.at[slot], j, dest_smem[0, j], 1).start(priority=k % 2)
        return carry

    lax.fori_loop(0, tile // DMA_GROUP, issue, 0)

    @pl.when(i >= 1)
    def _():
        scatter_done((i + 2) % 3).wait()

    @pl.when(i == n_steps - 1)
    def _():
        scatter_done(slot).wait()


def _scatter_rows(x1_z, cls, rank, cstart):
    nt, _, tile = cls.shape
    n = nt * tile
    per_tile = pl.BlockSpec((1, 1, tile), lambda i, cs: (i, 0, 0))
    return pl.pallas_call(
        functools.partial(_scatter_body, n_steps=nt, tile=tile),
        grid_spec=pltpu.PrefetchScalarGridSpec(
            num_scalar_prefetch=1,
            grid=(nt,),
            in_specs=[per_tile, per_tile, pl.BlockSpec(memory_space=pl.ANY)],
            out_specs=[per_tile, pl.BlockSpec(memory_space=pl.ANY)],
            scratch_shapes=[pltpu.VMEM((3, tile * SUBLANES, LANES), F32),
                            pltpu.VMEM((1, tile), I32), pltpu.SMEM((1, tile), I32),
                            pltpu.SemaphoreType.DMA((3,)), pltpu.SemaphoreType.DMA((3,)),
                            pltpu.SemaphoreType.DMA(())]),
        out_shape=[jax.ShapeDtypeStruct((nt, 1, tile), I32),
                   jax.ShapeDtypeStruct((n * SUBLANES, LANES), F32)],
        compiler_params=_params(("arbitrary",), 32),
        name="scatter_rows",
    )(cstart, cls, rank, x1_z)


def _expert_body(blk_ref, elo_ref, ehi_ref, start_ref, end_ref, lo_new_ref, hi_new_ref, xs_ref, wr_ref,
                 g_lo32, u_lo32, d_lo32, g_hi32, u_hi32, d_hi32, ys_ref,
                 g_lo, u_lo, d_lo, g_hi, u_hi, d_hi):
    i = pl.program_id(0)
    start = start_ref[i]
    end = end_ref[i]

    @pl.when(lo_new_ref[i] == 1)
    def _():
        for dst, src in ((g_lo, g_lo32), (u_lo, u_lo32), (d_lo, d_lo32)):
            dst[...] = src[0, 0].astype(BF16)

    @pl.when(hi_new_ref[i] == 1)
    def _():
        for dst, src in ((g_hi, g_hi32), (u_hi, u_hi32), (d_hi, d_hi32)):
            dst[...] = src[0, 0].astype(BF16)

    @pl.when(end > start)
    def _():
        x = _from_tiles(xs_ref[...], MOE_BLK).astype(BF16)
        scores = jax.nn.sigmoid(_dot(x, wr_ref[...]))
        lane = lax.broadcasted_iota(I32, (1, LANES), 1)
        s_lo = jnp.sum(jnp.where(lane == elo_ref[i], scores, 0.0), axis=-1, keepdims=True)
        s_hi = jnp.sum(jnp.where(lane == ehi_ref[i], scores, 0.0), axis=-1, keepdims=True)
        denom = s_lo + s_hi

        def ffn(gw, uw, dw, w_col):
            g = _dot(x, gw[...])
            u = _dot(x, uw[...])
            return _dot((((g * jax.nn.sigmoid(g)) * u) * w_col).astype(BF16), dw[...])

        y = _to_tiles(ffn(g_lo, u_lo, d_lo, s_lo / denom) + ffn(g_hi, u_hi, d_hi, s_hi / denom))
        first = start % MOE_BLK == 0

        @pl.when(first)
        def _():
            ys_ref[...] = y

        @pl.when(jnp.logical_not(first))
        def _():
            slot = blk_ref[i] * MOE_BLK \
                + lax.broadcasted_iota(I32, (MOE_BLK * ROW_TILES, 1), 0) // ROW_TILES
            ys_ref[...] = jnp.where((slot >= start) & (slot < end), y, ys_ref[...])


def _experts(xs_z, items, wr_pad, wg, wu, wd, layer):
    blk, elo, ehi, start, end = items
    n_items = blk.shape[0]
    n_slots = xs_z.shape[0] // SUBLANES
    changed = lambda e: jnp.concatenate([jnp.ones((1,), I32), (e[1:] != e[:-1]).astype(I32)])

    def data_map(i, blk, *_):
        return (blk[i], 0)

    def w_lo_map(i, blk, elo, *_):
        return (layer, elo[i], 0, 0)

    def w_hi_map(i, blk, elo, ehi, *_):
        return (layer, ehi[i], 0, 0)

    up_spec = lambda m: pl.BlockSpec((1, 1, D_MODEL, D_EXPERT), m)
    down_spec = lambda m: pl.BlockSpec((1, 1, D_EXPERT, D_MODEL), m)
    up_bf = pltpu.VMEM((D_MODEL, D_EXPERT), BF16)
    down_bf = pltpu.VMEM((D_EXPERT, D_MODEL), BF16)
    return pl.pallas_call(
        _expert_body,
        grid_spec=pltpu.PrefetchScalarGridSpec(
            num_scalar_prefetch=7,
            grid=(n_items,),
            in_specs=[pl.BlockSpec((MOE_BLK * SUBLANES, LANES), data_map),
                      pl.BlockSpec((D_MODEL, LANES), lambda i, *_: (0, 0)),
                      up_spec(w_lo_map), up_spec(w_lo_map), down_spec(w_lo_map),
                      up_spec(w_hi_map), up_spec(w_hi_map), down_spec(w_hi_map)],
            out_specs=pl.BlockSpec((MOE_BLK * SUBLANES, LANES), data_map),
            scratch_shapes=[up_bf, up_bf, down_bf, up_bf, up_bf, down_bf]),
        out_shape=jax.ShapeDtypeStruct((n_slots * SUBLANES, LANES), F32),
        compiler_params=_params(("arbitrary",), 56),
        name="experts",
    )(blk, elo, ehi, start, end, changed(elo), changed(ehi), xs_z, wr_pad, wg, wu, wd, wg, wu, wd)


def _gathered_ln2(dest_ref, x1_ref, ys_hbm, g_ref, b_ref, buf, sem, n_steps, tile):
    i = pl.program_id(0)
    slot = i % 2

    def gather(step, into):
        def issue(group, carry):
            for k in range(DMA_GROUP):
                j = group * DMA_GROUP + k
                _tile_copy(ys_hbm, buf.at[into], sem.at[into], dest_ref[step * tile + j], j, 1).start(
                    priority=1)
            return carry

        lax.fori_loop(0, tile // DMA_GROUP, issue, 0)

    @pl.when(i == 0)
    def _():
        gather(0, 0)

    @pl.when(i + 1 < n_steps)
    def _():
        gather(i + 1, (i + 1) % 2)

    pltpu.make_async_copy(ys_hbm.at[pl.ds(0, tile * SUBLANES), :], buf.at[slot], sem.at[slot]).wait()
    h = ALPHA * _from_tiles(x1_ref[...], tile) + _from_tiles(buf[slot], tile)
    return _layer_norm(h, g_ref[...], b_ref[...])


def _ln2_body(dest_ref, x1_ref, ys_hbm, g_ref, b_ref, o_ref, buf, sem, *, n_steps, tile):
    o_ref[...] = _gathered_ln2(dest_ref, x1_ref, ys_hbm, g_ref, b_ref, buf, sem, n_steps, tile)


def _ln2_in_proj_body(dest_ref, x1_ref, ys_hbm, g_ref, b_ref, w_ref, bias_ref, o_ref, *rest, n_steps, tile):
    *proj_refs, buf, sem = rest
    x = _gathered_ln2(dest_ref, x1_ref, ys_hbm, g_ref, b_ref, buf, sem, n_steps, tile)
    o_ref[...] = x
    acc = _dot(x.astype(BF16), w_ref[...]) + bias_ref[...]
    off = 0
    for ref, width in zip(proj_refs, SPLITS):
        ref[...] = acc[:, off:off + width].astype(ref.dtype)
        off += width


def _ln2(x1_z, ys_z, dest, g_row, b_row, tile, w_bf=None, bias_row=None):
    n = x1_z.shape[0] // SUBLANES
    nt = n // tile
    tok = lambda i, d: (i, 0)
    fixed = lambda i, d: (0, 0)
    in_specs = [pl.BlockSpec((tile * SUBLANES, LANES), tok), pl.BlockSpec(memory_space=pl.ANY),
                pl.BlockSpec((1, D_MODEL), fixed), pl.BlockSpec((1, D_MODEL), fixed)]
    out_specs = [pl.BlockSpec((tile, D_MODEL), tok)]
    out_shape = [jax.ShapeDtypeStruct((n, D_MODEL), F32)]
    operands = [dest, x1_z, ys_z, g_row, b_row]
    body = _ln2_body
    if w_bf is not None:
        in_specs += [pl.BlockSpec((D_MODEL, IN_WIDTH), fixed), pl.BlockSpec((1, IN_WIDTH), fixed)]
        out_specs += [pl.BlockSpec((tile, w), tok) for w in SPLITS]
        out_shape += [jax.ShapeDtypeStruct((n, w), dt) for w, dt in zip(SPLITS, PROJ_DTYPES)]
        operands += [w_bf, bias_row]
        body = _ln2_in_proj_body
    return pl.pallas_call(
        functools.partial(body, n_steps=nt, tile=tile),
        grid_spec=pltpu.PrefetchScalarGridSpec(
            num_scalar_prefetch=1,
            grid=(nt,),
            in_specs=in_specs,
            out_specs=out_specs,
            scratch_shapes=[pltpu.VMEM((2, tile * SUBLANES, LANES), F32),
                            pltpu.SemaphoreType.DMA((2,))]),
        out_shape=out_shape,
        compiler_params=_params(("arbitrary",), 56),
        name="gather_ln2" if w_bf is None else "gather_ln2_in_proj",
    )(*operands)


def _t5_bucket(dist):
    max_exact = N_BUCKETS // 2
    d = jnp.maximum(dist, 0)
    large = max_exact + (jnp.log(jnp.maximum(d, 1).astype(F32) / max_exact)
                         / math.log(MAX_DISTANCE / max_exact) * (N_BUCKETS - max_exact)).astype(I32)
    large = jnp.minimum(large, N_BUCKETS - 1)
    return jnp.where(d < max_exact, d, large)


def _banded_bias(rel_bias):
    i = jnp.arange(ATTN_BLOCK)[:, None]
    j = jnp.arange(2 * ATTN_BLOCK)[None, :]
    dist = i + ATTN_BLOCK - j
    bucket = _t5_bucket(dist)[None]
    table = rel_bias.astype(F32)
    bias = jnp.zeros((SWA_HEADS, ATTN_BLOCK, 2 * ATTN_BLOCK), F32)
    for b in range(N_BUCKETS):
        bias = jnp.where(bucket == b, table[b][:, None, None], bias)
    from_prev = (jnp.arange(ATTN_BLOCK)[None, :] > i)[None]
    prev, own = bias[:, :, :ATTN_BLOCK], bias[:, :, ATTN_BLOCK:]
    normal = jnp.where(from_prev, prev, own)
    first = jnp.where(from_prev, -jnp.inf, own)
    return jnp.swapaxes(jnp.stack([first, normal]), 2, 3)


def _work_items(counts, n_tok):
    n_blocks = n_tok // MOE_BLK
    cend = jnp.cumsum(counts)
    cstart = cend - counts
    blk_starts = jnp.arange(n_blocks, dtype=I32) * MOE_BLK
    cls_starts = jnp.where(counts > 0, cstart, n_tok)
    start = jnp.sort(jnp.concatenate([blk_starts, cls_starts]))
    end = jnp.concatenate([start[1:], jnp.full((1,), n_tok, I32)])
    blk = jnp.minimum(start, n_tok - 1) // MOE_BLK
    cls = jnp.minimum(jnp.sum((cend[None, :] <= start[:, None]).astype(I32), axis=1), N_CLASSES - 1)
    group, pair = cls // N_PAIRS, cls % N_PAIRS
    lo = (pair >= 3).astype(I32) + (pair >= 5).astype(I32)
    hi = jnp.where(pair < 3, pair + 1, jnp.where(pair < 5, pair - 1, 3))
    return cstart, (blk, group * EXPERTS_PER_GROUP + lo, group * EXPERTS_PER_GROUP + hi, start, end)


def kernel(x, mem, w_in, b_in, w_mem_kv, attn_sinks, rel_bias, hgrn_lb_logits, hgrn_norm, w_out,
           ln1_g, ln1_b, w_router, router_bias, w_gate, w_up, w_down, ln2_g, ln2_b):
    batch, seq, _ = x.shape
    mem_len = mem.shape[1]
    n_tok = batch * seq
    route_tile = min(512, n_tok)

    bias = _banded_bias(rel_bias)
    lb = jnp.cumsum(jax.nn.softmax(hgrn_lb_logits.astype(F32), axis=0), axis=0)
    lb = lb - lb[0:1]
    log_lb = jnp.log(lb)
    log_1m_lb = jnp.log1p(-lb)
    head_ones = (jnp.arange(HG_WIDTH)[:, None] // HG_DV == jnp.arange(HG_WIDTH)[None, :] // HG_DV).astype(BF16)
    tri = (jnp.arange(route_tile)[:, None] <= jnp.arange(route_tile)[None, :]).astype(BF16)
    wr_t = jnp.transpose(w_router).astype(BF16)
    wr_pad = jnp.pad(w_router.astype(BF16), ((0, 0), (0, LANES - N_EXPERTS)))
    rb_col = router_bias.astype(F32).reshape(N_EXPERTS, 1)
    w_in_bf = w_in.astype(BF16)
    w_out_bf = w_out.astype(BF16)

    mk, mv = _mem_kv(mem.reshape(batch * mem_len, D_MODEL), w_mem_kv)

    x2 = x.reshape(n_tok, D_MODEL)
    proj = _in_proj(x2, w_in_bf[0], b_in[0].reshape(1, IN_WIDTH))
    for l in range(DEPTH):
        sq, sk, sv, hq, hf, hi, hg, mq = proj
        swa_o = _swa(sq, sk, sv, attn_sinks[l].astype(F32), bias, batch, seq)
        hg_o = _hgrn(hq, hf, hi, hg, log_lb[l].reshape(1, HG_WIDTH), log_1m_lb[l].reshape(1, HG_WIDTH),
                     jnp.tile(hgrn_norm[l].astype(F32), HG_HEADS).reshape(1, HG_WIDTH), head_ones, batch, seq)
        mem_o = _mem_attn(mq, mk, mv, l, batch, seq, mem_len)
        x1_z, cls, rank, counts = _out_proj(
            x2, swa_o, hg_o, mem_o, w_out_bf[l], ln1_g[l].reshape(1, D_MODEL), ln1_b[l].reshape(1, D_MODEL),
            wr_t, rb_col, tri)
        cstart, items = _work_items(counts[:N_CLASSES, 0].astype(I32), n_tok)
        dest, xs_z = _scatter_rows(x1_z, cls, rank, cstart)
        ys_z = _experts(xs_z, items, wr_pad, w_gate, w_up, w_down, l)
        ln2_args = (x1_z, ys_z, dest.reshape(n_tok), ln2_g[l].reshape(1, D_MODEL), ln2_b[l].reshape(1, D_MODEL),
                    route_tile)
        if l + 1 < DEPTH:
            x2, *proj = _ln2(*ln2_args, w_in_bf[l + 1], b_in[l + 1].reshape(1, IN_WIDTH))
        else:
            (x2,) = _ln2(*ln2_args)
    return x2.reshape(batch, seq, D_MODEL)
```

```python
import functools
import math

import numpy as np
import jax
import jax.numpy as jnp
from jax import lax
from jax.experimental import pallas as pl
from jax.experimental.pallas import tpu as pltpu

F32 = jnp.float32
BF16 = jnp.bfloat16
I32 = jnp.int32

D_MODEL = 1024
DEPTH = 4
HEAD_DIM = 64
SWA_HEADS = 8
SWA_KV_HEADS = 2
SWA_GROUP = SWA_HEADS // SWA_KV_HEADS
WINDOW = 128
ATTN_BLOCK = 128
HG_HEADS = 4
HG_DK = 64
HG_DV = 64
HG_WIDTH = HG_HEADS * HG_DK
MEM_HEADS = 4
N_BUCKETS = 32
MAX_DISTANCE = 128
N_EXPERTS = 16
N_GROUPS = 4
EXPERTS_PER_GROUP = 4
D_EXPERT = 512
LN_EPS = 1e-5
RMS_EPS = 1e-6
ALPHA = (2 * DEPTH) ** 0.25
SPLITS = (512, 128, 128, 256, 256, 256, 256, 256)
IN_WIDTH = sum(SPLITS)
PROJ_DTYPES = (BF16, BF16, BF16, F32, F32, F32, F32, BF16)
MIX_WIDTH = 1024
ATTN_SCALE = HEAD_DIM ** -0.5

SUBLANES = 8
LANES = 128
ROW_TILES = D_MODEL // LANES

SWA_STEP_BLOCKS = 4
HG_CHUNK = 128
HG_SUB = 32
HG_SAFE_DECAY = 80.0
N_PAIRS = 6
N_CLASSES = N_GROUPS * N_PAIRS
CLASS_ROWS = 32
MOE_BLK = 256
DMA_GROUP = 8


def _params(semantics, vmem_mib):
    return pltpu.CompilerParams(dimension_semantics=semantics, vmem_limit_bytes=vmem_mib * 1024 * 1024)


def _dot(a, b):
    return jnp.dot(a, b, preferred_element_type=F32)


def _dot_nt(a, b):
    return lax.dot_general(a, b, (((1,), (1,)), ((), ())), preferred_element_type=F32)


def _dot_tn(a, b):
    return lax.dot_general(a, b, (((0,), (0,)), ((), ())), preferred_element_type=F32)


def _from_tiles(z, n_rows):
    k = z.shape[0] // n_rows
    return z.reshape(n_rows, k, LANES).reshape(n_rows, k * LANES)


def _to_tiles(val):
    n_rows, width = val.shape
    return val.reshape(n_rows, width // LANES, LANES).reshape(n_rows * (width // LANES), LANES)


def _in_proj_body(x_ref, w_ref, b_ref, *out_refs):
    acc = _dot(x_ref[...].astype(BF16), w_ref[...]) + b_ref[...]
    off = 0
    for ref, width in zip(out_refs, SPLITS):
        ref[...] = acc[:, off:off + width].astype(ref.dtype)
        off += width


def _in_proj(x2, w_bf, b_row):
    n = x2.shape[0]
    tm = min(512, n)
    dtypes = PROJ_DTYPES
    return pl.pallas_call(
        _in_proj_body,
        grid=(n // tm,),
        in_specs=[pl.BlockSpec((tm, D_MODEL), lambda i: (i, 0)),
                  pl.BlockSpec((D_MODEL, IN_WIDTH), lambda i: (0, 0)),
                  pl.BlockSpec((1, IN_WIDTH), lambda i: (0, 0))],
        out_specs=[pl.BlockSpec((tm, w), lambda i: (i, 0)) for w in SPLITS],
        out_shape=[jax.ShapeDtypeStruct((n, w), dt) for w, dt in zip(SPLITS, dtypes)],
        compiler_params=_params(("arbitrary",), 48),
        name="in_proj",
    )(x2, w_bf, b_row)


def _mem_kv_body(mem_ref, w_ref, k_ref, v_ref):
    acc = _dot(mem_ref[...].astype(BF16), w_ref[0].astype(BF16))
    half = MEM_HEADS * HEAD_DIM
    k_ref[0] = acc[:, :half].astype(BF16)
    v_ref[0] = acc[:, half:].astype(BF16)


def _mem_kv(mem2, w_mem_kv):
    rows = mem2.shape[0]
    tm = min(512, rows)
    half = MEM_HEADS * HEAD_DIM
    return pl.pallas_call(
        _mem_kv_body,
        grid=(DEPTH, rows // tm),
        in_specs=[pl.BlockSpec((tm, D_MODEL), lambda l, i: (i, 0)),
                  pl.BlockSpec((1, D_MODEL, 2 * half), lambda l, i: (l, 0, 0))],
        out_specs=[pl.BlockSpec((1, tm, half), lambda l, i: (l, i, 0)),
                   pl.BlockSpec((1, tm, half), lambda l, i: (l, i, 0))],
        out_shape=[jax.ShapeDtypeStruct((DEPTH, rows, half), BF16)] * 2,
        compiler_params=_params(("arbitrary", "arbitrary"), 32),
        name="mem_kv",
    )(mem2, w_mem_kv)


def _swa_body(sink_ref, q_ref, kp_ref, ko_ref, vp_ref, vo_ref, bias_ref, o_ref, *, blocks):
    n = pl.program_id(1)
    q_all = q_ref[...] * ATTN_SCALE
    k_all = jnp.concatenate([kp_ref[...], ko_ref[...]], axis=0)
    v_all = jnp.concatenate([vp_ref[...], vo_ref[...]], axis=0)
    from_prev = lax.broadcasted_iota(I32, (ATTN_BLOCK, ATTN_BLOCK), 0) > \
        lax.broadcasted_iota(I32, (ATTN_BLOCK, ATTN_BLOCK), 1)
    for j in range(blocks):
        lo = j * ATTN_BLOCK
        kk = k_all[lo:lo + 2 * ATTN_BLOCK]
        vv = v_all[lo:lo + 2 * ATTN_BLOCK]
        table = jnp.where(n > 0, 1, 0) if j == 0 else 1
        outs = []
        for h in range(SWA_HEADS):
            g = h // SWA_GROUP
            q_h = q_all[lo:lo + ATTN_BLOCK, h * HEAD_DIM:(h + 1) * HEAD_DIM]
            scores = _dot_nt(kk[:, g * HEAD_DIM:(g + 1) * HEAD_DIM], q_h)
            band = jnp.where(from_prev, scores[:ATTN_BLOCK], scores[ATTN_BLOCK:]) + bias_ref[table, h]
            sink = sink_ref[h]
            m = jnp.maximum(jnp.max(band, axis=0, keepdims=True), sink)
            e = jnp.exp(band - m)
            p = e * (1.0 / (jnp.sum(e, axis=0, keepdims=True) + jnp.exp(sink - m)))
            p = jnp.concatenate([jnp.where(from_prev, p, 0.0), jnp.where(from_prev, 0.0, p)], axis=0)
            outs.append(_dot_tn(p.astype(BF16), vv[:, g * HEAD_DIM:(g + 1) * HEAD_DIM]))
        o_ref[lo:lo + ATTN_BLOCK, :] = jnp.concatenate(outs, axis=-1).astype(o_ref.dtype)


def _swa(sq, sk, sv, sinks, bias2, batch, seq):
    nb = seq // ATTN_BLOCK
    blocks = math.gcd(SWA_STEP_BLOCKS, nb)
    ns = nb // blocks
    own = lambda b, n: (b * ns + n, 0)
    prev = lambda b, n: (b * nb + jnp.maximum(n * blocks - 1, 0), 0)
    kvw = SWA_KV_HEADS * HEAD_DIM
    qw = SWA_HEADS * HEAD_DIM
    rows = blocks * ATTN_BLOCK
    return pl.pallas_call(
        functools.partial(_swa_body, blocks=blocks),
        grid=(batch, ns),
        in_specs=[pl.BlockSpec(memory_space=pltpu.SMEM),
                  pl.BlockSpec((rows, qw), own),
                  pl.BlockSpec((ATTN_BLOCK, kvw), prev),
                  pl.BlockSpec((rows, kvw), own),
                  pl.BlockSpec((ATTN_BLOCK, kvw), prev),
                  pl.BlockSpec((rows, kvw), own),
                  pl.BlockSpec((2, SWA_HEADS, ATTN_BLOCK, ATTN_BLOCK), lambda b, n: (0, 0, 0, 0))],
        out_specs=pl.BlockSpec((rows, qw), own),
        out_shape=jax.ShapeDtypeStruct((batch * seq, qw), BF16),
        compiler_params=_params(("arbitrary", "arbitrary"), 32),
        name="swa",
    )(sinks, sq, sk, sk, sv, sv, bias2)


def _cumsum_rows(x):
    n = x.shape[0]
    row = lax.broadcasted_iota(I32, (n, 1), 0)
    s = 1
    while s < n:
        x = x + jnp.where(row >= s, pltpu.roll(x, s, 0), 0.0)
        s *= 2
    return x


def _rows_from_blocks(vals, width):
    return jnp.concatenate([jnp.broadcast_to(v, (HG_SUB, width)) for v in vals], axis=0)


def _hgrn_body(hq_ref, hf_ref, hi_ref, hg_ref, loglb_ref, log1mlb_ref, nw_ref, bo_ref, o_ref,
               st_ref, a_scr, q_scr, k_scr, oi_scr):
    c = pl.program_id(1)
    n_sub = HG_CHUNK // HG_SUB
    width = HG_WIDTH

    @pl.when(c == 0)
    def _():
        st_ref[...] = jnp.zeros_like(st_ref)

    z = hf_ref[...]
    log_sig = jnp.minimum(z, 0.0) - jnp.log(1.0 + jnp.exp(-jnp.abs(z)))
    t_a = loglb_ref[...]
    t_b = log1mlb_ref[...] + log_sig
    log_f = jnp.maximum(t_a, t_b) + jnp.log(1.0 + jnp.exp(-jnp.abs(t_a - t_b)))
    kk = jnp.exp(t_b - z)
    hq = hq_ref[...]
    qq = hq * jax.nn.sigmoid(hq) * (HG_DK ** -0.5)
    vv = hi_ref[...]

    a_cum = _cumsum_rows(log_f)
    ends = [a_cum[HG_SUB * j + HG_SUB - 1:HG_SUB * j + HG_SUB, :] for j in range(n_sub)]
    zero_row = jnp.zeros((1, width), F32)
    starts = [zero_row] + ends[:-1]
    a_loc = a_cum - _rows_from_blocks(starts, width)
    e_loc = _rows_from_blocks(ends, width) - a_cum
    worst = starts[0] - ends[0]
    for j in range(1, n_sub):
        worst = jnp.maximum(worst, starts[j] - ends[j])
    unsafe = jnp.max(worst) > HG_SAFE_DECAY

    q_sub = qq * jnp.exp(a_loc)
    k_sub = kk * jnp.exp(-a_loc)
    k_end = kk * jnp.exp(e_loc)
    one_row = jnp.ones((1, width), F32)
    q_dec = q_sub * _rows_from_blocks([jnp.exp(s) for s in starts], width)
    k_dec = k_end * _rows_from_blocks([jnp.exp(ends[-1] - e) for e in ends], width)
    q_top = q_sub * _rows_from_blocks(
        [zero_row, zero_row, one_row, jnp.exp(ends[2] - ends[1])], width)
    k_top = k_end * _rows_from_blocks(
        [jnp.exp(ends[1] - ends[0]), one_row, zero_row, zero_row], width)

    row = lax.broadcasted_iota(I32, (HG_CHUNK, 1), 0)
    sub_of_row = row // HG_SUB
    t_idx = lax.broadcasted_iota(I32, (HG_CHUNK, HG_CHUNK), 0)
    s_idx = lax.broadcasted_iota(I32, (HG_CHUNK, HG_CHUNK), 1)
    m_sub = ((t_idx // HG_SUB) == (s_idx // HG_SUB)) & (t_idx >= s_idx)
    m_sub2 = jnp.concatenate([m_sub, m_sub], axis=0)
    lane = lax.broadcasted_iota(I32, (1, LANES), 1)
    lo = lane < HG_DK
    bd = (lax.broadcasted_iota(I32, (LANES, LANES), 0) // HG_DV) == \
         (lax.broadcasted_iota(I32, (LANES, LANES), 1) // HG_DK)
    dec_row = jnp.exp(ends[-1])

    def heads_on_rows(x):
        return jnp.concatenate([jnp.where(lo, x, 0.0), jnp.where(lo, 0.0, x)], axis=0)

    o_inter = []
    for p in range(width // LANES):
        sl = slice(p * LANES, (p + 1) * LANES)
        qs, ks, ke = q_sub[:, sl], k_sub[:, sl], k_end[:, sl]
        p_sub = _dot_nt(heads_on_rows(qs).astype(BF16), ks.astype(BF16))
        q_x = jnp.concatenate([jnp.where(sub_of_row == 1, qs, 0.0),
                               jnp.where(sub_of_row == 3, qs, 0.0), q_top[:, sl]], axis=1)
        k_x = jnp.concatenate([jnp.where(sub_of_row == 0, ke, 0.0),
                               jnp.where(sub_of_row == 2, ke, 0.0), k_top[:, sl]], axis=1)
        q_x2 = jnp.concatenate([jnp.where(jnp.tile(lo, (1, 3)), q_x, 0.0),
                                jnp.where(jnp.tile(lo, (1, 3)), 0.0, q_x)], axis=0)
        p_x = _dot_nt(q_x2.astype(BF16), k_x.astype(BF16))
        p_all = jnp.where(m_sub2, p_sub, 0.0) + p_x
        p_cat = jnp.concatenate([p_all[:HG_CHUNK], p_all[HG_CHUNK:]], axis=1)
        v_p = vv[:, sl]
        oi_scr[:, sl] = _dot(p_cat.astype(BF16), heads_on_rows(v_p).astype(BF16))
        st = st_ref[p]
        o_inter.append(_dot_nt(q_dec[:, sl].astype(BF16), st.astype(BF16)))
        upd = _dot_tn(v_p.astype(BF16), k_dec[:, sl].astype(BF16))
        st_ref[p] = st * dec_row[:, sl] + jnp.where(bd, upd, 0.0)
    o_inter = jnp.concatenate(o_inter, axis=1)

    @pl.when(unsafe)
    def _():
        a_scr[...] = a_cum
        q_scr[...] = qq
        k_scr[...] = kk

        def body(t, carry):
            d = a_scr[pl.ds(t, 1), :] - a_scr[...]
            w = jnp.exp(jnp.where(row <= t, d, -jnp.inf))
            prod = (q_scr[pl.ds(t, 1), :] * k_scr[...]) * w
            e = _dot(prod.astype(BF16), bo_ref[...])
            oi_scr[pl.ds(t, 1), :] = jnp.sum(e * hi_ref[...], axis=0, keepdims=True)
            return carry

        lax.fori_loop(0, HG_CHUNK, body, 0)

    o = oi_scr[...] + o_inter
    sq = o * o
    sq_hi = sq.astype(BF16)
    sq_lo = (sq - sq_hi.astype(F32)).astype(BF16)
    ms = (_dot(sq_hi, bo_ref[...]) + _dot(sq_lo, bo_ref[...])) * (1.0 / HG_DV)
    gate = hg_ref[...]
    o = o * lax.rsqrt(ms + RMS_EPS) * nw_ref[...] * (gate * jax.nn.sigmoid(gate))
    o_ref[...] = o.astype(o_ref.dtype)


def _hgrn(hq, hf, hi, hg, loglb, log1mlb, nw_row, head_ones, batch, seq):
    nc = seq // HG_CHUNK
    blk = lambda b, c: (b * nc + c, 0)
    row = lambda b, c: (0, 0)
    w = HG_WIDTH
    return pl.pallas_call(
        _hgrn_body,
        grid=(batch, nc),
        in_specs=[pl.BlockSpec((HG_CHUNK, w), blk)] * 4 + [pl.BlockSpec((1, w), row)] * 3
                 + [pl.BlockSpec((w, w), row)],
        out_specs=pl.BlockSpec((HG_CHUNK, w), blk),
        out_shape=jax.ShapeDtypeStruct((batch * seq, w), BF16),
        scratch_shapes=[pltpu.VMEM((w // LANES, LANES, LANES), F32)]
                       + [pltpu.VMEM((HG_CHUNK, w), F32)] * 4,
        compiler_params=_params(("arbitrary", "arbitrary"), 32),
        name="hgrn",
    )(hq, hf, hi, hg, loglb, log1mlb, nw_row, head_ones)


def _mem_attn_body(q_ref, k_ref, v_ref, o_ref):
    q_all = q_ref[...] * ATTN_SCALE
    outs = []
    for h in range(MEM_HEADS):
        sl = slice(h * HEAD_DIM, (h + 1) * HEAD_DIM)
        scores = _dot_nt(k_ref[0, :, sl], q_all[:, sl])
        e = jnp.exp(scores - jnp.max(scores, axis=0, keepdims=True))
        p = e * (1.0 / jnp.sum(e, axis=0, keepdims=True))
        outs.append(_dot_tn(p.astype(BF16), v_ref[0, :, sl]))
    o_ref[...] = jnp.concatenate(outs, axis=-1).astype(o_ref.dtype)


def _mem_attn(mq, mk, mv, layer, batch, seq, mem_len):
    tq = min(512, seq)
    nq = seq // tq
    w = MEM_HEADS * HEAD_DIM
    return pl.pallas_call(
        _mem_attn_body,
        grid=(batch, nq),
        in_specs=[pl.BlockSpec((tq, w), lambda b, i: (b * nq + i, 0)),
                  pl.BlockSpec((1, mem_len, w), lambda b, i: (layer, b, 0)),
                  pl.BlockSpec((1, mem_len, w), lambda b, i: (layer, b, 0))],
        out_specs=pl.BlockSpec((tq, w), lambda b, i: (b * nq + i, 0)),
        out_shape=jax.ShapeDtypeStruct((batch * seq, w), BF16),
        compiler_params=_params(("arbitrary", "arbitrary"), 32),
        name="mem_attn",
    )(mq, mk, mv)


def _layer_norm(h, g, b):
    mu = jnp.mean(h, axis=-1, keepdims=True)
    d = h - mu
    var = jnp.mean(d * d, axis=-1, keepdims=True)
    return d * lax.rsqrt(var + LN_EPS) * g + b


def _route_rows(sel):
    def row(a, r):
        return a[r:r + 1, :]

    best = None
    for g in range(N_GROUPS):
        a = [row(sel, 4 * g + i) for i in range(4)]
        gs = a[0] + a[1]
        for i, j in ((0, 2), (0, 3), (1, 2), (1, 3), (2, 3)):
            gs = jnp.maximum(gs, a[i] + a[j])
        if best is None:
            best, best_score = jnp.zeros_like(gs, dtype=I32), gs
        else:
            better = gs > best_score
            best = jnp.where(better, g, best)
            best_score = jnp.where(better, gs, best_score)

    def pick(arr, i):
        out = row(arr, i)
        for g in range(1, N_GROUPS):
            out = jnp.where(best == g, row(arr, 4 * g + i), out)
        return out

    a = [pick(sel, i) for i in range(4)]
    i1, m1 = jnp.zeros_like(best), a[0]
    for i in range(1, 4):
        gt = a[i] > m1
        i1 = jnp.where(gt, i, i1)
        m1 = jnp.where(gt, a[i], m1)
    i2 = jnp.full_like(best, -1)
    m2 = jnp.full_like(m1, -jnp.inf)
    for i in range(4):
        gt = (i1 != i) & (a[i] > m2)
        i2 = jnp.where(gt, i, i2)
        m2 = jnp.where(gt, a[i], m2)
    lo = jnp.minimum(i1, i2)
    hi = jnp.maximum(i1, i2)
    pair = jnp.where(lo == 0, hi - 1, jnp.where(lo == 1, hi + 1, 5))
    return best * N_PAIRS + pair


def _out_proj_body(x_ref, swa_ref, hg_ref, mem_ref, w_ref, g_ref, b_ref, wr_ref, rb_ref, tri_ref,
                   x1_ref, cls_ref, rank_ref, cnt_ref, carry_ref):
    i = pl.program_id(0)
    t = x_ref.shape[0]

    @pl.when(i == 0)
    def _():
        carry_ref[...] = jnp.zeros_like(carry_ref)

    acc = _dot(swa_ref[...], w_ref[0:512, :]) + _dot(hg_ref[...], w_ref[512:768, :]) \
        + _dot(mem_ref[...], w_ref[768:1024, :])
    x1 = _layer_norm(ALPHA * x_ref[...] + acc, g_ref[...], b_ref[...])
    x1_ref[...] = _to_tiles(x1)

    logits = _dot_nt(wr_ref[...], x1.astype(BF16))
    cls = _route_rows(jax.nn.sigmoid(logits) + rb_ref[...])
    onehot = (lax.broadcasted_iota(I32, (CLASS_ROWS, t), 0) == cls).astype(F32)
    prefix = _dot(onehot.astype(BF16), tri_ref[...])
    carry = carry_ref[...]
    rank = jnp.sum(onehot * (prefix - 1.0 + carry[:, 0:1]), axis=0, keepdims=True)
    new_carry = carry + prefix[:, t - 1:t]
    carry_ref[...] = new_carry
    cnt_ref[...] = new_carry
    cls_ref[0] = cls
    rank_ref[0] = rank.astype(I32)


def _out_proj(x2, swa_o, hg_o, mem_o, w_bf, g_row, b_row, wr_t, rb_col, tri):
    n = x2.shape[0]
    t = tri.shape[0]
    nt = n // t
    tok = lambda i: (i, 0)
    fixed = lambda i: (0, 0)
    per_tile = pl.BlockSpec((1, 1, t), lambda i: (i, 0, 0))
    return pl.pallas_call(
        _out_proj_body,
        grid=(nt,),
        in_specs=[pl.BlockSpec((t, D_MODEL), tok), pl.BlockSpec((t, 512), tok),
                  pl.BlockSpec((t, 256), tok), pl.BlockSpec((t, 256), tok),
                  pl.BlockSpec((MIX_WIDTH, D_MODEL), fixed),
                  pl.BlockSpec((1, D_MODEL), fixed), pl.BlockSpec((1, D_MODEL), fixed),
                  pl.BlockSpec((N_EXPERTS, D_MODEL), fixed), pl.BlockSpec((N_EXPERTS, 1), fixed),
                  pl.BlockSpec((t, t), fixed)],
        out_specs=[pl.BlockSpec((t * ROW_TILES, LANES), tok), per_tile, per_tile,
                   pl.BlockSpec((CLASS_ROWS, LANES), fixed)],
        out_shape=[jax.ShapeDtypeStruct((n * ROW_TILES, LANES), F32),
                   jax.ShapeDtypeStruct((nt, 1, t), I32), jax.ShapeDtypeStruct((nt, 1, t), I32),
                   jax.ShapeDtypeStruct((CLASS_ROWS, LANES), F32)],
        scratch_shapes=[pltpu.VMEM((CLASS_ROWS, LANES), F32)],
        compiler_params=_params(("arbitrary",), 48),
        name="out_proj_ln_route",
    )(x2, swa_o, hg_o, mem_o, w_bf, g_row, b_row, wr_t, rb_col, tri)


def _tile_copy(src_ref, dst_ref, sem, src_row, dst_row, tiles):
    n = tiles * SUBLANES
    return pltpu.make_async_copy(
        src_ref.at[pl.ds(pl.multiple_of(src_row * n, n), n), :],
        dst_ref.at[pl.ds(pl.multiple_of(dst_row * n, n), n), :], sem)


def _scatter_body(cstart_ref, cls_ref, rank_ref, x_hbm, slot_tok_ref, xs_hbm,
                  buf, dest_vmem, dest_smem, load_sem, scat_sem, misc_sem, *, n_steps, tile):
    i = pl.program_id(0)
    rows = tile * SUBLANES
    slot = i % 3

    def load(step, into):
        return pltpu.make_async_copy(
            x_hbm.at[pl.ds(pl.multiple_of(step * rows, rows), rows), :], buf.at[into], load_sem.at[into])

    def scatter_done(of):
        return pltpu.make_async_copy(buf.at[of], xs_hbm.at[pl.ds(0, rows), :], scat_sem.at[of])

    @pl.when(i == 0)
    def _():
        load(0, 0).start()

    @pl.when(i + 1 < n_steps)
    def _():
        load(i + 1, (i + 1) % 3).start()

    cls = cls_ref[0]
    dest = rank_ref[0]
    for c in range(N_CLASSES):
        dest = dest + jnp.where(cls == c, cstart_ref[c], 0)
    dest_vmem[...] = dest
    to_smem = pltpu.make_async_copy(dest_vmem, dest_smem, misc_sem)
    to_smem.start()
    to_smem.wait()

    load(i, slot).wait()

    def issue(group, carry):
        for k in range(DMA_GROUP):
            j = group * DMA_GROUP + k
            dst = dest_smem[0, j]
            slot_tok_ref[dst] = i * tile + j
            _tile_copy(buf.at[slot], xs_hbm, scat_sem.at[slot], j, dst, 1).start(priority=k % 2)
        return carry

    lax.fori_loop(0, tile // DMA_GROUP, issue, 0)

    @pl.when(i >= 1)
    def _():
        scatter_done((i + 2) % 3).wait()

    @pl.when(i == n_steps - 1)
    def _():
        scatter_done(slot).wait()


def _scatter_rows(x1_z, cls, rank, cstart):
    nt, _, tile = cls.shape
    n = nt * tile
    per_tile = pl.BlockSpec((1, 1, tile), lambda i, cs: (i, 0, 0))
    return pl.pallas_call(
        functools.partial(_scatter_body, n_steps=nt, tile=tile),
        grid_spec=pltpu.PrefetchScalarGridSpec(
            num_scalar_prefetch=1,
            grid=(nt,),
            in_specs=[per_tile, per_tile, pl.BlockSpec(memory_space=pl.ANY)],
            out_specs=[pl.BlockSpec(memory_space=pltpu.SMEM), pl.BlockSpec(memory_space=pl.ANY)],
            scratch_shapes=[pltpu.VMEM((3, tile * SUBLANES, LANES), F32),
                            pltpu.VMEM((1, tile), I32), pltpu.SMEM((1, tile), I32),
                            pltpu.SemaphoreType.DMA((3,)), pltpu.SemaphoreType.DMA((3,)),
                            pltpu.SemaphoreType.DMA(())]),
        out_shape=[jax.ShapeDtypeStruct((n,), I32),
                   jax.ShapeDtypeStruct((n * SUBLANES, LANES), F32)],
        compiler_params=_params(("arbitrary",), 32),
        name="scatter_rows",
    )(cstart, cls, rank, x1_z)


def _expert_body(blk_ref, elo_ref, ehi_ref, start_ref, end_ref, lo_new_ref, hi_new_ref, slot_tok_ref,
                 xs_ref, wr_ref, g_lo32, u_lo32, d_lo32, g_hi32, u_hi32, d_hi32, y_hbm,
                 g_lo, u_lo, d_lo, g_hi, u_hi, d_hi, ybuf, state, sem, *, n_items, n_tok):
    i = pl.program_id(0)
    start = start_ref[i]
    end = end_ref[i]
    rows = MOE_BLK * SUBLANES

    def rows_done(of):
        return pltpu.make_async_copy(ybuf.at[of], y_hbm.at[pl.ds(0, rows), :], sem.at[of])

    @pl.when(i == 0)
    def _():
        state[0] = 0
        state[1] = -1

    @pl.when(lo_new_ref[i] == 1)
    def _():
        for dst, src in ((g_lo, g_lo32), (u_lo, u_lo32), (d_lo, d_lo32)):
            dst[...] = src[0, 0].astype(BF16)

    @pl.when(hi_new_ref[i] == 1)
    def _():
        for dst, src in ((g_hi, g_hi32), (u_hi, u_hi32), (d_hi, d_hi32)):
            dst[...] = src[0, 0].astype(BF16)

    @pl.when(end > start)
    def _():
        x = _from_tiles(xs_ref[...], MOE_BLK).astype(BF16)
        scores = jax.nn.sigmoid(_dot(x, wr_ref[...]))
        lane = lax.broadcasted_iota(I32, (1, LANES), 1)
        s_lo = jnp.sum(jnp.where(lane == elo_ref[i], scores, 0.0), axis=-1, keepdims=True)
        s_hi = jnp.sum(jnp.where(lane == ehi_ref[i], scores, 0.0), axis=-1, keepdims=True)
        denom = s_lo + s_hi

        def ffn(gw, uw, dw, w_col):
            g = _dot(x, gw[...])
            u = _dot(x, uw[...])
            return _dot((((g * jax.nn.sigmoid(g)) * u) * w_col).astype(BF16), dw[...])

        into = state[0] % 2
        ybuf[into] = _to_tiles(ffn(g_lo, u_lo, d_lo, s_lo / denom) + ffn(g_hi, u_hi, d_hi, s_hi / denom))

        base = blk_ref[i] * MOE_BLK
        spare = n_tok + into * MOE_BLK

        def issue(group, carry):
            for k in range(DMA_GROUP):
                r = group * DMA_GROUP + k
                s = base + r
                mine = (s >= start) & (s < end)
                dst = jnp.where(mine, slot_tok_ref[jnp.minimum(s, n_tok - 1)], spare + r)
                _tile_copy(ybuf.at[into], y_hbm, sem.at[into], r, dst, 1).start(priority=k % 2)
            return carry

        lax.fori_loop(0, MOE_BLK // DMA_GROUP, issue, 0)

        @pl.when(state[1] >= 0)
        def _():
            rows_done(state[1]).wait()

        state[0] = state[0] + 1
        state[1] = into

    @pl.when((i == n_items - 1) & (state[1] >= 0))
    def _():
        rows_done(state[1]).wait()


def _experts(xs_z, slot_tok, items, wr_pad, wg, wu, wd, layer):
    blk, elo, ehi, start, end = items
    n_items = blk.shape[0]
    n_tok = xs_z.shape[0] // SUBLANES
    changed = lambda e: jnp.concatenate([jnp.ones((1,), I32), (e[1:] != e[:-1]).astype(I32)])

    def data_map(i, blk, *_):
        return (blk[i], 0)

    def w_lo_map(i, blk, elo, *_):
        return (layer, elo[i], 0, 0)

    def w_hi_map(i, blk, elo, ehi, *_):
        return (layer, ehi[i], 0, 0)

    up_spec = lambda m: pl.BlockSpec((1, 1, D_MODEL, D_EXPERT), m)
    down_spec = lambda m: pl.BlockSpec((1, 1, D_EXPERT, D_MODEL), m)
    up_bf = pltpu.VMEM((D_MODEL, D_EXPERT), BF16)
    down_bf = pltpu.VMEM((D_EXPERT, D_MODEL), BF16)
    return pl.pallas_call(
        functools.partial(_expert_body, n_items=n_items, n_tok=n_tok),
        grid_spec=pltpu.PrefetchScalarGridSpec(
            num_scalar_prefetch=8,
            grid=(n_items,),
            in_specs=[pl.BlockSpec((MOE_BLK * SUBLANES, LANES), data_map),
                      pl.BlockSpec((D_MODEL, LANES), lambda i, *_: (0, 0)),
                      up_spec(w_lo_map), up_spec(w_lo_map), down_spec(w_lo_map),
                      up_spec(w_hi_map), up_spec(w_hi_map), down_spec(w_hi_map)],
            out_specs=pl.BlockSpec(memory_space=pl.ANY),
            scratch_shapes=[up_bf, up_bf, down_bf, up_bf, up_bf, down_bf,
                            pltpu.VMEM((2, MOE_BLK * SUBLANES, LANES), F32), pltpu.SMEM((2,), I32),
                            pltpu.SemaphoreType.DMA((2,))]),
        out_shape=jax.ShapeDtypeStruct(((n_tok + 2 * MOE_BLK) * SUBLANES, LANES), F32),
        compiler_params=_params(("arbitrary",), 56),
        name="experts",
    )(blk, elo, ehi, start, end, changed(elo), changed(ehi), slot_tok,
      xs_z, wr_pad, wg, wu, wd, wg, wu, wd)


def _residual_ln2(x1_ref, y_ref, g_ref, b_ref, tile):
    h = ALPHA * _from_tiles(x1_ref[...], tile) + _from_tiles(y_ref[...], tile)
    return _layer_norm(h, g_ref[...], b_ref[...])


def _ln2_body(x1_ref, y_ref, g_ref, b_ref, o_ref, *, tile):
    o_ref[...] = _residual_ln2(x1_ref, y_ref, g_ref, b_ref, tile)


def _ln2_in_proj_body(x1_ref, y_ref, g_ref, b_ref, w_ref, bias_ref, o_ref, *proj_refs, tile):
    x = _residual_ln2(x1_ref, y_ref, g_ref, b_ref, tile)
    o_ref[...] = x
    acc = _dot(x.astype(BF16), w_ref[...]) + bias_ref[...]
    off = 0
    for ref, width in zip(proj_refs, SPLITS):
        ref[...] = acc[:, off:off + width].astype(ref.dtype)
        off += width


def _ln2(x1_z, y_z, g_row, b_row, tile, w_bf=None, bias_row=None):
    n = x1_z.shape[0] // SUBLANES
    tok = lambda i: (i, 0)
    fixed = lambda i: (0, 0)
    in_specs = [pl.BlockSpec((tile * SUBLANES, LANES), tok), pl.BlockSpec((tile * SUBLANES, LANES), tok),
                pl.BlockSpec((1, D_MODEL), fixed), pl.BlockSpec((1, D_MODEL), fixed)]
    out_specs = [pl.BlockSpec((tile, D_MODEL), tok)]
    out_shape = [jax.ShapeDtypeStruct((n, D_MODEL), F32)]
    operands = [x1_z, y_z, g_row, b_row]
    body = _ln2_body
    if w_bf is not None:
        in_specs += [pl.BlockSpec((D_MODEL, IN_WIDTH), fixed), pl.BlockSpec((1, IN_WIDTH), fixed)]
        out_specs += [pl.BlockSpec((tile, w), tok) for w in SPLITS]
        out_shape += [jax.ShapeDtypeStruct((n, w), dt) for w, dt in zip(SPLITS, PROJ_DTYPES)]
        operands += [w_bf, bias_row]
        body = _ln2_in_proj_body
    return pl.pallas_call(
        functools.partial(body, tile=tile),
        grid=(n // tile,),
        in_specs=in_specs,
        out_specs=out_specs,
        out_shape=out_shape,
        compiler_params=_params(("arbitrary",), 56),
        name="ln2" if w_bf is None else "ln2_in_proj",
    )(*operands)


def _t5_bucket(dist):
    max_exact = N_BUCKETS // 2
    d = jnp.maximum(dist, 0)
    large = max_exact + (jnp.log(jnp.maximum(d, 1).astype(F32) / max_exact)
                         / math.log(MAX_DISTANCE / max_exact) * (N_BUCKETS - max_exact)).astype(I32)
    large = jnp.minimum(large, N_BUCKETS - 1)
    return jnp.where(d < max_exact, d, large)


def _banded_bias(rel_bias):
    i = jnp.arange(ATTN_BLOCK)[:, None]
    j = jnp.arange(2 * ATTN_BLOCK)[None, :]
    dist = i + ATTN_BLOCK - j
    bucket = _t5_bucket(dist)[None]
    table = rel_bias.astype(F32)
    bias = jnp.zeros((SWA_HEADS, ATTN_BLOCK, 2 * ATTN_BLOCK), F32)
    for b in range(N_BUCKETS):
        bias = jnp.where(bucket == b, table[b][:, None, None], bias)
    from_prev = (jnp.arange(ATTN_BLOCK)[None, :] > i)[None]
    prev, own = bias[:, :, :ATTN_BLOCK], bias[:, :, ATTN_BLOCK:]
    normal = jnp.where(from_prev, prev, own)
    first = jnp.where(from_prev, -jnp.inf, own)
    return jnp.swapaxes(jnp.stack([first, normal]), 2, 3)


def _work_items(counts, n_tok):
    n_blocks = n_tok // MOE_BLK
    cend = jnp.cumsum(counts)
    cstart = cend - counts
    blk_starts = jnp.arange(n_blocks, dtype=I32) * MOE_BLK
    cls_starts = jnp.where(counts > 0, cstart, n_tok)
    start = jnp.sort(jnp.concatenate([blk_starts, cls_starts]))
    end = jnp.concatenate([start[1:], jnp.full((1,), n_tok, I32)])
    blk = jnp.minimum(start, n_tok - 1) // MOE_BLK
    cls = jnp.minimum(jnp.sum((cend[None, :] <= start[:, None]).astype(I32), axis=1), N_CLASSES - 1)
    group, pair = cls // N_PAIRS, cls % N_PAIRS
    lo = (pair >= 3).astype(I32) + (pair >= 5).astype(I32)
    hi = jnp.where(pair < 3, pair + 1, jnp.where(pair < 5, pair - 1, 3))
    return cstart, (blk, group * EXPERTS_PER_GROUP + lo, group * EXPERTS_PER_GROUP + hi, start, end)


def kernel(x, mem, w_in, b_in, w_mem_kv, attn_sinks, rel_bias, hgrn_lb_logits, hgrn_norm, w_out,
           ln1_g, ln1_b, w_router, router_bias, w_gate, w_up, w_down, ln2_g, ln2_b):
    batch, seq, _ = x.shape
    mem_len = mem.shape[1]
    n_tok = batch * seq
    route_tile = min(512, n_tok)

    bias = _banded_bias(rel_bias)
    lb = jnp.cumsum(jax.nn.softmax(hgrn_lb_logits.astype(F32), axis=0), axis=0)
    lb = lb - lb[0:1]
    log_lb = jnp.log(lb)
    log_1m_lb = jnp.log1p(-lb)
    head_ones = (jnp.arange(HG_WIDTH)[:, None] // HG_DV == jnp.arange(HG_WIDTH)[None, :] // HG_DV).astype(BF16)
    tri = (jnp.arange(route_tile)[:, None] <= jnp.arange(route_tile)[None, :]).astype(BF16)
    wr_t = jnp.transpose(w_router).astype(BF16)
    wr_pad = jnp.pad(w_router.astype(BF16), ((0, 0), (0, LANES - N_EXPERTS)))
    rb_col = router_bias.astype(F32).reshape(N_EXPERTS, 1)
    w_in_bf = w_in.astype(BF16)
    w_out_bf = w_out.astype(BF16)

    mk, mv = _mem_kv(mem.reshape(batch * mem_len, D_MODEL), w_mem_kv)

    x2 = x.reshape(n_tok, D_MODEL)
    proj = _in_proj(x2, w_in_bf[0], b_in[0].reshape(1, IN_WIDTH))
    for l in range(DEPTH):
        sq, sk, sv, hq, hf, hi, hg, mq = proj
        swa_o = _swa(sq, sk, sv, attn_sinks[l].astype(F32), bias, batch, seq)
        hg_o = _hgrn(hq, hf, hi, hg, log_lb[l].reshape(1, HG_WIDTH), log_1m_lb[l].reshape(1, HG_WIDTH),
                     jnp.tile(hgrn_norm[l].astype(F32), HG_HEADS).reshape(1, HG_WIDTH), head_ones, batch, seq)
        mem_o = _mem_attn(mq, mk, mv, l, batch, seq, mem_len)
        x1_z, cls, rank, counts = _out_proj(
            x2, swa_o, hg_o, mem_o, w_out_bf[l], ln1_g[l].reshape(1, D_MODEL), ln1_b[l].reshape(1, D_MODEL),
            wr_t, rb_col, tri)
        cstart, items = _work_items(counts[:N_CLASSES, 0].astype(I32), n_tok)
        slot_tok, xs_z = _scatter_rows(x1_z, cls, rank, cstart)
        y_z = _experts(xs_z, slot_tok, items, wr_pad, w_gate, w_up, w_down, l)
        ln2_args = (x1_z, y_z, ln2_g[l].reshape(1, D_MODEL), ln2_b[l].reshape(1, D_MODEL), route_tile)
        if l + 1 < DEPTH:
            x2, *proj = _ln2(*ln2_args, w_in_bf[l + 1], b_in[l + 1].reshape(1, IN_WIDTH))
        else:
            (x2,) = _ln2(*ln2_args)
    return x2.reshape(batch, seq, D_MODEL)
```

```python
import functools
import math

import jax
import jax.numpy as jnp
from jax import lax
from jax.experimental import pallas as pl
from jax.experimental.pallas import tpu as pltpu

F32 = jnp.float32
BF16 = jnp.bfloat16
I32 = jnp.int32

D_MODEL = 1024
DEPTH = 4
HEAD_DIM = 64
SWA_HEADS = 8
SWA_KV_HEADS = 2
SWA_GROUP = SWA_HEADS // SWA_KV_HEADS
WINDOW = 128
ATTN_BLOCK = 128
HG_HEADS = 4
HG_DK = 64
HG_DV = 64
HG_WIDTH = HG_HEADS * HG_DK
MEM_HEADS = 4
N_BUCKETS = 32
MAX_DISTANCE = 128
N_EXPERTS = 16
N_GROUPS = 4
EXPERTS_PER_GROUP = 4
D_EXPERT = 512
LN_EPS = 1e-5
RMS_EPS = 1e-6
ALPHA = (2 * DEPTH) ** 0.25
SPLITS = (512, 128, 128, 256, 256, 256, 256, 256)
IN_WIDTH = sum(SPLITS)
PROJ_DTYPES = (BF16, BF16, BF16, F32, F32, F32, F32, BF16)
MIX_WIDTH = 1024
ATTN_SCALE = HEAD_DIM ** -0.5

SUBLANES = 8
LANES = 128
ROW_TILES = D_MODEL // LANES

SWA_STEP_BLOCKS = 4
HG_CHUNK = 128
HG_SUB = 32
HG_SAFE_DECAY = 80.0
N_PAIRS = 6
N_CLASSES = N_GROUPS * N_PAIRS
CLASS_ROWS = 32
MOE_BLK = 256
DMA_GROUP = 8


def _params(semantics, vmem_mib):
    return pltpu.CompilerParams(dimension_semantics=semantics, vmem_limit_bytes=vmem_mib * 1024 * 1024)


def _dot(a, b):
    return jnp.dot(a, b, preferred_element_type=F32)


def _dot_nt(a, b):
    return lax.dot_general(a, b, (((1,), (1,)), ((), ())), preferred_element_type=F32)


def _dot_tn(a, b):
    return lax.dot_general(a, b, (((0,), (0,)), ((), ())), preferred_element_type=F32)


def _from_tiles(z, n_rows):
    k = z.shape[0] // n_rows
    return z.reshape(n_rows, k, LANES).reshape(n_rows, k * LANES)


def _to_tiles(val):
    n_rows, width = val.shape
    return val.reshape(n_rows, width // LANES, LANES).reshape(n_rows * (width // LANES), LANES)


def _in_proj_body(x_ref, w_ref, b_ref, *out_refs):
    acc = _dot(x_ref[...].astype(BF16), w_ref[...]) + b_ref[...]
    off = 0
    for ref, width in zip(out_refs, SPLITS):
        ref[...] = acc[:, off:off + width].astype(ref.dtype)
        off += width


def _in_proj(x2, w_bf, b_row):
    n = x2.shape[0]
    tm = min(512, n)
    return pl.pallas_call(
        _in_proj_body,
        grid=(n // tm,),
        in_specs=[pl.BlockSpec((tm, D_MODEL), lambda i: (i, 0)),
                  pl.BlockSpec((D_MODEL, IN_WIDTH), lambda i: (0, 0)),
                  pl.BlockSpec((1, IN_WIDTH), lambda i: (0, 0))],
        out_specs=[pl.BlockSpec((tm, w), lambda i: (i, 0)) for w in SPLITS],
        out_shape=[jax.ShapeDtypeStruct((n, w), dt) for w, dt in zip(SPLITS, PROJ_DTYPES)],
        compiler_params=_params(("arbitrary",), 48),
        name="in_proj",
    )(x2, w_bf, b_row)


def _mem_kv_body(mem_ref, w_ref, k_ref, v_ref):
    acc = _dot(mem_ref[...].astype(BF16), w_ref[0].astype(BF16))
    half = MEM_HEADS * HEAD_DIM
    k_ref[0] = acc[:, :half].astype(BF16)
    v_ref[0] = acc[:, half:].astype(BF16)


def _mem_kv(mem2, w_mem_kv):
    rows = mem2.shape[0]
    tm = min(512, rows)
    half = MEM_HEADS * HEAD_DIM
    return pl.pallas_call(
        _mem_kv_body,
        grid=(DEPTH, rows // tm),
        in_specs=[pl.BlockSpec((tm, D_MODEL), lambda l, i: (i, 0)),
                  pl.BlockSpec((1, D_MODEL, 2 * half), lambda l, i: (l, 0, 0))],
        out_specs=[pl.BlockSpec((1, tm, half), lambda l, i: (l, i, 0)),
                   pl.BlockSpec((1, tm, half), lambda l, i: (l, i, 0))],
        out_shape=[jax.ShapeDtypeStruct((DEPTH, rows, half), BF16)] * 2,
        compiler_params=_params(("arbitrary", "arbitrary"), 32),
        name="mem_kv",
    )(mem2, w_mem_kv)


def _swa_body(sink_ref, q_ref, kp_ref, ko_ref, vp_ref, vo_ref, bias_ref, o_ref, *, blocks):
    n = pl.program_id(1)
    q_all = q_ref[...] * ATTN_SCALE
    k_all = jnp.concatenate([kp_ref[...], ko_ref[...]], axis=0)
    v_all = jnp.concatenate([vp_ref[...], vo_ref[...]], axis=0)
    from_prev = lax.broadcasted_iota(I32, (ATTN_BLOCK, ATTN_BLOCK), 0) > \
        lax.broadcasted_iota(I32, (ATTN_BLOCK, ATTN_BLOCK), 1)
    for j in range(blocks):
        lo = j * ATTN_BLOCK
        kk = k_all[lo:lo + 2 * ATTN_BLOCK]
        vv = v_all[lo:lo + 2 * ATTN_BLOCK]
        table = jnp.where(n > 0, 1, 0) if j == 0 else 1
        outs = []
        for h in range(SWA_HEADS):
            g = h // SWA_GROUP
            q_h = q_all[lo:lo + ATTN_BLOCK, h * HEAD_DIM:(h + 1) * HEAD_DIM]
            scores = _dot_nt(kk[:, g * HEAD_DIM:(g + 1) * HEAD_DIM], q_h)
            band = jnp.where(from_prev, scores[:ATTN_BLOCK], scores[ATTN_BLOCK:]) + bias_ref[table, h]
            sink = sink_ref[h]
            m = jnp.maximum(jnp.max(band, axis=0, keepdims=True), sink)
            e = jnp.exp(band - m)
            p = e * (1.0 / (jnp.sum(e, axis=0, keepdims=True) + jnp.exp(sink - m)))
            p = jnp.concatenate([jnp.where(from_prev, p, 0.0), jnp.where(from_prev, 0.0, p)], axis=0)
            outs.append(_dot_tn(p.astype(BF16), vv[:, g * HEAD_DIM:(g + 1) * HEAD_DIM]))
        o_ref[lo:lo + ATTN_BLOCK, :] = jnp.concatenate(outs, axis=-1).astype(o_ref.dtype)


def _swa(sq, sk, sv, sinks, bias2, batch, seq):
    nb = seq // ATTN_BLOCK
    blocks = math.gcd(SWA_STEP_BLOCKS, nb)
    ns = nb // blocks
    own = lambda b, n: (b * ns + n, 0)
    prev = lambda b, n: (b * nb + jnp.maximum(n * blocks - 1, 0), 0)
    kvw = SWA_KV_HEADS * HEAD_DIM
    qw = SWA_HEADS * HEAD_DIM
    rows = blocks * ATTN_BLOCK
    return pl.pallas_call(
        functools.partial(_swa_body, blocks=blocks),
        grid=(batch, ns),
        in_specs=[pl.BlockSpec(memory_space=pltpu.SMEM),
                  pl.BlockSpec((rows, qw), own),
                  pl.BlockSpec((ATTN_BLOCK, kvw), prev),
                  pl.BlockSpec((rows, kvw), own),
                  pl.BlockSpec((ATTN_BLOCK, kvw), prev),
                  pl.BlockSpec((rows, kvw), own),
                  pl.BlockSpec((2, SWA_HEADS, ATTN_BLOCK, ATTN_BLOCK), lambda b, n: (0, 0, 0, 0))],
        out_specs=pl.BlockSpec((rows, qw), own),
        out_shape=jax.ShapeDtypeStruct((batch * seq, qw), BF16),
        compiler_params=_params(("arbitrary", "arbitrary"), 32),
        name="swa",
    )(sinks, sq, sk, sk, sv, sv, bias2)


def _cumsum_rows(x):
    n = x.shape[0]
    row = lax.broadcasted_iota(I32, (n, 1), 0)
    s = 1
    while s < n:
        x = x + jnp.where(row >= s, pltpu.roll(x, s, 0), 0.0)
        s *= 2
    return x


def _rows_from_blocks(vals, width):
    return jnp.concatenate([jnp.broadcast_to(v, (HG_SUB, width)) for v in vals], axis=0)


def _hgrn_body(hq_ref, hf_ref, hi_ref, hg_ref, loglb_ref, log1mlb_ref, nw_ref, bo_ref, o_ref,
               st_ref, a_scr, q_scr, k_scr, oi_scr):
    c = pl.program_id(1)
    n_sub = HG_CHUNK // HG_SUB
    width = HG_WIDTH

    @pl.when(c == 0)
    def _():
        st_ref[...] = jnp.zeros_like(st_ref)

    z = hf_ref[...]
    log_sig = jnp.minimum(z, 0.0) - jnp.log(1.0 + jnp.exp(-jnp.abs(z)))
    t_a = loglb_ref[...]
    t_b = log1mlb_ref[...] + log_sig
    log_f = jnp.maximum(t_a, t_b) + jnp.log(1.0 + jnp.exp(-jnp.abs(t_a - t_b)))
    kk = jnp.exp(t_b - z)
    hq = hq_ref[...]
    qq = hq * jax.nn.sigmoid(hq) * (HG_DK ** -0.5)
    vv = hi_ref[...]

    a_cum = _cumsum_rows(log_f)
    ends = [a_cum[HG_SUB * j + HG_SUB - 1:HG_SUB * j + HG_SUB, :] for j in range(n_sub)]
    zero_row = jnp.zeros((1, width), F32)
    starts = [zero_row] + ends[:-1]
    a_loc = a_cum - _rows_from_blocks(starts, width)
    e_loc = _rows_from_blocks(ends, width) - a_cum
    worst = starts[0] - ends[0]
    for j in range(1, n_sub):
        worst = jnp.maximum(worst, starts[j] - ends[j])
    unsafe = jnp.max(worst) > HG_SAFE_DECAY

    q_sub = qq * jnp.exp(a_loc)
    k_sub = kk * jnp.exp(-a_loc)
    k_end = kk * jnp.exp(e_loc)
    one_row = jnp.ones((1, width), F32)
    q_dec = q_sub * _rows_from_blocks([jnp.exp(s) for s in starts], width)
    k_dec = k_end * _rows_from_blocks([jnp.exp(ends[-1] - e) for e in ends], width)
    q_top = q_sub * _rows_from_blocks(
        [zero_row, zero_row, one_row, jnp.exp(ends[2] - ends[1])], width)
    k_top = k_end * _rows_from_blocks(
        [jnp.exp(ends[1] - ends[0]), one_row, zero_row, zero_row], width)

    row = lax.broadcasted_iota(I32, (HG_CHUNK, 1), 0)
    sub_of_row = row // HG_SUB
    t_idx = lax.broadcasted_iota(I32, (HG_CHUNK, HG_CHUNK), 0)
    s_idx = lax.broadcasted_iota(I32, (HG_CHUNK, HG_CHUNK), 1)
    m_sub = ((t_idx // HG_SUB) == (s_idx // HG_SUB)) & (t_idx >= s_idx)
    m_sub2 = jnp.concatenate([m_sub, m_sub], axis=0)
    lane = lax.broadcasted_iota(I32, (1, LANES), 1)
    lo = lane < HG_DK
    bd = (lax.broadcasted_iota(I32, (LANES, LANES), 0) // HG_DV) == \
         (lax.broadcasted_iota(I32, (LANES, LANES), 1) // HG_DK)
    dec_row = jnp.exp(ends[-1])

    def heads_on_rows(x):
        return jnp.concatenate([jnp.where(lo, x, 0.0), jnp.where(lo, 0.0, x)], axis=0)

    o_inter = []
    for p in range(width // LANES):
        sl = slice(p * LANES, (p + 1) * LANES)
        qs, ks, ke = q_sub[:, sl], k_sub[:, sl], k_end[:, sl]
        p_sub = _dot_nt(heads_on_rows(qs).astype(BF16), ks.astype(BF16))
        q_x = jnp.concatenate([jnp.where(sub_of_row == 1, qs, 0.0),
                               jnp.where(sub_of_row == 3, qs, 0.0), q_top[:, sl]], axis=1)
        k_x = jnp.concatenate([jnp.where(sub_of_row == 0, ke, 0.0),
                               jnp.where(sub_of_row == 2, ke, 0.0), k_top[:, sl]], axis=1)
        q_x2 = jnp.concatenate([jnp.where(jnp.tile(lo, (1, 3)), q_x, 0.0),
                                jnp.where(jnp.tile(lo, (1, 3)), 0.0, q_x)], axis=0)
        p_x = _dot_nt(q_x2.astype(BF16), k_x.astype(BF16))
        p_all = jnp.where(m_sub2, p_sub, 0.0) + p_x
        p_cat = jnp.concatenate([p_all[:HG_CHUNK], p_all[HG_CHUNK:]], axis=1)
        v_p = vv[:, sl]
        oi_scr[:, sl] = _dot(p_cat.astype(BF16), heads_on_rows(v_p).astype(BF16))
        st = st_ref[p]
        o_inter.append(_dot_nt(q_dec[:, sl].astype(BF16), st.astype(BF16)))
        upd = _dot_tn(v_p.astype(BF16), k_dec[:, sl].astype(BF16))
        st_ref[p] = st * dec_row[:, sl] + jnp.where(bd, upd, 0.0)
    o_inter = jnp.concatenate(o_inter, axis=1)

    @pl.when(unsafe)
    def _():
        a_scr[...] = a_cum
        q_scr[...] = qq
        k_scr[...] = kk

        def body(t, carry):
            d = a_scr[pl.ds(t, 1), :] - a_scr[...]
            w = jnp.exp(jnp.where(row <= t, d, -jnp.inf))
            prod = (q_scr[pl.ds(t, 1), :] * k_scr[...]) * w
            e = _dot(prod.astype(BF16), bo_ref[...])
            oi_scr[pl.ds(t, 1), :] = jnp.sum(e * hi_ref[...], axis=0, keepdims=True)
            return carry

        lax.fori_loop(0, HG_CHUNK, body, 0)

    o = oi_scr[...] + o_inter
    sq = o * o
    sq_hi = sq.astype(BF16)
    sq_lo = (sq - sq_hi.astype(F32)).astype(BF16)
    ms = (_dot(sq_hi, bo_ref[...]) + _dot(sq_lo, bo_ref[...])) * (1.0 / HG_DV)
    gate = hg_ref[...]
    o = o * lax.rsqrt(ms + RMS_EPS) * nw_ref[...] * (gate * jax.nn.sigmoid(gate))
    o_ref[...] = o.astype(o_ref.dtype)


def _hgrn(hq, hf, hi, hg, loglb, log1mlb, nw_row, head_ones, batch, seq):
    nc = seq // HG_CHUNK
    blk = lambda b, c: (b * nc + c, 0)
    row = lambda b, c: (0, 0)
    w = HG_WIDTH
    return pl.pallas_call(
        _hgrn_body,
        grid=(batch, nc),
        in_specs=[pl.BlockSpec((HG_CHUNK, w), blk)] * 4 + [pl.BlockSpec((1, w), row)] * 3
                 + [pl.BlockSpec((w, w), row)],
        out_specs=pl.BlockSpec((HG_CHUNK, w), blk),
        out_shape=jax.ShapeDtypeStruct((batch * seq, w), BF16),
        scratch_shapes=[pltpu.VMEM((w // LANES, LANES, LANES), F32)]
                       + [pltpu.VMEM((HG_CHUNK, w), F32)] * 4,
        compiler_params=_params(("arbitrary", "arbitrary"), 32),
        name="hgrn",
    )(hq, hf, hi, hg, loglb, log1mlb, nw_row, head_ones)


def _mem_attn_body(q_ref, k_ref, v_ref, o_ref):
    q_all = q_ref[...] * ATTN_SCALE
    outs = []
    for h in range(MEM_HEADS):
        sl = slice(h * HEAD_DIM, (h + 1) * HEAD_DIM)
        scores = _dot_nt(k_ref[0, :, sl], q_all[:, sl])
        e = jnp.exp(scores - jnp.max(scores, axis=0, keepdims=True))
        p = e * (1.0 / jnp.sum(e, axis=0, keepdims=True))
        outs.append(_dot_tn(p.astype(BF16), v_ref[0, :, sl]))
    o_ref[...] = jnp.concatenate(outs, axis=-1).astype(o_ref.dtype)


def _mem_attn(mq, mk, mv, layer, batch, seq, mem_len):
    tq = min(1024, seq)
    nq = seq // tq
    w = MEM_HEADS * HEAD_DIM
    return pl.pallas_call(
        _mem_attn_body,
        grid=(batch, nq),
        in_specs=[pl.BlockSpec((tq, w), lambda b, i: (b * nq + i, 0)),
                  pl.BlockSpec((1, mem_len, w), lambda b, i: (layer, b, 0)),
                  pl.BlockSpec((1, mem_len, w), lambda b, i: (layer, b, 0))],
        out_specs=pl.BlockSpec((tq, w), lambda b, i: (b * nq + i, 0)),
        out_shape=jax.ShapeDtypeStruct((batch * seq, w), BF16),
        compiler_params=_params(("arbitrary", "arbitrary"), 32),
        name="mem_attn",
    )(mq, mk, mv)


def _layer_norm(h, g, b):
    mu = jnp.mean(h, axis=-1, keepdims=True)
    d = h - mu
    var = jnp.mean(d * d, axis=-1, keepdims=True)
    return d * lax.rsqrt(var + LN_EPS) * g + b


def _route_rows(sel):
    def row(a, r):
        return a[r:r + 1, :]

    best = None
    for g in range(N_GROUPS):
        a = [row(sel, 4 * g + i) for i in range(4)]
        gs = a[0] + a[1]
        for i, j in ((0, 2), (0, 3), (1, 2), (1, 3), (2, 3)):
            gs = jnp.maximum(gs, a[i] + a[j])
        if best is None:
            best, best_score = jnp.zeros_like(gs, dtype=I32), gs
        else:
            better = gs > best_score
            best = jnp.where(better, g, best)
            best_score = jnp.where(better, gs, best_score)

    def pick(arr, i):
        out = row(arr, i)
        for g in range(1, N_GROUPS):
            out = jnp.where(best == g, row(arr, 4 * g + i), out)
        return out

    a = [pick(sel, i) for i in range(4)]
    i1, m1 = jnp.zeros_like(best), a[0]
    for i in range(1, 4):
        gt = a[i] > m1
        i1 = jnp.where(gt, i, i1)
        m1 = jnp.where(gt, a[i], m1)
    i2 = jnp.full_like(best, -1)
    m2 = jnp.full_like(m1, -jnp.inf)
    for i in range(4):
        gt = (i1 != i) & (a[i] > m2)
        i2 = jnp.where(gt, i, i2)
        m2 = jnp.where(gt, a[i], m2)
    lo = jnp.minimum(i1, i2)
    hi = jnp.maximum(i1, i2)
    pair = jnp.where(lo == 0, hi - 1, jnp.where(lo == 1, hi + 1, 5))
    return best * N_PAIRS + pair


def _out_proj_body(x_ref, swa_ref, hg_ref, mem_ref, w_ref, g_ref, b_ref, wr_ref, rb_ref, tri_ref,
                   x1_ref, cls_ref, rank_ref, cnt_ref, carry_ref):
    i = pl.program_id(0)
    t = x_ref.shape[0]

    @pl.when(i == 0)
    def _():
        carry_ref[...] = jnp.zeros_like(carry_ref)

    acc = _dot(swa_ref[...], w_ref[0:512, :]) + _dot(hg_ref[...], w_ref[512:768, :]) \
        + _dot(mem_ref[...], w_ref[768:1024, :])
    x1 = _layer_norm(ALPHA * x_ref[...] + acc, g_ref[...], b_ref[...])
    x1_ref[...] = _to_tiles(x1)

    logits = _dot_nt(wr_ref[...], x1.astype(BF16))
    cls = _route_rows(jax.nn.sigmoid(logits) + rb_ref[...])
    onehot = (lax.broadcasted_iota(I32, (CLASS_ROWS, t), 0) == cls).astype(F32)
    prefix = _dot(onehot.astype(BF16), tri_ref[...])
    carry = carry_ref[...]
    rank = jnp.sum(onehot * (prefix - 1.0 + carry[:, 0:1]), axis=0, keepdims=True)
    new_carry = carry + prefix[:, t - 1:t]
    carry_ref[...] = new_carry
    cnt_ref[...] = new_carry
    cls_ref[0] = cls
    rank_ref[0] = rank.astype(I32)


def _out_proj(x2, swa_o, hg_o, mem_o, w_bf, g_row, b_row, wr_t, rb_col, tri):
    n = x2.shape[0]
    t = tri.shape[0]
    nt = n // t
    tok = lambda i: (i, 0)
    fixed = lambda i: (0, 0)
    per_tile = pl.BlockSpec((1, 1, t), lambda i: (i, 0, 0))
    return pl.pallas_call(
        _out_proj_body,
        grid=(nt,),
        in_specs=[pl.BlockSpec((t, D_MODEL), tok), pl.BlockSpec((t, 512), tok),
                  pl.BlockSpec((t, 256), tok), pl.BlockSpec((t, 256), tok),
                  pl.BlockSpec((MIX_WIDTH, D_MODEL), fixed),
                  pl.BlockSpec((1, D_MODEL), fixed), pl.BlockSpec((1, D_MODEL), fixed),
                  pl.BlockSpec((N_EXPERTS, D_MODEL), fixed), pl.BlockSpec((N_EXPERTS, 1), fixed),
                  pl.BlockSpec((t, t), fixed)],
        out_specs=[pl.BlockSpec((t * ROW_TILES, LANES), tok), per_tile, per_tile,
                   pl.BlockSpec((CLASS_ROWS, LANES), fixed)],
        out_shape=[jax.ShapeDtypeStruct((n * ROW_TILES, LANES), F32),
                   jax.ShapeDtypeStruct((nt, 1, t), I32), jax.ShapeDtypeStruct((nt, 1, t), I32),
                   jax.ShapeDtypeStruct((CLASS_ROWS, LANES), F32)],
        scratch_shapes=[pltpu.VMEM((CLASS_ROWS, LANES), F32)],
        compiler_params=_params(("arbitrary",), 48),
        name="out_proj_ln_route",
    )(x2, swa_o, hg_o, mem_o, w_bf, g_row, b_row, wr_t, rb_col, tri)


def _tile_copy(src_ref, dst_ref, sem, src_row, dst_row, tiles):
    n = tiles * SUBLANES
    return pltpu.make_async_copy(
        src_ref.at[pl.ds(pl.multiple_of(src_row * n, n), n), :],
        dst_ref.at[pl.ds(pl.multiple_of(dst_row * n, n), n), :], sem)


def _scatter_body(cstart_ref, cls_ref, rank_ref, x_hbm, dest_ref, xs_hbm,
                  buf, dest_vmem, dest_smem, load_sem, scat_sem, misc_sem, *, n_steps, tile):
    i = pl.program_id(0)
    rows = tile * SUBLANES
    slot = i % 3

    def load(step, into):
        return pltpu.make_async_copy(
            x_hbm.at[pl.ds(pl.multiple_of(step * rows, rows), rows), :], buf.at[into], load_sem.at[into])

    def scatter_done(of):
        return pltpu.make_async_copy(buf.at[of], xs_hbm.at[pl.ds(0, rows), :], scat_sem.at[of])

    @pl.when(i == 0)
    def _():
        load(0, 0).start()

    @pl.when(i + 1 < n_steps)
    def _():
        load(i + 1, (i + 1) % 3).start()

    cls = cls_ref[0]
    dest = rank_ref[0]
    for c in range(N_CLASSES):
        dest = dest + jnp.where(cls == c, cstart_ref[c], 0)
    dest_ref[0] = dest
    dest_vmem[...] = dest
    to_smem = pltpu.make_async_copy(dest_vmem, dest_smem, misc_sem)
    to_smem.start()
    to_smem.wait()

    load(i, slot).wait()

    def issue(group, carry):
        for k in range(DMA_GROUP):
            j = group * DMA_GROUP + k
            _tile_copy(buf.at[slot], xs_hbm, scat_sem.at[slot], j, dest_smem[0, j], 1).start(priority=k % 2)
        return carry

    lax.fori_loop(0, tile // DMA_GROUP, issue, 0)

    @pl.when(i >= 1)
    def _():
        scatter_done((i + 2) % 3).wait()

    @pl.when(i == n_steps - 1)
    def _():
        scatter_done(slot).wait()


def _scatter_rows(x1_z, cls, rank, cstart):
    nt, _, tile = cls.shape
    n = nt * tile
    per_tile = pl.BlockSpec((1, 1, tile), lambda i, cs: (i, 0, 0))
    return pl.pallas_call(
        functools.partial(_scatter_body, n_steps=nt, tile=tile),
        grid_spec=pltpu.PrefetchScalarGridSpec(
            num_scalar_prefetch=1,
            grid=(nt,),
            in_specs=[per_tile, per_tile, pl.BlockSpec(memory_space=pl.ANY)],
            out_specs=[per_tile, pl.BlockSpec(memory_space=pl.ANY)],
            scratch_shapes=[pltpu.VMEM((3, tile * SUBLANES, LANES), F32),
                            pltpu.VMEM((1, tile), I32), pltpu.SMEM((1, tile), I32),
                            pltpu.SemaphoreType.DMA((3,)), pltpu.SemaphoreType.DMA((3,)),
                            pltpu.SemaphoreType.DMA(())]),
        out_shape=[jax.ShapeDtypeStruct((nt, 1, tile), I32),
                   jax.ShapeDtypeStruct((n * SUBLANES, LANES), F32)],
        compiler_params=_params(("arbitrary",), 32),
        name="scatter_rows",
    )(cstart, cls, rank, x1_z)


def _expert_body(blk_ref, elo_ref, ehi_ref, start_ref, end_ref, xs_ref, wr_ref,
                 g_lo, u_lo, d_lo, g_hi, u_hi, d_hi, ys_ref):
    i = pl.program_id(0)
    start = start_ref[i]
    end = end_ref[i]

    @pl.when(end > start)
    def _():
        x = _from_tiles(xs_ref[...], MOE_BLK).astype(BF16)
        scores = jax.nn.sigmoid(_dot(x, wr_ref[...]))
        lane = lax.broadcasted_iota(I32, (1, LANES), 1)
        s_lo = jnp.sum(jnp.where(lane == elo_ref[i], scores, 0.0), axis=-1, keepdims=True)
        s_hi = jnp.sum(jnp.where(lane == ehi_ref[i], scores, 0.0), axis=-1, keepdims=True)
        denom = s_lo + s_hi

        def ffn(gw, uw, dw, w_col):
            g = _dot(x, gw[0, 0])
            u = _dot(x, uw[0, 0])
            return _dot((((g * jax.nn.sigmoid(g)) * u) * w_col).astype(BF16), dw[0, 0])

        y = _to_tiles(ffn(g_lo, u_lo, d_lo, s_lo / denom) + ffn(g_hi, u_hi, d_hi, s_hi / denom))
        first = start % MOE_BLK == 0

        @pl.when(first)
        def _():
            ys_ref[...] = y

        @pl.when(jnp.logical_not(first))
        def _():
            slot = blk_ref[i] * MOE_BLK \
                + lax.broadcasted_iota(I32, (MOE_BLK * ROW_TILES, 1), 0) // ROW_TILES
            ys_ref[...] = jnp.where((slot >= start) & (slot < end), y, ys_ref[...])


def _experts(xs_z, items, wr_pad, wg, wu, wd, layer):
    blk, elo, ehi, start, end = items
    n_items = blk.shape[0]
    n_slots = xs_z.shape[0] // SUBLANES

    def data_map(i, blk, elo, ehi, start, end):
        return (blk[i], 0)

    def w_lo_map(i, blk, elo, ehi, start, end):
        return (layer, elo[i], 0, 0)

    def w_hi_map(i, blk, elo, ehi, start, end):
        return (layer, ehi[i], 0, 0)

    up_spec = lambda m: pl.BlockSpec((1, 1, D_MODEL, D_EXPERT), m)
    down_spec = lambda m: pl.BlockSpec((1, 1, D_EXPERT, D_MODEL), m)
    return pl.pallas_call(
        _expert_body,
        grid_spec=pltpu.PrefetchScalarGridSpec(
            num_scalar_prefetch=5,
            grid=(n_items,),
            in_specs=[pl.BlockSpec((MOE_BLK * SUBLANES, LANES), data_map),
                      pl.BlockSpec((D_MODEL, LANES), lambda i, *_: (0, 0)),
                      up_spec(w_lo_map), up_spec(w_lo_map), down_spec(w_lo_map),
                      up_spec(w_hi_map), up_spec(w_hi_map), down_spec(w_hi_map)],
            out_specs=pl.BlockSpec((MOE_BLK * SUBLANES, LANES), data_map)),
        out_shape=jax.ShapeDtypeStruct((n_slots * SUBLANES, LANES), F32),
        compiler_params=_params(("arbitrary",), 48),
        name="experts",
    )(blk, elo, ehi, start, end, xs_z, wr_pad, wg, wu, wd, wg, wu, wd)


def _gathered_ln2(dest_ref, x1_ref, ys_hbm, g_ref, b_ref, buf, sem, n_steps, tile):
    i = pl.program_id(0)
    slot = i % 2

    def gather(step, into):
        def issue(group, carry):
            for k in range(DMA_GROUP):
                j = group * DMA_GROUP + k
                _tile_copy(ys_hbm, buf.at[into], sem.at[into], dest_ref[step * tile + j], j, 1).start(
                    priority=k % 2)
            return carry

        lax.fori_loop(0, tile // DMA_GROUP, issue, 0)

    @pl.when(i == 0)
    def _():
        gather(0, 0)

    @pl.when(i + 1 < n_steps)
    def _():
        gather(i + 1, (i + 1) % 2)

    pltpu.make_async_copy(ys_hbm.at[pl.ds(0, tile * SUBLANES), :], buf.at[slot], sem.at[slot]).wait()
    h = ALPHA * _from_tiles(x1_ref[...], tile) + _from_tiles(buf[slot], tile)
    return _layer_norm(h, g_ref[...], b_ref[...])


def _ln2_body(dest_ref, x1_ref, ys_hbm, g_ref, b_ref, o_ref, buf, sem, *, n_steps, tile):
    o_ref[...] = _gathered_ln2(dest_ref, x1_ref, ys_hbm, g_ref, b_ref, buf, sem, n_steps, tile)


def _ln2_in_proj_body(dest_ref, x1_ref, ys_hbm, g_ref, b_ref, w_ref, bias_ref, o_ref, *rest, n_steps, tile):
    *proj_refs, buf, sem = rest
    x = _gathered_ln2(dest_ref, x1_ref, ys_hbm, g_ref, b_ref, buf, sem, n_steps, tile)
    o_ref[...] = x
    acc = _dot(x.astype(BF16), w_ref[...]) + bias_ref[...]
    off = 0
    for ref, width in zip(proj_refs, SPLITS):
        ref[...] = acc[:, off:off + width].astype(ref.dtype)
        off += width


def _ln2(x1_z, ys_z, dest, g_row, b_row, tile, w_bf=None, bias_row=None):
    n = x1_z.shape[0] // SUBLANES
    nt = n // tile
    tok = lambda i, d: (i, 0)
    fixed = lambda i, d: (0, 0)
    in_specs = [pl.BlockSpec((tile * SUBLANES, LANES), tok), pl.BlockSpec(memory_space=pl.ANY),
                pl.BlockSpec((1, D_MODEL), fixed), pl.BlockSpec((1, D_MODEL), fixed)]
    out_specs = [pl.BlockSpec((tile, D_MODEL), tok)]
    out_shape = [jax.ShapeDtypeStruct((n, D_MODEL), F32)]
    operands = [dest, x1_z, ys_z, g_row, b_row]
    body = _ln2_body
    if w_bf is not None:
        in_specs += [pl.BlockSpec((D_MODEL, IN_WIDTH), fixed), pl.BlockSpec((1, IN_WIDTH), fixed)]
        out_specs += [pl.BlockSpec((tile, w), tok) for w in SPLITS]
        out_shape += [jax.ShapeDtypeStruct((n, w), dt) for w, dt in zip(SPLITS, PROJ_DTYPES)]
        operands += [w_bf, bias_row]
        body = _ln2_in_proj_body
    return pl.pallas_call(
        functools.partial(body, n_steps=nt, tile=tile),
        grid_spec=pltpu.PrefetchScalarGridSpec(
            num_scalar_prefetch=1,
            grid=(nt,),
            in_specs=in_specs,
            out_specs=out_specs,
            scratch_shapes=[pltpu.VMEM((2, tile * SUBLANES, LANES), F32),
                            pltpu.SemaphoreType.DMA((2,))]),
        out_shape=out_shape,
        compiler_params=_params(("arbitrary",), 56),
        name="gather_ln2" if w_bf is None else "gather_ln2_in_proj",
    )(*operands)


def _t5_bucket(dist):
    max_exact = N_BUCKETS // 2
    d = jnp.maximum(dist, 0)
    large = max_exact + (jnp.log(jnp.maximum(d, 1).astype(F32) / max_exact)
                         / math.log(MAX_DISTANCE / max_exact) * (N_BUCKETS - max_exact)).astype(I32)
    large = jnp.minimum(large, N_BUCKETS - 1)
    return jnp.where(d < max_exact, d, large)


def _banded_bias(rel_bias):
    i = jnp.arange(ATTN_BLOCK)[:, None]
    j = jnp.arange(2 * ATTN_BLOCK)[None, :]
    dist = i + ATTN_BLOCK - j
    bucket = _t5_bucket(dist)[None]
    table = rel_bias.astype(F32)
    bias = jnp.zeros((SWA_HEADS, ATTN_BLOCK, 2 * ATTN_BLOCK), F32)
    for b in range(N_BUCKETS):
        bias = jnp.where(bucket == b, table[b][:, None, None], bias)
    from_prev = (jnp.arange(ATTN_BLOCK)[None, :] > i)[None]
    prev, own = bias[:, :, :ATTN_BLOCK], bias[:, :, ATTN_BLOCK:]
    normal = jnp.where(from_prev, prev, own)
    first = jnp.where(from_prev, -jnp.inf, own)
    return jnp.swapaxes(jnp.stack([first, normal]), 2, 3)


def _work_items(counts, n_tok):
    n_blocks = n_tok // MOE_BLK
    cend = jnp.cumsum(counts)
    cstart = cend - counts
    blk_starts = jnp.arange(n_blocks, dtype=I32) * MOE_BLK
    cls_starts = jnp.where(counts > 0, cstart, n_tok)
    start = jnp.sort(jnp.concatenate([blk_starts, cls_starts]))
    end = jnp.concatenate([start[1:], jnp.full((1,), n_tok, I32)])
    blk = jnp.minimum(start, n_tok - 1) // MOE_BLK
    cls = jnp.minimum(jnp.sum((cend[None, :] <= start[:, None]).astype(I32), axis=1), N_CLASSES - 1)
    group, pair = cls // N_PAIRS, cls % N_PAIRS
    lo = (pair >= 3).astype(I32) + (pair >= 5).astype(I32)
    hi = jnp.where(pair < 3, pair + 1, jnp.where(pair < 5, pair - 1, 3))
    return cstart, (blk, group * EXPERTS_PER_GROUP + lo, group * EXPERTS_PER_GROUP + hi, start, end)


def kernel(x, mem, w_in, b_in, w_mem_kv, attn_sinks, rel_bias, hgrn_lb_logits, hgrn_norm, w_out,
           ln1_g, ln1_b, w_router, router_bias, w_gate, w_up, w_down, ln2_g, ln2_b):
    batch, seq, _ = x.shape
    mem_len = mem.shape[1]
    n_tok = batch * seq
    route_tile = min(512, n_tok)

    bias = _banded_bias(rel_bias)
    lb = jnp.cumsum(jax.nn.softmax(hgrn_lb_logits.astype(F32), axis=0), axis=0)
    lb = lb - lb[0:1]
    log_lb = jnp.log(lb)
    log_1m_lb = jnp.log1p(-lb)
    head_ones = (jnp.arange(HG_WIDTH)[:, None] // HG_DV == jnp.arange(HG_WIDTH)[None, :] // HG_DV).astype(BF16)
    tri = (jnp.arange(route_tile)[:, None] <= jnp.arange(route_tile)[None, :]).astype(BF16)
    wr_t = jnp.transpose(w_router).astype(BF16)
    wr_pad = jnp.pad(w_router.astype(BF16), ((0, 0), (0, LANES - N_EXPERTS)))
    rb_col = router_bias.astype(F32).reshape(N_EXPERTS, 1)
    w_in_bf = w_in.astype(BF16)
    w_out_bf = w_out.astype(BF16)
    wg_bf, wu_bf, wd_bf = w_gate.astype(BF16), w_up.astype(BF16), w_down.astype(BF16)

    mk, mv = _mem_kv(mem.reshape(batch * mem_len, D_MODEL), w_mem_kv)

    x2 = x.reshape(n_tok, D_MODEL)
    proj = _in_proj(x2, w_in_bf[0], b_in[0].reshape(1, IN_WIDTH))
    for l in range(DEPTH):
        sq, sk, sv, hq, hf, hi, hg, mq = proj
        swa_o = _swa(sq, sk, sv, attn_sinks[l].astype(F32), bias, batch, seq)
        hg_o = _hgrn(hq, hf, hi, hg, log_lb[l].reshape(1, HG_WIDTH), log_1m_lb[l].reshape(1, HG_WIDTH),
                     jnp.tile(hgrn_norm[l].astype(F32), HG_HEADS).reshape(1, HG_WIDTH), head_ones, batch, seq)
        mem_o = _mem_attn(mq, mk, mv, l, batch, seq, mem_len)
        x1_z, cls, rank, counts = _out_proj(
            x2, swa_o, hg_o, mem_o, w_out_bf[l], ln1_g[l].reshape(1, D_MODEL), ln1_b[l].reshape(1, D_MODEL),
            wr_t, rb_col, tri)
        cstart, items = _work_items(counts[:N_CLASSES, 0].astype(I32), n_tok)
        dest, xs_z = _scatter_rows(x1_z, cls, rank, cstart)
        ys_z = _experts(xs_z, items, wr_pad, wg_bf, wu_bf, wd_bf, l)
        ln2_args = (x1_z, ys_z, dest.reshape(n_tok), ln2_g[l].reshape(1, D_MODEL), ln2_b[l].reshape(1, D_MODEL),
                    route_tile)
        if l + 1 < DEPTH:
            x2, *proj = _ln2(*ln2_args, w_in_bf[l + 1], b_in[l + 1].reshape(1, IN_WIDTH))
        else:
            (x2,) = _ln2(*ln2_args)
    return x2.reshape(batch, seq, D_MODEL)
```

```python
import functools
import math

import jax
import jax.numpy as jnp
from jax import lax
from jax.experimental import pallas as pl
from jax.experimental.pallas import tpu as pltpu

F32 = jnp.float32
BF16 = jnp.bfloat16
I32 = jnp.int32

D_MODEL = 1024
DEPTH = 4
HEAD_DIM = 64
SWA_HEADS = 8
SWA_KV_HEADS = 2
SWA_GROUP = SWA_HEADS // SWA_KV_HEADS
WINDOW = 128
ATTN_BLOCK = 128
HG_HEADS = 4
HG_DK = 64
HG_DV = 64
HG_WIDTH = HG_HEADS * HG_DK
MEM_HEADS = 4
N_BUCKETS = 32
MAX_DISTANCE = 128
N_EXPERTS = 16
N_GROUPS = 4
EXPERTS_PER_GROUP = 4
D_EXPERT = 512
LN_EPS = 1e-5
RMS_EPS = 1e-6
ALPHA = (2 * DEPTH) ** 0.25
SPLITS = (512, 128, 128, 256, 256, 256, 256, 256)
IN_WIDTH = sum(SPLITS)
PROJ_WIDTHS = (512, 128, 128, 256, 256, 256, 256, 256, 256)
PROJ_DTYPES = (BF16, BF16, BF16, F32, F32, F32, BF16, F32, BF16)
MIX_WIDTH = 1024
ATTN_SCALE = HEAD_DIM ** -0.5

SUBLANES = 8
LANES = 128
ROW_TILES = D_MODEL // LANES

SWA_STEP_BLOCKS = 4
HG_CHUNK = 128
HG_SUB = 32
HG_SAFE_DECAY = 80.0
N_PAIRS = 6
N_CLASSES = N_GROUPS * N_PAIRS
CLASS_ROWS = 32
MOE_BLK = 256
DMA_GROUP = 8


def _params(semantics, vmem_mib):
    return pltpu.CompilerParams(dimension_semantics=semantics, vmem_limit_bytes=vmem_mib * 1024 * 1024)


def _dot(a, b):
    return jnp.dot(a, b, preferred_element_type=F32)


def _dot_nt(a, b):
    return lax.dot_general(a, b, (((1,), (1,)), ((), ())), preferred_element_type=F32)


def _dot_tn(a, b):
    return lax.dot_general(a, b, (((0,), (0,)), ((), ())), preferred_element_type=F32)


def _from_tiles(z, n_rows):
    k = z.shape[0] // n_rows
    return z.reshape(n_rows, k, LANES).reshape(n_rows, k * LANES)


def _to_tiles(val):
    n_rows, width = val.shape
    return val.reshape(n_rows, width // LANES, LANES).reshape(n_rows * (width // LANES), LANES)


def _project(x, w_ref, b_ref, loglb_ref, log1mlb_ref, out_refs):
    acc = _dot(x.astype(BF16), w_ref[...]) + b_ref[...]
    parts, off = [], 0
    for width in SPLITS:
        parts.append(acc[:, off:off + width])
        off += width
    sq, sk, sv, hq, z, hi, gate, mq = parts
    log_sig = jnp.minimum(z, 0.0) - jnp.log(1.0 + jnp.exp(-jnp.abs(z)))
    t_a = loglb_ref[...]
    t_b = log1mlb_ref[...] + log_sig
    log_f = jnp.maximum(t_a, t_b) + jnp.log(1.0 + jnp.exp(-jnp.abs(t_a - t_b)))
    one_minus_f = jnp.exp(t_b - z)
    qq = hq * jax.nn.sigmoid(hq) * (HG_DK ** -0.5)
    gate_act = gate * jax.nn.sigmoid(gate)
    for ref, val in zip(out_refs, (sq, sk, sv, qq, log_f, one_minus_f, hi, gate_act, mq)):
        ref[...] = val.astype(ref.dtype)


def _in_proj_body(x_ref, w_ref, b_ref, loglb_ref, log1mlb_ref, *out_refs):
    _project(x_ref[...], w_ref, b_ref, loglb_ref, log1mlb_ref, out_refs)


def _in_proj(x2, w_bf, b_row, loglb, log1mlb):
    n = x2.shape[0]
    tm = min(512, n)
    fixed = lambda i: (0, 0)
    return pl.pallas_call(
        _in_proj_body,
        grid=(n // tm,),
        in_specs=[pl.BlockSpec((tm, D_MODEL), lambda i: (i, 0)),
                  pl.BlockSpec((D_MODEL, IN_WIDTH), fixed), pl.BlockSpec((1, IN_WIDTH), fixed),
                  pl.BlockSpec((1, HG_WIDTH), fixed), pl.BlockSpec((1, HG_WIDTH), fixed)],
        out_specs=[pl.BlockSpec((tm, w), lambda i: (i, 0)) for w in PROJ_WIDTHS],
        out_shape=[jax.ShapeDtypeStruct((n, w), dt) for w, dt in zip(PROJ_WIDTHS, PROJ_DTYPES)],
        compiler_params=_params(("arbitrary",), 48),
        name="in_proj",
    )(x2, w_bf, b_row, loglb, log1mlb)


def _mem_kv_body(mem_ref, w_ref, k_ref, v_ref):
    acc = _dot(mem_ref[...].astype(BF16), w_ref[0].astype(BF16))
    half = MEM_HEADS * HEAD_DIM
    k_ref[0] = acc[:, :half].astype(BF16)
    v_ref[0] = acc[:, half:].astype(BF16)


def _mem_kv(mem2, w_mem_kv):
    rows = mem2.shape[0]
    tm = min(512, rows)
    half = MEM_HEADS * HEAD_DIM
    return pl.pallas_call(
        _mem_kv_body,
        grid=(DEPTH, rows // tm),
        in_specs=[pl.BlockSpec((tm, D_MODEL), lambda l, i: (i, 0)),
                  pl.BlockSpec((1, D_MODEL, 2 * half), lambda l, i: (l, 0, 0))],
        out_specs=[pl.BlockSpec((1, tm, half), lambda l, i: (l, i, 0)),
                   pl.BlockSpec((1, tm, half), lambda l, i: (l, i, 0))],
        out_shape=[jax.ShapeDtypeStruct((DEPTH, rows, half), BF16)] * 2,
        compiler_params=_params(("arbitrary", "arbitrary"), 32),
        name="mem_kv",
    )(mem2, w_mem_kv)


def _swa_body(sink_ref, q_ref, kp_ref, ko_ref, vp_ref, vo_ref, bias_ref, o_ref, *, blocks):
    n = pl.program_id(1)
    q_all = q_ref[...] * ATTN_SCALE
    k_all = jnp.concatenate([kp_ref[...], ko_ref[...]], axis=0)
    v_all = jnp.concatenate([vp_ref[...], vo_ref[...]], axis=0)
    from_prev = lax.broadcasted_iota(I32, (ATTN_BLOCK, ATTN_BLOCK), 0) > \
        lax.broadcasted_iota(I32, (ATTN_BLOCK, ATTN_BLOCK), 1)
    for j in range(blocks):
        lo = j * ATTN_BLOCK
        kk = k_all[lo:lo + 2 * ATTN_BLOCK]
        vv = v_all[lo:lo + 2 * ATTN_BLOCK]
        table = jnp.where(n > 0, 1, 0) if j == 0 else 1
        outs = []
        for h in range(SWA_HEADS):
            g = h // SWA_GROUP
            q_h = q_all[lo:lo + ATTN_BLOCK, h * HEAD_DIM:(h + 1) * HEAD_DIM]
            scores = _dot_nt(kk[:, g * HEAD_DIM:(g + 1) * HEAD_DIM], q_h)
            band = jnp.where(from_prev, scores[:ATTN_BLOCK], scores[ATTN_BLOCK:]) + bias_ref[table, h]
            sink = sink_ref[h]
            m = jnp.maximum(jnp.max(band, axis=0, keepdims=True), sink)
            e = jnp.exp(band - m)
            p = e * (1.0 / (jnp.sum(e, axis=0, keepdims=True) + jnp.exp(sink - m)))
            p = jnp.concatenate([jnp.where(from_prev, p, 0.0), jnp.where(from_prev, 0.0, p)], axis=0)
            outs.append(_dot_tn(p.astype(BF16), vv[:, g * HEAD_DIM:(g + 1) * HEAD_DIM]))
        o_ref[lo:lo + ATTN_BLOCK, :] = jnp.concatenate(outs, axis=-1).astype(o_ref.dtype)


def _swa(sq, sk, sv, sinks, bias2, batch, seq):
    nb = seq // ATTN_BLOCK
    blocks = math.gcd(SWA_STEP_BLOCKS, nb)
    ns = nb // blocks
    own = lambda b, n: (b * ns + n, 0)
    prev = lambda b, n: (b * nb + jnp.maximum(n * blocks - 1, 0), 0)
    kvw = SWA_KV_HEADS * HEAD_DIM
    qw = SWA_HEADS * HEAD_DIM
    rows = blocks * ATTN_BLOCK
    return pl.pallas_call(
        functools.partial(_swa_body, blocks=blocks),
        grid=(batch, ns),
        in_specs=[pl.BlockSpec(memory_space=pltpu.SMEM),
                  pl.BlockSpec((rows, qw), own),
                  pl.BlockSpec((ATTN_BLOCK, kvw), prev),
                  pl.BlockSpec((rows, kvw), own),
                  pl.BlockSpec((ATTN_BLOCK, kvw), prev),
                  pl.BlockSpec((rows, kvw), own),
                  pl.BlockSpec((2, SWA_HEADS, ATTN_BLOCK, ATTN_BLOCK), lambda b, n: (0, 0, 0, 0))],
        out_specs=pl.BlockSpec((rows, qw), own),
        out_shape=jax.ShapeDtypeStruct((batch * seq, qw), BF16),
        compiler_params=_params(("arbitrary", "arbitrary"), 32),
        name="swa",
    )(sinks, sq, sk, sk, sv, sv, bias2)


def _cumsum_rows(x):
    n = x.shape[0]
    row = lax.broadcasted_iota(I32, (n, 1), 0)
    s = 1
    while s < n:
        x = x + jnp.where(row >= s, pltpu.roll(x, s, 0), 0.0)
        s *= 2
    return x


def _rows_from_blocks(vals, width):
    return jnp.concatenate([jnp.broadcast_to(v, (HG_SUB, width)) for v in vals], axis=0)


def _hgrn_body(qq_ref, logf_ref, kk_ref, vv_ref, gate_ref, nw_ref, bo_ref, o_ref,
               st_ref, a_scr, q_scr, k_scr, oi_scr):
    c = pl.program_id(1)
    n_sub = HG_CHUNK // HG_SUB
    width = HG_WIDTH

    @pl.when(c == 0)
    def _():
        st_ref[...] = jnp.zeros_like(st_ref)

    log_f = logf_ref[...]
    kk = kk_ref[...]
    qq = qq_ref[...]
    vv = vv_ref[...]

    a_cum = _cumsum_rows(log_f)
    ends = [a_cum[HG_SUB * j + HG_SUB - 1:HG_SUB * j + HG_SUB, :] for j in range(n_sub)]
    zero_row = jnp.zeros((1, width), F32)
    starts = [zero_row] + ends[:-1]
    a_loc = a_cum - _rows_from_blocks(starts, width)
    e_loc = _rows_from_blocks(ends, width) - a_cum
    worst = starts[0] - ends[0]
    for j in range(1, n_sub):
        worst = jnp.maximum(worst, starts[j] - ends[j])
    unsafe = jnp.max(worst) > HG_SAFE_DECAY

    q_sub = qq * jnp.exp(a_loc)
    k_sub = kk * jnp.exp(-a_loc)
    k_end = kk * jnp.exp(e_loc)
    one_row = jnp.ones((1, width), F32)
    q_dec = q_sub * _rows_from_blocks([jnp.exp(s) for s in starts], width)
    k_dec = k_end * _rows_from_blocks([jnp.exp(ends[-1] - e) for e in ends], width)
    q_top = q_sub * _rows_from_blocks(
        [zero_row, zero_row, one_row, jnp.exp(ends[2] - ends[1])], width)
    k_top = k_end * _rows_from_blocks(
        [jnp.exp(ends[1] - ends[0]), one_row, zero_row, zero_row], width)

    row = lax.broadcasted_iota(I32, (HG_CHUNK, 1), 0)
    sub_of_row = row // HG_SUB
    t_idx = lax.broadcasted_iota(I32, (HG_CHUNK, HG_CHUNK), 0)
    s_idx = lax.broadcasted_iota(I32, (HG_CHUNK, HG_CHUNK), 1)
    m_sub = ((t_idx // HG_SUB) == (s_idx // HG_SUB)) & (t_idx >= s_idx)
    m_sub2 = jnp.concatenate([m_sub, m_sub], axis=0)
    lane = lax.broadcasted_iota(I32, (1, LANES), 1)
    lo = lane < HG_DK
    bd = (lax.broadcasted_iota(I32, (LANES, LANES), 0) // HG_DV) == \
         (lax.broadcasted_iota(I32, (LANES, LANES), 1) // HG_DK)
    dec_row = jnp.exp(ends[-1])

    def heads_on_rows(x):
        return jnp.concatenate([jnp.where(lo, x, 0.0), jnp.where(lo, 0.0, x)], axis=0)

    o_inter = []
    for p in range(width // LANES):
        sl = slice(p * LANES, (p + 1) * LANES)
        qs, ks, ke = q_sub[:, sl], k_sub[:, sl], k_end[:, sl]
        p_sub = _dot_nt(heads_on_rows(qs).astype(BF16), ks.astype(BF16))
        q_x = jnp.concatenate([jnp.where(sub_of_row == 1, qs, 0.0),
                               jnp.where(sub_of_row == 3, qs, 0.0), q_top[:, sl]], axis=1)
        k_x = jnp.concatenate([jnp.where(sub_of_row == 0, ke, 0.0),
                               jnp.where(sub_of_row == 2, ke, 0.0), k_top[:, sl]], axis=1)
        q_x2 = jnp.concatenate([jnp.where(jnp.tile(lo, (1, 3)), q_x, 0.0),
                                jnp.where(jnp.tile(lo, (1, 3)), 0.0, q_x)], axis=0)
        p_x = _dot_nt(q_x2.astype(BF16), k_x.astype(BF16))
        p_all = jnp.where(m_sub2, p_sub, 0.0) + p_x
        p_cat = jnp.concatenate([p_all[:HG_CHUNK], p_all[HG_CHUNK:]], axis=1)
        v_p = vv[:, sl]
        oi_scr[:, sl] = _dot(p_cat.astype(BF16), heads_on_rows(v_p).astype(BF16))
        st = st_ref[p]
        o_inter.append(_dot_nt(q_dec[:, sl].astype(BF16), st.astype(BF16)))
        upd = _dot_tn(v_p.astype(BF16), k_dec[:, sl].astype(BF16))
        st_ref[p] = st * dec_row[:, sl] + jnp.where(bd, upd, 0.0)
    o_inter = jnp.concatenate(o_inter, axis=1)

    @pl.when(unsafe)
    def _():
        a_scr[...] = a_cum
        q_scr[...] = qq
        k_scr[...] = kk

        def body(t, carry):
            d = a_scr[pl.ds(t, 1), :] - a_scr[...]
            w = jnp.exp(jnp.where(row <= t, d, -jnp.inf))
            prod = (q_scr[pl.ds(t, 1), :] * k_scr[...]) * w
            e = _dot(prod.astype(BF16), bo_ref[...])
            oi_scr[pl.ds(t, 1), :] = jnp.sum(e * vv_ref[...].astype(F32), axis=0, keepdims=True)
            return carry

        lax.fori_loop(0, HG_CHUNK, body, 0)

    o = oi_scr[...] + o_inter
    sq = o * o
    sq_hi = sq.astype(BF16)
    sq_lo = (sq - sq_hi.astype(F32)).astype(BF16)
    ms = (_dot(sq_hi, bo_ref[...]) + _dot(sq_lo, bo_ref[...])) * (1.0 / HG_DV)
    o = o * lax.rsqrt(ms + RMS_EPS) * nw_ref[...] * gate_ref[...]
    o_ref[...] = o.astype(o_ref.dtype)


def _hgrn(qq, log_f, one_minus_f, vv, gate_act, nw_row, head_ones, batch, seq):
    nc = seq // HG_CHUNK
    blk = lambda b, c: (b * nc + c, 0)
    row = lambda b, c: (0, 0)
    w = HG_WIDTH
    return pl.pallas_call(
        _hgrn_body,
        grid=(batch, nc),
        in_specs=[pl.BlockSpec((HG_CHUNK, w), blk)] * 5 + [pl.BlockSpec((1, w), row)]
                 + [pl.BlockSpec((w, w), row)],
        out_specs=pl.BlockSpec((HG_CHUNK, w), blk),
        out_shape=jax.ShapeDtypeStruct((batch * seq, w), BF16),
        scratch_shapes=[pltpu.VMEM((w // LANES, LANES, LANES), F32)]
                       + [pltpu.VMEM((HG_CHUNK, w), F32)] * 4,
        compiler_params=_params(("arbitrary", "arbitrary"), 32),
        name="hgrn",
    )(qq, log_f, one_minus_f, vv, gate_act, nw_row, head_ones)


def _mem_attn_body(q_ref, k_ref, v_ref, o_ref):
    q_all = q_ref[...] * ATTN_SCALE
    outs = []
    for h in range(MEM_HEADS):
        sl = slice(h * HEAD_DIM, (h + 1) * HEAD_DIM)
        scores = _dot_nt(k_ref[0, :, sl], q_all[:, sl])
        e = jnp.exp(scores - jnp.max(scores, axis=0, keepdims=True))
        p = e * (1.0 / jnp.sum(e, axis=0, keepdims=True))
        outs.append(_dot_tn(p.astype(BF16), v_ref[0, :, sl]))
    o_ref[...] = jnp.concatenate(outs, axis=-1).astype(o_ref.dtype)


def _mem_attn(mq, mk, mv, layer, batch, seq, mem_len):
    tq = min(1024, seq)
    nq = seq // tq
    w = MEM_HEADS * HEAD_DIM
    return pl.pallas_call(
        _mem_attn_body,
        grid=(batch, nq),
        in_specs=[pl.BlockSpec((tq, w), lambda b, i: (b * nq + i, 0)),
                  pl.BlockSpec((1, mem_len, w), lambda b, i: (layer, b, 0)),
                  pl.BlockSpec((1, mem_len, w), lambda b, i: (layer, b, 0))],
        out_specs=pl.BlockSpec((tq, w), lambda b, i: (b * nq + i, 0)),
        out_shape=jax.ShapeDtypeStruct((batch * seq, w), BF16),
        compiler_params=_params(("arbitrary", "arbitrary"), 32),
        name="mem_attn",
    )(mq, mk, mv)


def _layer_norm(h, g, b):
    mu = jnp.mean(h, axis=-1, keepdims=True)
    d = h - mu
    var = jnp.mean(d * d, axis=-1, keepdims=True)
    return d * lax.rsqrt(var + LN_EPS) * g + b


def _route_rows(sel):
    def row(a, r):
        return a[r:r + 1, :]

    best = None
    for g in range(N_GROUPS):
        a = [row(sel, 4 * g + i) for i in range(4)]
        gs = a[0] + a[1]
        for i, j in ((0, 2), (0, 3), (1, 2), (1, 3), (2, 3)):
            gs = jnp.maximum(gs, a[i] + a[j])
        if best is None:
            best, best_score = jnp.zeros_like(gs, dtype=I32), gs
        else:
            better = gs > best_score
            best = jnp.where(better, g, best)
            best_score = jnp.where(better, gs, best_score)

    def pick(arr, i):
        out = row(arr, i)
        for g in range(1, N_GROUPS):
            out = jnp.where(best == g, row(arr, 4 * g + i), out)
        return out

    a = [pick(sel, i) for i in range(4)]
    i1, m1 = jnp.zeros_like(best), a[0]
    for i in range(1, 4):
        gt = a[i] > m1
        i1 = jnp.where(gt, i, i1)
        m1 = jnp.where(gt, a[i], m1)
    i2 = jnp.full_like(best, -1)
    m2 = jnp.full_like(m1, -jnp.inf)
    for i in range(4):
        gt = (i1 != i) & (a[i] > m2)
        i2 = jnp.where(gt, i, i2)
        m2 = jnp.where(gt, a[i], m2)
    lo = jnp.minimum(i1, i2)
    hi = jnp.maximum(i1, i2)
    pair = jnp.where(lo == 0, hi - 1, jnp.where(lo == 1, hi + 1, 5))
    return best * N_PAIRS + pair


def _out_proj_body(x_ref, swa_ref, hg_ref, mem_ref, w_ref, g_ref, b_ref, wr_ref, rb_ref, tri_ref,
                   x1_ref, cls_ref, rank_ref, cnt_ref, carry_ref):
    i = pl.program_id(0)
    t = x_ref.shape[0]

    @pl.when(i == 0)
    def _():
        carry_ref[...] = jnp.zeros_like(carry_ref)

    acc = _dot(swa_ref[...], w_ref[0:512, :]) + _dot(hg_ref[...], w_ref[512:768, :]) \
        + _dot(mem_ref[...], w_ref[768:1024, :])
    x1 = _layer_norm(ALPHA * x_ref[...] + acc, g_ref[...], b_ref[...])
    x1_ref[...] = _to_tiles(x1)

    logits = _dot_nt(wr_ref[...], x1.astype(BF16))
    cls = _route_rows(jax.nn.sigmoid(logits) + rb_ref[...])
    onehot = (lax.broadcasted_iota(I32, (CLASS_ROWS, t), 0) == cls).astype(F32)
    prefix = _dot(onehot.astype(BF16), tri_ref[...])
    carry = carry_ref[...]
    rank = jnp.sum(onehot * (prefix - 1.0 + carry[:, 0:1]), axis=0, keepdims=True)
    new_carry = carry + prefix[:, t - 1:t]
    carry_ref[...] = new_carry
    cnt_ref[...] = new_carry
    cls_ref[0] = cls
    rank_ref[0] = rank.astype(I32)


def _out_proj(x2, swa_o, hg_o, mem_o, w_bf, g_row, b_row, wr_t, rb_col, tri):
    n = x2.shape[0]
    t = tri.shape[0]
    nt = n // t
    tok = lambda i: (i, 0)
    fixed = lambda i: (0, 0)
    per_tile = pl.BlockSpec((1, 1, t), lambda i: (i, 0, 0))
    return pl.pallas_call(
        _out_proj_body,
        grid=(nt,),
        in_specs=[pl.BlockSpec((t, D_MODEL), tok), pl.BlockSpec((t, 512), tok),
                  pl.BlockSpec((t, 256), tok), pl.BlockSpec((t, 256), tok),
                  pl.BlockSpec((MIX_WIDTH, D_MODEL), fixed),
                  pl.BlockSpec((1, D_MODEL), fixed), pl.BlockSpec((1, D_MODEL), fixed),
                  pl.BlockSpec((N_EXPERTS, D_MODEL), fixed), pl.BlockSpec((N_EXPERTS, 1), fixed),
                  pl.BlockSpec((t, t), fixed)],
        out_specs=[pl.BlockSpec((t * ROW_TILES, LANES), tok), per_tile, per_tile,
                   pl.BlockSpec((CLASS_ROWS, LANES), fixed)],
        out_shape=[jax.ShapeDtypeStruct((n * ROW_TILES, LANES), F32),
                   jax.ShapeDtypeStruct((nt, 1, t), I32), jax.ShapeDtypeStruct((nt, 1, t), I32),
                   jax.ShapeDtypeStruct((CLASS_ROWS, LANES), F32)],
        scratch_shapes=[pltpu.VMEM((CLASS_ROWS, LANES), F32)],
        compiler_params=_params(("arbitrary",), 48),
        name="out_proj_ln_route",
    )(x2, swa_o, hg_o, mem_o, w_bf, g_row, b_row, wr_t, rb_col, tri)


def _tile_copy(src_ref, dst_ref, sem, src_row, dst_row, tiles):
    n = tiles * SUBLANES
    return pltpu.make_async_copy(
        src_ref.at[pl.ds(pl.multiple_of(src_row * n, n), n), :],
        dst_ref.at[pl.ds(pl.multiple_of(dst_row * n, n), n), :], sem)


def _scatter_body(cstart_ref, cls_ref, rank_ref, x_hbm, dest_ref, xs_hbm,
                  buf, dest_vmem, dest_smem, load_sem, scat_sem, misc_sem, *, n_steps, tile):
    i = pl.program_id(0)
    rows = tile * SUBLANES
    slot = i % 3

    def load(step, into):
        return pltpu.make_async_copy(
            x_hbm.at[pl.ds(pl.multiple_of(step * rows, rows), rows), :], buf.at[into], load_sem.at[into])

    def scatter_done(of):
        return pltpu.make_async_copy(buf.at[of], xs_hbm.at[pl.ds(0, rows), :], scat_sem.at[of])

    @pl.when(i == 0)
    def _():
        load(0, 0).start()

    @pl.when(i + 1 < n_steps)
    def _():
        load(i + 1, (i + 1) % 3).start()

    cls = cls_ref[0]
    dest = rank_ref[0]
    for c in range(N_CLASSES):
        dest = dest + jnp.where(cls == c, cstart_ref[c], 0)
    dest_ref[0] = dest
    dest_vmem[...] = dest
    to_smem = pltpu.make_async_copy(dest_vmem, dest_smem, misc_sem)
    to_smem.start()
    to_smem.wait()

    load(i, slot).wait()

    def issue(group, carry):
        for k in range(DMA_GROUP):
            j = group * DMA_GROUP + k
            _tile_copy(buf.at[slot], xs_hbm, scat_sem.at[slot], j, dest_smem[0, j], 1).start(priority=k % 2)
        return carry

    lax.fori_loop(0, tile // DMA_GROUP, issue, 0)

    @pl.when(i >= 1)
    def _():
        scatter_done((i + 2) % 3).wait()

    @pl.when(i == n_steps - 1)
    def _():
        scatter_done(slot).wait()


def _scatter_rows(x1_z, cls, rank, cstart):
    nt, _, tile = cls.shape
    n = nt * tile
    per_tile = pl.BlockSpec((1, 1, tile), lambda i, cs: (i, 0, 0))
    return pl.pallas_call(
        functools.partial(_scatter_body, n_steps=nt, tile=tile),
        grid_spec=pltpu.PrefetchScalarGridSpec(
            num_scalar_prefetch=1,
            grid=(nt,),
            in_specs=[per_tile, per_tile, pl.BlockSpec(memory_space=pl.ANY)],
            out_specs=[per_tile, pl.BlockSpec(memory_space=pl.ANY)],
            scratch_shapes=[pltpu.VMEM((3, tile * SUBLANES, LANES), F32),
                            pltpu.VMEM((1, tile), I32), pltpu.SMEM((1, tile), I32),
                            pltpu.SemaphoreType.DMA((3,)), pltpu.SemaphoreType.DMA((3,)),
                            pltpu.SemaphoreType.DMA(())]),
        out_shape=[jax.ShapeDtypeStruct((nt, 1, tile), I32),
                   jax.ShapeDtypeStruct((n * SUBLANES, LANES), F32)],
        compiler_params=_params(("arbitrary",), 32),
        name="scatter_rows",
    )(cstart, cls, rank, x1_z)


def _expert_body(blk_ref, elo_ref, ehi_ref, start_ref, end_ref, xs_ref, wr_ref,
                 g_lo, u_lo, d_lo, g_hi, u_hi, d_hi, ys_ref):
    i = pl.program_id(0)
    start = start_ref[i]
    end = end_ref[i]

    @pl.when(end > start)
    def _():
        x = _from_tiles(xs_ref[...], MOE_BLK).astype(BF16)
        scores = jax.nn.sigmoid(_dot(x, wr_ref[...]))
        lane = lax.broadcasted_iota(I32, (1, LANES), 1)
        s_lo = jnp.sum(jnp.where(lane == elo_ref[i], scores, 0.0), axis=-1, keepdims=True)
        s_hi = jnp.sum(jnp.where(lane == ehi_ref[i], scores, 0.0), axis=-1, keepdims=True)
        denom = s_lo + s_hi

        def ffn(gw, uw, dw, w_col):
            g = _dot(x, gw[0, 0])
            u = _dot(x, uw[0, 0])
            return _dot((((g * jax.nn.sigmoid(g)) * u) * w_col).astype(BF16), dw[0, 0])

        y = _to_tiles(ffn(g_lo, u_lo, d_lo, s_lo / denom) + ffn(g_hi, u_hi, d_hi, s_hi / denom))
        first = start % MOE_BLK == 0

        @pl.when(first)
        def _():
            ys_ref[...] = y

        @pl.when(jnp.logical_not(first))
        def _():
            slot = blk_ref[i] * MOE_BLK \
                + lax.broadcasted_iota(I32, (MOE_BLK * ROW_TILES, 1), 0) // ROW_TILES
            ys_ref[...] = jnp.where((slot >= start) & (slot < end), y, ys_ref[...])


def _experts(xs_z, items, wr_pad, wg, wu, wd, layer):
    blk, elo, ehi, start, end = items
    n_items = blk.shape[0]
    n_slots = xs_z.shape[0] // SUBLANES

    def data_map(i, blk, elo, ehi, start, end):
        return (blk[i], 0)

    def w_lo_map(i, blk, elo, ehi, start, end):
        return (layer, elo[i], 0, 0)

    def w_hi_map(i, blk, elo, ehi, start, end):
        return (layer, ehi[i], 0, 0)

    up_spec = lambda m: pl.BlockSpec((1, 1, D_MODEL, D_EXPERT), m)
    down_spec = lambda m: pl.BlockSpec((1, 1, D_EXPERT, D_MODEL), m)
    return pl.pallas_call(
        _expert_body,
        grid_spec=pltpu.PrefetchScalarGridSpec(
            num_scalar_prefetch=5,
            grid=(n_items,),
            in_specs=[pl.BlockSpec((MOE_BLK * SUBLANES, LANES), data_map),
                      pl.BlockSpec((D_MODEL, LANES), lambda i, *_: (0, 0)),
                      up_spec(w_lo_map), up_spec(w_lo_map), down_spec(w_lo_map),
                      up_spec(w_hi_map), up_spec(w_hi_map), down_spec(w_hi_map)],
            out_specs=pl.BlockSpec((MOE_BLK * SUBLANES, LANES), data_map)),
        out_shape=jax.ShapeDtypeStruct((n_slots * SUBLANES, LANES), F32),
        compiler_params=_params(("arbitrary",), 48),
        name="experts",
    )(blk, elo, ehi, start, end, xs_z, wr_pad, wg, wu, wd, wg, wu, wd)


def _gathered_ln2(dest_ref, x1_ref, ys_hbm, g_ref, b_ref, buf, sem, n_steps, tile):
    i = pl.program_id(0)
    slot = i % 2

    def gather(step, into):
        def issue(group, carry):
            for k in range(DMA_GROUP):
                j = group * DMA_GROUP + k
                _tile_copy(ys_hbm, buf.at[into], sem.at[into], dest_ref[step * tile + j], j, 1).start(
                    priority=k % 2)
            return carry

        lax.fori_loop(0, tile // DMA_GROUP, issue, 0)

    @pl.when(i == 0)
    def _():
        gather(0, 0)

    @pl.when(i + 1 < n_steps)
    def _():
        gather(i + 1, (i + 1) % 2)

    pltpu.make_async_copy(ys_hbm.at[pl.ds(0, tile * SUBLANES), :], buf.at[slot], sem.at[slot]).wait()
    h = ALPHA * _from_tiles(x1_ref[...], tile) + _from_tiles(buf[slot], tile)
    return _layer_norm(h, g_ref[...], b_ref[...])


def _ln2_body(dest_ref, x1_ref, ys_hbm, g_ref, b_ref, o_ref, buf, sem, *, n_steps, tile):
    o_ref[...] = _gathered_ln2(dest_ref, x1_ref, ys_hbm, g_ref, b_ref, buf, sem, n_steps, tile)


def _ln2_in_proj_body(dest_ref, x1_ref, ys_hbm, g_ref, b_ref, w_ref, bias_ref, loglb_ref, log1mlb_ref,
                      o_ref, *rest, n_steps, tile):
    *proj_refs, buf, sem = rest
    x = _gathered_ln2(dest_ref, x1_ref, ys_hbm, g_ref, b_ref, buf, sem, n_steps, tile)
    o_ref[...] = x
    _project(x, w_ref, bias_ref, loglb_ref, log1mlb_ref, proj_refs)


def _ln2(x1_z, ys_z, dest, g_row, b_row, tile, w_bf=None, bias_row=None, loglb=None, log1mlb=None):
    n = x1_z.shape[0] // SUBLANES
    nt = n // tile
    tok = lambda i, d: (i, 0)
    fixed = lambda i, d: (0, 0)
    in_specs = [pl.BlockSpec((tile * SUBLANES, LANES), tok), pl.BlockSpec(memory_space=pl.ANY),
                pl.BlockSpec((1, D_MODEL), fixed), pl.BlockSpec((1, D_MODEL), fixed)]
    out_specs = [pl.BlockSpec((tile, D_MODEL), tok)]
    out_shape = [jax.ShapeDtypeStruct((n, D_MODEL), F32)]
    operands = [dest, x1_z, ys_z, g_row, b_row]
    body = _ln2_body
    if w_bf is not None:
        in_specs += [pl.BlockSpec((D_MODEL, IN_WIDTH), fixed), pl.BlockSpec((1, IN_WIDTH), fixed),
                     pl.BlockSpec((1, HG_WIDTH), fixed), pl.BlockSpec((1, HG_WIDTH), fixed)]
        out_specs += [pl.BlockSpec((tile, w), tok) for w in PROJ_WIDTHS]
        out_shape += [jax.ShapeDtypeStruct((n, w), dt) for w, dt in zip(PROJ_WIDTHS, PROJ_DTYPES)]
        operands += [w_bf, bias_row, loglb, log1mlb]
        body = _ln2_in_proj_body
    return pl.pallas_call(
        functools.partial(body, n_steps=nt, tile=tile),
        grid_spec=pltpu.PrefetchScalarGridSpec(
            num_scalar_prefetch=1,
            grid=(nt,),
            in_specs=in_specs,
            out_specs=out_specs,
            scratch_shapes=[pltpu.VMEM((2, tile * SUBLANES, LANES), F32),
                            pltpu.SemaphoreType.DMA((2,))]),
        out_shape=out_shape,
        compiler_params=_params(("arbitrary",), 56),
        name="gather_ln2" if w_bf is None else "gather_ln2_in_proj",
    )(*operands)


def _t5_bucket(dist):
    max_exact = N_BUCKETS // 2
    d = jnp.maximum(dist, 0)
    large = max_exact + (jnp.log(jnp.maximum(d, 1).astype(F32) / max_exact)
                         / math.log(MAX_DISTANCE / max_exact) * (N_BUCKETS - max_exact)).astype(I32)
    large = jnp.minimum(large, N_BUCKETS - 1)
    return jnp.where(d < max_exact, d, large)


def _banded_bias(rel_bias):
    i = jnp.arange(ATTN_BLOCK)[:, None]
    j = jnp.arange(2 * ATTN_BLOCK)[None, :]
    dist = i + ATTN_BLOCK - j
    bucket = _t5_bucket(dist)[None]
    table = rel_bias.astype(F32)
    bias = jnp.zeros((SWA_HEADS, ATTN_BLOCK, 2 * ATTN_BLOCK), F32)
    for b in range(N_BUCKETS):
        bias = jnp.where(bucket == b, table[b][:, None, None], bias)
    from_prev = (jnp.arange(ATTN_BLOCK)[None, :] > i)[None]
    prev, own = bias[:, :, :ATTN_BLOCK], bias[:, :, ATTN_BLOCK:]
    normal = jnp.where(from_prev, prev, own)
    first = jnp.where(from_prev, -jnp.inf, own)
    return jnp.swapaxes(jnp.stack([first, normal]), 2, 3)


def _work_items(counts, n_tok):
    n_blocks = n_tok // MOE_BLK
    cend = jnp.cumsum(counts)
    cstart = cend - counts
    blk_starts = jnp.arange(n_blocks, dtype=I32) * MOE_BLK
    cls_starts = jnp.where(counts > 0, cstart, n_tok)
    start = jnp.sort(jnp.concatenate([blk_starts, cls_starts]))
    end = jnp.concatenate([start[1:], jnp.full((1,), n_tok, I32)])
    blk = jnp.minimum(start, n_tok - 1) // MOE_BLK
    cls = jnp.minimum(jnp.sum((cend[None, :] <= start[:, None]).astype(I32), axis=1), N_CLASSES - 1)
    group, pair = cls // N_PAIRS, cls % N_PAIRS
    lo = (pair >= 3).astype(I32) + (pair >= 5).astype(I32)
    hi = jnp.where(pair < 3, pair + 1, jnp.where(pair < 5, pair - 1, 3))
    return cstart, (blk, group * EXPERTS_PER_GROUP + lo, group * EXPERTS_PER_GROUP + hi, start, end)


def kernel(x, mem, w_in, b_in, w_mem_kv, attn_sinks, rel_bias, hgrn_lb_logits, hgrn_norm, w_out,
           ln1_g, ln1_b, w_router, router_bias, w_gate, w_up, w_down, ln2_g, ln2_b):
    batch, seq, _ = x.shape
    mem_len = mem.shape[1]
    n_tok = batch * seq
    route_tile = min(512, n_tok)

    bias = _banded_bias(rel_bias)
    lb = jnp.cumsum(jax.nn.softmax(hgrn_lb_logits.astype(F32), axis=0), axis=0)
    lb = lb - lb[0:1]
    log_lb = jnp.log(lb)
    log_1m_lb = jnp.log1p(-lb)
    head_ones = (jnp.arange(HG_WIDTH)[:, None] // HG_DV == jnp.arange(HG_WIDTH)[None, :] // HG_DV).astype(BF16)
    tri = (jnp.arange(route_tile)[:, None] <= jnp.arange(route_tile)[None, :]).astype(BF16)
    wr_t = jnp.transpose(w_router).astype(BF16)
    wr_pad = jnp.pad(w_router.astype(BF16), ((0, 0), (0, LANES - N_EXPERTS)))
    rb_col = router_bias.astype(F32).reshape(N_EXPERTS, 1)
    w_in_bf = w_in.astype(BF16)
    w_out_bf = w_out.astype(BF16)
    wg_bf, wu_bf, wd_bf = w_gate.astype(BF16), w_up.astype(BF16), w_down.astype(BF16)

    mk, mv = _mem_kv(mem.reshape(batch * mem_len, D_MODEL), w_mem_kv)

    x2 = x.reshape(n_tok, D_MODEL)
    gate_rows = lambda l: (log_lb[l].reshape(1, HG_WIDTH), log_1m_lb[l].reshape(1, HG_WIDTH))
    proj = _in_proj(x2, w_in_bf[0], b_in[0].reshape(1, IN_WIDTH), *gate_rows(0))
    for l in range(DEPTH):
        sq, sk, sv, qq, log_f, one_minus_f, vv, gate_act, mq = proj
        swa_o = _swa(sq, sk, sv, attn_sinks[l].astype(F32), bias, batch, seq)
        hg_o = _hgrn(qq, log_f, one_minus_f, vv, gate_act,
                     jnp.tile(hgrn_norm[l].astype(F32), HG_HEADS).reshape(1, HG_WIDTH), head_ones, batch, seq)
        mem_o = _mem_attn(mq, mk, mv, l, batch, seq, mem_len)
        x1_z, cls, rank, counts = _out_proj(
            x2, swa_o, hg_o, mem_o, w_out_bf[l], ln1_g[l].reshape(1, D_MODEL), ln1_b[l].reshape(1, D_MODEL),
            wr_t, rb_col, tri)
        cstart, items = _work_items(counts[:N_CLASSES, 0].astype(I32), n_tok)
        dest, xs_z = _scatter_rows(x1_z, cls, rank, cstart)
        ys_z = _experts(xs_z, items, wr_pad, wg_bf, wu_bf, wd_bf, l)
        ln2_args = (x1_z, ys_z, dest.reshape(n_tok), ln2_g[l].reshape(1, D_MODEL), ln2_b[l].reshape(1, D_MODEL),
                    route_tile)
        if l + 1 < DEPTH:
            x2, *proj = _ln2(*ln2_args, w_in_bf[l + 1], b_in[l + 1].reshape(1, IN_WIDTH), *gate_rows(l + 1))
        else:
            (x2,) = _ln2(*ln2_args)
    return x2.reshape(batch, seq, D_MODEL)
```

```python
import functools
import math

import jax
import jax.numpy as jnp
from jax import lax
from jax.experimental import pallas as pl
from jax.experimental.pallas import tpu as pltpu

F32 = jnp.float32
BF16 = jnp.bfloat16
I32 = jnp.int32

D_MODEL = 1024
DEPTH = 4
HEAD_DIM = 64
SWA_HEADS = 8
SWA_KV_HEADS = 2
SWA_GROUP = SWA_HEADS // SWA_KV_HEADS
WINDOW = 128
ATTN_BLOCK = 128
HG_HEADS = 4
HG_DK = 64
HG_DV = 64
HG_WIDTH = HG_HEADS * HG_DK
MEM_HEADS = 4
N_BUCKETS = 32
MAX_DISTANCE = 128
N_EXPERTS = 16
N_GROUPS = 4
EXPERTS_PER_GROUP = 4
D_EXPERT = 512
LN_EPS = 1e-5
RMS_EPS = 1e-6
ALPHA = (2 * DEPTH) ** 0.25
SPLITS = (512, 128, 128, 256, 256, 256, 256, 256)
IN_WIDTH = sum(SPLITS)
PROJ_DTYPES = (BF16, BF16, BF16, F32, F32, F32, F32, BF16)
MIX_WIDTH = 1024
ATTN_SCALE = HEAD_DIM ** -0.5

SUBLANES = 8
LANES = 128
ROW_TILES = D_MODEL // LANES

SWA_STEP_BLOCKS = 4
HG_CHUNK = 128
HG_SUB = 32
HG_SAFE_DECAY = 80.0
N_PAIRS = 6
N_CLASSES = N_GROUPS * N_PAIRS
CLASS_ROWS = 32
MOE_BLK = 256
DMA_GROUP = 8
GATHER_AHEAD = 2


def _params(semantics, vmem_mib):
    return pltpu.CompilerParams(dimension_semantics=semantics, vmem_limit_bytes=vmem_mib * 1024 * 1024)


def _dot(a, b):
    return jnp.dot(a, b, preferred_element_type=F32)


def _dot_nt(a, b):
    return lax.dot_general(a, b, (((1,), (1,)), ((), ())), preferred_element_type=F32)


def _dot_tn(a, b):
    return lax.dot_general(a, b, (((0,), (0,)), ((), ())), preferred_element_type=F32)


def _from_tiles(z, n_rows):
    k = z.shape[0] // n_rows
    return z.reshape(n_rows, k, LANES).reshape(n_rows, k * LANES)


def _to_tiles(val):
    n_rows, width = val.shape
    return val.reshape(n_rows, width // LANES, LANES).reshape(n_rows * (width // LANES), LANES)


def _in_proj_body(x_ref, w_ref, b_ref, *out_refs):
    acc = _dot(x_ref[...].astype(BF16), w_ref[...]) + b_ref[...]
    off = 0
    for ref, width in zip(out_refs, SPLITS):
        ref[...] = acc[:, off:off + width].astype(ref.dtype)
        off += width


def _in_proj(x2, w_bf, b_row):
    n = x2.shape[0]
    tm = min(512, n)
    return pl.pallas_call(
        _in_proj_body,
        grid=(n // tm,),
        in_specs=[pl.BlockSpec((tm, D_MODEL), lambda i: (i, 0)),
                  pl.BlockSpec((D_MODEL, IN_WIDTH), lambda i: (0, 0)),
                  pl.BlockSpec((1, IN_WIDTH), lambda i: (0, 0))],
        out_specs=[pl.BlockSpec((tm, w), lambda i: (i, 0)) for w in SPLITS],
        out_shape=[jax.ShapeDtypeStruct((n, w), dt) for w, dt in zip(SPLITS, PROJ_DTYPES)],
        compiler_params=_params(("arbitrary",), 48),
        name="in_proj",
    )(x2, w_bf, b_row)


def _mem_kv_body(mem_ref, w_ref, k_ref, v_ref):
    acc = _dot(mem_ref[...].astype(BF16), w_ref[0].astype(BF16))
    half = MEM_HEADS * HEAD_DIM
    k_ref[0] = acc[:, :half].astype(BF16)
    v_ref[0] = acc[:, half:].astype(BF16)


def _mem_kv(mem2, w_mem_kv):
    rows = mem2.shape[0]
    tm = min(512, rows)
    half = MEM_HEADS * HEAD_DIM
    return pl.pallas_call(
        _mem_kv_body,
        grid=(DEPTH, rows // tm),
        in_specs=[pl.BlockSpec((tm, D_MODEL), lambda l, i: (i, 0)),
                  pl.BlockSpec((1, D_MODEL, 2 * half), lambda l, i: (l, 0, 0))],
        out_specs=[pl.BlockSpec((1, tm, half), lambda l, i: (l, i, 0)),
                   pl.BlockSpec((1, tm, half), lambda l, i: (l, i, 0))],
        out_shape=[jax.ShapeDtypeStruct((DEPTH, rows, half), BF16)] * 2,
        compiler_params=_params(("arbitrary", "arbitrary"), 32),
        name="mem_kv",
    )(mem2, w_mem_kv)


def _swa_body(sink_ref, q_ref, kp_ref, ko_ref, vp_ref, vo_ref, bias_ref, o_ref, *, blocks):
    n = pl.program_id(1)
    q_all = q_ref[...] * ATTN_SCALE
    k_all = jnp.concatenate([kp_ref[...], ko_ref[...]], axis=0)
    v_all = jnp.concatenate([vp_ref[...], vo_ref[...]], axis=0)
    from_prev = lax.broadcasted_iota(I32, (ATTN_BLOCK, ATTN_BLOCK), 0) > \
        lax.broadcasted_iota(I32, (ATTN_BLOCK, ATTN_BLOCK), 1)
    for j in range(blocks):
        lo = j * ATTN_BLOCK
        kk = k_all[lo:lo + 2 * ATTN_BLOCK]
        vv = v_all[lo:lo + 2 * ATTN_BLOCK]
        table = jnp.where(n > 0, 1, 0) if j == 0 else 1
        outs = []
        for h in range(SWA_HEADS):
            g = h // SWA_GROUP
            q_h = q_all[lo:lo + ATTN_BLOCK, h * HEAD_DIM:(h + 1) * HEAD_DIM]
            scores = _dot_nt(kk[:, g * HEAD_DIM:(g + 1) * HEAD_DIM], q_h)
            band = jnp.where(from_prev, scores[:ATTN_BLOCK], scores[ATTN_BLOCK:]) + bias_ref[table, h]
            sink = sink_ref[h]
            m = jnp.maximum(jnp.max(band, axis=0, keepdims=True), sink)
            e = jnp.exp(band - m)
            p = e * (1.0 / (jnp.sum(e, axis=0, keepdims=True) + jnp.exp(sink - m)))
            p = jnp.concatenate([jnp.where(from_prev, p, 0.0), jnp.where(from_prev, 0.0, p)], axis=0)
            outs.append(_dot_tn(p.astype(BF16), vv[:, g * HEAD_DIM:(g + 1) * HEAD_DIM]))
        o_ref[lo:lo + ATTN_BLOCK, :] = jnp.concatenate(outs, axis=-1).astype(o_ref.dtype)


def _swa(sq, sk, sv, sinks, bias2, batch, seq):
    nb = seq // ATTN_BLOCK
    blocks = math.gcd(SWA_STEP_BLOCKS, nb)
    ns = nb // blocks
    own = lambda b, n: (b * ns + n, 0)
    prev = lambda b, n: (b * nb + jnp.maximum(n * blocks - 1, 0), 0)
    kvw = SWA_KV_HEADS * HEAD_DIM
    qw = SWA_HEADS * HEAD_DIM
    rows = blocks * ATTN_BLOCK
    return pl.pallas_call(
        functools.partial(_swa_body, blocks=blocks),
        grid=(batch, ns),
        in_specs=[pl.BlockSpec(memory_space=pltpu.SMEM),
                  pl.BlockSpec((rows, qw), own),
                  pl.BlockSpec((ATTN_BLOCK, kvw), prev),
                  pl.BlockSpec((rows, kvw), own),
                  pl.BlockSpec((ATTN_BLOCK, kvw), prev),
                  pl.BlockSpec((rows, kvw), own),
                  pl.BlockSpec((2, SWA_HEADS, ATTN_BLOCK, ATTN_BLOCK), lambda b, n: (0, 0, 0, 0))],
        out_specs=pl.BlockSpec((rows, qw), own),
        out_shape=jax.ShapeDtypeStruct((batch * seq, qw), BF16),
        compiler_params=_params(("arbitrary", "arbitrary"), 32),
        name="swa",
    )(sinks, sq, sk, sk, sv, sv, bias2)


def _cumsum_rows(x):
    n = x.shape[0]
    row = lax.broadcasted_iota(I32, (n, 1), 0)
    s = 1
    while s < n:
        x = x + jnp.where(row >= s, pltpu.roll(x, s, 0), 0.0)
        s *= 2
    return x


def _rows_from_blocks(vals, width):
    return jnp.concatenate([jnp.broadcast_to(v, (HG_SUB, width)) for v in vals], axis=0)


def _hgrn_body(hq_ref, hf_ref, hi_ref, hg_ref, loglb_ref, log1mlb_ref, nw_ref, bo_ref, o_ref,
               st_ref, a_scr, q_scr, k_scr, oi_scr):
    c = pl.program_id(1)
    n_sub = HG_CHUNK // HG_SUB
    width = HG_WIDTH

    @pl.when(c == 0)
    def _():
        st_ref[...] = jnp.zeros_like(st_ref)

    z = hf_ref[...]
    log_sig = jnp.minimum(z, 0.0) - jnp.log(1.0 + jnp.exp(-jnp.abs(z)))
    t_a = loglb_ref[...]
    t_b = log1mlb_ref[...] + log_sig
    log_f = jnp.maximum(t_a, t_b) + jnp.log(1.0 + jnp.exp(-jnp.abs(t_a - t_b)))
    kk = jnp.exp(t_b - z)
    hq = hq_ref[...]
    qq = hq * jax.nn.sigmoid(hq) * (HG_DK ** -0.5)
    vv = hi_ref[...]

    a_cum = _cumsum_rows(log_f)
    ends = [a_cum[HG_SUB * j + HG_SUB - 1:HG_SUB * j + HG_SUB, :] for j in range(n_sub)]
    zero_row = jnp.zeros((1, width), F32)
    starts = [zero_row] + ends[:-1]
    a_loc = a_cum - _rows_from_blocks(starts, width)
    e_loc = _rows_from_blocks(ends, width) - a_cum
    worst = starts[0] - ends[0]
    for j in range(1, n_sub):
        worst = jnp.maximum(worst, starts[j] - ends[j])
    unsafe = jnp.max(worst) > HG_SAFE_DECAY

    q_sub = qq * jnp.exp(a_loc)
    k_sub = kk * jnp.exp(-a_loc)
    k_end = kk * jnp.exp(e_loc)
    one_row = jnp.ones((1, width), F32)
    q_dec = q_sub * _rows_from_blocks([jnp.exp(s) for s in starts], width)
    k_dec = k_end * _rows_from_blocks([jnp.exp(ends[-1] - e) for e in ends], width)
    q_top = q_sub * _rows_from_blocks(
        [zero_row, zero_row, one_row, jnp.exp(ends[2] - ends[1])], width)
    k_top = k_end * _rows_from_blocks(
        [jnp.exp(ends[1] - ends[0]), one_row, zero_row, zero_row], width)

    row = lax.broadcasted_iota(I32, (HG_CHUNK, 1), 0)
    sub_of_row = row // HG_SUB
    t_idx = lax.broadcasted_iota(I32, (HG_CHUNK, HG_CHUNK), 0)
    s_idx = lax.broadcasted_iota(I32, (HG_CHUNK, HG_CHUNK), 1)
    m_sub = ((t_idx // HG_SUB) == (s_idx // HG_SUB)) & (t_idx >= s_idx)
    m_sub2 = jnp.concatenate([m_sub, m_sub], axis=0)
    lane = lax.broadcasted_iota(I32, (1, LANES), 1)
    lo = lane < HG_DK
    bd = (lax.broadcasted_iota(I32, (LANES, LANES), 0) // HG_DV) == \
         (lax.broadcasted_iota(I32, (LANES, LANES), 1) // HG_DK)
    dec_row = jnp.exp(ends[-1])

    def heads_on_rows(x):
        return jnp.concatenate([jnp.where(lo, x, 0.0), jnp.where(lo, 0.0, x)], axis=0)

    o_inter = []
    for p in range(width // LANES):
        sl = slice(p * LANES, (p + 1) * LANES)
        qs, ks, ke = q_sub[:, sl], k_sub[:, sl], k_end[:, sl]
        p_sub = _dot_nt(heads_on_rows(qs).astype(BF16), ks.astype(BF16))
        q_x = jnp.concatenate([jnp.where(sub_of_row == 1, qs, 0.0),
                               jnp.where(sub_of_row == 3, qs, 0.0), q_top[:, sl]], axis=1)
        k_x = jnp.concatenate([jnp.where(sub_of_row == 0, ke, 0.0),
                               jnp.where(sub_of_row == 2, ke, 0.0), k_top[:, sl]], axis=1)
        q_x2 = jnp.concatenate([jnp.where(jnp.tile(lo, (1, 3)), q_x, 0.0),
                                jnp.where(jnp.tile(lo, (1, 3)), 0.0, q_x)], axis=0)
        p_x = _dot_nt(q_x2.astype(BF16), k_x.astype(BF16))
        p_all = jnp.where(m_sub2, p_sub, 0.0) + p_x
        p_cat = jnp.concatenate([p_all[:HG_CHUNK], p_all[HG_CHUNK:]], axis=1)
        v_p = vv[:, sl]
        oi_scr[:, sl] = _dot(p_cat.astype(BF16), heads_on_rows(v_p).astype(BF16))
        st = st_ref[p]
        o_inter.append(_dot_nt(q_dec[:, sl].astype(BF16), st.astype(BF16)))
        upd = _dot_tn(v_p.astype(BF16), k_dec[:, sl].astype(BF16))
        st_ref[p] = st * dec_row[:, sl] + jnp.where(bd, upd, 0.0)
    o_inter = jnp.concatenate(o_inter, axis=1)

    @pl.when(unsafe)
    def _():
        a_scr[...] = a_cum
        q_scr[...] = qq
        k_scr[...] = kk

        def body(t, carry):
            d = a_scr[pl.ds(t, 1), :] - a_scr[...]
            w = jnp.exp(jnp.where(row <= t, d, -jnp.inf))
            prod = (q_scr[pl.ds(t, 1), :] * k_scr[...]) * w
            e = _dot(prod.astype(BF16), bo_ref[...])
            oi_scr[pl.ds(t, 1), :] = jnp.sum(e * hi_ref[...], axis=0, keepdims=True)
            return carry

        lax.fori_loop(0, HG_CHUNK, body, 0)

    o = oi_scr[...] + o_inter
    sq = o * o
    sq_hi = sq.astype(BF16)
    sq_lo = (sq - sq_hi.astype(F32)).astype(BF16)
    ms = (_dot(sq_hi, bo_ref[...]) + _dot(sq_lo, bo_ref[...])) * (1.0 / HG_DV)
    gate = hg_ref[...]
    o = o * lax.rsqrt(ms + RMS_EPS) * nw_ref[...] * (gate * jax.nn.sigmoid(gate))
    o_ref[...] = o.astype(o_ref.dtype)


def _hgrn(hq, hf, hi, hg, loglb, log1mlb, nw_row, head_ones, batch, seq):
    nc = seq // HG_CHUNK
    blk = lambda b, c: (b * nc + c, 0)
    row = lambda b, c: (0, 0)
    w = HG_WIDTH
    return pl.pallas_call(
        _hgrn_body,
        grid=(batch, nc),
        in_specs=[pl.BlockSpec((HG_CHUNK, w), blk)] * 4 + [pl.BlockSpec((1, w), row)] * 3
                 + [pl.BlockSpec((w, w), row)],
        out_specs=pl.BlockSpec((HG_CHUNK, w), blk),
        out_shape=jax.ShapeDtypeStruct((batch * seq, w), BF16),
        scratch_shapes=[pltpu.VMEM((w // LANES, LANES, LANES), F32)]
                       + [pltpu.VMEM((HG_CHUNK, w), F32)] * 4,
        compiler_params=_params(("arbitrary", "arbitrary"), 32),
        name="hgrn",
    )(hq, hf, hi, hg, loglb, log1mlb, nw_row, head_ones)


def _mem_attn_body(q_ref, k_ref, v_ref, o_ref):
    q_all = q_ref[...] * ATTN_SCALE
    outs = []
    for h in range(MEM_HEADS):
        sl = slice(h * HEAD_DIM, (h + 1) * HEAD_DIM)
        scores = _dot_nt(k_ref[0, :, sl], q_all[:, sl])
        e = jnp.exp(scores - jnp.max(scores, axis=0, keepdims=True))
        p = e * (1.0 / jnp.sum(e, axis=0, keepdims=True))
        outs.append(_dot_tn(p.astype(BF16), v_ref[0, :, sl]))
    o_ref[...] = jnp.concatenate(outs, axis=-1).astype(o_ref.dtype)


def _mem_attn(mq, mk, mv, layer, batch, seq, mem_len):
    tq = min(1024, seq)
    nq = seq // tq
    w = MEM_HEADS * HEAD_DIM
    return pl.pallas_call(
        _mem_attn_body,
        grid=(batch, nq),
        in_specs=[pl.BlockSpec((tq, w), lambda b, i: (b * nq + i, 0)),
                  pl.BlockSpec((1, mem_len, w), lambda b, i: (layer, b, 0)),
                  pl.BlockSpec((1, mem_len, w), lambda b, i: (layer, b, 0))],
        out_specs=pl.BlockSpec((tq, w), lambda b, i: (b * nq + i, 0)),
        out_shape=jax.ShapeDtypeStruct((batch * seq, w), BF16),
        compiler_params=_params(("arbitrary", "arbitrary"), 32),
        name="mem_attn",
    )(mq, mk, mv)


def _layer_norm(h, g, b):
    mu = jnp.mean(h, axis=-1, keepdims=True)
    d = h - mu
    var = jnp.mean(d * d, axis=-1, keepdims=True)
    return d * lax.rsqrt(var + LN_EPS) * g + b


def _route_rows(sel):
    def row(a, r):
        return a[r:r + 1, :]

    best = None
    for g in range(N_GROUPS):
        a = [row(sel, 4 * g + i) for i in range(4)]
        gs = a[0] + a[1]
        for i, j in ((0, 2), (0, 3), (1, 2), (1, 3), (2, 3)):
            gs = jnp.maximum(gs, a[i] + a[j])
        if best is None:
            best, best_score = jnp.zeros_like(gs, dtype=I32), gs
        else:
            better = gs > best_score
            best = jnp.where(better, g, best)
            best_score = jnp.where(better, gs, best_score)

    def pick(arr, i):
        out = row(arr, i)
        for g in range(1, N_GROUPS):
            out = jnp.where(best == g, row(arr, 4 * g + i), out)
        return out

    a = [pick(sel, i) for i in range(4)]
    i1, m1 = jnp.zeros_like(best), a[0]
    for i in range(1, 4):
        gt = a[i] > m1
        i1 = jnp.where(gt, i, i1)
        m1 = jnp.where(gt, a[i], m1)
    i2 = jnp.full_like(best, -1)
    m2 = jnp.full_like(m1, -jnp.inf)
    for i in range(4):
        gt = (i1 != i) & (a[i] > m2)
        i2 = jnp.where(gt, i, i2)
        m2 = jnp.where(gt, a[i], m2)
    lo = jnp.minimum(i1, i2)
    hi = jnp.maximum(i1, i2)
    pair = jnp.where(lo == 0, hi - 1, jnp.where(lo == 1, hi + 1, 5))
    return best * N_PAIRS + pair


def _out_proj_body(x_ref, swa_ref, hg_ref, mem_ref, w_ref, g_ref, b_ref, wr_ref, rb_ref, tri_ref,
                   x1_ref, cls_ref, rank_ref, cnt_ref, carry_ref):
    i = pl.program_id(0)
    t = x_ref.shape[0]

    @pl.when(i == 0)
    def _():
        carry_ref[...] = jnp.zeros_like(carry_ref)

    acc = _dot(swa_ref[...], w_ref[0:512, :]) + _dot(hg_ref[...], w_ref[512:768, :]) \
        + _dot(mem_ref[...], w_ref[768:1024, :])
    x1 = _layer_norm(ALPHA * x_ref[...] + acc, g_ref[...], b_ref[...])
    x1_ref[...] = _to_tiles(x1)

    logits = _dot_nt(wr_ref[...], x1.astype(BF16))
    cls = _route_rows(jax.nn.sigmoid(logits) + rb_ref[...])
    onehot = (lax.broadcasted_iota(I32, (CLASS_ROWS, t), 0) == cls).astype(F32)
    prefix = _dot(onehot.astype(BF16), tri_ref[...])
    carry = carry_ref[...]
    rank = jnp.sum(onehot * (prefix - 1.0 + carry[:, 0:1]), axis=0, keepdims=True)
    new_carry = carry + prefix[:, t - 1:t]
    carry_ref[...] = new_carry
    cnt_ref[...] = new_carry
    cls_ref[0] = cls
    rank_ref[0] = rank.astype(I32)


def _out_proj(x2, swa_o, hg_o, mem_o, w_bf, g_row, b_row, wr_t, rb_col, tri):
    n = x2.shape[0]
    t = tri.shape[0]
    nt = n // t
    tok = lambda i: (i, 0)
    fixed = lambda i: (0, 0)
    per_tile = pl.BlockSpec((1, 1, t), lambda i: (i, 0, 0))
    return pl.pallas_call(
        _out_proj_body,
        grid=(nt,),
        in_specs=[pl.BlockSpec((t, D_MODEL), tok), pl.BlockSpec((t, 512), tok),
                  pl.BlockSpec((t, 256), tok), pl.BlockSpec((t, 256), tok),
                  pl.BlockSpec((MIX_WIDTH, D_MODEL), fixed),
                  pl.BlockSpec((1, D_MODEL), fixed), pl.BlockSpec((1, D_MODEL), fixed),
                  pl.BlockSpec((N_EXPERTS, D_MODEL), fixed), pl.BlockSpec((N_EXPERTS, 1), fixed),
                  pl.BlockSpec((t, t), fixed)],
        out_specs=[pl.BlockSpec((t * ROW_TILES, LANES), tok), per_tile, per_tile,
                   pl.BlockSpec((CLASS_ROWS, LANES), fixed)],
        out_shape=[jax.ShapeDtypeStruct((n * ROW_TILES, LANES), F32),
                   jax.ShapeDtypeStruct((nt, 1, t), I32), jax.ShapeDtypeStruct((nt, 1, t), I32),
                   jax.ShapeDtypeStruct((CLASS_ROWS, LANES), F32)],
        scratch_shapes=[pltpu.VMEM((CLASS_ROWS, LANES), F32)],
        compiler_params=_params(("arbitrary",), 48),
        name="out_proj_ln_route",
    )(x2, swa_o, hg_o, mem_o, w_bf, g_row, b_row, wr_t, rb_col, tri)


def _tile_copy(src_ref, dst_ref, sem, src_row, dst_row, tiles):
    n = tiles * SUBLANES
    return pltpu.make_async_copy(
        src_ref.at[pl.ds(pl.multiple_of(src_row * n, n), n), :],
        dst_ref.at[pl.ds(pl.multiple_of(dst_row * n, n), n), :], sem)


def _scatter_body(cstart_ref, cls_ref, rank_ref, x_hbm, dest_ref, xs_hbm,
                  buf, dest_vmem, dest_smem, load_sem, scat_sem, misc_sem, *, n_steps, tile):
    i = pl.program_id(0)
    rows = tile * SUBLANES
    slot = i % 3

    def load(step, into):
        return pltpu.make_async_copy(
            x_hbm.at[pl.ds(pl.multiple_of(step * rows, rows), rows), :], buf.at[into], load_sem.at[into])

    def scatter_done(of):
        return pltpu.make_async_copy(buf.at[of], xs_hbm.at[pl.ds(0, rows), :], scat_sem.at[of])

    @pl.when(i == 0)
    def _():
        load(0, 0).start()

    @pl.when(i + 1 < n_steps)
    def _():
        load(i + 1, (i + 1) % 3).start()

    cls = cls_ref[0]
    dest = rank_ref[0]
    for c in range(N_CLASSES):
        dest = dest + jnp.where(cls == c, cstart_ref[c], 0)
    dest_ref[0] = dest
    dest_vmem[...] = dest
    to_smem = pltpu.make_async_copy(dest_vmem, dest_smem, misc_sem)
    to_smem.start()
    to_smem.wait()

    load(i, slot).wait()

    def issue(group, carry):
        for k in range(DMA_GROUP):
            j = group * DMA_GROUP + k
            _tile_copy(buf.at[slot], xs_hbm, scat_sem.at[slot], j, dest_smem[0, j], 1).start(priority=k % 2)
        return carry

    lax.fori_loop(0, tile // DMA_GROUP, issue, 0)

    @pl.when(i >= 1)
    def _():
        scatter_done((i + 2) % 3).wait()

    @pl.when(i == n_steps - 1)
    def _():
        scatter_done(slot).wait()


def _scatter_rows(x1_z, cls, rank, cstart):
    nt, _, tile = cls.shape
    n = nt * tile
    per_tile = pl.BlockSpec((1, 1, tile), lambda i, cs: (i, 0, 0))
    return pl.pallas_call(
        functools.partial(_scatter_body, n_steps=nt, tile=tile),
        grid_spec=pltpu.PrefetchScalarGridSpec(
            num_scalar_prefetch=1,
            grid=(nt,),
            in_specs=[per_tile, per_tile, pl.BlockSpec(memory_space=pl.ANY)],
            out_specs=[per_tile, pl.BlockSpec(memory_space=pl.ANY)],
            scratch_shapes=[pltpu.VMEM((3, tile * SUBLANES, LANES), F32),
                            pltpu.VMEM((1, tile), I32), pltpu.SMEM((1, tile), I32),
                            pltpu.SemaphoreType.DMA((3,)), pltpu.SemaphoreType.DMA((3,)),
                            pltpu.SemaphoreType.DMA(())]),
        out_shape=[jax.ShapeDtypeStruct((nt, 1, tile), I32),
                   jax.ShapeDtypeStruct((n * SUBLANES, LANES), F32)],
        compiler_params=_params(("arbitrary",), 32),
        name="scatter_rows",
    )(cstart, cls, rank, x1_z)


def _expert_body(blk_ref, elo_ref, ehi_ref, start_ref, end_ref, xs_ref, wr_ref,
                 g_lo, u_lo, d_lo, g_hi, u_hi, d_hi, ys_ref):
    i = pl.program_id(0)
    start = start_ref[i]
    end = end_ref[i]

    @pl.when(end > start)
    def _():
        x = _from_tiles(xs_ref[...], MOE_BLK).astype(BF16)
        scores = jax.nn.sigmoid(_dot(x, wr_ref[...]))
        lane = lax.broadcasted_iota(I32, (1, LANES), 1)
        s_lo = jnp.sum(jnp.where(lane == elo_ref[i], scores, 0.0), axis=-1, keepdims=True)
        s_hi = jnp.sum(jnp.where(lane == ehi_ref[i], scores, 0.0), axis=-1, keepdims=True)
        denom = s_lo + s_hi

        def ffn(gw, uw, dw, w_col):
            g = _dot(x, gw[0, 0])
            u = _dot(x, uw[0, 0])
            return _dot((((g * jax.nn.sigmoid(g)) * u) * w_col).astype(BF16), dw[0, 0])

        y = _to_tiles(ffn(g_lo, u_lo, d_lo, s_lo / denom) + ffn(g_hi, u_hi, d_hi, s_hi / denom))
        first = start % MOE_BLK == 0

        @pl.when(first)
        def _():
            ys_ref[...] = y

        @pl.when(jnp.logical_not(first))
        def _():
            slot = blk_ref[i] * MOE_BLK \
                + lax.broadcasted_iota(I32, (MOE_BLK * ROW_TILES, 1), 0) // ROW_TILES
            ys_ref[...] = jnp.where((slot >= start) & (slot < end), y, ys_ref[...])


def _experts(xs_z, items, wr_pad, wg, wu, wd, layer):
    blk, elo, ehi, start, end = items
    n_items = blk.shape[0]
    n_slots = xs_z.shape[0] // SUBLANES

    def data_map(i, blk, elo, ehi, start, end):
        return (blk[i], 0)

    def w_lo_map(i, blk, elo, ehi, start, end):
        return (layer, elo[i], 0, 0)

    def w_hi_map(i, blk, elo, ehi, start, end):
        return (layer, ehi[i], 0, 0)

    up_spec = lambda m: pl.BlockSpec((1, 1, D_MODEL, D_EXPERT), m)
    down_spec = lambda m: pl.BlockSpec((1, 1, D_EXPERT, D_MODEL), m)
    return pl.pallas_call(
        _expert_body,
        grid_spec=pltpu.PrefetchScalarGridSpec(
            num_scalar_prefetch=5,
            grid=(n_items,),
            in_specs=[pl.BlockSpec((MOE_BLK * SUBLANES, LANES), data_map),
                      pl.BlockSpec((D_MODEL, LANES), lambda i, *_: (0, 0)),
                      up_spec(w_lo_map), up_spec(w_lo_map), down_spec(w_lo_map),
                      up_spec(w_hi_map), up_spec(w_hi_map), down_spec(w_hi_map)],
            out_specs=pl.BlockSpec((MOE_BLK * SUBLANES, LANES), data_map)),
        out_shape=jax.ShapeDtypeStruct((n_slots * SUBLANES, LANES), F32),
        compiler_params=_params(("arbitrary",), 48),
        name="experts",
    )(blk, elo, ehi, start, end, xs_z, wr_pad, wg, wu, wd, wg, wu, wd)


def _gathered_ln2(dest_ref, x1_ref, ys_hbm, g_ref, b_ref, buf, sem, n_steps, tile):
    i = pl.program_id(0)
    slots = GATHER_AHEAD + 1
    slot = i % slots

    def gather(step, into):
        def issue(group, carry):
            for k in range(DMA_GROUP):
                j = group * DMA_GROUP + k
                _tile_copy(ys_hbm, buf.at[into], sem.at[into], dest_ref[step * tile + j], j, 1).start(
                    priority=k % 2)
            return carry

        lax.fori_loop(0, tile // DMA_GROUP, issue, 0)

    @pl.when(i == 0)
    def _():
        for s in range(min(GATHER_AHEAD, n_steps)):
            gather(s, s)

    @pl.when(i + GATHER_AHEAD < n_steps)
    def _():
        gather(i + GATHER_AHEAD, (i + GATHER_AHEAD) % slots)

    pltpu.make_async_copy(ys_hbm.at[pl.ds(0, tile * SUBLANES), :], buf.at[slot], sem.at[slot]).wait()
    h = ALPHA * _from_tiles(x1_ref[...], tile) + _from_tiles(buf[slot], tile)
    return _layer_norm(h, g_ref[...], b_ref[...])


def _ln2_body(dest_ref, x1_ref, ys_hbm, g_ref, b_ref, o_ref, buf, sem, *, n_steps, tile):
    o_ref[...] = _gathered_ln2(dest_ref, x1_ref, ys_hbm, g_ref, b_ref, buf, sem, n_steps, tile)


def _ln2_in_proj_body(dest_ref, x1_ref, ys_hbm, g_ref, b_ref, w_ref, bias_ref, o_ref, *rest, n_steps, tile):
    *proj_refs, buf, sem = rest
    x = _gathered_ln2(dest_ref, x1_ref, ys_hbm, g_ref, b_ref, buf, sem, n_steps, tile)
    o_ref[...] = x
    acc = _dot(x.astype(BF16), w_ref[...]) + bias_ref[...]
    off = 0
    for ref, width in zip(proj_refs, SPLITS):
        ref[...] = acc[:, off:off + width].astype(ref.dtype)
        off += width


def _ln2(x1_z, ys_z, dest, g_row, b_row, tile, w_bf=None, bias_row=None):
    n = x1_z.shape[0] // SUBLANES
    nt = n // tile
    tok = lambda i, d: (i, 0)
    fixed = lambda i, d: (0, 0)
    in_specs = [pl.BlockSpec((tile * SUBLANES, LANES), tok), pl.BlockSpec(memory_space=pl.ANY),
                pl.BlockSpec((1, D_MODEL), fixed), pl.BlockSpec((1, D_MODEL), fixed)]
    out_specs = [pl.BlockSpec((tile, D_MODEL), tok)]
    out_shape = [jax.ShapeDtypeStruct((n, D_MODEL), F32)]
    operands = [dest, x1_z, ys_z, g_row, b_row]
    body = _ln2_body
    if w_bf is not None:
        in_specs += [pl.BlockSpec((D_MODEL, IN_WIDTH), fixed), pl.BlockSpec((1, IN_WIDTH), fixed)]
        out_specs += [pl.BlockSpec((tile, w), tok) for w in SPLITS]
        out_shape += [jax.ShapeDtypeStruct((n, w), dt) for w, dt in zip(SPLITS, PROJ_DTYPES)]
        operands += [w_bf, bias_row]
        body = _ln2_in_proj_body
    return pl.pallas_call(
        functools.partial(body, n_steps=nt, tile=tile),
        grid_spec=pltpu.PrefetchScalarGridSpec(
            num_scalar_prefetch=1,
            grid=(nt,),
            in_specs=in_specs,
            out_specs=out_specs,
            scratch_shapes=[pltpu.VMEM((GATHER_AHEAD + 1, tile * SUBLANES, LANES), F32),
                            pltpu.SemaphoreType.DMA((GATHER_AHEAD + 1,))]),
        out_shape=out_shape,
        compiler_params=_params(("arbitrary",), 56),
        name="gather_ln2" if w_bf is None else "gather_ln2_in_proj",
    )(*operands)


def _t5_bucket(dist):
    max_exact = N_BUCKETS // 2
    d = jnp.maximum(dist, 0)
    large = max_exact + (jnp.log(jnp.maximum(d, 1).astype(F32) / max_exact)
                         / math.log(MAX_DISTANCE / max_exact) * (N_BUCKETS - max_exact)).astype(I32)
    large = jnp.minimum(large, N_BUCKETS - 1)
    return jnp.where(d < max_exact, d, large)


def _banded_bias(rel_bias):
    i = jnp.arange(ATTN_BLOCK)[:, None]
    j = jnp.arange(2 * ATTN_BLOCK)[None, :]
    dist = i + ATTN_BLOCK - j
    bucket = _t5_bucket(dist)[None]
    table = rel_bias.astype(F32)
    bias = jnp.zeros((SWA_HEADS, ATTN_BLOCK, 2 * ATTN_BLOCK), F32)
    for b in range(N_BUCKETS):
        bias = jnp.where(bucket == b, table[b][:, None, None], bias)
    from_prev = (jnp.arange(ATTN_BLOCK)[None, :] > i)[None]
    prev, own = bias[:, :, :ATTN_BLOCK], bias[:, :, ATTN_BLOCK:]
    normal = jnp.where(from_prev, prev, own)
    first = jnp.where(from_prev, -jnp.inf, own)
    return jnp.swapaxes(jnp.stack([first, normal]), 2, 3)


def _work_items(counts, n_tok):
    n_blocks = n_tok // MOE_BLK
    cend = jnp.cumsum(counts)
    cstart = cend - counts
    blk_starts = jnp.arange(n_blocks, dtype=I32) * MOE_BLK
    cls_starts = jnp.where(counts > 0, cstart, n_tok)
    start = jnp.sort(jnp.concatenate([blk_starts, cls_starts]))
    end = jnp.concatenate([start[1:], jnp.full((1,), n_tok, I32)])
    blk = jnp.minimum(start, n_tok - 1) // MOE_BLK
    cls = jnp.minimum(jnp.sum((cend[None, :] <= start[:, None]).astype(I32), axis=1), N_CLASSES - 1)
    group, pair = cls // N_PAIRS, cls % N_PAIRS
    lo = (pair >= 3).astype(I32) + (pair >= 5).astype(I32)
    hi = jnp.where(pair < 3, pair + 1, jnp.where(pair < 5, pair - 1, 3))
    return cstart, (blk, group * EXPERTS_PER_GROUP + lo, group * EXPERTS_PER_GROUP + hi, start, end)


def kernel(x, mem, w_in, b_in, w_mem_kv, attn_sinks, rel_bias, hgrn_lb_logits, hgrn_norm, w_out,
           ln1_g, ln1_b, w_router, router_bias, w_gate, w_up, w_down, ln2_g, ln2_b):
    batch, seq, _ = x.shape
    mem_len = mem.shape[1]
    n_tok = batch * seq
    route_tile = min(512, n_tok)

    bias = _banded_bias(rel_bias)
    lb = jnp.cumsum(jax.nn.softmax(hgrn_lb_logits.astype(F32), axis=0), axis=0)
    lb = lb - lb[0:1]
    log_lb = jnp.log(lb)
    log_1m_lb = jnp.log1p(-lb)
    head_ones = (jnp.arange(HG_WIDTH)[:, None] // HG_DV == jnp.arange(HG_WIDTH)[None, :] // HG_DV).astype(BF16)
    tri = (jnp.arange(route_tile)[:, None] <= jnp.arange(route_tile)[None, :]).astype(BF16)
    wr_t = jnp.transpose(w_router).astype(BF16)
    wr_pad = jnp.pad(w_router.astype(BF16), ((0, 0), (0, LANES - N_EXPERTS)))
    rb_col = router_bias.astype(F32).reshape(N_EXPERTS, 1)
    w_in_bf = w_in.astype(BF16)
    w_out_bf = w_out.astype(BF16)
    wg_bf, wu_bf, wd_bf = w_gate.astype(BF16), w_up.astype(BF16), w_down.astype(BF16)

    mk, mv = _mem_kv(mem.reshape(batch * mem_len, D_MODEL), w_mem_kv)

    x2 = x.reshape(n_tok, D_MODEL)
    proj = _in_proj(x2, w_in_bf[0], b_in[0].reshape(1, IN_WIDTH))
    for l in range(DEPTH):
        sq, sk, sv, hq, hf, hi, hg, mq = proj
        swa_o = _swa(sq, sk, sv, attn_sinks[l].astype(F32), bias, batch, seq)
        hg_o = _hgrn(hq, hf, hi, hg, log_lb[l].reshape(1, HG_WIDTH), log_1m_lb[l].reshape(1, HG_WIDTH),
                     jnp.tile(hgrn_norm[l].astype(F32), HG_HEADS).reshape(1, HG_WIDTH), head_ones, batch, seq)
        mem_o = _mem_attn(mq, mk, mv, l, batch, seq, mem_len)
        x1_z, cls, rank, counts = _out_proj(
            x2, swa_o, hg_o, mem_o, w_out_bf[l], ln1_g[l].reshape(1, D_MODEL), ln1_b[l].reshape(1, D_MODEL),
            wr_t, rb_col, tri)
        cstart, items = _work_items(counts[:N_CLASSES, 0].astype(I32), n_tok)
        dest, xs_z = _scatter_rows(x1_z, cls, rank, cstart)
        ys_z = _experts(xs_z, items, wr_pad, wg_bf, wu_bf, wd_bf, l)
        ln2_args = (x1_z, ys_z, dest.reshape(n_tok), ln2_g[l].reshape(1, D_MODEL), ln2_b[l].reshape(1, D_MODEL),
                    route_tile)
        if l + 1 < DEPTH:
            x2, *proj = _ln2(*ln2_args, w_in_bf[l + 1], b_in[l + 1].reshape(1, IN_WIDTH))
        else:
            (x2,) = _ln2(*ln2_args)
    return x2.reshape(batch, seq, D_MODEL)
```

```python
import functools
import math

import jax
import jax.numpy as jnp
from jax import lax
from jax.experimental import pallas as pl
from jax.experimental.pallas import tpu as pltpu

F32 = jnp.float32
BF16 = jnp.bfloat16
I32 = jnp.int32

D_MODEL = 1024
DEPTH = 4
HEAD_DIM = 64
SWA_HEADS = 8
SWA_KV_HEADS = 2
SWA_GROUP = SWA_HEADS // SWA_KV_HEADS
WINDOW = 128
ATTN_BLOCK = 128
HG_HEADS = 4
HG_DK = 64
HG_DV = 64
HG_WIDTH = HG_HEADS * HG_DK
MEM_HEADS = 4
N_BUCKETS = 32
MAX_DISTANCE = 128
N_EXPERTS = 16
N_GROUPS = 4
EXPERTS_PER_GROUP = 4
D_EXPERT = 512
LN_EPS = 1e-5
RMS_EPS = 1e-6
ALPHA = (2 * DEPTH) ** 0.25
SPLITS = (512, 128, 128, 256, 256, 256, 256, 256)
IN_WIDTH = sum(SPLITS)
PROJ_DTYPES = (BF16, BF16, BF16, F32, F32, F32, F32, BF16)
MIX_WIDTH = 1024
ATTN_SCALE = HEAD_DIM ** -0.5

SUBLANES = 8
LANES = 128
ROW_TILES = D_MODEL // LANES

SWA_STEP_BLOCKS = 4
HG_CHUNK = 128
HG_SUB = 32
HG_SAFE_DECAY = 80.0
N_PAIRS = 6
N_CLASSES = N_GROUPS * N_PAIRS
CLASS_ROWS = 32
MOE_BLK = 256
DMA_GROUP = 8


def _params(semantics, vmem_mib):
    return pltpu.CompilerParams(dimension_semantics=semantics, vmem_limit_bytes=vmem_mib * 1024 * 1024)


def _dot(a, b):
    return jnp.dot(a, b, preferred_element_type=F32)


def _dot_nt(a, b):
    return lax.dot_general(a, b, (((1,), (1,)), ((), ())), preferred_element_type=F32)


def _dot_tn(a, b):
    return lax.dot_general(a, b, (((0,), (0,)), ((), ())), preferred_element_type=F32)


def _from_tiles(z, n_rows):
    k = z.shape[0] // n_rows
    return z.reshape(n_rows, k, LANES).reshape(n_rows, k * LANES)


def _to_tiles(val):
    n_rows, width = val.shape
    return val.reshape(n_rows, width // LANES, LANES).reshape(n_rows * (width // LANES), LANES)


def _in_proj_body(x_ref, w_ref, b_ref, *out_refs):
    acc = _dot(x_ref[...].astype(BF16), w_ref[...]) + b_ref[...]
    off = 0
    for ref, width in zip(out_refs, SPLITS):
        ref[...] = acc[:, off:off + width].astype(ref.dtype)
        off += width


def _in_proj(x2, w_bf, b_row):
    n = x2.shape[0]
    tm = min(512, n)
    return pl.pallas_call(
        _in_proj_body,
        grid=(n // tm,),
        in_specs=[pl.BlockSpec((tm, D_MODEL), lambda i: (i, 0)),
                  pl.BlockSpec((D_MODEL, IN_WIDTH), lambda i: (0, 0)),
                  pl.BlockSpec((1, IN_WIDTH), lambda i: (0, 0))],
        out_specs=[pl.BlockSpec((tm, w), lambda i: (i, 0)) for w in SPLITS],
        out_shape=[jax.ShapeDtypeStruct((n, w), dt) for w, dt in zip(SPLITS, PROJ_DTYPES)],
        compiler_params=_params(("arbitrary",), 48),
        name="in_proj",
    )(x2, w_bf, b_row)


def _mem_kv_body(mem_ref, w_ref, k_ref, v_ref):
    acc = _dot(mem_ref[...].astype(BF16), w_ref[0].astype(BF16))
    half = MEM_HEADS * HEAD_DIM
    k_ref[0] = acc[:, :half].astype(BF16)
    v_ref[0] = acc[:, half:].astype(BF16)


def _mem_kv(mem2, w_mem_kv):
    rows = mem2.shape[0]
    tm = min(512, rows)
    half = MEM_HEADS * HEAD_DIM
    return pl.pallas_call(
        _mem_kv_body,
        grid=(DEPTH, rows // tm),
        in_specs=[pl.BlockSpec((tm, D_MODEL), lambda l, i: (i, 0)),
                  pl.BlockSpec((1, D_MODEL, 2 * half), lambda l, i: (l, 0, 0))],
        out_specs=[pl.BlockSpec((1, tm, half), lambda l, i: (l, i, 0)),
                   pl.BlockSpec((1, tm, half), lambda l, i: (l, i, 0))],
        out_shape=[jax.ShapeDtypeStruct((DEPTH, rows, half), BF16)] * 2,
        compiler_params=_params(("arbitrary", "arbitrary"), 32),
        name="mem_kv",
    )(mem2, w_mem_kv)


def _swa_body(sink_ref, q_ref, kp_ref, ko_ref, vp_ref, vo_ref, bias_ref, o_ref, *, blocks):
    n = pl.program_id(1)
    q_all = q_ref[...] * ATTN_SCALE
    k_all = jnp.concatenate([kp_ref[...], ko_ref[...]], axis=0)
    v_all = jnp.concatenate([vp_ref[...], vo_ref[...]], axis=0)
    from_prev = lax.broadcasted_iota(I32, (ATTN_BLOCK, ATTN_BLOCK), 0) > \
        lax.broadcasted_iota(I32, (ATTN_BLOCK, ATTN_BLOCK), 1)
    for j in range(blocks):
        lo = j * ATTN_BLOCK
        kk = k_all[lo:lo + 2 * ATTN_BLOCK]
        vv = v_all[lo:lo + 2 * ATTN_BLOCK]
        table = jnp.where(n > 0, 1, 0) if j == 0 else 1
        outs = []
        for h in range(SWA_HEADS):
            g = h // SWA_GROUP
            q_h = q_all[lo:lo + ATTN_BLOCK, h * HEAD_DIM:(h + 1) * HEAD_DIM]
            scores = _dot_nt(kk[:, g * HEAD_DIM:(g + 1) * HEAD_DIM], q_h)
            band = jnp.where(from_prev, scores[:ATTN_BLOCK], scores[ATTN_BLOCK:]) + bias_ref[table, h]
            sink = sink_ref[h]
            m = jnp.maximum(jnp.max(band, axis=0, keepdims=True), sink)
            e = jnp.exp(band - m)
            p = e * (1.0 / (jnp.sum(e, axis=0, keepdims=True) + jnp.exp(sink - m)))
            p = jnp.concatenate([jnp.where(from_prev, p, 0.0), jnp.where(from_prev, 0.0, p)], axis=0)
            outs.append(_dot_tn(p.astype(BF16), vv[:, g * HEAD_DIM:(g + 1) * HEAD_DIM]))
        o_ref[lo:lo + ATTN_BLOCK, :] = jnp.concatenate(outs, axis=-1).astype(o_ref.dtype)


def _swa(sq, sk, sv, sinks, bias2, batch, seq):
    nb = seq // ATTN_BLOCK
    blocks = math.gcd(SWA_STEP_BLOCKS, nb)
    ns = nb // blocks
    own = lambda b, n: (b * ns + n, 0)
    prev = lambda b, n: (b * nb + jnp.maximum(n * blocks - 1, 0), 0)
    kvw = SWA_KV_HEADS * HEAD_DIM
    qw = SWA_HEADS * HEAD_DIM
    rows = blocks * ATTN_BLOCK
    return pl.pallas_call(
        functools.partial(_swa_body, blocks=blocks),
        grid=(batch, ns),
        in_specs=[pl.BlockSpec(memory_space=pltpu.SMEM),
                  pl.BlockSpec((rows, qw), own),
                  pl.BlockSpec((ATTN_BLOCK, kvw), prev),
                  pl.BlockSpec((rows, kvw), own),
                  pl.BlockSpec((ATTN_BLOCK, kvw), prev),
                  pl.BlockSpec((rows, kvw), own),
                  pl.BlockSpec((2, SWA_HEADS, ATTN_BLOCK, ATTN_BLOCK), lambda b, n: (0, 0, 0, 0))],
        out_specs=pl.BlockSpec((rows, qw), own),
        out_shape=jax.ShapeDtypeStruct((batch * seq, qw), BF16),
        compiler_params=_params(("arbitrary", "arbitrary"), 32),
        name="swa",
    )(sinks, sq, sk, sk, sv, sv, bias2)


def _cumsum_rows(x):
    n = x.shape[0]
    row = lax.broadcasted_iota(I32, (n, 1), 0)
    s = 1
    while s < n:
        x = x + jnp.where(row >= s, pltpu.roll(x, s, 0), 0.0)
        s *= 2
    return x


def _rows_from_blocks(vals, width):
    return jnp.concatenate([jnp.broadcast_to(v, (HG_SUB, width)) for v in vals], axis=0)


def _hgrn_body(hq_ref, hf_ref, hi_ref, hg_ref, loglb_ref, log1mlb_ref, nw_ref, bo_ref, *rest):
    n_cast = (len(rest) - 6) // 2
    cast_in, o_ref, cast_out = rest[:n_cast], rest[n_cast], rest[n_cast + 1:2 * n_cast + 1]
    st_ref, a_scr, q_scr, k_scr, oi_scr = rest[2 * n_cast + 1:]
    for src, dst in zip(cast_in, cast_out):
        dst[...] = src[0].astype(BF16)

    c = pl.program_id(1)
    n_sub = HG_CHUNK // HG_SUB
    width = HG_WIDTH

    @pl.when(c == 0)
    def _():
        st_ref[...] = jnp.zeros_like(st_ref)

    z = hf_ref[...]
    log_sig = jnp.minimum(z, 0.0) - jnp.log(1.0 + jnp.exp(-jnp.abs(z)))
    t_a = loglb_ref[...]
    t_b = log1mlb_ref[...] + log_sig
    log_f = jnp.maximum(t_a, t_b) + jnp.log(1.0 + jnp.exp(-jnp.abs(t_a - t_b)))
    kk = jnp.exp(t_b - z)
    hq = hq_ref[...]
    qq = hq * jax.nn.sigmoid(hq) * (HG_DK ** -0.5)
    vv = hi_ref[...]

    a_cum = _cumsum_rows(log_f)
    ends = [a_cum[HG_SUB * j + HG_SUB - 1:HG_SUB * j + HG_SUB, :] for j in range(n_sub)]
    zero_row = jnp.zeros((1, width), F32)
    starts = [zero_row] + ends[:-1]
    a_loc = a_cum - _rows_from_blocks(starts, width)
    e_loc = _rows_from_blocks(ends, width) - a_cum
    worst = starts[0] - ends[0]
    for j in range(1, n_sub):
        worst = jnp.maximum(worst, starts[j] - ends[j])
    unsafe = jnp.max(worst) > HG_SAFE_DECAY

    q_sub = qq * jnp.exp(a_loc)
    k_sub = kk * jnp.exp(-a_loc)
    k_end = kk * jnp.exp(e_loc)
    one_row = jnp.ones((1, width), F32)
    q_dec = q_sub * _rows_from_blocks([jnp.exp(s) for s in starts], width)
    k_dec = k_end * _rows_from_blocks([jnp.exp(ends[-1] - e) for e in ends], width)
    q_top = q_sub * _rows_from_blocks(
        [zero_row, zero_row, one_row, jnp.exp(ends[2] - ends[1])], width)
    k_top = k_end * _rows_from_blocks(
        [jnp.exp(ends[1] - ends[0]), one_row, zero_row, zero_row], width)

    row = lax.broadcasted_iota(I32, (HG_CHUNK, 1), 0)
    sub_of_row = row // HG_SUB
    t_idx = lax.broadcasted_iota(I32, (HG_CHUNK, HG_CHUNK), 0)
    s_idx = lax.broadcasted_iota(I32, (HG_CHUNK, HG_CHUNK), 1)
    m_sub = ((t_idx // HG_SUB) == (s_idx // HG_SUB)) & (t_idx >= s_idx)
    m_sub2 = jnp.concatenate([m_sub, m_sub], axis=0)
    lane = lax.broadcasted_iota(I32, (1, LANES), 1)
    lo = lane < HG_DK
    bd = (lax.broadcasted_iota(I32, (LANES, LANES), 0) // HG_DV) == \
         (lax.broadcasted_iota(I32, (LANES, LANES), 1) // HG_DK)
    dec_row = jnp.exp(ends[-1])

    def heads_on_rows(x):
        return jnp.concatenate([jnp.where(lo, x, 0.0), jnp.where(lo, 0.0, x)], axis=0)

    o_inter = []
    for p in range(width // LANES):
        sl = slice(p * LANES, (p + 1) * LANES)
        qs, ks, ke = q_sub[:, sl], k_sub[:, sl], k_end[:, sl]
        p_sub = _dot_nt(heads_on_rows(qs).astype(BF16), ks.astype(BF16))
        q_x = jnp.concatenate([jnp.where(sub_of_row == 1, qs, 0.0),
                               jnp.where(sub_of_row == 3, qs, 0.0), q_top[:, sl]], axis=1)
        k_x = jnp.concatenate([jnp.where(sub_of_row == 0, ke, 0.0),
                               jnp.where(sub_of_row == 2, ke, 0.0), k_top[:, sl]], axis=1)
        q_x2 = jnp.concatenate([jnp.where(jnp.tile(lo, (1, 3)), q_x, 0.0),
                                jnp.where(jnp.tile(lo, (1, 3)), 0.0, q_x)], axis=0)
        p_x = _dot_nt(q_x2.astype(BF16), k_x.astype(BF16))
        p_all = jnp.where(m_sub2, p_sub, 0.0) + p_x
        p_cat = jnp.concatenate([p_all[:HG_CHUNK], p_all[HG_CHUNK:]], axis=1)
        v_p = vv[:, sl]
        oi_scr[:, sl] = _dot(p_cat.astype(BF16), heads_on_rows(v_p).astype(BF16))
        st = st_ref[p]
        o_inter.append(_dot_nt(q_dec[:, sl].astype(BF16), st.astype(BF16)))
        upd = _dot_tn(v_p.astype(BF16), k_dec[:, sl].astype(BF16))
        st_ref[p] = st * dec_row[:, sl] + jnp.where(bd, upd, 0.0)
    o_inter = jnp.concatenate(o_inter, axis=1)

    @pl.when(unsafe)
    def _():
        a_scr[...] = a_cum
        q_scr[...] = qq
        k_scr[...] = kk

        def body(t, carry):
            d = a_scr[pl.ds(t, 1), :] - a_scr[...]
            w = jnp.exp(jnp.where(row <= t, d, -jnp.inf))
            prod = (q_scr[pl.ds(t, 1), :] * k_scr[...]) * w
            e = _dot(prod.astype(BF16), bo_ref[...])
            oi_scr[pl.ds(t, 1), :] = jnp.sum(e * hi_ref[...], axis=0, keepdims=True)
            return carry

        lax.fori_loop(0, HG_CHUNK, body, 0)

    o = oi_scr[...] + o_inter
    sq = o * o
    sq_hi = sq.astype(BF16)
    sq_lo = (sq - sq_hi.astype(F32)).astype(BF16)
    ms = (_dot(sq_hi, bo_ref[...]) + _dot(sq_lo, bo_ref[...])) * (1.0 / HG_DV)
    gate = hg_ref[...]
    o = o * lax.rsqrt(ms + RMS_EPS) * nw_ref[...] * (gate * jax.nn.sigmoid(gate))
    o_ref[...] = o.astype(o_ref.dtype)


def _hgrn(hq, hf, hi, hg, loglb, log1mlb, nw_row, head_ones, batch, seq, layer, cast=()):
    nc = seq // HG_CHUNK
    steps = batch * nc
    blk = lambda b, c: (b * nc + c, 0)
    row = lambda b, c: (0, 0)
    w = HG_WIDTH
    cast_in = [pl.BlockSpec((1, a.shape[1] // steps, a.shape[2]), lambda b, c: (layer, b * nc + c, 0))
               for a in cast]
    cast_out = [pl.BlockSpec((a.shape[1] // steps, a.shape[2]), blk) for a in cast]
    return pl.pallas_call(
        _hgrn_body,
        grid=(batch, nc),
        in_specs=[pl.BlockSpec((HG_CHUNK, w), blk)] * 4 + [pl.BlockSpec((1, w), row)] * 3
                 + [pl.BlockSpec((w, w), row)] + cast_in,
        out_specs=[pl.BlockSpec((HG_CHUNK, w), blk)] + cast_out,
        out_shape=[jax.ShapeDtypeStruct((batch * seq, w), BF16)]
                  + [jax.ShapeDtypeStruct(a.shape[1:], BF16) for a in cast],
        scratch_shapes=[pltpu.VMEM((w // LANES, LANES, LANES), F32)]
                       + [pltpu.VMEM((HG_CHUNK, w), F32)] * 4,
        compiler_params=_params(("arbitrary", "arbitrary"), 32),
        name="hgrn",
    )(hq, hf, hi, hg, loglb, log1mlb, nw_row, head_ones, *cast)


def _mem_attn_body(q_ref, k_ref, v_ref, o_ref):
    q_all = q_ref[...] * ATTN_SCALE
    outs = []
    for h in range(MEM_HEADS):
        sl = slice(h * HEAD_DIM, (h + 1) * HEAD_DIM)
        scores = _dot_nt(k_ref[0, :, sl], q_all[:, sl])
        e = jnp.exp(scores - jnp.max(scores, axis=0, keepdims=True))
        p = e * (1.0 / jnp.sum(e, axis=0, keepdims=True))
        outs.append(_dot_tn(p.astype(BF16), v_ref[0, :, sl]))
    o_ref[...] = jnp.concatenate(outs, axis=-1).astype(o_ref.dtype)


def _mem_attn(mq, mk, mv, layer, batch, seq, mem_len):
    tq = min(1024, seq)
    nq = seq // tq
    w = MEM_HEADS * HEAD_DIM
    return pl.pallas_call(
        _mem_attn_body,
        grid=(batch, nq),
        in_specs=[pl.BlockSpec((tq, w), lambda b, i: (b * nq + i, 0)),
                  pl.BlockSpec((1, mem_len, w), lambda b, i: (layer, b, 0)),
                  pl.BlockSpec((1, mem_len, w), lambda b, i: (layer, b, 0))],
        out_specs=pl.BlockSpec((tq, w), lambda b, i: (b * nq + i, 0)),
        out_shape=jax.ShapeDtypeStruct((batch * seq, w), BF16),
        compiler_params=_params(("arbitrary", "arbitrary"), 32),
        name="mem_attn",
    )(mq, mk, mv)


def _layer_norm(h, g, b):
    mu = jnp.mean(h, axis=-1, keepdims=True)
    d = h - mu
    var = jnp.mean(d * d, axis=-1, keepdims=True)
    return d * lax.rsqrt(var + LN_EPS) * g + b


def _route_rows(sel):
    def row(a, r):
        return a[r:r + 1, :]

    best = None
    for g in range(N_GROUPS):
        a = [row(sel, 4 * g + i) for i in range(4)]
        gs = a[0] + a[1]
        for i, j in ((0, 2), (0, 3), (1, 2), (1, 3), (2, 3)):
            gs = jnp.maximum(gs, a[i] + a[j])
        if best is None:
            best, best_score = jnp.zeros_like(gs, dtype=I32), gs
        else:
            better = gs > best_score
            best = jnp.where(better, g, best)
            best_score = jnp.where(better, gs, best_score)

    def pick(arr, i):
        out = row(arr, i)
        for g in range(1, N_GROUPS):
            out = jnp.where(best == g, row(arr, 4 * g + i), out)
        return out

    a = [pick(sel, i) for i in range(4)]
    i1, m1 = jnp.zeros_like(best), a[0]
    for i in range(1, 4):
        gt = a[i] > m1
        i1 = jnp.where(gt, i, i1)
        m1 = jnp.where(gt, a[i], m1)
    i2 = jnp.full_like(best, -1)
    m2 = jnp.full_like(m1, -jnp.inf)
    for i in range(4):
        gt = (i1 != i) & (a[i] > m2)
        i2 = jnp.where(gt, i, i2)
        m2 = jnp.where(gt, a[i], m2)
    lo = jnp.minimum(i1, i2)
    hi = jnp.maximum(i1, i2)
    pair = jnp.where(lo == 0, hi - 1, jnp.where(lo == 1, hi + 1, 5))
    return best * N_PAIRS + pair


def _out_proj_body(x_ref, swa_ref, hg_ref, mem_ref, w_ref, g_ref, b_ref, wr_ref, rb_ref, tri_ref,
                   x1_ref, cls_ref, rank_ref, cnt_ref, carry_ref):
    i = pl.program_id(0)
    t = x_ref.shape[0]

    @pl.when(i == 0)
    def _():
        carry_ref[...] = jnp.zeros_like(carry_ref)

    acc = _dot(swa_ref[...], w_ref[0:512, :]) + _dot(hg_ref[...], w_ref[512:768, :]) \
        + _dot(mem_ref[...], w_ref[768:1024, :])
    x1 = _layer_norm(ALPHA * x_ref[...] + acc, g_ref[...], b_ref[...])
    x1_ref[...] = _to_tiles(x1)

    logits = _dot_nt(wr_ref[...], x1.astype(BF16))
    cls = _route_rows(jax.nn.sigmoid(logits) + rb_ref[...])
    onehot = (lax.broadcasted_iota(I32, (CLASS_ROWS, t), 0) == cls).astype(F32)
    prefix = _dot(onehot.astype(BF16), tri_ref[...])
    carry = carry_ref[...]
    rank = jnp.sum(onehot * (prefix - 1.0 + carry[:, 0:1]), axis=0, keepdims=True)
    new_carry = carry + prefix[:, t - 1:t]
    carry_ref[...] = new_carry
    cnt_ref[...] = new_carry
    cls_ref[0] = cls
    rank_ref[0] = rank.astype(I32)


def _out_proj(x2, swa_o, hg_o, mem_o, w_bf, g_row, b_row, wr_t, rb_col, tri):
    n = x2.shape[0]
    t = tri.shape[0]
    nt = n // t
    tok = lambda i: (i, 0)
    fixed = lambda i: (0, 0)
    per_tile = pl.BlockSpec((1, 1, t), lambda i: (i, 0, 0))
    return pl.pallas_call(
        _out_proj_body,
        grid=(nt,),
        in_specs=[pl.BlockSpec((t, D_MODEL), tok), pl.BlockSpec((t, 512), tok),
                  pl.BlockSpec((t, 256), tok), pl.BlockSpec((t, 256), tok),
                  pl.BlockSpec((MIX_WIDTH, D_MODEL), fixed),
                  pl.BlockSpec((1, D_MODEL), fixed), pl.BlockSpec((1, D_MODEL), fixed),
                  pl.BlockSpec((N_EXPERTS, D_MODEL), fixed), pl.BlockSpec((N_EXPERTS, 1), fixed),
                  pl.BlockSpec((t, t), fixed)],
        out_specs=[pl.BlockSpec((t * ROW_TILES, LANES), tok), per_tile, per_tile,
                   pl.BlockSpec((CLASS_ROWS, LANES), fixed)],
        out_shape=[jax.ShapeDtypeStruct((n * ROW_TILES, LANES), F32),
                   jax.ShapeDtypeStruct((nt, 1, t), I32), jax.ShapeDtypeStruct((nt, 1, t), I32),
                   jax.ShapeDtypeStruct((CLASS_ROWS, LANES), F32)],
        scratch_shapes=[pltpu.VMEM((CLASS_ROWS, LANES), F32)],
        compiler_params=_params(("arbitrary",), 48),
        name="out_proj_ln_route",
    )(x2, swa_o, hg_o, mem_o, w_bf, g_row, b_row, wr_t, rb_col, tri)


def _tile_copy(src_ref, dst_ref, sem, src_row, dst_row, tiles):
    n = tiles * SUBLANES
    return pltpu.make_async_copy(
        src_ref.at[pl.ds(pl.multiple_of(src_row * n, n), n), :],
        dst_ref.at[pl.ds(pl.multiple_of(dst_row * n, n), n), :], sem)


def _scatter_body(cstart_ref, cls_ref, rank_ref, x_hbm, dest_ref, xs_hbm,
                  buf, dest_vmem, dest_smem, load_sem, scat_sem, misc_sem, *, n_steps, tile):
    i = pl.program_id(0)
    rows = tile * SUBLANES
    slot = i % 3

    def load(step, into):
        return pltpu.make_async_copy(
            x_hbm.at[pl.ds(pl.multiple_of(step * rows, rows), rows), :], buf.at[into], load_sem.at[into])

    def scatter_done(of):
        return pltpu.make_async_copy(buf.at[of], xs_hbm.at[pl.ds(0, rows), :], scat_sem.at[of])

    @pl.when(i == 0)
    def _():
        load(0, 0).start()

    @pl.when(i + 1 < n_steps)
    def _():
        load(i + 1, (i + 1) % 3).start()

    cls = cls_ref[0]
    dest = rank_ref[0]
    for c in range(N_CLASSES):
        dest = dest + jnp.where(cls == c, cstart_ref[c], 0)
    dest_ref[0] = dest
    dest_vmem[...] = dest
    to_smem = pltpu.make_async_copy(dest_vmem, dest_smem, misc_sem)
    to_smem.start()
    to_smem.wait()

    load(i, slot).wait()

    def issue(group, carry):
        for k in range(DMA_GROUP):
            j = group * DMA_GROUP + k
            _tile_copy(buf.at[slot], xs_hbm, scat_sem.at[slot], j, dest_smem[0, j], 1).start(priority=k % 2)
        return carry

    lax.fori_loop(0, tile // DMA_GROUP, issue, 0)

    @pl.when(i >= 1)
    def _():
        scatter_done((i + 2) % 3).wait()

    @pl.when(i == n_steps - 1)
    def _():
        scatter_done(slot).wait()


def _scatter_rows(x1_z, cls, rank, cstart):
    nt, _, tile = cls.shape
    n = nt * tile
    per_tile = pl.BlockSpec((1, 1, tile), lambda i, cs: (i, 0, 0))
    return pl.pallas_call(
        functools.partial(_scatter_body, n_steps=nt, tile=tile),
        grid_spec=pltpu.PrefetchScalarGridSpec(
            num_scalar_prefetch=1,
            grid=(nt,),
            in_specs=[per_tile, per_tile, pl.BlockSpec(memory_space=pl.ANY)],
            out_specs=[per_tile, pl.BlockSpec(memory_space=pl.ANY)],
            scratch_shapes=[pltpu.VMEM((3, tile * SUBLANES, LANES), F32),
                            pltpu.VMEM((1, tile), I32), pltpu.SMEM((1, tile), I32),
                            pltpu.SemaphoreType.DMA((3,)), pltpu.SemaphoreType.DMA((3,)),
                            pltpu.SemaphoreType.DMA(())]),
        out_shape=[jax.ShapeDtypeStruct((nt, 1, tile), I32),
                   jax.ShapeDtypeStruct((n * SUBLANES, LANES), F32)],
        compiler_params=_params(("arbitrary",), 32),
        name="scatter_rows",
    )(cstart, cls, rank, x1_z)


def _expert_body(blk_ref, elo_ref, ehi_ref, start_ref, end_ref, xs_ref, wr_ref,
                 g_lo, u_lo, d_lo, g_hi, u_hi, d_hi, ys_ref):
    i = pl.program_id(0)
    start = start_ref[i]
    end = end_ref[i]

    @pl.when(end > start)
    def _():
        x = _from_tiles(xs_ref[...], MOE_BLK).astype(BF16)
        scores = jax.nn.sigmoid(_dot(x, wr_ref[...]))
        lane = lax.broadcasted_iota(I32, (1, LANES), 1)
        s_lo = jnp.sum(jnp.where(lane == elo_ref[i], scores, 0.0), axis=-1, keepdims=True)
        s_hi = jnp.sum(jnp.where(lane == ehi_ref[i], scores, 0.0), axis=-1, keepdims=True)
        denom = s_lo + s_hi

        def ffn(gw, uw, dw, w_col):
            g = _dot(x, gw[0, 0])
            u = _dot(x, uw[0, 0])
            return _dot((((g * jax.nn.sigmoid(g)) * u) * w_col).astype(BF16), dw[0, 0])

        y = _to_tiles(ffn(g_lo, u_lo, d_lo, s_lo / denom) + ffn(g_hi, u_hi, d_hi, s_hi / denom))
        first = start % MOE_BLK == 0

        @pl.when(first)
        def _():
            ys_ref[...] = y

        @pl.when(jnp.logical_not(first))
        def _():
            slot = blk_ref[i] * MOE_BLK \
                + lax.broadcasted_iota(I32, (MOE_BLK * ROW_TILES, 1), 0) // ROW_TILES
            ys_ref[...] = jnp.where((slot >= start) & (slot < end), y, ys_ref[...])


def _experts(xs_z, items, wr_pad, wg, wu, wd, layer):
    blk, elo, ehi, start, end = items
    n_items = blk.shape[0]
    n_slots = xs_z.shape[0] // SUBLANES

    def data_map(i, blk, elo, ehi, start, end):
        return (blk[i], 0)

    def w_lo_map(i, blk, elo, ehi, start, end):
        return (layer, elo[i], 0, 0)

    def w_hi_map(i, blk, elo, ehi, start, end):
        return (layer, ehi[i], 0, 0)

    up_spec = lambda m: pl.BlockSpec((1, 1, D_MODEL, D_EXPERT), m)
    down_spec = lambda m: pl.BlockSpec((1, 1, D_EXPERT, D_MODEL), m)
    return pl.pallas_call(
        _expert_body,
        grid_spec=pltpu.PrefetchScalarGridSpec(
            num_scalar_prefetch=5,
            grid=(n_items,),
            in_specs=[pl.BlockSpec((MOE_BLK * SUBLANES, LANES), data_map),
                      pl.BlockSpec((D_MODEL, LANES), lambda i, *_: (0, 0)),
                      up_spec(w_lo_map), up_spec(w_lo_map), down_spec(w_lo_map),
                      up_spec(w_hi_map), up_spec(w_hi_map), down_spec(w_hi_map)],
            out_specs=pl.BlockSpec((MOE_BLK * SUBLANES, LANES), data_map)),
        out_shape=jax.ShapeDtypeStruct((n_slots * SUBLANES, LANES), F32),
        compiler_params=_params(("arbitrary",), 48),
        name="experts",
    )(blk, elo, ehi, start, end, xs_z, wr_pad, wg, wu, wd, wg, wu, wd)


def _gathered_ln2(dest_ref, x1_ref, ys_hbm, g_ref, b_ref, buf, sem, n_steps, tile):
    i = pl.program_id(0)
    slot = i % 2

    def gather(step, into):
        def issue(group, carry):
            for k in range(DMA_GROUP):
                j = group * DMA_GROUP + k
                _tile_copy(ys_hbm, buf.at[into], sem.at[into], dest_ref[step * tile + j], j, 1).start(
                    priority=k % 2)
            return carry

        lax.fori_loop(0, tile // DMA_GROUP, issue, 0)

    @pl.when(i == 0)
    def _():
        gather(0, 0)

    @pl.when(i + 1 < n_steps)
    def _():
        gather(i + 1, (i + 1) % 2)

    pltpu.make_async_copy(ys_hbm.at[pl.ds(0, tile * SUBLANES), :], buf.at[slot], sem.at[slot]).wait()
    h = ALPHA * _from_tiles(x1_ref[...], tile) + _from_tiles(buf[slot], tile)
    return _layer_norm(h, g_ref[...], b_ref[...])


def _ln2_body(dest_ref, x1_ref, ys_hbm, g_ref, b_ref, o_ref, buf, sem, *, n_steps, tile):
    o_ref[...] = _gathered_ln2(dest_ref, x1_ref, ys_hbm, g_ref, b_ref, buf, sem, n_steps, tile)


def _ln2_in_proj_body(dest_ref, x1_ref, ys_hbm, g_ref, b_ref, w_ref, bias_ref, o_ref, *rest, n_steps, tile):
    *proj_refs, buf, sem = rest
    x = _gathered_ln2(dest_ref, x1_ref, ys_hbm, g_ref, b_ref, buf, sem, n_steps, tile)
    o_ref[...] = x
    acc = _dot(x.astype(BF16), w_ref[...]) + bias_ref[...]
    off = 0
    for ref, width in zip(proj_refs, SPLITS):
        ref[...] = acc[:, off:off + width].astype(ref.dtype)
        off += width


def _ln2(x1_z, ys_z, dest, g_row, b_row, tile, w_bf=None, bias_row=None):
    n = x1_z.shape[0] // SUBLANES
    nt = n // tile
    tok = lambda i, d: (i, 0)
    fixed = lambda i, d: (0, 0)
    in_specs = [pl.BlockSpec((tile * SUBLANES, LANES), tok), pl.BlockSpec(memory_space=pl.ANY),
                pl.BlockSpec((1, D_MODEL), fixed), pl.BlockSpec((1, D_MODEL), fixed)]
    out_specs = [pl.BlockSpec((tile, D_MODEL), tok)]
    out_shape = [jax.ShapeDtypeStruct((n, D_MODEL), F32)]
    operands = [dest, x1_z, ys_z, g_row, b_row]
    body = _ln2_body
    if w_bf is not None:
        in_specs += [pl.BlockSpec((D_MODEL, IN_WIDTH), fixed), pl.BlockSpec((1, IN_WIDTH), fixed)]
        out_specs += [pl.BlockSpec((tile, w), tok) for w in SPLITS]
        out_shape += [jax.ShapeDtypeStruct((n, w), dt) for w, dt in zip(SPLITS, PROJ_DTYPES)]
        operands += [w_bf, bias_row]
        body = _ln2_in_proj_body
    return pl.pallas_call(
        functools.partial(body, n_steps=nt, tile=tile),
        grid_spec=pltpu.PrefetchScalarGridSpec(
            num_scalar_prefetch=1,
            grid=(nt,),
            in_specs=in_specs,
            out_specs=out_specs,
            scratch_shapes=[pltpu.VMEM((2, tile * SUBLANES, LANES), F32),
                            pltpu.SemaphoreType.DMA((2,))]),
        out_shape=out_shape,
        compiler_params=_params(("arbitrary",), 56),
        name="gather_ln2" if w_bf is None else "gather_ln2_in_proj",
    )(*operands)


def _t5_bucket(dist):
    max_exact = N_BUCKETS // 2
    d = jnp.maximum(dist, 0)
    large = max_exact + (jnp.log(jnp.maximum(d, 1).astype(F32) / max_exact)
                         / math.log(MAX_DISTANCE / max_exact) * (N_BUCKETS - max_exact)).astype(I32)
    large = jnp.minimum(large, N_BUCKETS - 1)
    return jnp.where(d < max_exact, d, large)


def _banded_bias(rel_bias):
    i = jnp.arange(ATTN_BLOCK)[:, None]
    j = jnp.arange(2 * ATTN_BLOCK)[None, :]
    dist = i + ATTN_BLOCK - j
    bucket = _t5_bucket(dist)[None]
    table = rel_bias.astype(F32)
    bias = jnp.zeros((SWA_HEADS, ATTN_BLOCK, 2 * ATTN_BLOCK), F32)
    for b in range(N_BUCKETS):
        bias = jnp.where(bucket == b, table[b][:, None, None], bias)
    from_prev = (jnp.arange(ATTN_BLOCK)[None, :] > i)[None]
    prev, own = bias[:, :, :ATTN_BLOCK], bias[:, :, ATTN_BLOCK:]
    normal = jnp.where(from_prev, prev, own)
    first = jnp.where(from_prev, -jnp.inf, own)
    return jnp.swapaxes(jnp.stack([first, normal]), 2, 3)


def _work_items(counts, n_tok):
    n_blocks = n_tok // MOE_BLK
    cend = jnp.cumsum(counts)
    cstart = cend - counts
    blk_starts = jnp.arange(n_blocks, dtype=I32) * MOE_BLK
    cls_starts = jnp.where(counts > 0, cstart, n_tok)
    start = jnp.sort(jnp.concatenate([blk_starts, cls_starts]))
    end = jnp.concatenate([start[1:], jnp.full((1,), n_tok, I32)])
    blk = jnp.minimum(start, n_tok - 1) // MOE_BLK
    cls = jnp.minimum(jnp.sum((cend[None, :] <= start[:, None]).astype(I32), axis=1), N_CLASSES - 1)
    group, pair = cls // N_PAIRS, cls % N_PAIRS
    lo = (pair >= 3).astype(I32) + (pair >= 5).astype(I32)
    hi = jnp.where(pair < 3, pair + 1, jnp.where(pair < 5, pair - 1, 3))
    return cstart, (blk, group * EXPERTS_PER_GROUP + lo, group * EXPERTS_PER_GROUP + hi, start, end)


def kernel(x, mem, w_in, b_in, w_mem_kv, attn_sinks, rel_bias, hgrn_lb_logits, hgrn_norm, w_out,
           ln1_g, ln1_b, w_router, router_bias, w_gate, w_up, w_down, ln2_g, ln2_b):
    batch, seq, _ = x.shape
    mem_len = mem.shape[1]
    n_tok = batch * seq
    route_tile = min(512, n_tok)

    bias = _banded_bias(rel_bias)
    lb = jnp.cumsum(jax.nn.softmax(hgrn_lb_logits.astype(F32), axis=0), axis=0)
    lb = lb - lb[0:1]
    log_lb = jnp.log(lb)
    log_1m_lb = jnp.log1p(-lb)
    head_ones = (jnp.arange(HG_WIDTH)[:, None] // HG_DV == jnp.arange(HG_WIDTH)[None, :] // HG_DV).astype(BF16)
    tri = (jnp.arange(route_tile)[:, None] <= jnp.arange(route_tile)[None, :]).astype(BF16)
    wr_t = jnp.transpose(w_router).astype(BF16)
    wr_pad = jnp.pad(w_router.astype(BF16), ((0, 0), (0, LANES - N_EXPERTS)))
    rb_col = router_bias.astype(F32).reshape(N_EXPERTS, 1)
    w_in_bf = w_in.astype(BF16)
    w_out_bf = w_out.astype(BF16)
    hg_steps = batch * (seq // HG_CHUNK)
    down_rows = N_EXPERTS * D_EXPERT
    cast_in_hgrn = down_rows % hg_steps == 0 and (down_rows // hg_steps) % (2 * SUBLANES) == 0
    if cast_in_hgrn:
        w_slabs = (w_gate.reshape(DEPTH, N_EXPERTS * D_MODEL, D_EXPERT),
                   w_up.reshape(DEPTH, N_EXPERTS * D_MODEL, D_EXPERT),
                   w_down.reshape(DEPTH, N_EXPERTS * D_EXPERT, D_MODEL))
    else:
        w_slabs = ()
        expert_w = (w_gate.astype(BF16), w_up.astype(BF16), w_down.astype(BF16))

    mk, mv = _mem_kv(mem.reshape(batch * mem_len, D_MODEL), w_mem_kv)

    x2 = x.reshape(n_tok, D_MODEL)
    proj = _in_proj(x2, w_in_bf[0], b_in[0].reshape(1, IN_WIDTH))
    for l in range(DEPTH):
        sq, sk, sv, hq, hf, hi, hg, mq = proj
        swa_o = _swa(sq, sk, sv, attn_sinks[l].astype(F32), bias, batch, seq)
        hg_o, *cast_w = _hgrn(hq, hf, hi, hg, log_lb[l].reshape(1, HG_WIDTH), log_1m_lb[l].reshape(1, HG_WIDTH),
                              jnp.tile(hgrn_norm[l].astype(F32), HG_HEADS).reshape(1, HG_WIDTH), head_ones,
                              batch, seq, l, w_slabs)
        if cast_in_hgrn:
            up_shape, down_shape = (1, N_EXPERTS, D_MODEL, D_EXPERT), (1, N_EXPERTS, D_EXPERT, D_MODEL)
            layer_w = (cast_w[0].reshape(up_shape), cast_w[1].reshape(up_shape), cast_w[2].reshape(down_shape), 0)
        else:
            layer_w = (*expert_w, l)
        mem_o = _mem_attn(mq, mk, mv, l, batch, seq, mem_len)
        x1_z, cls, rank, counts = _out_proj(
            x2, swa_o, hg_o, mem_o, w_out_bf[l], ln1_g[l].reshape(1, D_MODEL), ln1_b[l].reshape(1, D_MODEL),
            wr_t, rb_col, tri)
        cstart, items = _work_items(counts[:N_CLASSES, 0].astype(I32), n_tok)
        dest, xs_z = _scatter_rows(x1_z, cls, rank, cstart)
        ys_z = _experts(xs_z, items, wr_pad, *layer_w)
        ln2_args = (x1_z, ys_z, dest.reshape(n_tok), ln2_g[l].reshape(1, D_MODEL), ln2_b[l].reshape(1, D_MODEL),
                    route_tile)
        if l + 1 < DEPTH:
            x2, *proj = _ln2(*ln2_args, w_in_bf[l + 1], b_in[l + 1].reshape(1, IN_WIDTH))
        else:
            (x2,) = _ln2(*ln2_args)
    return x2.reshape(batch, seq, D_MODEL)
```

```python
import functools
import math

import jax
import jax.numpy as jnp
from jax import lax
from jax.experimental import pallas as pl
from jax.experimental.pallas import tpu as pltpu

F32 = jnp.float32
BF16 = jnp.bfloat16
I32 = jnp.int32

D_MODEL = 1024
DEPTH = 4
HEAD_DIM = 64
SWA_HEADS = 8
SWA_KV_HEADS = 2
SWA_GROUP = SWA_HEADS // SWA_KV_HEADS
WINDOW = 128
ATTN_BLOCK = 128
HG_HEADS = 4
HG_DK = 64
HG_DV = 64
HG_WIDTH = HG_HEADS * HG_DK
MEM_HEADS = 4
N_BUCKETS = 32
MAX_DISTANCE = 128
N_EXPERTS = 16
N_GROUPS = 4
EXPERTS_PER_GROUP = 4
D_EXPERT = 512
LN_EPS = 1e-5
RMS_EPS = 1e-6
ALPHA = (2 * DEPTH) ** 0.25
SPLITS = (512, 128, 128, 256, 256, 256, 256, 256)
IN_WIDTH = sum(SPLITS)
PROJ_DTYPES = (BF16, BF16, BF16, F32, F32, F32, F32, BF16)
MIX_WIDTH = 1024
ATTN_SCALE = HEAD_DIM ** -0.5

SUBLANES = 8
LANES = 128
ROW_TILES = D_MODEL // LANES

SWA_STEP_BLOCKS = 4
HG_CHUNK = 128
HG_SUB = 32
HG_SAFE_DECAY = 80.0
N_PAIRS = 6
N_CLASSES = N_GROUPS * N_PAIRS
CLASS_ROWS = 32
MOE_BLK = 256
DMA_GROUP = 8


def _params(semantics, vmem_mib):
    return pltpu.CompilerParams(dimension_semantics=semantics, vmem_limit_bytes=vmem_mib * 1024 * 1024)


def _dot(a, b):
    return jnp.dot(a, b, preferred_element_type=F32)


def _dot_nt(a, b):
    return lax.dot_general(a, b, (((1,), (1,)), ((), ())), preferred_element_type=F32)


def _dot_tn(a, b):
    return lax.dot_general(a, b, (((0,), (0,)), ((), ())), preferred_element_type=F32)


def _from_tiles(z, n_rows):
    k = z.shape[0] // n_rows
    return z.reshape(n_rows, k, LANES).reshape(n_rows, k * LANES)


def _to_tiles(val):
    n_rows, width = val.shape
    return val.reshape(n_rows, width // LANES, LANES).reshape(n_rows * (width // LANES), LANES)


def _in_proj_body(x_ref, w_ref, b_ref, *out_refs):
    acc = _dot(x_ref[...].astype(BF16), w_ref[...]) + b_ref[...]
    off = 0
    for ref, width in zip(out_refs, SPLITS):
        ref[...] = acc[:, off:off + width].astype(ref.dtype)
        off += width


def _in_proj(x2, w_bf, b_row):
    n = x2.shape[0]
    tm = min(512, n)
    return pl.pallas_call(
        _in_proj_body,
        grid=(n // tm,),
        in_specs=[pl.BlockSpec((tm, D_MODEL), lambda i: (i, 0)),
                  pl.BlockSpec((D_MODEL, IN_WIDTH), lambda i: (0, 0)),
                  pl.BlockSpec((1, IN_WIDTH), lambda i: (0, 0))],
        out_specs=[pl.BlockSpec((tm, w), lambda i: (i, 0)) for w in SPLITS],
        out_shape=[jax.ShapeDtypeStruct((n, w), dt) for w, dt in zip(SPLITS, PROJ_DTYPES)],
        compiler_params=_params(("arbitrary",), 48),
        name="in_proj",
    )(x2, w_bf, b_row)


def _mem_kv_body(mem_ref, w_ref, k_ref, v_ref):
    acc = _dot(mem_ref[...].astype(BF16), w_ref[0].astype(BF16))
    half = MEM_HEADS * HEAD_DIM
    k_ref[0] = acc[:, :half].astype(BF16)
    v_ref[0] = acc[:, half:].astype(BF16)


def _mem_kv(mem2, w_mem_kv):
    rows = mem2.shape[0]
    tm = min(512, rows)
    half = MEM_HEADS * HEAD_DIM
    return pl.pallas_call(
        _mem_kv_body,
        grid=(DEPTH, rows // tm),
        in_specs=[pl.BlockSpec((tm, D_MODEL), lambda l, i: (i, 0)),
                  pl.BlockSpec((1, D_MODEL, 2 * half), lambda l, i: (l, 0, 0))],
        out_specs=[pl.BlockSpec((1, tm, half), lambda l, i: (l, i, 0)),
                   pl.BlockSpec((1, tm, half), lambda l, i: (l, i, 0))],
        out_shape=[jax.ShapeDtypeStruct((DEPTH, rows, half), BF16)] * 2,
        compiler_params=_params(("arbitrary", "arbitrary"), 32),
        name="mem_kv",
    )(mem2, w_mem_kv)


def _swa_body(sink_ref, q_ref, kp_ref, ko_ref, vp_ref, vo_ref, bias_ref, o_ref, *, blocks):
    n = pl.program_id(1)
    q_all = q_ref[...] * ATTN_SCALE
    k_all = jnp.concatenate([kp_ref[...], ko_ref[...]], axis=0)
    v_all = jnp.concatenate([vp_ref[...], vo_ref[...]], axis=0)
    from_prev = lax.broadcasted_iota(I32, (ATTN_BLOCK, ATTN_BLOCK), 0) > \
        lax.broadcasted_iota(I32, (ATTN_BLOCK, ATTN_BLOCK), 1)
    for j in range(blocks):
        lo = j * ATTN_BLOCK
        kk = k_all[lo:lo + 2 * ATTN_BLOCK]
        vv = v_all[lo:lo + 2 * ATTN_BLOCK]
        table = jnp.where(n > 0, 1, 0) if j == 0 else 1
        outs = []
        for h in range(SWA_HEADS):
            g = h // SWA_GROUP
            q_h = q_all[lo:lo + ATTN_BLOCK, h * HEAD_DIM:(h + 1) * HEAD_DIM]
            scores = _dot_nt(kk[:, g * HEAD_DIM:(g + 1) * HEAD_DIM], q_h)
            band = jnp.where(from_prev, scores[:ATTN_BLOCK], scores[ATTN_BLOCK:]) + bias_ref[table, h]
            sink = sink_ref[h]
            m = jnp.maximum(jnp.max(band, axis=0, keepdims=True), sink)
            e = jnp.exp(band - m)
            p = e * (1.0 / (jnp.sum(e, axis=0, keepdims=True) + jnp.exp(sink - m)))
            p = jnp.concatenate([jnp.where(from_prev, p, 0.0), jnp.where(from_prev, 0.0, p)], axis=0)
            outs.append(_dot_tn(p.astype(BF16), vv[:, g * HEAD_DIM:(g + 1) * HEAD_DIM]))
        o_ref[lo:lo + ATTN_BLOCK, :] = jnp.concatenate(outs, axis=-1).astype(o_ref.dtype)


def _swa(sq, sk, sv, sinks, bias2, batch, seq):
    nb = seq // ATTN_BLOCK
    blocks = math.gcd(SWA_STEP_BLOCKS, nb)
    ns = nb // blocks
    own = lambda b, n: (b * ns + n, 0)
    prev = lambda b, n: (b * nb + jnp.maximum(n * blocks - 1, 0), 0)
    kvw = SWA_KV_HEADS * HEAD_DIM
    qw = SWA_HEADS * HEAD_DIM
    rows = blocks * ATTN_BLOCK
    return pl.pallas_call(
        functools.partial(_swa_body, blocks=blocks),
        grid=(batch, ns),
        in_specs=[pl.BlockSpec(memory_space=pltpu.SMEM),
                  pl.BlockSpec((rows, qw), own),
                  pl.BlockSpec((ATTN_BLOCK, kvw), prev),
                  pl.BlockSpec((rows, kvw), own),
                  pl.BlockSpec((ATTN_BLOCK, kvw), prev),
                  pl.BlockSpec((rows, kvw), own),
                  pl.BlockSpec((2, SWA_HEADS, ATTN_BLOCK, ATTN_BLOCK), lambda b, n: (0, 0, 0, 0))],
        out_specs=pl.BlockSpec((rows, qw), own),
        out_shape=jax.ShapeDtypeStruct((batch * seq, qw), BF16),
        compiler_params=_params(("arbitrary", "arbitrary"), 32),
        name="swa",
    )(sinks, sq, sk, sk, sv, sv, bias2)


def _cumsum_rows(x):
    n = x.shape[0]
    row = lax.broadcasted_iota(I32, (n, 1), 0)
    s = 1
    while s < n:
        x = x + jnp.where(row >= s, pltpu.roll(x, s, 0), 0.0)
        s *= 2
    return x


def _rows_from_blocks(vals, width):
    return jnp.concatenate([jnp.broadcast_to(v, (HG_SUB, width)) for v in vals], axis=0)


def _hgrn_body(hq_ref, hf_ref, hi_ref, hg_ref, loglb_ref, log1mlb_ref, nw_ref, bo_ref, o_ref,
               st_ref, a_scr, q_scr, k_scr, oi_scr):
    c = pl.program_id(1)
    n_sub = HG_CHUNK // HG_SUB
    width = HG_WIDTH

    @pl.when(c == 0)
    def _():
        st_ref[...] = jnp.zeros_like(st_ref)

    z = hf_ref[...]
    log_sig = jnp.minimum(z, 0.0) - jnp.log(1.0 + jnp.exp(-jnp.abs(z)))
    t_a = loglb_ref[...]
    t_b = log1mlb_ref[...] + log_sig
    log_f = jnp.maximum(t_a, t_b) + jnp.log(1.0 + jnp.exp(-jnp.abs(t_a - t_b)))
    kk = jnp.exp(t_b - z)
    hq = hq_ref[...]
    qq = hq * jax.nn.sigmoid(hq) * (HG_DK ** -0.5)
    vv = hi_ref[...]

    a_cum = _cumsum_rows(log_f)
    ends = [a_cum[HG_SUB * j + HG_SUB - 1:HG_SUB * j + HG_SUB, :] for j in range(n_sub)]
    zero_row = jnp.zeros((1, width), F32)
    starts = [zero_row] + ends[:-1]
    a_loc = a_cum - _rows_from_blocks(starts, width)
    e_loc = _rows_from_blocks(ends, width) - a_cum
    worst = starts[0] - ends[0]
    for j in range(1, n_sub):
        worst = jnp.maximum(worst, starts[j] - ends[j])
    unsafe = jnp.max(worst) > HG_SAFE_DECAY

    q_sub = qq * jnp.exp(a_loc)
    k_sub = kk * jnp.exp(-a_loc)
    k_end = kk * jnp.exp(e_loc)
    one_row = jnp.ones((1, width), F32)
    q_dec = q_sub * _rows_from_blocks([jnp.exp(s) for s in starts], width)
    k_dec = k_end * _rows_from_blocks([jnp.exp(ends[-1] - e) for e in ends], width)
    q_top = q_sub * _rows_from_blocks(
        [zero_row, zero_row, one_row, jnp.exp(ends[2] - ends[1])], width)
    k_top = k_end * _rows_from_blocks(
        [jnp.exp(ends[1] - ends[0]), one_row, zero_row, zero_row], width)

    row = lax.broadcasted_iota(I32, (HG_CHUNK, 1), 0)
    sub_of_row = row // HG_SUB
    t_idx = lax.broadcasted_iota(I32, (HG_CHUNK, HG_CHUNK), 0)
    s_idx = lax.broadcasted_iota(I32, (HG_CHUNK, HG_CHUNK), 1)
    m_sub = ((t_idx // HG_SUB) == (s_idx // HG_SUB)) & (t_idx >= s_idx)
    m_sub2 = jnp.concatenate([m_sub, m_sub], axis=0)
    lane = lax.broadcasted_iota(I32, (1, LANES), 1)
    lo = lane < HG_DK
    bd = (lax.broadcasted_iota(I32, (LANES, LANES), 0) // HG_DV) == \
         (lax.broadcasted_iota(I32, (LANES, LANES), 1) // HG_DK)
    dec_row = jnp.exp(ends[-1])

    def heads_on_rows(x):
        return jnp.concatenate([jnp.where(lo, x, 0.0), jnp.where(lo, 0.0, x)], axis=0)

    o_inter = []
    for p in range(width // LANES):
        sl = slice(p * LANES, (p + 1) * LANES)
        qs, ks, ke = q_sub[:, sl], k_sub[:, sl], k_end[:, sl]
        p_sub = _dot_nt(heads_on_rows(qs).astype(BF16), ks.astype(BF16))
        q_x = jnp.concatenate([jnp.where(sub_of_row == 1, qs, 0.0),
                               jnp.where(sub_of_row == 3, qs, 0.0), q_top[:, sl]], axis=1)
        k_x = jnp.concatenate([jnp.where(sub_of_row == 0, ke, 0.0),
                               jnp.where(sub_of_row == 2, ke, 0.0), k_top[:, sl]], axis=1)
        q_x2 = jnp.concatenate([jnp.where(jnp.tile(lo, (1, 3)), q_x, 0.0),
                                jnp.where(jnp.tile(lo, (1, 3)), 0.0, q_x)], axis=0)
        p_x = _dot_nt(q_x2.astype(BF16), k_x.astype(BF16))
        p_all = jnp.where(m_sub2, p_sub, 0.0) + p_x
        p_cat = jnp.concatenate([p_all[:HG_CHUNK], p_all[HG_CHUNK:]], axis=1)
        v_p = vv[:, sl]
        oi_scr[:, sl] = _dot(p_cat.astype(BF16), heads_on_rows(v_p).astype(BF16))
        st = st_ref[p]
        o_inter.append(_dot_nt(q_dec[:, sl].astype(BF16), st.astype(BF16)))
        upd = _dot_tn(v_p.astype(BF16), k_dec[:, sl].astype(BF16))
        st_ref[p] = st * dec_row[:, sl] + jnp.where(bd, upd, 0.0)
    o_inter = jnp.concatenate(o_inter, axis=1)

    @pl.when(unsafe)
    def _():
        a_scr[...] = a_cum
        q_scr[...] = qq
        k_scr[...] = kk

        def body(t, carry):
            d = a_scr[pl.ds(t, 1), :] - a_scr[...]
            w = jnp.exp(jnp.where(row <= t, d, -jnp.inf))
            prod = (q_scr[pl.ds(t, 1), :] * k_scr[...]) * w
            e = _dot(prod.astype(BF16), bo_ref[...])
            oi_scr[pl.ds(t, 1), :] = jnp.sum(e * hi_ref[...], axis=0, keepdims=True)
            return carry

        lax.fori_loop(0, HG_CHUNK, body, 0)

    o = oi_scr[...] + o_inter
    sq = o * o
    sq_hi = sq.astype(BF16)
    sq_lo = (sq - sq_hi.astype(F32)).astype(BF16)
    ms = (_dot(sq_hi, bo_ref[...]) + _dot(sq_lo, bo_ref[...])) * (1.0 / HG_DV)
    gate = hg_ref[...]
    o = o * lax.rsqrt(ms + RMS_EPS) * nw_ref[...] * (gate * jax.nn.sigmoid(gate))
    o_ref[...] = o.astype(o_ref.dtype)


def _hgrn(hq, hf, hi, hg, loglb, log1mlb, nw_row, head_ones, batch, seq):
    nc = seq // HG_CHUNK
    blk = lambda b, c: (b * nc + c, 0)
    row = lambda b, c: (0, 0)
    w = HG_WIDTH
    return pl.pallas_call(
        _hgrn_body,
        grid=(batch, nc),
        in_specs=[pl.BlockSpec((HG_CHUNK, w), blk)] * 4 + [pl.BlockSpec((1, w), row)] * 3
                 + [pl.BlockSpec((w, w), row)],
        out_specs=pl.BlockSpec((HG_CHUNK, w), blk),
        out_shape=jax.ShapeDtypeStruct((batch * seq, w), BF16),
        scratch_shapes=[pltpu.VMEM((w // LANES, LANES, LANES), F32)]
                       + [pltpu.VMEM((HG_CHUNK, w), F32)] * 4,
        compiler_params=_params(("arbitrary", "arbitrary"), 32),
        name="hgrn",
    )(hq, hf, hi, hg, loglb, log1mlb, nw_row, head_ones)


def _mem_attn_body(q_ref, k_ref, v_ref, o_ref):
    q_all = q_ref[...] * ATTN_SCALE
    outs = []
    for h in range(MEM_HEADS):
        sl = slice(h * HEAD_DIM, (h + 1) * HEAD_DIM)
        scores = _dot_nt(k_ref[0, :, sl], q_all[:, sl])
        e = jnp.exp(scores - jnp.max(scores, axis=0, keepdims=True))
        p = e * (1.0 / jnp.sum(e, axis=0, keepdims=True))
        outs.append(_dot_tn(p.astype(BF16), v_ref[0, :, sl]))
    o_ref[...] = jnp.concatenate(outs, axis=-1).astype(o_ref.dtype)


def _mem_attn(mq, mk, mv, layer, batch, seq, mem_len):
    tq = min(1024, seq)
    nq = seq // tq
    w = MEM_HEADS * HEAD_DIM
    return pl.pallas_call(
        _mem_attn_body,
        grid=(batch, nq),
        in_specs=[pl.BlockSpec((tq, w), lambda b, i: (b * nq + i, 0)),
                  pl.BlockSpec((1, mem_len, w), lambda b, i: (layer, b, 0)),
                  pl.BlockSpec((1, mem_len, w), lambda b, i: (layer, b, 0))],
        out_specs=pl.BlockSpec((tq, w), lambda b, i: (b * nq + i, 0)),
        out_shape=jax.ShapeDtypeStruct((batch * seq, w), BF16),
        compiler_params=_params(("arbitrary", "arbitrary"), 32),
        name="mem_attn",
    )(mq, mk, mv)


def _layer_norm(h, g, b):
    mu = jnp.mean(h, axis=-1, keepdims=True)
    d = h - mu
    var = jnp.mean(d * d, axis=-1, keepdims=True)
    return d * lax.rsqrt(var + LN_EPS) * g + b


def _route_rows(sel):
    def row(a, r):
        return a[r:r + 1, :]

    best = None
    for g in range(N_GROUPS):
        a = [row(sel, 4 * g + i) for i in range(4)]
        gs = a[0] + a[1]
        for i, j in ((0, 2), (0, 3), (1, 2), (1, 3), (2, 3)):
            gs = jnp.maximum(gs, a[i] + a[j])
        if best is None:
            best, best_score = jnp.zeros_like(gs, dtype=I32), gs
        else:
            better = gs > best_score
            best = jnp.where(better, g, best)
            best_score = jnp.where(better, gs, best_score)

    def pick(arr, i):
        out = row(arr, i)
        for g in range(1, N_GROUPS):
            out = jnp.where(best == g, row(arr, 4 * g + i), out)
        return out

    a = [pick(sel, i) for i in range(4)]
    i1, m1 = jnp.zeros_like(best), a[0]
    for i in range(1, 4):
        gt = a[i] > m1
        i1 = jnp.where(gt, i, i1)
        m1 = jnp.where(gt, a[i], m1)
    i2 = jnp.full_like(best, -1)
    m2 = jnp.full_like(m1, -jnp.inf)
    for i in range(4):
        gt = (i1 != i) & (a[i] > m2)
        i2 = jnp.where(gt, i, i2)
        m2 = jnp.where(gt, a[i], m2)
    lo = jnp.minimum(i1, i2)
    hi = jnp.maximum(i1, i2)
    pair = jnp.where(lo == 0, hi - 1, jnp.where(lo == 1, hi + 1, 5))
    return best * N_PAIRS + pair


def _out_proj_body(x_ref, swa_ref, hg_ref, mem_ref, w_ref, g_ref, b_ref, wr_ref, rb_ref, tri_ref, *rest):
    n_cast = (len(rest) - 5) // 2
    cast_in = rest[:n_cast]
    x1_ref, cls_ref, rank_ref, cnt_ref = rest[n_cast:n_cast + 4]
    cast_out, carry_ref = rest[n_cast + 4:-1], rest[-1]
    for src, dst in zip(cast_in, cast_out):
        dst[...] = src[0].astype(BF16)

    i = pl.program_id(0)
    t = x_ref.shape[0]

    @pl.when(i == 0)
    def _():
        carry_ref[...] = jnp.zeros_like(carry_ref)

    acc = _dot(swa_ref[...], w_ref[0:512, :]) + _dot(hg_ref[...], w_ref[512:768, :]) \
        + _dot(mem_ref[...], w_ref[768:1024, :])
    x1 = _layer_norm(ALPHA * x_ref[...] + acc, g_ref[...], b_ref[...])
    x1_ref[...] = _to_tiles(x1)

    logits = _dot_nt(wr_ref[...], x1.astype(BF16))
    cls = _route_rows(jax.nn.sigmoid(logits) + rb_ref[...])
    onehot = (lax.broadcasted_iota(I32, (CLASS_ROWS, t), 0) == cls).astype(F32)
    prefix = _dot(onehot.astype(BF16), tri_ref[...])
    carry = carry_ref[...]
    rank = jnp.sum(onehot * (prefix - 1.0 + carry[:, 0:1]), axis=0, keepdims=True)
    new_carry = carry + prefix[:, t - 1:t]
    carry_ref[...] = new_carry
    cnt_ref[...] = new_carry
    cls_ref[0] = cls
    rank_ref[0] = rank.astype(I32)


def _out_proj(x2, swa_o, hg_o, mem_o, w_bf, g_row, b_row, wr_t, rb_col, tri, layer, cast=()):
    n = x2.shape[0]
    t = tri.shape[0]
    nt = n // t
    tok = lambda i: (i, 0)
    fixed = lambda i: (0, 0)
    per_tile = pl.BlockSpec((1, 1, t), lambda i: (i, 0, 0))
    cast_in = [pl.BlockSpec((1, a.shape[1] // nt, a.shape[2]), lambda i: (layer, i, 0)) for a in cast]
    cast_out = [pl.BlockSpec((a.shape[1] // nt, a.shape[2]), tok) for a in cast]
    return pl.pallas_call(
        _out_proj_body,
        grid=(nt,),
        in_specs=[pl.BlockSpec((t, D_MODEL), tok), pl.BlockSpec((t, 512), tok),
                  pl.BlockSpec((t, 256), tok), pl.BlockSpec((t, 256), tok),
                  pl.BlockSpec((MIX_WIDTH, D_MODEL), fixed),
                  pl.BlockSpec((1, D_MODEL), fixed), pl.BlockSpec((1, D_MODEL), fixed),
                  pl.BlockSpec((N_EXPERTS, D_MODEL), fixed), pl.BlockSpec((N_EXPERTS, 1), fixed),
                  pl.BlockSpec((t, t), fixed)] + cast_in,
        out_specs=[pl.BlockSpec((t * ROW_TILES, LANES), tok), per_tile, per_tile,
                   pl.BlockSpec((CLASS_ROWS, LANES), fixed)] + cast_out,
        out_shape=[jax.ShapeDtypeStruct((n * ROW_TILES, LANES), F32),
                   jax.ShapeDtypeStruct((nt, 1, t), I32), jax.ShapeDtypeStruct((nt, 1, t), I32),
                   jax.ShapeDtypeStruct((CLASS_ROWS, LANES), F32)]
                  + [jax.ShapeDtypeStruct(a.shape[1:], BF16) for a in cast],
        scratch_shapes=[pltpu.VMEM((CLASS_ROWS, LANES), F32)],
        compiler_params=_params(("arbitrary",), 56),
        name="out_proj_ln_route",
    )(x2, swa_o, hg_o, mem_o, w_bf, g_row, b_row, wr_t, rb_col, tri, *cast)


def _tile_copy(src_ref, dst_ref, sem, src_row, dst_row, tiles):
    n = tiles * SUBLANES
    return pltpu.make_async_copy(
        src_ref.at[pl.ds(pl.multiple_of(src_row * n, n), n), :],
        dst_ref.at[pl.ds(pl.multiple_of(dst_row * n, n), n), :], sem)


def _scatter_body(cstart_ref, cls_ref, rank_ref, x_hbm, dest_ref, xs_hbm,
                  buf, dest_vmem, dest_smem, load_sem, scat_sem, misc_sem, *, n_steps, tile):
    i = pl.program_id(0)
    rows = tile * SUBLANES
    slot = i % 3

    def load(step, into):
        return pltpu.make_async_copy(
            x_hbm.at[pl.ds(pl.multiple_of(step * rows, rows), rows), :], buf.at[into], load_sem.at[into])

    def scatter_done(of):
        return pltpu.make_async_copy(buf.at[of], xs_hbm.at[pl.ds(0, rows), :], scat_sem.at[of])

    @pl.when(i == 0)
    def _():
        load(0, 0).start()

    @pl.when(i + 1 < n_steps)
    def _():
        load(i + 1, (i + 1) % 3).start()

    cls = cls_ref[0]
    dest = rank_ref[0]
    for c in range(N_CLASSES):
        dest = dest + jnp.where(cls == c, cstart_ref[c], 0)
    dest_ref[0] = dest
    dest_vmem[...] = dest
    to_smem = pltpu.make_async_copy(dest_vmem, dest_smem, misc_sem)
    to_smem.start()
    to_smem.wait()

    load(i, slot).wait()

    def issue(group, carry):
        for k in range(DMA_GROUP):
            j = group * DMA_GROUP + k
            _tile_copy(buf.at[slot], xs_hbm, scat_sem.at[slot], j, dest_smem[0, j], 1).start(priority=k % 2)
        return carry

    lax.fori_loop(0, tile // DMA_GROUP, issue, 0)

    @pl.when(i >= 1)
    def _():
        scatter_done((i + 2) % 3).wait()

    @pl.when(i == n_steps - 1)
    def _():
        scatter_done(slot).wait()


def _scatter_rows(x1_z, cls, rank, cstart):
    nt, _, tile = cls.shape
    n = nt * tile
    per_tile = pl.BlockSpec((1, 1, tile), lambda i, cs: (i, 0, 0))
    return pl.pallas_call(
        functools.partial(_scatter_body, n_steps=nt, tile=tile),
        grid_spec=pltpu.PrefetchScalarGridSpec(
            num_scalar_prefetch=1,
            grid=(nt,),
            in_specs=[per_tile, per_tile, pl.BlockSpec(memory_space=pl.ANY)],
            out_specs=[per_tile, pl.BlockSpec(memory_space=pl.ANY)],
            scratch_shapes=[pltpu.VMEM((3, tile * SUBLANES, LANES), F32),
                            pltpu.VMEM((1, tile), I32), pltpu.SMEM((1, tile), I32),
                            pltpu.SemaphoreType.DMA((3,)), pltpu.SemaphoreType.DMA((3,)),
                            pltpu.SemaphoreType.DMA(())]),
        out_shape=[jax.ShapeDtypeStruct((nt, 1, tile), I32),
                   jax.ShapeDtypeStruct((n * SUBLANES, LANES), F32)],
        compiler_params=_params(("arbitrary",), 32),
        name="scatter_rows",
    )(cstart, cls, rank, x1_z)


def _expert_body(blk_ref, elo_ref, ehi_ref, start_ref, end_ref, xs_ref, wr_ref,
                 g_lo, u_lo, d_lo, g_hi, u_hi, d_hi, ys_ref):
    i = pl.program_id(0)
    start = start_ref[i]
    end = end_ref[i]

    @pl.when(end > start)
    def _():
        x = _from_tiles(xs_ref[...], MOE_BLK).astype(BF16)
        scores = jax.nn.sigmoid(_dot(x, wr_ref[...]))
        lane = lax.broadcasted_iota(I32, (1, LANES), 1)
        s_lo = jnp.sum(jnp.where(lane == elo_ref[i], scores, 0.0), axis=-1, keepdims=True)
        s_hi = jnp.sum(jnp.where(lane == ehi_ref[i], scores, 0.0), axis=-1, keepdims=True)
        denom = s_lo + s_hi

        def ffn(gw, uw, dw, w_col):
            g = _dot(x, gw[0, 0])
            u = _dot(x, uw[0, 0])
            return _dot((((g * jax.nn.sigmoid(g)) * u) * w_col).astype(BF16), dw[0, 0])

        y = _to_tiles(ffn(g_lo, u_lo, d_lo, s_lo / denom) + ffn(g_hi, u_hi, d_hi, s_hi / denom))
        first = start % MOE_BLK == 0

        @pl.when(first)
        def _():
            ys_ref[...] = y

        @pl.when(jnp.logical_not(first))
        def _():
            slot = blk_ref[i] * MOE_BLK \
                + lax.broadcasted_iota(I32, (MOE_BLK * ROW_TILES, 1), 0) // ROW_TILES
            ys_ref[...] = jnp.where((slot >= start) & (slot < end), y, ys_ref[...])


def _experts(xs_z, items, wr_pad, wg, wu, wd, layer):
    blk, elo, ehi, start, end = items
    n_items = blk.shape[0]
    n_slots = xs_z.shape[0] // SUBLANES

    def data_map(i, blk, elo, ehi, start, end):
        return (blk[i], 0)

    def w_lo_map(i, blk, elo, ehi, start, end):
        return (layer, elo[i], 0, 0)

    def w_hi_map(i, blk, elo, ehi, start, end):
        return (layer, ehi[i], 0, 0)

    up_spec = lambda m: pl.BlockSpec((1, 1, D_MODEL, D_EXPERT), m)
    down_spec = lambda m: pl.BlockSpec((1, 1, D_EXPERT, D_MODEL), m)
    return pl.pallas_call(
        _expert_body,
        grid_spec=pltpu.PrefetchScalarGridSpec(
            num_scalar_prefetch=5,
            grid=(n_items,),
            in_specs=[pl.BlockSpec((MOE_BLK * SUBLANES, LANES), data_map),
                      pl.BlockSpec((D_MODEL, LANES), lambda i, *_: (0, 0)),
                      up_spec(w_lo_map), up_spec(w_lo_map), down_spec(w_lo_map),
                      up_spec(w_hi_map), up_spec(w_hi_map), down_spec(w_hi_map)],
            out_specs=pl.BlockSpec((MOE_BLK * SUBLANES, LANES), data_map)),
        out_shape=jax.ShapeDtypeStruct((n_slots * SUBLANES, LANES), F32),
        compiler_params=_params(("arbitrary",), 48),
        name="experts",
    )(blk, elo, ehi, start, end, xs_z, wr_pad, wg, wu, wd, wg, wu, wd)


def _gathered_ln2(dest_ref, x1_ref, ys_hbm, g_ref, b_ref, buf, sem, n_steps, tile):
    i = pl.program_id(0)
    slot = i % 2

    def gather(step, into):
        def issue(group, carry):
            for k in range(DMA_GROUP):
                j = group * DMA_GROUP + k
                _tile_copy(ys_hbm, buf.at[into], sem.at[into], dest_ref[step * tile + j], j, 1).start(
                    priority=k % 2)
            return carry

        lax.fori_loop(0, tile // DMA_GROUP, issue, 0)

    @pl.when(i == 0)
    def _():
        gather(0, 0)

    @pl.when(i + 1 < n_steps)
    def _():
        gather(i + 1, (i + 1) % 2)

    pltpu.make_async_copy(ys_hbm.at[pl.ds(0, tile * SUBLANES), :], buf.at[slot], sem.at[slot]).wait()
    h = ALPHA * _from_tiles(x1_ref[...], tile) + _from_tiles(buf[slot], tile)
    return _layer_norm(h, g_ref[...], b_ref[...])


def _ln2_body(dest_ref, x1_ref, ys_hbm, g_ref, b_ref, o_ref, buf, sem, *, n_steps, tile):
    o_ref[...] = _gathered_ln2(dest_ref, x1_ref, ys_hbm, g_ref, b_ref, buf, sem, n_steps, tile)


def _ln2_in_proj_body(dest_ref, x1_ref, ys_hbm, g_ref, b_ref, w_ref, bias_ref, o_ref, *rest, n_steps, tile):
    *proj_refs, buf, sem = rest
    x = _gathered_ln2(dest_ref, x1_ref, ys_hbm, g_ref, b_ref, buf, sem, n_steps, tile)
    o_ref[...] = x
    acc = _dot(x.astype(BF16), w_ref[...]) + bias_ref[...]
    off = 0
    for ref, width in zip(proj_refs, SPLITS):
        ref[...] = acc[:, off:off + width].astype(ref.dtype)
        off += width


def _ln2(x1_z, ys_z, dest, g_row, b_row, tile, w_bf=None, bias_row=None):
    n = x1_z.shape[0] // SUBLANES
    nt = n // tile
    tok = lambda i, d: (i, 0)
    fixed = lambda i, d: (0, 0)
    in_specs = [pl.BlockSpec((tile * SUBLANES, LANES), tok), pl.BlockSpec(memory_space=pl.ANY),
                pl.BlockSpec((1, D_MODEL), fixed), pl.BlockSpec((1, D_MODEL), fixed)]
    out_specs = [pl.BlockSpec((tile, D_MODEL), tok)]
    out_shape = [jax.ShapeDtypeStruct((n, D_MODEL), F32)]
    operands = [dest, x1_z, ys_z, g_row, b_row]
    body = _ln2_body
    if w_bf is not None:
        in_specs += [pl.BlockSpec((D_MODEL, IN_WIDTH), fixed), pl.BlockSpec((1, IN_WIDTH), fixed)]
        out_specs += [pl.BlockSpec((tile, w), tok) for w in SPLITS]
        out_shape += [jax.ShapeDtypeStruct((n, w), dt) for w, dt in zip(SPLITS, PROJ_DTYPES)]
        operands += [w_bf, bias_row]
        body = _ln2_in_proj_body
    return pl.pallas_call(
        functools.partial(body, n_steps=nt, tile=tile),
        grid_spec=pltpu.PrefetchScalarGridSpec(
            num_scalar_prefetch=1,
            grid=(nt,),
            in_specs=in_specs,
            out_specs=out_specs,
            scratch_shapes=[pltpu.VMEM((2, tile * SUBLANES, LANES), F32),
                            pltpu.SemaphoreType.DMA((2,))]),
        out_shape=out_shape,
        compiler_params=_params(("arbitrary",), 56),
        name="gather_ln2" if w_bf is None else "gather_ln2_in_proj",
    )(*operands)


def _t5_bucket(dist):
    max_exact = N_BUCKETS // 2
    d = jnp.maximum(dist, 0)
    large = max_exact + (jnp.log(jnp.maximum(d, 1).astype(F32) / max_exact)
                         / math.log(MAX_DISTANCE / max_exact) * (N_BUCKETS - max_exact)).astype(I32)
    large = jnp.minimum(large, N_BUCKETS - 1)
    return jnp.where(d < max_exact, d, large)


def _banded_bias(rel_bias):
    i = jnp.arange(ATTN_BLOCK)[:, None]
    j = jnp.arange(2 * ATTN_BLOCK)[None, :]
    dist = i + ATTN_BLOCK - j
    bucket = _t5_bucket(dist)[None]
    table = rel_bias.astype(F32)
    bias = jnp.zeros((SWA_HEADS, ATTN_BLOCK, 2 * ATTN_BLOCK), F32)
    for b in range(N_BUCKETS):
        bias = jnp.where(bucket == b, table[b][:, None, None], bias)
    from_prev = (jnp.arange(ATTN_BLOCK)[None, :] > i)[None]
    prev, own = bias[:, :, :ATTN_BLOCK], bias[:, :, ATTN_BLOCK:]
    normal = jnp.where(from_prev, prev, own)
    first = jnp.where(from_prev, -jnp.inf, own)
    return jnp.swapaxes(jnp.stack([first, normal]), 2, 3)


def _work_items(counts, n_tok):
    n_blocks = n_tok // MOE_BLK
    cend = jnp.cumsum(counts)
    cstart = cend - counts
    blk_starts = jnp.arange(n_blocks, dtype=I32) * MOE_BLK
    cls_starts = jnp.where(counts > 0, cstart, n_tok)
    start = jnp.sort(jnp.concatenate([blk_starts, cls_starts]))
    end = jnp.concatenate([start[1:], jnp.full((1,), n_tok, I32)])
    blk = jnp.minimum(start, n_tok - 1) // MOE_BLK
    cls = jnp.minimum(jnp.sum((cend[None, :] <= start[:, None]).astype(I32), axis=1), N_CLASSES - 1)
    group, pair = cls // N_PAIRS, cls % N_PAIRS
    lo = (pair >= 3).astype(I32) + (pair >= 5).astype(I32)
    hi = jnp.where(pair < 3, pair + 1, jnp.where(pair < 5, pair - 1, 3))
    return cstart, (blk, group * EXPERTS_PER_GROUP + lo, group * EXPERTS_PER_GROUP + hi, start, end)


def kernel(x, mem, w_in, b_in, w_mem_kv, attn_sinks, rel_bias, hgrn_lb_logits, hgrn_norm, w_out,
           ln1_g, ln1_b, w_router, router_bias, w_gate, w_up, w_down, ln2_g, ln2_b):
    batch, seq, _ = x.shape
    mem_len = mem.shape[1]
    n_tok = batch * seq
    route_tile = min(512, n_tok)

    bias = _banded_bias(rel_bias)
    lb = jnp.cumsum(jax.nn.softmax(hgrn_lb_logits.astype(F32), axis=0), axis=0)
    lb = lb - lb[0:1]
    log_lb = jnp.log(lb)
    log_1m_lb = jnp.log1p(-lb)
    head_ones = (jnp.arange(HG_WIDTH)[:, None] // HG_DV == jnp.arange(HG_WIDTH)[None, :] // HG_DV).astype(BF16)
    tri = (jnp.arange(route_tile)[:, None] <= jnp.arange(route_tile)[None, :]).astype(BF16)
    wr_t = jnp.transpose(w_router).astype(BF16)
    wr_pad = jnp.pad(w_router.astype(BF16), ((0, 0), (0, LANES - N_EXPERTS)))
    rb_col = router_bias.astype(F32).reshape(N_EXPERTS, 1)
    w_in_bf = w_in.astype(BF16)
    w_out_bf = w_out.astype(BF16)
    route_steps = n_tok // route_tile
    down_rows = N_EXPERTS * D_EXPERT
    cast_in_out_proj = down_rows % route_steps == 0 and (down_rows // route_steps) % (2 * SUBLANES) == 0
    if cast_in_out_proj:
        w_slabs = (w_gate.reshape(DEPTH, N_EXPERTS * D_MODEL, D_EXPERT),
                   w_up.reshape(DEPTH, N_EXPERTS * D_MODEL, D_EXPERT),
                   w_down.reshape(DEPTH, N_EXPERTS * D_EXPERT, D_MODEL))
    else:
        w_slabs = ()
        expert_w = (w_gate.astype(BF16), w_up.astype(BF16), w_down.astype(BF16))

    mk, mv = _mem_kv(mem.reshape(batch * mem_len, D_MODEL), w_mem_kv)

    x2 = x.reshape(n_tok, D_MODEL)
    proj = _in_proj(x2, w_in_bf[0], b_in[0].reshape(1, IN_WIDTH))
    for l in range(DEPTH):
        sq, sk, sv, hq, hf, hi, hg, mq = proj
        swa_o = _swa(sq, sk, sv, attn_sinks[l].astype(F32), bias, batch, seq)
        hg_o = _hgrn(hq, hf, hi, hg, log_lb[l].reshape(1, HG_WIDTH), log_1m_lb[l].reshape(1, HG_WIDTH),
                     jnp.tile(hgrn_norm[l].astype(F32), HG_HEADS).reshape(1, HG_WIDTH), head_ones, batch, seq)
        mem_o = _mem_attn(mq, mk, mv, l, batch, seq, mem_len)
        x1_z, cls, rank, counts, *cast_w = _out_proj(
            x2, swa_o, hg_o, mem_o, w_out_bf[l], ln1_g[l].reshape(1, D_MODEL), ln1_b[l].reshape(1, D_MODEL),
            wr_t, rb_col, tri, l, w_slabs)
        if cast_in_out_proj:
            up_shape, down_shape = (1, N_EXPERTS, D_MODEL, D_EXPERT), (1, N_EXPERTS, D_EXPERT, D_MODEL)
            layer_w = (cast_w[0].reshape(up_shape), cast_w[1].reshape(up_shape), cast_w[2].reshape(down_shape), 0)
        else:
            layer_w = (*expert_w, l)
        cstart, items = _work_items(counts[:N_CLASSES, 0].astype(I32), n_tok)
        dest, xs_z = _scatter_rows(x1_z, cls, rank, cstart)
        ys_z = _experts(xs_z, items, wr_pad, *layer_w)
        ln2_args = (x1_z, ys_z, dest.reshape(n_tok), ln2_g[l].reshape(1, D_MODEL), ln2_b[l].reshape(1, D_MODEL),
                    route_tile)
        if l + 1 < DEPTH:
            x2, *proj = _ln2(*ln2_args, w_in_bf[l + 1], b_in[l + 1].reshape(1, IN_WIDTH))
        else:
            (x2,) = _ln2(*ln2_args)
    return x2.reshape(batch, seq, D_MODEL)
```

```python
import functools
import math

import jax
import jax.numpy as jnp
from jax import lax
from jax.experimental import pallas as pl
from jax.experimental.pallas import tpu as pltpu

F32 = jnp.float32
BF16 = jnp.bfloat16
I32 = jnp.int32

D_MODEL = 1024
DEPTH = 4
HEAD_DIM = 64
SWA_HEADS = 8
SWA_KV_HEADS = 2
SWA_GROUP = SWA_HEADS // SWA_KV_HEADS
WINDOW = 128
ATTN_BLOCK = 128
HG_HEADS = 4
HG_DK = 64
HG_DV = 64
HG_WIDTH = HG_HEADS * HG_DK
MEM_HEADS = 4
N_BUCKETS = 32
MAX_DISTANCE = 128
N_EXPERTS = 16
N_GROUPS = 4
EXPERTS_PER_GROUP = 4
D_EXPERT = 512
LN_EPS = 1e-5
RMS_EPS = 1e-6
ALPHA = (2 * DEPTH) ** 0.25
SPLITS = (512, 128, 128, 256, 256, 256, 256, 256)
IN_WIDTH = sum(SPLITS)
PROJ_DTYPES = (BF16, BF16, BF16, F32, F32, F32, F32, BF16)
MIX_WIDTH = 1024
ATTN_SCALE = HEAD_DIM ** -0.5

SUBLANES = 8
LANES = 128
ROW_TILES = D_MODEL // LANES

SWA_STEP_BLOCKS = 4
HG_CHUNK = 128
HG_STEP_CHUNKS = 4
HG_SUB = 32
HG_SAFE_DECAY = 80.0
N_PAIRS = 6
N_CLASSES = N_GROUPS * N_PAIRS
CLASS_ROWS = 32
MOE_BLK = 256
DMA_GROUP = 8


def _params(semantics, vmem_mib):
    return pltpu.CompilerParams(dimension_semantics=semantics, vmem_limit_bytes=vmem_mib * 1024 * 1024)


def _dot(a, b):
    return jnp.dot(a, b, preferred_element_type=F32)


def _dot_nt(a, b):
    return lax.dot_general(a, b, (((1,), (1,)), ((), ())), preferred_element_type=F32)


def _dot_tn(a, b):
    return lax.dot_general(a, b, (((0,), (0,)), ((), ())), preferred_element_type=F32)


def _from_tiles(z, n_rows):
    k = z.shape[0] // n_rows
    return z.reshape(n_rows, k, LANES).reshape(n_rows, k * LANES)


def _to_tiles(val):
    n_rows, width = val.shape
    return val.reshape(n_rows, width // LANES, LANES).reshape(n_rows * (width // LANES), LANES)


def _in_proj_body(x_ref, w_ref, b_ref, *out_refs):
    acc = _dot(x_ref[...].astype(BF16), w_ref[...]) + b_ref[...]
    off = 0
    for ref, width in zip(out_refs, SPLITS):
        ref[...] = acc[:, off:off + width].astype(ref.dtype)
        off += width


def _in_proj(x2, w_bf, b_row):
    n = x2.shape[0]
    tm = min(512, n)
    return pl.pallas_call(
        _in_proj_body,
        grid=(n // tm,),
        in_specs=[pl.BlockSpec((tm, D_MODEL), lambda i: (i, 0)),
                  pl.BlockSpec((D_MODEL, IN_WIDTH), lambda i: (0, 0)),
                  pl.BlockSpec((1, IN_WIDTH), lambda i: (0, 0))],
        out_specs=[pl.BlockSpec((tm, w), lambda i: (i, 0)) for w in SPLITS],
        out_shape=[jax.ShapeDtypeStruct((n, w), dt) for w, dt in zip(SPLITS, PROJ_DTYPES)],
        compiler_params=_params(("arbitrary",), 48),
        name="in_proj",
    )(x2, w_bf, b_row)


def _mem_kv_body(mem_ref, w_ref, k_ref, v_ref):
    acc = _dot(mem_ref[...].astype(BF16), w_ref[0].astype(BF16))
    half = MEM_HEADS * HEAD_DIM
    k_ref[0] = acc[:, :half].astype(BF16)
    v_ref[0] = acc[:, half:].astype(BF16)


def _mem_kv(mem2, w_mem_kv):
    rows = mem2.shape[0]
    tm = min(512, rows)
    half = MEM_HEADS * HEAD_DIM
    return pl.pallas_call(
        _mem_kv_body,
        grid=(DEPTH, rows // tm),
        in_specs=[pl.BlockSpec((tm, D_MODEL), lambda l, i: (i, 0)),
                  pl.BlockSpec((1, D_MODEL, 2 * half), lambda l, i: (l, 0, 0))],
        out_specs=[pl.BlockSpec((1, tm, half), lambda l, i: (l, i, 0)),
                   pl.BlockSpec((1, tm, half), lambda l, i: (l, i, 0))],
        out_shape=[jax.ShapeDtypeStruct((DEPTH, rows, half), BF16)] * 2,
        compiler_params=_params(("arbitrary", "arbitrary"), 32),
        name="mem_kv",
    )(mem2, w_mem_kv)


def _swa_body(sink_ref, q_ref, kp_ref, ko_ref, vp_ref, vo_ref, bias_ref, o_ref, *, blocks):
    n = pl.program_id(1)
    q_all = q_ref[...] * ATTN_SCALE
    k_all = jnp.concatenate([kp_ref[...], ko_ref[...]], axis=0)
    v_all = jnp.concatenate([vp_ref[...], vo_ref[...]], axis=0)
    from_prev = lax.broadcasted_iota(I32, (ATTN_BLOCK, ATTN_BLOCK), 0) > \
        lax.broadcasted_iota(I32, (ATTN_BLOCK, ATTN_BLOCK), 1)
    for j in range(blocks):
        lo = j * ATTN_BLOCK
        kk = k_all[lo:lo + 2 * ATTN_BLOCK]
        vv = v_all[lo:lo + 2 * ATTN_BLOCK]
        table = jnp.where(n > 0, 1, 0) if j == 0 else 1
        outs = []
        for h in range(SWA_HEADS):
            g = h // SWA_GROUP
            q_h = q_all[lo:lo + ATTN_BLOCK, h * HEAD_DIM:(h + 1) * HEAD_DIM]
            scores = _dot_nt(kk[:, g * HEAD_DIM:(g + 1) * HEAD_DIM], q_h)
            band = jnp.where(from_prev, scores[:ATTN_BLOCK], scores[ATTN_BLOCK:]) + bias_ref[table, h]
            sink = sink_ref[h]
            m = jnp.maximum(jnp.max(band, axis=0, keepdims=True), sink)
            e = jnp.exp(band - m)
            p = e * (1.0 / (jnp.sum(e, axis=0, keepdims=True) + jnp.exp(sink - m)))
            p = jnp.concatenate([jnp.where(from_prev, p, 0.0), jnp.where(from_prev, 0.0, p)], axis=0)
            outs.append(_dot_tn(p.astype(BF16), vv[:, g * HEAD_DIM:(g + 1) * HEAD_DIM]))
        o_ref[lo:lo + ATTN_BLOCK, :] = jnp.concatenate(outs, axis=-1).astype(o_ref.dtype)


def _swa(sq, sk, sv, sinks, bias2, batch, seq):
    nb = seq // ATTN_BLOCK
    blocks = math.gcd(SWA_STEP_BLOCKS, nb)
    ns = nb // blocks
    own = lambda b, n: (b * ns + n, 0)
    prev = lambda b, n: (b * nb + jnp.maximum(n * blocks - 1, 0), 0)
    kvw = SWA_KV_HEADS * HEAD_DIM
    qw = SWA_HEADS * HEAD_DIM
    rows = blocks * ATTN_BLOCK
    return pl.pallas_call(
        functools.partial(_swa_body, blocks=blocks),
        grid=(batch, ns),
        in_specs=[pl.BlockSpec(memory_space=pltpu.SMEM),
                  pl.BlockSpec((rows, qw), own),
                  pl.BlockSpec((ATTN_BLOCK, kvw), prev),
                  pl.BlockSpec((rows, kvw), own),
                  pl.BlockSpec((ATTN_BLOCK, kvw), prev),
                  pl.BlockSpec((rows, kvw), own),
                  pl.BlockSpec((2, SWA_HEADS, ATTN_BLOCK, ATTN_BLOCK), lambda b, n: (0, 0, 0, 0))],
        out_specs=pl.BlockSpec((rows, qw), own),
        out_shape=jax.ShapeDtypeStruct((batch * seq, qw), BF16),
        compiler_params=_params(("arbitrary", "arbitrary"), 32),
        name="swa",
    )(sinks, sq, sk, sk, sv, sv, bias2)


def _cumsum_rows(x):
    n = x.shape[0]
    row = lax.broadcasted_iota(I32, (n, 1), 0)
    s = 1
    while s < n:
        x = x + jnp.where(row >= s, pltpu.roll(x, s, 0), 0.0)
        s *= 2
    return x


def _rows_from_blocks(vals, width):
    return jnp.concatenate([jnp.broadcast_to(v, (HG_SUB, width)) for v in vals], axis=0)


def _hgrn_body(hq_ref, hf_ref, hi_ref, hg_ref, loglb_ref, log1mlb_ref, nw_ref, bo_ref, o_ref, *scratch,
               chunks):
    sequence_start = pl.program_id(1) == 0
    for j in range(chunks):
        rows = pl.ds(j * HG_CHUNK, HG_CHUNK)
        _hgrn_chunk(sequence_start if j == 0 else None,
                    hq_ref.at[rows], hf_ref.at[rows], hi_ref.at[rows], hg_ref.at[rows],
                    loglb_ref, log1mlb_ref, nw_ref, bo_ref, o_ref.at[rows], *scratch)


def _hgrn_chunk(reset, hq_ref, hf_ref, hi_ref, hg_ref, loglb_ref, log1mlb_ref, nw_ref, bo_ref, o_ref,
                st_ref, a_scr, q_scr, k_scr, oi_scr):
    n_sub = HG_CHUNK // HG_SUB
    width = HG_WIDTH

    if reset is not None:
        @pl.when(reset)
        def _():
            st_ref[...] = jnp.zeros_like(st_ref)

    z = hf_ref[...]
    log_sig = jnp.minimum(z, 0.0) - jnp.log(1.0 + jnp.exp(-jnp.abs(z)))
    t_a = loglb_ref[...]
    t_b = log1mlb_ref[...] + log_sig
    log_f = jnp.maximum(t_a, t_b) + jnp.log(1.0 + jnp.exp(-jnp.abs(t_a - t_b)))
    kk = jnp.exp(t_b - z)
    hq = hq_ref[...]
    qq = hq * jax.nn.sigmoid(hq) * (HG_DK ** -0.5)
    vv = hi_ref[...]

    a_cum = _cumsum_rows(log_f)
    ends = [a_cum[HG_SUB * j + HG_SUB - 1:HG_SUB * j + HG_SUB, :] for j in range(n_sub)]
    zero_row = jnp.zeros((1, width), F32)
    starts = [zero_row] + ends[:-1]
    a_loc = a_cum - _rows_from_blocks(starts, width)
    e_loc = _rows_from_blocks(ends, width) - a_cum
    worst = starts[0] - ends[0]
    for j in range(1, n_sub):
        worst = jnp.maximum(worst, starts[j] - ends[j])
    unsafe = jnp.max(worst) > HG_SAFE_DECAY

    q_sub = qq * jnp.exp(a_loc)
    k_sub = kk * jnp.exp(-a_loc)
    k_end = kk * jnp.exp(e_loc)
    one_row = jnp.ones((1, width), F32)
    q_dec = q_sub * _rows_from_blocks([jnp.exp(s) for s in starts], width)
    k_dec = k_end * _rows_from_blocks([jnp.exp(ends[-1] - e) for e in ends], width)
    q_top = q_sub * _rows_from_blocks(
        [zero_row, zero_row, one_row, jnp.exp(ends[2] - ends[1])], width)
    k_top = k_end * _rows_from_blocks(
        [jnp.exp(ends[1] - ends[0]), one_row, zero_row, zero_row], width)

    row = lax.broadcasted_iota(I32, (HG_CHUNK, 1), 0)
    sub_of_row = row // HG_SUB
    t_idx = lax.broadcasted_iota(I32, (HG_CHUNK, HG_CHUNK), 0)
    s_idx = lax.broadcasted_iota(I32, (HG_CHUNK, HG_CHUNK), 1)
    m_sub = ((t_idx // HG_SUB) == (s_idx // HG_SUB)) & (t_idx >= s_idx)
    m_sub2 = jnp.concatenate([m_sub, m_sub], axis=0)
    lane = lax.broadcasted_iota(I32, (1, LANES), 1)
    lo = lane < HG_DK
    bd = (lax.broadcasted_iota(I32, (LANES, LANES), 0) // HG_DV) == \
         (lax.broadcasted_iota(I32, (LANES, LANES), 1) // HG_DK)
    dec_row = jnp.exp(ends[-1])

    def heads_on_rows(x):
        return jnp.concatenate([jnp.where(lo, x, 0.0), jnp.where(lo, 0.0, x)], axis=0)

    o_inter = []
    for p in range(width // LANES):
        sl = slice(p * LANES, (p + 1) * LANES)
        qs, ks, ke = q_sub[:, sl], k_sub[:, sl], k_end[:, sl]
        p_sub = _dot_nt(heads_on_rows(qs).astype(BF16), ks.astype(BF16))
        q_x = jnp.concatenate([jnp.where(sub_of_row == 1, qs, 0.0),
                               jnp.where(sub_of_row == 3, qs, 0.0), q_top[:, sl]], axis=1)
        k_x = jnp.concatenate([jnp.where(sub_of_row == 0, ke, 0.0),
                               jnp.where(sub_of_row == 2, ke, 0.0), k_top[:, sl]], axis=1)
        q_x2 = jnp.concatenate([jnp.where(jnp.tile(lo, (1, 3)), q_x, 0.0),
                                jnp.where(jnp.tile(lo, (1, 3)), 0.0, q_x)], axis=0)
        p_x = _dot_nt(q_x2.astype(BF16), k_x.astype(BF16))
        p_all = jnp.where(m_sub2, p_sub, 0.0) + p_x
        p_cat = jnp.concatenate([p_all[:HG_CHUNK], p_all[HG_CHUNK:]], axis=1)
        v_p = vv[:, sl]
        oi_scr[:, sl] = _dot(p_cat.astype(BF16), heads_on_rows(v_p).astype(BF16))
        st = st_ref[p]
        o_inter.append(_dot_nt(q_dec[:, sl].astype(BF16), st.astype(BF16)))
        upd = _dot_tn(v_p.astype(BF16), k_dec[:, sl].astype(BF16))
        st_ref[p] = st * dec_row[:, sl] + jnp.where(bd, upd, 0.0)
    o_inter = jnp.concatenate(o_inter, axis=1)

    @pl.when(unsafe)
    def _():
        a_scr[...] = a_cum
        q_scr[...] = qq
        k_scr[...] = kk

        def body(t, carry):
            d = a_scr[pl.ds(t, 1), :] - a_scr[...]
            w = jnp.exp(jnp.where(row <= t, d, -jnp.inf))
            prod = (q_scr[pl.ds(t, 1), :] * k_scr[...]) * w
            e = _dot(prod.astype(BF16), bo_ref[...])
            oi_scr[pl.ds(t, 1), :] = jnp.sum(e * hi_ref[...], axis=0, keepdims=True)
            return carry

        lax.fori_loop(0, HG_CHUNK, body, 0)

    o = oi_scr[...] + o_inter
    sq = o * o
    sq_hi = sq.astype(BF16)
    sq_lo = (sq - sq_hi.astype(F32)).astype(BF16)
    ms = (_dot(sq_hi, bo_ref[...]) + _dot(sq_lo, bo_ref[...])) * (1.0 / HG_DV)
    gate = hg_ref[...]
    o = o * lax.rsqrt(ms + RMS_EPS) * nw_ref[...] * (gate * jax.nn.sigmoid(gate))
    o_ref[...] = o.astype(o_ref.dtype)


def _hgrn(hq, hf, hi, hg, loglb, log1mlb, nw_row, head_ones, batch, seq):
    chunks = math.gcd(HG_STEP_CHUNKS, seq // HG_CHUNK)
    ns = seq // (HG_CHUNK * chunks)
    blk = lambda b, c: (b * ns + c, 0)
    row = lambda b, c: (0, 0)
    w = HG_WIDTH
    return pl.pallas_call(
        functools.partial(_hgrn_body, chunks=chunks),
        grid=(batch, ns),
        in_specs=[pl.BlockSpec((HG_CHUNK * chunks, w), blk)] * 4 + [pl.BlockSpec((1, w), row)] * 3
                 + [pl.BlockSpec((w, w), row)],
        out_specs=pl.BlockSpec((HG_CHUNK * chunks, w), blk),
        out_shape=jax.ShapeDtypeStruct((batch * seq, w), BF16),
        scratch_shapes=[pltpu.VMEM((w // LANES, LANES, LANES), F32)]
                       + [pltpu.VMEM((HG_CHUNK, w), F32)] * 4,
        compiler_params=_params(("arbitrary", "arbitrary"), 32),
        name="hgrn",
    )(hq, hf, hi, hg, loglb, log1mlb, nw_row, head_ones)


def _mem_attn_body(q_ref, k_ref, v_ref, o_ref):
    q_all = q_ref[...] * ATTN_SCALE
    outs = []
    for h in range(MEM_HEADS):
        sl = slice(h * HEAD_DIM, (h + 1) * HEAD_DIM)
        scores = _dot_nt(k_ref[0, :, sl], q_all[:, sl])
        e = jnp.exp(scores - jnp.max(scores, axis=0, keepdims=True))
        p = e * (1.0 / jnp.sum(e, axis=0, keepdims=True))
        outs.append(_dot_tn(p.astype(BF16), v_ref[0, :, sl]))
    o_ref[...] = jnp.concatenate(outs, axis=-1).astype(o_ref.dtype)


def _mem_attn(mq, mk, mv, layer, batch, seq, mem_len):
    tq = min(1024, seq)
    nq = seq // tq
    w = MEM_HEADS * HEAD_DIM
    return pl.pallas_call(
        _mem_attn_body,
        grid=(batch, nq),
        in_specs=[pl.BlockSpec((tq, w), lambda b, i: (b * nq + i, 0)),
                  pl.BlockSpec((1, mem_len, w), lambda b, i: (layer, b, 0)),
                  pl.BlockSpec((1, mem_len, w), lambda b, i: (layer, b, 0))],
        out_specs=pl.BlockSpec((tq, w), lambda b, i: (b * nq + i, 0)),
        out_shape=jax.ShapeDtypeStruct((batch * seq, w), BF16),
        compiler_params=_params(("arbitrary", "arbitrary"), 32),
        name="mem_attn",
    )(mq, mk, mv)


def _layer_norm(h, g, b):
    mu = jnp.mean(h, axis=-1, keepdims=True)
    d = h - mu
    var = jnp.mean(d * d, axis=-1, keepdims=True)
    return d * lax.rsqrt(var + LN_EPS) * g + b


def _route_rows(sel):
    def row(a, r):
        return a[r:r + 1, :]

    best = None
    for g in range(N_GROUPS):
        a = [row(sel, 4 * g + i) for i in range(4)]
        gs = a[0] + a[1]
        for i, j in ((0, 2), (0, 3), (1, 2), (1, 3), (2, 3)):
            gs = jnp.maximum(gs, a[i] + a[j])
        if best is None:
            best, best_score = jnp.zeros_like(gs, dtype=I32), gs
        else:
            better = gs > best_score
            best = jnp.where(better, g, best)
            best_score = jnp.where(better, gs, best_score)

    def pick(arr, i):
        out = row(arr, i)
        for g in range(1, N_GROUPS):
            out = jnp.where(best == g, row(arr, 4 * g + i), out)
        return out

    a = [pick(sel, i) for i in range(4)]
    i1, m1 = jnp.zeros_like(best), a[0]
    for i in range(1, 4):
        gt = a[i] > m1
        i1 = jnp.where(gt, i, i1)
        m1 = jnp.where(gt, a[i], m1)
    i2 = jnp.full_like(best, -1)
    m2 = jnp.full_like(m1, -jnp.inf)
    for i in range(4):
        gt = (i1 != i) & (a[i] > m2)
        i2 = jnp.where(gt, i, i2)
        m2 = jnp.where(gt, a[i], m2)
    lo = jnp.minimum(i1, i2)
    hi = jnp.maximum(i1, i2)
    pair = jnp.where(lo == 0, hi - 1, jnp.where(lo == 1, hi + 1, 5))
    return best * N_PAIRS + pair


def _out_proj_body(x_ref, swa_ref, hg_ref, mem_ref, w_ref, g_ref, b_ref, wr_ref, rb_ref, tri_ref, *rest):
    n_cast = (len(rest) - 5) // 2
    cast_in = rest[:n_cast]
    x1_ref, cls_ref, rank_ref, cnt_ref = rest[n_cast:n_cast + 4]
    cast_out, carry_ref = rest[n_cast + 4:-1], rest[-1]
    for src, dst in zip(cast_in, cast_out):
        dst[...] = src[0].astype(BF16)

    i = pl.program_id(0)
    t = x_ref.shape[0]

    @pl.when(i == 0)
    def _():
        carry_ref[...] = jnp.zeros_like(carry_ref)

    acc = _dot(swa_ref[...], w_ref[0:512, :]) + _dot(hg_ref[...], w_ref[512:768, :]) \
        + _dot(mem_ref[...], w_ref[768:1024, :])
    x1 = _layer_norm(ALPHA * x_ref[...] + acc, g_ref[...], b_ref[...])
    x1_ref[...] = _to_tiles(x1)

    logits = _dot_nt(wr_ref[...], x1.astype(BF16))
    cls = _route_rows(jax.nn.sigmoid(logits) + rb_ref[...])
    onehot = (lax.broadcasted_iota(I32, (CLASS_ROWS, t), 0) == cls).astype(F32)
    prefix = _dot(onehot.astype(BF16), tri_ref[...])
    carry = carry_ref[...]
    rank = jnp.sum(onehot * (prefix - 1.0 + carry[:, 0:1]), axis=0, keepdims=True)
    new_carry = carry + prefix[:, t - 1:t]
    carry_ref[...] = new_carry
    cnt_ref[...] = new_carry
    cls_ref[0] = cls
    rank_ref[0] = rank.astype(I32)


def _out_proj(x2, swa_o, hg_o, mem_o, w_bf, g_row, b_row, wr_t, rb_col, tri, layer, cast=()):
    n = x2.shape[0]
    t = tri.shape[0]
    nt = n // t
    tok = lambda i: (i, 0)
    fixed = lambda i: (0, 0)
    per_tile = pl.BlockSpec((1, 1, t), lambda i: (i, 0, 0))
    cast_in = [pl.BlockSpec((1, a.shape[1] // nt, a.shape[2]), lambda i: (layer, i, 0)) for a in cast]
    cast_out = [pl.BlockSpec((a.shape[1] // nt, a.shape[2]), tok) for a in cast]
    return pl.pallas_call(
        _out_proj_body,
        grid=(nt,),
        in_specs=[pl.BlockSpec((t, D_MODEL), tok), pl.BlockSpec((t, 512), tok),
                  pl.BlockSpec((t, 256), tok), pl.BlockSpec((t, 256), tok),
                  pl.BlockSpec((MIX_WIDTH, D_MODEL), fixed),
                  pl.BlockSpec((1, D_MODEL), fixed), pl.BlockSpec((1, D_MODEL), fixed),
                  pl.BlockSpec((N_EXPERTS, D_MODEL), fixed), pl.BlockSpec((N_EXPERTS, 1), fixed),
                  pl.BlockSpec((t, t), fixed)] + cast_in,
        out_specs=[pl.BlockSpec((t * ROW_TILES, LANES), tok), per_tile, per_tile,
                   pl.BlockSpec((CLASS_ROWS, LANES), fixed)] + cast_out,
        out_shape=[jax.ShapeDtypeStruct((n * ROW_TILES, LANES), F32),
                   jax.ShapeDtypeStruct((nt, 1, t), I32), jax.ShapeDtypeStruct((nt, 1, t), I32),
                   jax.ShapeDtypeStruct((CLASS_ROWS, LANES), F32)]
                  + [jax.ShapeDtypeStruct(a.shape[1:], BF16) for a in cast],
        scratch_shapes=[pltpu.VMEM((CLASS_ROWS, LANES), F32)],
        compiler_params=_params(("arbitrary",), 56),
        name="out_proj_ln_route",
    )(x2, swa_o, hg_o, mem_o, w_bf, g_row, b_row, wr_t, rb_col, tri, *cast)


def _tile_copy(src_ref, dst_ref, sem, src_row, dst_row, tiles):
    n = tiles * SUBLANES
    return pltpu.make_async_copy(
        src_ref.at[pl.ds(pl.multiple_of(src_row * n, n), n), :],
        dst_ref.at[pl.ds(pl.multiple_of(dst_row * n, n), n), :], sem)


def _scatter_body(cstart_ref, cls_ref, rank_ref, x_hbm, dest_ref, xs_hbm,
                  buf, dest_vmem, dest_smem, load_sem, scat_sem, misc_sem, *, n_steps, tile):
    i = pl.program_id(0)
    rows = tile * SUBLANES
    slot = i % 3

    def load(step, into):
        return pltpu.make_async_copy(
            x_hbm.at[pl.ds(pl.multiple_of(step * rows, rows), rows), :], buf.at[into], load_sem.at[into])

    def scatter_done(of):
        return pltpu.make_async_copy(buf.at[of], xs_hbm.at[pl.ds(0, rows), :], scat_sem.at[of])

    @pl.when(i == 0)
    def _():
        load(0, 0).start()

    @pl.when(i + 1 < n_steps)
    def _():
        load(i + 1, (i + 1) % 3).start()

    cls = cls_ref[0]
    dest = rank_ref[0]
    for c in range(N_CLASSES):
        dest = dest + jnp.where(cls == c, cstart_ref[c], 0)
    dest_ref[0] = dest
    dest_vmem[...] = dest
    to_smem = pltpu.make_async_copy(dest_vmem, dest_smem, misc_sem)
    to_smem.start()
    to_smem.wait()

    load(i, slot).wait()

    def issue(group, carry):
        for k in range(DMA_GROUP):
            j = group * DMA_GROUP + k
            _tile_copy(buf.at[slot], xs_hbm, scat_sem.at[slot], j, dest_smem[0, j], 1).start(priority=k % 2)
        return carry

    lax.fori_loop(0, tile // DMA_GROUP, issue, 0)

    @pl.when(i >= 1)
    def _():
        scatter_done((i + 2) % 3).wait()

    @pl.when(i == n_steps - 1)
    def _():
        scatter_done(slot).wait()


def _scatter_rows(x1_z, cls, rank, cstart):
    nt, _, tile = cls.shape
    n = nt * tile
    per_tile = pl.BlockSpec((1, 1, tile), lambda i, cs: (i, 0, 0))
    return pl.pallas_call(
        functools.partial(_scatter_body, n_steps=nt, tile=tile),
        grid_spec=pltpu.PrefetchScalarGridSpec(
            num_scalar_prefetch=1,
            grid=(nt,),
            in_specs=[per_tile, per_tile, pl.BlockSpec(memory_space=pl.ANY)],
            out_specs=[per_tile, pl.BlockSpec(memory_space=pl.ANY)],
            scratch_shapes=[pltpu.VMEM((3, tile * SUBLANES, LANES), F32),
                            pltpu.VMEM((1, tile), I32), pltpu.SMEM((1, tile), I32),
                            pltpu.SemaphoreType.DMA((3,)), pltpu.SemaphoreType.DMA((3,)),
                            pltpu.SemaphoreType.DMA(())]),
        out_shape=[jax.ShapeDtypeStruct((nt, 1, tile), I32),
                   jax.ShapeDtypeStruct((n * SUBLANES, LANES), F32)],
        compiler_params=_params(("arbitrary",), 32),
        name="scatter_rows",
    )(cstart, cls, rank, x1_z)


def _expert_body(blk_ref, elo_ref, ehi_ref, start_ref, end_ref, xs_ref, wr_ref,
                 g_lo, u_lo, d_lo, g_hi, u_hi, d_hi, ys_ref):
    i = pl.program_id(0)
    start = start_ref[i]
    end = end_ref[i]

    @pl.when(end > start)
    def _():
        x = _from_tiles(xs_ref[...], MOE_BLK).astype(BF16)
        scores = jax.nn.sigmoid(_dot(x, wr_ref[...]))
        lane = lax.broadcasted_iota(I32, (1, LANES), 1)
        s_lo = jnp.sum(jnp.where(lane == elo_ref[i], scores, 0.0), axis=-1, keepdims=True)
        s_hi = jnp.sum(jnp.where(lane == ehi_ref[i], scores, 0.0), axis=-1, keepdims=True)
        denom = s_lo + s_hi

        def ffn(gw, uw, dw, w_col):
            g = _dot(x, gw[0, 0])
            u = _dot(x, uw[0, 0])
            return _dot((((g * jax.nn.sigmoid(g)) * u) * w_col).astype(BF16), dw[0, 0])

        y = _to_tiles(ffn(g_lo, u_lo, d_lo, s_lo / denom) + ffn(g_hi, u_hi, d_hi, s_hi / denom))
        first = start % MOE_BLK == 0

        @pl.when(first)
        def _():
            ys_ref[...] = y

        @pl.when(jnp.logical_not(first))
        def _():
            slot = blk_ref[i] * MOE_BLK \
                + lax.broadcasted_iota(I32, (MOE_BLK * ROW_TILES, 1), 0) // ROW_TILES
            ys_ref[...] = jnp.where((slot >= start) & (slot < end), y, ys_ref[...])


def _experts(xs_z, items, wr_pad, wg, wu, wd, layer):
    blk, elo, ehi, start, end = items
    n_items = blk.shape[0]
    n_slots = xs_z.shape[0] // SUBLANES

    def data_map(i, blk, elo, ehi, start, end):
        return (blk[i], 0)

    def w_lo_map(i, blk, elo, ehi, start, end):
        return (layer, elo[i], 0, 0)

    def w_hi_map(i, blk, elo, ehi, start, end):
        return (layer, ehi[i], 0, 0)

    up_spec = lambda m: pl.BlockSpec((1, 1, D_MODEL, D_EXPERT), m)
    down_spec = lambda m: pl.BlockSpec((1, 1, D_EXPERT, D_MODEL), m)
    return pl.pallas_call(
        _expert_body,
        grid_spec=pltpu.PrefetchScalarGridSpec(
            num_scalar_prefetch=5,
            grid=(n_items,),
            in_specs=[pl.BlockSpec((MOE_BLK * SUBLANES, LANES), data_map),
                      pl.BlockSpec((D_MODEL, LANES), lambda i, *_: (0, 0)),
                      up_spec(w_lo_map), up_spec(w_lo_map), down_spec(w_lo_map),
                      up_spec(w_hi_map), up_spec(w_hi_map), down_spec(w_hi_map)],
            out_specs=pl.BlockSpec((MOE_BLK * SUBLANES, LANES), data_map)),
        out_shape=jax.ShapeDtypeStruct((n_slots * SUBLANES, LANES), F32),
        compiler_params=_params(("arbitrary",), 48),
        name="experts",
    )(blk, elo, ehi, start, end, xs_z, wr_pad, wg, wu, wd, wg, wu, wd)


def _gathered_ln2(dest_ref, x1_ref, ys_hbm, g_ref, b_ref, buf, sem, n_steps, tile):
    i = pl.program_id(0)
    slot = i % 2

    def gather(step, into):
        def issue(group, carry):
            for k in range(DMA_GROUP):
                j = group * DMA_GROUP + k
                _tile_copy(ys_hbm, buf.at[into], sem.at[into], dest_ref[step * tile + j], j, 1).start(
                    priority=k % 2)
            return carry

        lax.fori_loop(0, tile // DMA_GROUP, issue, 0)

    @pl.when(i == 0)
    def _():
        gather(0, 0)

    @pl.when(i + 1 < n_steps)
    def _():
        gather(i + 1, (i + 1) % 2)

    pltpu.make_async_copy(ys_hbm.at[pl.ds(0, tile * SUBLANES), :], buf.at[slot], sem.at[slot]).wait()
    h = ALPHA * _from_tiles(x1_ref[...], tile) + _from_tiles(buf[slot], tile)
    return _layer_norm(h, g_ref[...], b_ref[...])


def _ln2_body(dest_ref, x1_ref, ys_hbm, g_ref, b_ref, o_ref, buf, sem, *, n_steps, tile):
    o_ref[...] = _gathered_ln2(dest_ref, x1_ref, ys_hbm, g_ref, b_ref, buf, sem, n_steps, tile)


def _ln2_in_proj_body(dest_ref, x1_ref, ys_hbm, g_ref, b_ref, w_ref, bias_ref, o_ref, *rest, n_steps, tile):
    *proj_refs, buf, sem = rest
    x = _gathered_ln2(dest_ref, x1_ref, ys_hbm, g_ref, b_ref, buf, sem, n_steps, tile)
    o_ref[...] = x
    acc = _dot(x.astype(BF16), w_ref[...]) + bias_ref[...]
    off = 0
    for ref, width in zip(proj_refs, SPLITS):
        ref[...] = acc[:, off:off + width].astype(ref.dtype)
        off += width


def _ln2(x1_z, ys_z, dest, g_row, b_row, tile, w_bf=None, bias_row=None):
    n = x1_z.shape[0] // SUBLANES
    nt = n // tile
    tok = lambda i, d: (i, 0)
    fixed = lambda i, d: (0, 0)
    in_specs = [pl.BlockSpec((tile * SUBLANES, LANES), tok), pl.BlockSpec(memory_space=pl.ANY),
                pl.BlockSpec((1, D_MODEL), fixed), pl.BlockSpec((1, D_MODEL), fixed)]
    out_specs = [pl.BlockSpec((tile, D_MODEL), tok)]
    out_shape = [jax.ShapeDtypeStruct((n, D_MODEL), F32)]
    operands = [dest, x1_z, ys_z, g_row, b_row]
    body = _ln2_body
    if w_bf is not None:
        in_specs += [pl.BlockSpec((D_MODEL, IN_WIDTH), fixed), pl.BlockSpec((1, IN_WIDTH), fixed)]
        out_specs += [pl.BlockSpec((tile, w), tok) for w in SPLITS]
        out_shape += [jax.ShapeDtypeStruct((n, w), dt) for w, dt in zip(SPLITS, PROJ_DTYPES)]
        operands += [w_bf, bias_row]
        body = _ln2_in_proj_body
    return pl.pallas_call(
        functools.partial(body, n_steps=nt, tile=tile),
        grid_spec=pltpu.PrefetchScalarGridSpec(
            num_scalar_prefetch=1,
            grid=(nt,),
            in_specs=in_specs,
            out_specs=out_specs,
            scratch_shapes=[pltpu.VMEM((2, tile * SUBLANES, LANES), F32),
                            pltpu.SemaphoreType.DMA((2,))]),
        out_shape=out_shape,
        compiler_params=_params(("arbitrary",), 56),
        name="gather_ln2" if w_bf is None else "gather_ln2_in_proj",
    )(*operands)


def _t5_bucket(dist):
    max_exact = N_BUCKETS // 2
    d = jnp.maximum(dist, 0)
    large = max_exact + (jnp.log(jnp.maximum(d, 1).astype(F32) / max_exact)
                         / math.log(MAX_DISTANCE / max_exact) * (N_BUCKETS - max_exact)).astype(I32)
    large = jnp.minimum(large, N_BUCKETS - 1)
    return jnp.where(d < max_exact, d, large)


def _banded_bias(rel_bias):
    i = jnp.arange(ATTN_BLOCK)[:, None]
    j = jnp.arange(2 * ATTN_BLOCK)[None, :]
    dist = i + ATTN_BLOCK - j
    bucket = _t5_bucket(dist)[None]
    table = rel_bias.astype(F32)
    bias = jnp.zeros((SWA_HEADS, ATTN_BLOCK, 2 * ATTN_BLOCK), F32)
    for b in range(N_BUCKETS):
        bias = jnp.where(bucket == b, table[b][:, None, None], bias)
    from_prev = (jnp.arange(ATTN_BLOCK)[None, :] > i)[None]
    prev, own = bias[:, :, :ATTN_BLOCK], bias[:, :, ATTN_BLOCK:]
    normal = jnp.where(from_prev, prev, own)
    first = jnp.where(from_prev, -jnp.inf, own)
    return jnp.swapaxes(jnp.stack([first, normal]), 2, 3)


def _work_items(counts, n_tok):
    n_blocks = n_tok // MOE_BLK
    cend = jnp.cumsum(counts)
    cstart = cend - counts
    blk_starts = jnp.arange(n_blocks, dtype=I32) * MOE_BLK
    cls_starts = jnp.where(counts > 0, cstart, n_tok)
    start = jnp.sort(jnp.concatenate([blk_starts, cls_starts]))
    end = jnp.concatenate([start[1:], jnp.full((1,), n_tok, I32)])
    blk = jnp.minimum(start, n_tok - 1) // MOE_BLK
    cls = jnp.minimum(jnp.sum((cend[None, :] <= start[:, None]).astype(I32), axis=1), N_CLASSES - 1)
    group, pair = cls // N_PAIRS, cls % N_PAIRS
    lo = (pair >= 3).astype(I32) + (pair >= 5).astype(I32)
    hi = jnp.where(pair < 3, pair + 1, jnp.where(pair < 5, pair - 1, 3))
    return cstart, (blk, group * EXPERTS_PER_GROUP + lo, group * EXPERTS_PER_GROUP + hi, start, end)


def kernel(x, mem, w_in, b_in, w_mem_kv, attn_sinks, rel_bias, hgrn_lb_logits, hgrn_norm, w_out,
           ln1_g, ln1_b, w_router, router_bias, w_gate, w_up, w_down, ln2_g, ln2_b):
    batch, seq, _ = x.shape
    mem_len = mem.shape[1]
    n_tok = batch * seq
    route_tile = min(512, n_tok)

    bias = _banded_bias(rel_bias)
    lb = jnp.cumsum(jax.nn.softmax(hgrn_lb_logits.astype(F32), axis=0), axis=0)
    lb = lb - lb[0:1]
    log_lb = jnp.log(lb)
    log_1m_lb = jnp.log1p(-lb)
    head_ones = (jnp.arange(HG_WIDTH)[:, None] // HG_DV == jnp.arange(HG_WIDTH)[None, :] // HG_DV).astype(BF16)
    tri = (jnp.arange(route_tile)[:, None] <= jnp.arange(route_tile)[None, :]).astype(BF16)
    wr_t = jnp.transpose(w_router).astype(BF16)
    wr_pad = jnp.pad(w_router.astype(BF16), ((0, 0), (0, LANES - N_EXPERTS)))
    rb_col = router_bias.astype(F32).reshape(N_EXPERTS, 1)
    w_in_bf = w_in.astype(BF16)
    w_out_bf = w_out.astype(BF16)
    route_steps = n_tok // route_tile
    down_rows = N_EXPERTS * D_EXPERT
    cast_in_out_proj = down_rows % route_steps == 0 and (down_rows // route_steps) % (2 * SUBLANES) == 0
    if cast_in_out_proj:
        w_slabs = (w_gate.reshape(DEPTH, N_EXPERTS * D_MODEL, D_EXPERT),
                   w_up.reshape(DEPTH, N_EXPERTS * D_MODEL, D_EXPERT),
                   w_down.reshape(DEPTH, N_EXPERTS * D_EXPERT, D_MODEL))
    else:
        w_slabs = ()
        expert_w = (w_gate.astype(BF16), w_up.astype(BF16), w_down.astype(BF16))

    mk, mv = _mem_kv(mem.reshape(batch * mem_len, D_MODEL), w_mem_kv)

    x2 = x.reshape(n_tok, D_MODEL)
    proj = _in_proj(x2, w_in_bf[0], b_in[0].reshape(1, IN_WIDTH))
    for l in range(DEPTH):
        sq, sk, sv, hq, hf, hi, hg, mq = proj
        swa_o = _swa(sq, sk, sv, attn_sinks[l].astype(F32), bias, batch, seq)
        hg_o = _hgrn(hq, hf, hi, hg, log_lb[l].reshape(1, HG_WIDTH), log_1m_lb[l].reshape(1, HG_WIDTH),
                     jnp.tile(hgrn_norm[l].astype(F32), HG_HEADS).reshape(1, HG_WIDTH), head_ones, batch, seq)
        mem_o = _mem_attn(mq, mk, mv, l, batch, seq, mem_len)
        x1_z, cls, rank, counts, *cast_w = _out_proj(
            x2, swa_o, hg_o, mem_o, w_out_bf[l], ln1_g[l].reshape(1, D_MODEL), ln1_b[l].reshape(1, D_MODEL),
            wr_t, rb_col, tri, l, w_slabs)
        if cast_in_out_proj:
            up_shape, down_shape = (1, N_EXPERTS, D_MODEL, D_EXPERT), (1, N_EXPERTS, D_EXPERT, D_MODEL)
            layer_w = (cast_w[0].reshape(up_shape), cast_w[1].reshape(up_shape), cast_w[2].reshape(down_shape), 0)
        else:
            layer_w = (*expert_w, l)
        cstart, items = _work_items(counts[:N_CLASSES, 0].astype(I32), n_tok)
        dest, xs_z = _scatter_rows(x1_z, cls, rank, cstart)
        ys_z = _experts(xs_z, items, wr_pad, *layer_w)
        ln2_args = (x1_z, ys_z, dest.reshape(n_tok), ln2_g[l].reshape(1, D_MODEL), ln2_b[l].reshape(1, D_MODEL),
                    route_tile)
        if l + 1 < DEPTH:
            x2, *proj = _ln2(*ln2_args, w_in_bf[l + 1], b_in[l + 1].reshape(1, IN_WIDTH))
        else:
            (x2,) = _ln2(*ln2_args)
    return x2.reshape(batch, seq, D_MODEL)
```

```python
import functools
import math

import jax
import jax.numpy as jnp
from jax import lax
from jax.experimental import pallas as pl
from jax.experimental.pallas import tpu as pltpu

F32 = jnp.float32
BF16 = jnp.bfloat16
I32 = jnp.int32

D_MODEL = 1024
DEPTH = 4
HEAD_DIM = 64
SWA_HEADS = 8
SWA_KV_HEADS = 2
SWA_GROUP = SWA_HEADS // SWA_KV_HEADS
WINDOW = 128
ATTN_BLOCK = 128
HG_HEADS = 4
HG_DK = 64
HG_DV = 64
HG_WIDTH = HG_HEADS * HG_DK
MEM_HEADS = 4
N_BUCKETS = 32
MAX_DISTANCE = 128
N_EXPERTS = 16
N_GROUPS = 4
EXPERTS_PER_GROUP = 4
D_EXPERT = 512
LN_EPS = 1e-5
RMS_EPS = 1e-6
ALPHA = (2 * DEPTH) ** 0.25
SPLITS = (512, 128, 128, 256, 256, 256, 256, 256)
IN_WIDTH = sum(SPLITS)
PROJ_DTYPES = (BF16, BF16, BF16, F32, F32, F32, F32, BF16)
MIX_WIDTH = 1024
ATTN_SCALE = HEAD_DIM ** -0.5

SUBLANES = 8
LANES = 128
ROW_TILES = D_MODEL // LANES

SWA_STEP_BLOCKS = 8
HG_CHUNK = 128
HG_STEP_CHUNKS = 8
HG_SUB = 32
HG_SAFE_DECAY = 80.0
N_PAIRS = 6
N_CLASSES = N_GROUPS * N_PAIRS
CLASS_ROWS = 32
MOE_BLK = 256
DMA_GROUP = 8


def _params(semantics, vmem_mib):
    return pltpu.CompilerParams(dimension_semantics=semantics, vmem_limit_bytes=vmem_mib * 1024 * 1024)


def _dot(a, b):
    return jnp.dot(a, b, preferred_element_type=F32)


def _dot_nt(a, b):
    return lax.dot_general(a, b, (((1,), (1,)), ((), ())), preferred_element_type=F32)


def _dot_tn(a, b):
    return lax.dot_general(a, b, (((0,), (0,)), ((), ())), preferred_element_type=F32)


def _from_tiles(z, n_rows):
    k = z.shape[0] // n_rows
    return z.reshape(n_rows, k, LANES).reshape(n_rows, k * LANES)


def _to_tiles(val):
    n_rows, width = val.shape
    return val.reshape(n_rows, width // LANES, LANES).reshape(n_rows * (width // LANES), LANES)


def _in_proj_body(x_ref, w_ref, b_ref, *out_refs):
    acc = _dot(x_ref[...].astype(BF16), w_ref[...]) + b_ref[...]
    off = 0
    for ref, width in zip(out_refs, SPLITS):
        ref[...] = acc[:, off:off + width].astype(ref.dtype)
        off += width


def _in_proj(x2, w_bf, b_row):
    n = x2.shape[0]
    tm = min(512, n)
    return pl.pallas_call(
        _in_proj_body,
        grid=(n // tm,),
        in_specs=[pl.BlockSpec((tm, D_MODEL), lambda i: (i, 0)),
                  pl.BlockSpec((D_MODEL, IN_WIDTH), lambda i: (0, 0)),
                  pl.BlockSpec((1, IN_WIDTH), lambda i: (0, 0))],
        out_specs=[pl.BlockSpec((tm, w), lambda i: (i, 0)) for w in SPLITS],
        out_shape=[jax.ShapeDtypeStruct((n, w), dt) for w, dt in zip(SPLITS, PROJ_DTYPES)],
        compiler_params=_params(("arbitrary",), 48),
        name="in_proj",
    )(x2, w_bf, b_row)


def _mem_kv_body(mem_ref, w_ref, k_ref, v_ref):
    acc = _dot(mem_ref[...].astype(BF16), w_ref[0].astype(BF16))
    half = MEM_HEADS * HEAD_DIM
    k_ref[0] = acc[:, :half].astype(BF16)
    v_ref[0] = acc[:, half:].astype(BF16)


def _mem_kv(mem2, w_mem_kv):
    rows = mem2.shape[0]
    tm = min(512, rows)
    half = MEM_HEADS * HEAD_DIM
    return pl.pallas_call(
        _mem_kv_body,
        grid=(DEPTH, rows // tm),
        in_specs=[pl.BlockSpec((tm, D_MODEL), lambda l, i: (i, 0)),
                  pl.BlockSpec((1, D_MODEL, 2 * half), lambda l, i: (l, 0, 0))],
        out_specs=[pl.BlockSpec((1, tm, half), lambda l, i: (l, i, 0)),
                   pl.BlockSpec((1, tm, half), lambda l, i: (l, i, 0))],
        out_shape=[jax.ShapeDtypeStruct((DEPTH, rows, half), BF16)] * 2,
        compiler_params=_params(("arbitrary", "arbitrary"), 32),
        name="mem_kv",
    )(mem2, w_mem_kv)


def _swa_body(sink_ref, q_ref, kp_ref, ko_ref, vp_ref, vo_ref, bias_ref, o_ref, *, blocks):
    n = pl.program_id(1)
    q_all = q_ref[...] * ATTN_SCALE
    k_all = jnp.concatenate([kp_ref[...], ko_ref[...]], axis=0)
    v_all = jnp.concatenate([vp_ref[...], vo_ref[...]], axis=0)
    from_prev = lax.broadcasted_iota(I32, (ATTN_BLOCK, ATTN_BLOCK), 0) > \
        lax.broadcasted_iota(I32, (ATTN_BLOCK, ATTN_BLOCK), 1)
    for j in range(blocks):
        lo = j * ATTN_BLOCK
        kk = k_all[lo:lo + 2 * ATTN_BLOCK]
        vv = v_all[lo:lo + 2 * ATTN_BLOCK]
        table = jnp.where(n > 0, 1, 0) if j == 0 else 1
        outs = []
        for h in range(SWA_HEADS):
            g = h // SWA_GROUP
            q_h = q_all[lo:lo + ATTN_BLOCK, h * HEAD_DIM:(h + 1) * HEAD_DIM]
            scores = _dot_nt(kk[:, g * HEAD_DIM:(g + 1) * HEAD_DIM], q_h)
            band = jnp.where(from_prev, scores[:ATTN_BLOCK], scores[ATTN_BLOCK:]) + bias_ref[table, h]
            sink = sink_ref[h]
            m = jnp.maximum(jnp.max(band, axis=0, keepdims=True), sink)
            e = jnp.exp(band - m)
            p = e * (1.0 / (jnp.sum(e, axis=0, keepdims=True) + jnp.exp(sink - m)))
            p = jnp.concatenate([jnp.where(from_prev, p, 0.0), jnp.where(from_prev, 0.0, p)], axis=0)
            outs.append(_dot_tn(p.astype(BF16), vv[:, g * HEAD_DIM:(g + 1) * HEAD_DIM]))
        o_ref[lo:lo + ATTN_BLOCK, :] = jnp.concatenate(outs, axis=-1).astype(o_ref.dtype)


def _swa(sq, sk, sv, sinks, bias2, batch, seq):
    nb = seq // ATTN_BLOCK
    blocks = math.gcd(SWA_STEP_BLOCKS, nb)
    ns = nb // blocks
    own = lambda b, n: (b * ns + n, 0)
    prev = lambda b, n: (b * nb + jnp.maximum(n * blocks - 1, 0), 0)
    kvw = SWA_KV_HEADS * HEAD_DIM
    qw = SWA_HEADS * HEAD_DIM
    rows = blocks * ATTN_BLOCK
    return pl.pallas_call(
        functools.partial(_swa_body, blocks=blocks),
        grid=(batch, ns),
        in_specs=[pl.BlockSpec(memory_space=pltpu.SMEM),
                  pl.BlockSpec((rows, qw), own),
                  pl.BlockSpec((ATTN_BLOCK, kvw), prev),
                  pl.BlockSpec((rows, kvw), own),
                  pl.BlockSpec((ATTN_BLOCK, kvw), prev),
                  pl.BlockSpec((rows, kvw), own),
                  pl.BlockSpec((2, SWA_HEADS, ATTN_BLOCK, ATTN_BLOCK), lambda b, n: (0, 0, 0, 0))],
        out_specs=pl.BlockSpec((rows, qw), own),
        out_shape=jax.ShapeDtypeStruct((batch * seq, qw), BF16),
        compiler_params=_params(("arbitrary", "arbitrary"), 32),
        name="swa",
    )(sinks, sq, sk, sk, sv, sv, bias2)


def _cumsum_rows(x):
    n = x.shape[0]
    row = lax.broadcasted_iota(I32, (n, 1), 0)
    s = 1
    while s < n:
        x = x + jnp.where(row >= s, pltpu.roll(x, s, 0), 0.0)
        s *= 2
    return x


def _rows_from_blocks(vals, width):
    return jnp.concatenate([jnp.broadcast_to(v, (HG_SUB, width)) for v in vals], axis=0)


def _hgrn_body(hq_ref, hf_ref, hi_ref, hg_ref, loglb_ref, log1mlb_ref, nw_ref, bo_ref, o_ref, *scratch,
               chunks):
    sequence_start = pl.program_id(1) == 0
    for j in range(chunks):
        rows = pl.ds(j * HG_CHUNK, HG_CHUNK)
        _hgrn_chunk(sequence_start if j == 0 else None,
                    hq_ref.at[rows], hf_ref.at[rows], hi_ref.at[rows], hg_ref.at[rows],
                    loglb_ref, log1mlb_ref, nw_ref, bo_ref, o_ref.at[rows], *scratch)


def _hgrn_chunk(reset, hq_ref, hf_ref, hi_ref, hg_ref, loglb_ref, log1mlb_ref, nw_ref, bo_ref, o_ref,
                st_ref, a_scr, q_scr, k_scr, oi_scr):
    n_sub = HG_CHUNK // HG_SUB
    width = HG_WIDTH

    if reset is not None:
        @pl.when(reset)
        def _():
            st_ref[...] = jnp.zeros_like(st_ref)

    z = hf_ref[...]
    log_sig = jnp.minimum(z, 0.0) - jnp.log(1.0 + jnp.exp(-jnp.abs(z)))
    t_a = loglb_ref[...]
    t_b = log1mlb_ref[...] + log_sig
    log_f = jnp.maximum(t_a, t_b) + jnp.log(1.0 + jnp.exp(-jnp.abs(t_a - t_b)))
    kk = jnp.exp(t_b - z)
    hq = hq_ref[...]
    qq = hq * jax.nn.sigmoid(hq) * (HG_DK ** -0.5)
    vv = hi_ref[...]

    a_cum = _cumsum_rows(log_f)
    ends = [a_cum[HG_SUB * j + HG_SUB - 1:HG_SUB * j + HG_SUB, :] for j in range(n_sub)]
    zero_row = jnp.zeros((1, width), F32)
    starts = [zero_row] + ends[:-1]
    a_loc = a_cum - _rows_from_blocks(starts, width)
    e_loc = _rows_from_blocks(ends, width) - a_cum
    worst = starts[0] - ends[0]
    for j in range(1, n_sub):
        worst = jnp.maximum(worst, starts[j] - ends[j])
    unsafe = jnp.max(worst) > HG_SAFE_DECAY

    q_sub = qq * jnp.exp(a_loc)
    k_sub = kk * jnp.exp(-a_loc)
    k_end = kk * jnp.exp(e_loc)
    one_row = jnp.ones((1, width), F32)
    q_dec = q_sub * _rows_from_blocks([jnp.exp(s) for s in starts], width)
    k_dec = k_end * _rows_from_blocks([jnp.exp(ends[-1] - e) for e in ends], width)
    q_top = q_sub * _rows_from_blocks(
        [zero_row, zero_row, one_row, jnp.exp(ends[2] - ends[1])], width)
    k_top = k_end * _rows_from_blocks(
        [jnp.exp(ends[1] - ends[0]), one_row, zero_row, zero_row], width)

    row = lax.broadcasted_iota(I32, (HG_CHUNK, 1), 0)
    sub_of_row = row // HG_SUB
    t_idx = lax.broadcasted_iota(I32, (HG_CHUNK, HG_CHUNK), 0)
    s_idx = lax.broadcasted_iota(I32, (HG_CHUNK, HG_CHUNK), 1)
    m_sub = ((t_idx // HG_SUB) == (s_idx // HG_SUB)) & (t_idx >= s_idx)
    m_sub2 = jnp.concatenate([m_sub, m_sub], axis=0)
    lane = lax.broadcasted_iota(I32, (1, LANES), 1)
    lo = lane < HG_DK
    bd = (lax.broadcasted_iota(I32, (LANES, LANES), 0) // HG_DV) == \
         (lax.broadcasted_iota(I32, (LANES, LANES), 1) // HG_DK)
    dec_row = jnp.exp(ends[-1])

    def heads_on_rows(x):
        return jnp.concatenate([jnp.where(lo, x, 0.0), jnp.where(lo, 0.0, x)], axis=0)

    o_inter = []
    for p in range(width // LANES):
        sl = slice(p * LANES, (p + 1) * LANES)
        qs, ks, ke = q_sub[:, sl], k_sub[:, sl], k_end[:, sl]
        p_sub = _dot_nt(heads_on_rows(qs).astype(BF16), ks.astype(BF16))
        q_x = jnp.concatenate([jnp.where(sub_of_row == 1, qs, 0.0),
                               jnp.where(sub_of_row == 3, qs, 0.0), q_top[:, sl]], axis=1)
        k_x = jnp.concatenate([jnp.where(sub_of_row == 0, ke, 0.0),
                               jnp.where(sub_of_row == 2, ke, 0.0), k_top[:, sl]], axis=1)
        q_x2 = jnp.concatenate([jnp.where(jnp.tile(lo, (1, 3)), q_x, 0.0),
                                jnp.where(jnp.tile(lo, (1, 3)), 0.0, q_x)], axis=0)
        p_x = _dot_nt(q_x2.astype(BF16), k_x.astype(BF16))
        p_all = jnp.where(m_sub2, p_sub, 0.0) + p_x
        p_cat = jnp.concatenate([p_all[:HG_CHUNK], p_all[HG_CHUNK:]], axis=1)
        v_p = vv[:, sl]
        oi_scr[:, sl] = _dot(p_cat.astype(BF16), heads_on_rows(v_p).astype(BF16))
        st = st_ref[p]
        o_inter.append(_dot_nt(q_dec[:, sl].astype(BF16), st.astype(BF16)))
        upd = _dot_tn(v_p.astype(BF16), k_dec[:, sl].astype(BF16))
        st_ref[p] = st * dec_row[:, sl] + jnp.where(bd, upd, 0.0)
    o_inter = jnp.concatenate(o_inter, axis=1)

    @pl.when(unsafe)
    def _():
        a_scr[...] = a_cum
        q_scr[...] = qq
        k_scr[...] = kk

        def body(t, carry):
            d = a_scr[pl.ds(t, 1), :] - a_scr[...]
            w = jnp.exp(jnp.where(row <= t, d, -jnp.inf))
            prod = (q_scr[pl.ds(t, 1), :] * k_scr[...]) * w
            e = _dot(prod.astype(BF16), bo_ref[...])
            oi_scr[pl.ds(t, 1), :] = jnp.sum(e * hi_ref[...], axis=0, keepdims=True)
            return carry

        lax.fori_loop(0, HG_CHUNK, body, 0)

    o = oi_scr[...] + o_inter
    sq = o * o
    sq_hi = sq.astype(BF16)
    sq_lo = (sq - sq_hi.astype(F32)).astype(BF16)
    ms = (_dot(sq_hi, bo_ref[...]) + _dot(sq_lo, bo_ref[...])) * (1.0 / HG_DV)
    gate = hg_ref[...]
    o = o * lax.rsqrt(ms + RMS_EPS) * nw_ref[...] * (gate * jax.nn.sigmoid(gate))
    o_ref[...] = o.astype(o_ref.dtype)


def _hgrn(hq, hf, hi, hg, loglb, log1mlb, nw_row, head_ones, batch, seq):
    chunks = math.gcd(HG_STEP_CHUNKS, seq // HG_CHUNK)
    ns = seq // (HG_CHUNK * chunks)
    blk = lambda b, c: (b * ns + c, 0)
    row = lambda b, c: (0, 0)
    w = HG_WIDTH
    return pl.pallas_call(
        functools.partial(_hgrn_body, chunks=chunks),
        grid=(batch, ns),
        in_specs=[pl.BlockSpec((HG_CHUNK * chunks, w), blk)] * 4 + [pl.BlockSpec((1, w), row)] * 3
                 + [pl.BlockSpec((w, w), row)],
        out_specs=pl.BlockSpec((HG_CHUNK * chunks, w), blk),
        out_shape=jax.ShapeDtypeStruct((batch * seq, w), BF16),
        scratch_shapes=[pltpu.VMEM((w // LANES, LANES, LANES), F32)]
                       + [pltpu.VMEM((HG_CHUNK, w), F32)] * 4,
        compiler_params=_params(("arbitrary", "arbitrary"), 32),
        name="hgrn",
    )(hq, hf, hi, hg, loglb, log1mlb, nw_row, head_ones)


def _mem_attn_body(q_ref, k_ref, v_ref, o_ref):
    q_all = q_ref[...] * ATTN_SCALE
    outs = []
    for h in range(MEM_HEADS):
        sl = slice(h * HEAD_DIM, (h + 1) * HEAD_DIM)
        scores = _dot_nt(k_ref[0, :, sl], q_all[:, sl])
        e = jnp.exp(scores - jnp.max(scores, axis=0, keepdims=True))
        p = e * (1.0 / jnp.sum(e, axis=0, keepdims=True))
        outs.append(_dot_tn(p.astype(BF16), v_ref[0, :, sl]))
    o_ref[...] = jnp.concatenate(outs, axis=-1).astype(o_ref.dtype)


def _mem_attn(mq, mk, mv, layer, batch, seq, mem_len):
    tq = min(1024, seq)
    nq = seq // tq
    w = MEM_HEADS * HEAD_DIM
    return pl.pallas_call(
        _mem_attn_body,
        grid=(batch, nq),
        in_specs=[pl.BlockSpec((tq, w), lambda b, i: (b * nq + i, 0)),
                  pl.BlockSpec((1, mem_len, w), lambda b, i: (layer, b, 0)),
                  pl.BlockSpec((1, mem_len, w), lambda b, i: (layer, b, 0))],
        out_specs=pl.BlockSpec((tq, w), lambda b, i: (b * nq + i, 0)),
        out_shape=jax.ShapeDtypeStruct((batch * seq, w), BF16),
        compiler_params=_params(("arbitrary", "arbitrary"), 32),
        name="mem_attn",
    )(mq, mk, mv)


def _layer_norm(h, g, b):
    mu = jnp.mean(h, axis=-1, keepdims=True)
    d = h - mu
    var = jnp.mean(d * d, axis=-1, keepdims=True)
    return d * lax.rsqrt(var + LN_EPS) * g + b


def _route_rows(sel):
    def row(a, r):
        return a[r:r + 1, :]

    best = None
    for g in range(N_GROUPS):
        a = [row(sel, 4 * g + i) for i in range(4)]
        gs = a[0] + a[1]
        for i, j in ((0, 2), (0, 3), (1, 2), (1, 3), (2, 3)):
            gs = jnp.maximum(gs, a[i] + a[j])
        if best is None:
            best, best_score = jnp.zeros_like(gs, dtype=I32), gs
        else:
            better = gs > best_score
            best = jnp.where(better, g, best)
            best_score = jnp.where(better, gs, best_score)

    def pick(arr, i):
        out = row(arr, i)
        for g in range(1, N_GROUPS):
            out = jnp.where(best == g, row(arr, 4 * g + i), out)
        return out

    a = [pick(sel, i) for i in range(4)]
    i1, m1 = jnp.zeros_like(best), a[0]
    for i in range(1, 4):
        gt = a[i] > m1
        i1 = jnp.where(gt, i, i1)
        m1 = jnp.where(gt, a[i], m1)
    i2 = jnp.full_like(best, -1)
    m2 = jnp.full_like(m1, -jnp.inf)
    for i in range(4):
        gt = (i1 != i) & (a[i] > m2)
        i2 = jnp.where(gt, i, i2)
        m2 = jnp.where(gt, a[i], m2)
    lo = jnp.minimum(i1, i2)
    hi = jnp.maximum(i1, i2)
    pair = jnp.where(lo == 0, hi - 1, jnp.where(lo == 1, hi + 1, 5))
    return best * N_PAIRS + pair


def _out_proj_body(x_ref, swa_ref, hg_ref, mem_ref, w_ref, g_ref, b_ref, wr_ref, rb_ref, tri_ref, *rest):
    n_cast = (len(rest) - 5) // 2
    cast_in = rest[:n_cast]
    x1_ref, cls_ref, rank_ref, cnt_ref = rest[n_cast:n_cast + 4]
    cast_out, carry_ref = rest[n_cast + 4:-1], rest[-1]
    for src, dst in zip(cast_in, cast_out):
        dst[...] = src[0].astype(BF16)

    i = pl.program_id(0)
    t = x_ref.shape[0]

    @pl.when(i == 0)
    def _():
        carry_ref[...] = jnp.zeros_like(carry_ref)

    acc = _dot(swa_ref[...], w_ref[0:512, :]) + _dot(hg_ref[...], w_ref[512:768, :]) \
        + _dot(mem_ref[...], w_ref[768:1024, :])
    x1 = _layer_norm(ALPHA * x_ref[...] + acc, g_ref[...], b_ref[...])
    x1_ref[...] = _to_tiles(x1)

    logits = _dot_nt(wr_ref[...], x1.astype(BF16))
    cls = _route_rows(jax.nn.sigmoid(logits) + rb_ref[...])
    onehot = (lax.broadcasted_iota(I32, (CLASS_ROWS, t), 0) == cls).astype(F32)
    prefix = _dot(onehot.astype(BF16), tri_ref[...])
    carry = carry_ref[...]
    rank = jnp.sum(onehot * (prefix - 1.0 + carry[:, 0:1]), axis=0, keepdims=True)
    new_carry = carry + prefix[:, t - 1:t]
    carry_ref[...] = new_carry
    cnt_ref[...] = new_carry
    cls_ref[0] = cls
    rank_ref[0] = rank.astype(I32)


def _out_proj(x2, swa_o, hg_o, mem_o, w_bf, g_row, b_row, wr_t, rb_col, tri, layer, cast=()):
    n = x2.shape[0]
    t = tri.shape[0]
    nt = n // t
    tok = lambda i: (i, 0)
    fixed = lambda i: (0, 0)
    per_tile = pl.BlockSpec((1, 1, t), lambda i: (i, 0, 0))
    cast_in = [pl.BlockSpec((1, a.shape[1] // nt, a.shape[2]), lambda i: (layer, i, 0)) for a in cast]
    cast_out = [pl.BlockSpec((a.shape[1] // nt, a.shape[2]), tok) for a in cast]
    return pl.pallas_call(
        _out_proj_body,
        grid=(nt,),
        in_specs=[pl.BlockSpec((t, D_MODEL), tok), pl.BlockSpec((t, 512), tok),
                  pl.BlockSpec((t, 256), tok), pl.BlockSpec((t, 256), tok),
                  pl.BlockSpec((MIX_WIDTH, D_MODEL), fixed),
                  pl.BlockSpec((1, D_MODEL), fixed), pl.BlockSpec((1, D_MODEL), fixed),
                  pl.BlockSpec((N_EXPERTS, D_MODEL), fixed), pl.BlockSpec((N_EXPERTS, 1), fixed),
                  pl.BlockSpec((t, t), fixed)] + cast_in,
        out_specs=[pl.BlockSpec((t * ROW_TILES, LANES), tok), per_tile, per_tile,
                   pl.BlockSpec((CLASS_ROWS, LANES), fixed)] + cast_out,
        out_shape=[jax.ShapeDtypeStruct((n * ROW_TILES, LANES), F32),
                   jax.ShapeDtypeStruct((nt, 1, t), I32), jax.ShapeDtypeStruct((nt, 1, t), I32),
                   jax.ShapeDtypeStruct((CLASS_ROWS, LANES), F32)]
                  + [jax.ShapeDtypeStruct(a.shape[1:], BF16) for a in cast],
        scratch_shapes=[pltpu.VMEM((CLASS_ROWS, LANES), F32)],
        compiler_params=_params(("arbitrary",), 56),
        name="out_proj_ln_route",
    )(x2, swa_o, hg_o, mem_o, w_bf, g_row, b_row, wr_t, rb_col, tri, *cast)


def _tile_copy(src_ref, dst_ref, sem, src_row, dst_row, tiles):
    n = tiles * SUBLANES
    return pltpu.make_async_copy(
        src_ref.at[pl.ds(pl.multiple_of(src_row * n, n), n), :],
        dst_ref.at[pl.ds(pl.multiple_of(dst_row * n, n), n), :], sem)


def _scatter_body(cstart_ref, cls_ref, rank_ref, x_hbm, dest_ref, xs_hbm,
                  buf, dest_vmem, dest_smem, load_sem, scat_sem, misc_sem, *, n_steps, tile):
    i = pl.program_id(0)
    rows = tile * SUBLANES
    slot = i % 3

    def load(step, into):
        return pltpu.make_async_copy(
            x_hbm.at[pl.ds(pl.multiple_of(step * rows, rows), rows), :], buf.at[into], load_sem.at[into])

    def scatter_done(of):
        return pltpu.make_async_copy(buf.at[of], xs_hbm.at[pl.ds(0, rows), :], scat_sem.at[of])

    @pl.when(i == 0)
    def _():
        load(0, 0).start()

    @pl.when(i + 1 < n_steps)
    def _():
        load(i + 1, (i + 1) % 3).start()

    cls = cls_ref[0]
    dest = rank_ref[0]
    for c in range(N_CLASSES):
        dest = dest + jnp.where(cls == c, cstart_ref[c], 0)
    dest_ref[0] = dest
    dest_vmem[...] = dest
    to_smem = pltpu.make_async_copy(dest_vmem, dest_smem, misc_sem)
    to_smem.start()
    to_smem.wait()

    load(i, slot).wait()

    def issue(group, carry):
        for k in range(DMA_GROUP):
            j = group * DMA_GROUP + k
            _tile_copy(buf.at[slot], xs_hbm, scat_sem.at[slot], j, dest_smem[0, j], 1).start(priority=k % 2)
        return carry

    lax.fori_loop(0, tile // DMA_GROUP, issue, 0)

    @pl.when(i >= 1)
    def _():
        scatter_done((i + 2) % 3).wait()

    @pl.when(i == n_steps - 1)
    def _():
        scatter_done(slot).wait()


def _scatter_rows(x1_z, cls, rank, cstart):
    nt, _, tile = cls.shape
    n = nt * tile
    per_tile = pl.BlockSpec((1, 1, tile), lambda i, cs: (i, 0, 0))
    return pl.pallas_call(
        functools.partial(_scatter_body, n_steps=nt, tile=tile),
        grid_spec=pltpu.PrefetchScalarGridSpec(
            num_scalar_prefetch=1,
            grid=(nt,),
            in_specs=[per_tile, per_tile, pl.BlockSpec(memory_space=pl.ANY)],
            out_specs=[per_tile, pl.BlockSpec(memory_space=pl.ANY)],
            scratch_shapes=[pltpu.VMEM((3, tile * SUBLANES, LANES), F32),
                            pltpu.VMEM((1, tile), I32), pltpu.SMEM((1, tile), I32),
                            pltpu.SemaphoreType.DMA((3,)), pltpu.SemaphoreType.DMA((3,)),
                            pltpu.SemaphoreType.DMA(())]),
        out_shape=[jax.ShapeDtypeStruct((nt, 1, tile), I32),
                   jax.ShapeDtypeStruct((n * SUBLANES, LANES), F32)],
        compiler_params=_params(("arbitrary",), 32),
        name="scatter_rows",
    )(cstart, cls, rank, x1_z)


def _expert_body(blk_ref, elo_ref, ehi_ref, start_ref, end_ref, xs_ref, wr_ref,
                 g_lo, u_lo, d_lo, g_hi, u_hi, d_hi, ys_ref):
    i = pl.program_id(0)
    start = start_ref[i]
    end = end_ref[i]

    @pl.when(end > start)
    def _():
        x = _from_tiles(xs_ref[...], MOE_BLK).astype(BF16)
        scores = jax.nn.sigmoid(_dot(x, wr_ref[...]))
        lane = lax.broadcasted_iota(I32, (1, LANES), 1)
        s_lo = jnp.sum(jnp.where(lane == elo_ref[i], scores, 0.0), axis=-1, keepdims=True)
        s_hi = jnp.sum(jnp.where(lane == ehi_ref[i], scores, 0.0), axis=-1, keepdims=True)
        denom = s_lo + s_hi

        def ffn(gw, uw, dw, w_col):
            g = _dot(x, gw[0, 0])
            u = _dot(x, uw[0, 0])
            return _dot((((g * jax.nn.sigmoid(g)) * u) * w_col).astype(BF16), dw[0, 0])

        y = _to_tiles(ffn(g_lo, u_lo, d_lo, s_lo / denom) + ffn(g_hi, u_hi, d_hi, s_hi / denom))
        first = start % MOE_BLK == 0

        @pl.when(first)
        def _():
            ys_ref[...] = y

        @pl.when(jnp.logical_not(first))
        def _():
            slot = blk_ref[i] * MOE_BLK \
                + lax.broadcasted_iota(I32, (MOE_BLK * ROW_TILES, 1), 0) // ROW_TILES
            ys_ref[...] = jnp.where((slot >= start) & (slot < end), y, ys_ref[...])


def _experts(xs_z, items, wr_pad, wg, wu, wd, layer):
    blk, elo, ehi, start, end = items
    n_items = blk.shape[0]
    n_slots = xs_z.shape[0] // SUBLANES

    def data_map(i, blk, elo, ehi, start, end):
        return (blk[i], 0)

    def w_lo_map(i, blk, elo, ehi, start, end):
        return (layer, elo[i], 0, 0)

    def w_hi_map(i, blk, elo, ehi, start, end):
        return (layer, ehi[i], 0, 0)

    up_spec = lambda m: pl.BlockSpec((1, 1, D_MODEL, D_EXPERT), m)
    down_spec = lambda m: pl.BlockSpec((1, 1, D_EXPERT, D_MODEL), m)
    return pl.pallas_call(
        _expert_body,
        grid_spec=pltpu.PrefetchScalarGridSpec(
            num_scalar_prefetch=5,
            grid=(n_items,),
            in_specs=[pl.BlockSpec((MOE_BLK * SUBLANES, LANES), data_map),
                      pl.BlockSpec((D_MODEL, LANES), lambda i, *_: (0, 0)),
                      up_spec(w_lo_map), up_spec(w_lo_map), down_spec(w_lo_map),
                      up_spec(w_hi_map), up_spec(w_hi_map), down_spec(w_hi_map)],
            out_specs=pl.BlockSpec((MOE_BLK * SUBLANES, LANES), data_map)),
        out_shape=jax.ShapeDtypeStruct((n_slots * SUBLANES, LANES), F32),
        compiler_params=_params(("arbitrary",), 48),
        name="experts",
    )(blk, elo, ehi, start, end, xs_z, wr_pad, wg, wu, wd, wg, wu, wd)


def _gathered_ln2(dest_ref, x1_ref, ys_hbm, g_ref, b_ref, buf, sem, n_steps, tile):
    i = pl.program_id(0)
    slot = i % 2

    def gather(step, into):
        def issue(group, carry):
            for k in range(DMA_GROUP):
                j = group * DMA_GROUP + k
                _tile_copy(ys_hbm, buf.at[into], sem.at[into], dest_ref[step * tile + j], j, 1).start(
                    priority=k % 2)
            return carry

        lax.fori_loop(0, tile // DMA_GROUP, issue, 0)

    @pl.when(i == 0)
    def _():
        gather(0, 0)

    @pl.when(i + 1 < n_steps)
    def _():
        gather(i + 1, (i + 1) % 2)

    pltpu.make_async_copy(ys_hbm.at[pl.ds(0, tile * SUBLANES), :], buf.at[slot], sem.at[slot]).wait()
    h = ALPHA * _from_tiles(x1_ref[...], tile) + _from_tiles(buf[slot], tile)
    return _layer_norm(h, g_ref[...], b_ref[...])


def _ln2_body(dest_ref, x1_ref, ys_hbm, g_ref, b_ref, o_ref, buf, sem, *, n_steps, tile):
    o_ref[...] = _gathered_ln2(dest_ref, x1_ref, ys_hbm, g_ref, b_ref, buf, sem, n_steps, tile)


def _ln2_in_proj_body(dest_ref, x1_ref, ys_hbm, g_ref, b_ref, w_ref, bias_ref, o_ref, *rest, n_steps, tile):
    *proj_refs, buf, sem = rest
    x = _gathered_ln2(dest_ref, x1_ref, ys_hbm, g_ref, b_ref, buf, sem, n_steps, tile)
    o_ref[...] = x
    acc = _dot(x.astype(BF16), w_ref[...]) + bias_ref[...]
    off = 0
    for ref, width in zip(proj_refs, SPLITS):
        ref[...] = acc[:, off:off + width].astype(ref.dtype)
        off += width


def _ln2(x1_z, ys_z, dest, g_row, b_row, tile, w_bf=None, bias_row=None):
    n = x1_z.shape[0] // SUBLANES
    nt = n // tile
    tok = lambda i, d: (i, 0)
    fixed = lambda i, d: (0, 0)
    in_specs = [pl.BlockSpec((tile * SUBLANES, LANES), tok), pl.BlockSpec(memory_space=pl.ANY),
                pl.BlockSpec((1, D_MODEL), fixed), pl.BlockSpec((1, D_MODEL), fixed)]
    out_specs = [pl.BlockSpec((tile, D_MODEL), tok)]
    out_shape = [jax.ShapeDtypeStruct((n, D_MODEL), F32)]
    operands = [dest, x1_z, ys_z, g_row, b_row]
    body = _ln2_body
    if w_bf is not None:
        in_specs += [pl.BlockSpec((D_MODEL, IN_WIDTH), fixed), pl.BlockSpec((1, IN_WIDTH), fixed)]
        out_specs += [pl.BlockSpec((tile, w), tok) for w in SPLITS]
        out_shape += [jax.ShapeDtypeStruct((n, w), dt) for w, dt in zip(SPLITS, PROJ_DTYPES)]
        operands += [w_bf, bias_row]
        body = _ln2_in_proj_body
    return pl.pallas_call(
        functools.partial(body, n_steps=nt, tile=tile),
        grid_spec=pltpu.PrefetchScalarGridSpec(
            num_scalar_prefetch=1,
            grid=(nt,),
            in_specs=in_specs,
            out_specs=out_specs,
            scratch_shapes=[pltpu.VMEM((2, tile * SUBLANES, LANES), F32),
                            pltpu.SemaphoreType.DMA((2,))]),
        out_shape=out_shape,
        compiler_params=_params(("arbitrary",), 56),
        name="gather_ln2" if w_bf is None else "gather_ln2_in_proj",
    )(*operands)


def _t5_bucket(dist):
    max_exact = N_BUCKETS // 2
    d = jnp.maximum(dist, 0)
    large = max_exact + (jnp.log(jnp.maximum(d, 1).astype(F32) / max_exact)
                         / math.log(MAX_DISTANCE / max_exact) * (N_BUCKETS - max_exact)).astype(I32)
    large = jnp.minimum(large, N_BUCKETS - 1)
    return jnp.where(d < max_exact, d, large)


def _banded_bias(rel_bias):
    i = jnp.arange(ATTN_BLOCK)[:, None]
    j = jnp.arange(2 * ATTN_BLOCK)[None, :]
    dist = i + ATTN_BLOCK - j
    bucket = _t5_bucket(dist)[None]
    table = rel_bias.astype(F32)
    bias = jnp.zeros((SWA_HEADS, ATTN_BLOCK, 2 * ATTN_BLOCK), F32)
    for b in range(N_BUCKETS):
        bias = jnp.where(bucket == b, table[b][:, None, None], bias)
    from_prev = (jnp.arange(ATTN_BLOCK)[None, :] > i)[None]
    prev, own = bias[:, :, :ATTN_BLOCK], bias[:, :, ATTN_BLOCK:]
    normal = jnp.where(from_prev, prev, own)
    first = jnp.where(from_prev, -jnp.inf, own)
    return jnp.swapaxes(jnp.stack([first, normal]), 2, 3)


def _work_items(counts, n_tok):
    n_blocks = n_tok // MOE_BLK
    cend = jnp.cumsum(counts)
    cstart = cend - counts
    blk_starts = jnp.arange(n_blocks, dtype=I32) * MOE_BLK
    cls_starts = jnp.where(counts > 0, cstart, n_tok)
    start = jnp.sort(jnp.concatenate([blk_starts, cls_starts]))
    end = jnp.concatenate([start[1:], jnp.full((1,), n_tok, I32)])
    blk = jnp.minimum(start, n_tok - 1) // MOE_BLK
    cls = jnp.minimum(jnp.sum((cend[None, :] <= start[:, None]).astype(I32), axis=1), N_CLASSES - 1)
    group, pair = cls // N_PAIRS, cls % N_PAIRS
    lo = (pair >= 3).astype(I32) + (pair >= 5).astype(I32)
    hi = jnp.where(pair < 3, pair + 1, jnp.where(pair < 5, pair - 1, 3))
    return cstart, (blk, group * EXPERTS_PER_GROUP + lo, group * EXPERTS_PER_GROUP + hi, start, end)


def kernel(x, mem, w_in, b_in, w_mem_kv, attn_sinks, rel_bias, hgrn_lb_logits, hgrn_norm, w_out,
           ln1_g, ln1_b, w_router, router_bias, w_gate, w_up, w_down, ln2_g, ln2_b):
    batch, seq, _ = x.shape
    mem_len = mem.shape[1]
    n_tok = batch * seq
    route_tile = min(512, n_tok)

    bias = _banded_bias(rel_bias)
    lb = jnp.cumsum(jax.nn.softmax(hgrn_lb_logits.astype(F32), axis=0), axis=0)
    lb = lb - lb[0:1]
    log_lb = jnp.log(lb)
    log_1m_lb = jnp.log1p(-lb)
    head_ones = (jnp.arange(HG_WIDTH)[:, None] // HG_DV == jnp.arange(HG_WIDTH)[None, :] // HG_DV).astype(BF16)
    tri = (jnp.arange(route_tile)[:, None] <= jnp.arange(route_tile)[None, :]).astype(BF16)
    wr_t = jnp.transpose(w_router).astype(BF16)
    wr_pad = jnp.pad(w_router.astype(BF16), ((0, 0), (0, LANES - N_EXPERTS)))
    rb_col = router_bias.astype(F32).reshape(N_EXPERTS, 1)
    w_in_bf = w_in.astype(BF16)
    w_out_bf = w_out.astype(BF16)
    route_steps = n_tok // route_tile
    down_rows = N_EXPERTS * D_EXPERT
    cast_in_out_proj = down_rows % route_steps == 0 and (down_rows // route_steps) % (2 * SUBLANES) == 0
    if cast_in_out_proj:
        w_slabs = (w_gate.reshape(DEPTH, N_EXPERTS * D_MODEL, D_EXPERT),
                   w_up.reshape(DEPTH, N_EXPERTS * D_MODEL, D_EXPERT),
                   w_down.reshape(DEPTH, N_EXPERTS * D_EXPERT, D_MODEL))
    else:
        w_slabs = ()
        expert_w = (w_gate.astype(BF16), w_up.astype(BF16), w_down.astype(BF16))

    mk, mv = _mem_kv(mem.reshape(batch * mem_len, D_MODEL), w_mem_kv)

    x2 = x.reshape(n_tok, D_MODEL)
    proj = _in_proj(x2, w_in_bf[0], b_in[0].reshape(1, IN_WIDTH))
    for l in range(DEPTH):
        sq, sk, sv, hq, hf, hi, hg, mq = proj
        swa_o = _swa(sq, sk, sv, attn_sinks[l].astype(F32), bias, batch, seq)
        hg_o = _hgrn(hq, hf, hi, hg, log_lb[l].reshape(1, HG_WIDTH), log_1m_lb[l].reshape(1, HG_WIDTH),
                     jnp.tile(hgrn_norm[l].astype(F32), HG_HEADS).reshape(1, HG_WIDTH), head_ones, batch, seq)
        mem_o = _mem_attn(mq, mk, mv, l, batch, seq, mem_len)
        x1_z, cls, rank, counts, *cast_w = _out_proj(
            x2, swa_o, hg_o, mem_o, w_out_bf[l], ln1_g[l].reshape(1, D_MODEL), ln1_b[l].reshape(1, D_MODEL),
            wr_t, rb_col, tri, l, w_slabs)
        if cast_in_out_proj:
            up_shape, down_shape = (1, N_EXPERTS, D_MODEL, D_EXPERT), (1, N_EXPERTS, D_EXPERT, D_MODEL)
            layer_w = (cast_w[0].reshape(up_shape), cast_w[1].reshape(up_shape), cast_w[2].reshape(down_shape), 0)
        else:
            layer_w = (*expert_w, l)
        cstart, items = _work_items(counts[:N_CLASSES, 0].astype(I32), n_tok)
        dest, xs_z = _scatter_rows(x1_z, cls, rank, cstart)
        ys_z = _experts(xs_z, items, wr_pad, *layer_w)
        ln2_args = (x1_z, ys_z, dest.reshape(n_tok), ln2_g[l].reshape(1, D_MODEL), ln2_b[l].reshape(1, D_MODEL),
                    route_tile)
        if l + 1 < DEPTH:
            x2, *proj = _ln2(*ln2_args, w_in_bf[l + 1], b_in[l + 1].reshape(1, IN_WIDTH))
        else:
            (x2,) = _ln2(*ln2_args)
    return x2.reshape(batch, seq, D_MODEL)
```
